```python
import jax
import jax.numpy as jnp
from jax import lax
import numpy as np

D_MODEL = 1024
BATCH = 4
SEQ = 4096
DEPTH = 1

GRID_W = 64
CTX_LEN = 256
FOUR_WIDTH = D_MODEL // 2
FOUR_GROUPS = 4
FOUR_GROUP_DIM = FOUR_WIDTH // FOUR_GROUPS
RET_WIDTH = D_MODEL // 2
RET_HEAD_DIM = 64
RET_HEADS = RET_WIDTH // RET_HEAD_DIM
RET_CHUNK = 128
D_FF = 4 * D_MODEL
N_MOD = 6
ROPE_BASE = 10000.0
EPS = 1e-6
IN_SPLITS = (FOUR_WIDTH,
             FOUR_WIDTH + RET_WIDTH,
             FOUR_WIDTH + 2 * RET_WIDTH,
             FOUR_WIDTH + 3 * RET_WIDTH,
             FOUR_WIDTH + 4 * RET_WIDTH,
             FOUR_WIDTH + 4 * RET_WIDTH + D_MODEL)
IN_COLS = FOUR_WIDTH + 4 * RET_WIDTH + 2 * D_MODEL
K_OFF = IN_SPLITS[1]
V_OFF = IN_SPLITS[2]

kernel_name = 'hybrid_fourier_retention_dit_block'


def _rms_norm(x, gain):
    xf = x.astype(jnp.float32)
    y = xf * lax.rsqrt(jnp.mean(xf * xf, axis=-1, keepdims=True) + EPS)
    return (y * gain.astype(jnp.float32)).astype(x.dtype)


def _modulate(h, shift, scale):
    return h * (1.0 + scale) + shift


def _to_heads(t):
    b, l, _ = t.shape
    return t.reshape(b, l, RET_HEADS, RET_HEAD_DIM).transpose(0, 2, 1, 3)


def _rope_tables(rows):
    nf = RET_HEAD_DIM // 4
    inv = ROPE_BASE ** (-jnp.arange(nf, dtype=jnp.float32) / nf)
    r, cc = jnp.meshgrid(jnp.arange(rows, dtype=jnp.float32),
                         jnp.arange(GRID_W, dtype=jnp.float32), indexing='ij')
    ang = jnp.concatenate([r.reshape(-1)[:, None] * inv, cc.reshape(-1)[:, None] * inv], axis=-1)
    return jnp.cos(ang), jnp.sin(ang)


def _apply_rope(t, cos, sin):
    half = t.shape[-1] // 2
    cos = cos.astype(t.dtype)
    sin = sin.astype(t.dtype)
    t1, t2 = t[..., :half], t[..., half:]
    return jnp.concatenate([t1 * cos - t2 * sin, t1 * sin + t2 * cos], axis=-1)


def _retention_chunkwise(q, k, v, log_gamma, s0):
    b, h, l, dk = q.shape
    dv = v.shape[-1]
    n = l // RET_CHUNK
    dt = q.dtype
    pos = jnp.arange(RET_CHUNK, dtype=jnp.float32)
    diff = pos[:, None] - pos[None, :]
    lg = log_gamma[:, None, None]
    intra_decay = jnp.where(diff >= 0, jnp.exp(lg * jnp.maximum(diff, 0.0)), 0.0).astype(dt)
    q_decay = jnp.exp(log_gamma[:, None] * (pos + 1.0)).astype(dt)
    k_decay = jnp.exp(log_gamma[:, None] * (RET_CHUNK - 1.0 - pos)).astype(dt)
    chunk_decay = jnp.exp(log_gamma * RET_CHUNK).astype(dt)
    qc = q.reshape(b, h, n, RET_CHUNK, dk)
    kc = k.reshape(b, h, n, RET_CHUNK, dk)
    vc = v.reshape(b, h, n, RET_CHUNK, dv)
    scores = jnp.einsum('bhnid,bhnjd->bhnij', qc, kc) * intra_decay[None, :, None]
    o_intra = jnp.einsum('bhnij,bhnjv->bhniv', scores, vc)
    kv = jnp.einsum('bhncd,bhncv->nbhdv', kc * k_decay[None, :, None, :, None], vc)

    def step(state, kv_n):
        return chunk_decay[None, :, None, None] * state + kv_n, state

    s_final, s_prev = lax.scan(step, s0.astype(dt), kv)
    o_inter = jnp.einsum('bhncd,nbhdv->bhncv', qc * q_decay[None, :, None, :, None], s_prev)
    return (o_intra + o_inter).reshape(b, h, l, dv), s_final


def _retention_final_state(k, v, log_gamma):
    l = k.shape[2]
    w = jnp.exp(log_gamma[:, None] * (l - 1.0 - jnp.arange(l, dtype=jnp.float32))).astype(k.dtype)
    return jnp.einsum('bhld,bhlv->bhdv', k * w[None, :, :, None], v)


def _bidir_retention(q, k, v, log_gamma, s_f, s_b):
    o_f, sf_out = _retention_chunkwise(q, k, v, log_gamma[0], s_f)
    o_b, sb_out = _retention_chunkwise(jnp.flip(q, 2), jnp.flip(k, 2), jnp.flip(v, 2), log_gamma[1], s_b)
    return o_f + jnp.flip(o_b, 2), sf_out, sb_out


def _fourier_mix(u):
    b, l, _ = u.shape
    ug = u.astype(jnp.float32).reshape(b, l, FOUR_GROUPS, FOUR_GROUP_DIM)
    f = jnp.fft.fft2(ug, axes=(1, 3), norm='ortho').real
    return f.reshape(b, l, FOUR_WIDTH).astype(u.dtype)


def _merge(u, o, g, gate_a, gate_b, four_w_out, gn_gain, ret_w_out, w_out):
    y_four = _fourier_mix(u) @ four_w_out
    b, h, l, dv = o.shape
    of = o.astype(jnp.float32)
    of = of * lax.rsqrt(jnp.mean(of * of, axis=-1, keepdims=True) + EPS)
    o_n = of.transpose(0, 2, 1, 3).reshape(b, l, h * dv).astype(u.dtype) * gn_gain
    y_ret = (jax.nn.silu(g) * o_n) @ ret_w_out
    y = jax.nn.sigmoid(gate_a) * y_four + jax.nn.sigmoid(gate_b) * y_ret
    return y @ w_out


def _token_mixer(h_x, h_c, w_in, four_w_out, log_gamma, gn_gain, ret_w_out, w_out, with_ctx_out):
    scale = RET_HEAD_DIM ** -0.5
    u, q, k, v, g, ga, gb = jnp.split(h_x @ w_in, list(IN_SPLITS), axis=-1)
    rows = h_x.shape[1] // GRID_W
    cos, sin = _rope_tables(rows)
    q = _apply_rope(_to_heads(q), cos, sin)
    k = _apply_rope(_to_heads(k), cos, sin) * scale
    v = _to_heads(v)
    if with_ctx_out:
        uc, qc, kc, vc, gc, gac, gbc = jnp.split(h_c @ w_in, list(IN_SPLITS), axis=-1)
        qc = _to_heads(qc)
        kc = _to_heads(kc) * scale
        vc = _to_heads(vc)
        zero = jnp.zeros((qc.shape[0], RET_HEADS, RET_HEAD_DIM, RET_HEAD_DIM), qc.dtype)
        oc, s_f, s_b = _bidir_retention(qc, kc, vc, log_gamma, zero, zero)
        y_c = _merge(uc, oc, gc, gac, gbc, four_w_out, gn_gain, ret_w_out, w_out)
    else:
        kc = _to_heads(h_c @ w_in[:, K_OFF:K_OFF + RET_WIDTH]) * scale
        vc = _to_heads(h_c @ w_in[:, V_OFF:V_OFF + RET_WIDTH])
        s_f = _retention_final_state(kc, vc, log_gamma[0])
        s_b = _retention_final_state(jnp.flip(kc, 2), jnp.flip(vc, 2), log_gamma[1])
        y_c = None
    o, _, _ = _bidir_retention(q, k, v, log_gamma, s_f, s_b)
    y_x = _merge(u, o, g, ga, gb, four_w_out, gn_gain, ret_w_out, w_out)
    return y_x, y_c


def _sq_relu_mlp(h, w1, w2):
    return jnp.square(jax.nn.relu(h @ w1)) @ w2


def setup_inputs(seed: int = 0) -> dict:
    key = jax.random.key(seed)
    ks = jax.random.split(key, 18)
    f32 = jnp.float32

    def nrm(k, shape, s):
        return jax.random.normal(k, shape, f32) * s

    logit0 = jnp.log(2.0 ** (5.0 + jnp.arange(RET_HEADS, dtype=f32)) - 1.0)
    return {
        'x': nrm(ks[0], (BATCH, SEQ, D_MODEL), 1.0),
        'c': nrm(ks[1], (BATCH, D_MODEL), 1.0),
        'ctx': nrm(ks[2], (BATCH, CTX_LEN, D_MODEL), 1.0),
        'c_ctx': nrm(ks[3], (D_MODEL,), 1.0),
        'w_ada': nrm(ks[4], (DEPTH, D_MODEL, N_MOD * D_MODEL), 0.5 * D_MODEL ** -0.5),
        'b_ada': nrm(ks[5], (DEPTH, N_MOD * D_MODEL), 0.01),
        'norm1_gain': 1.0 + nrm(ks[6], (DEPTH, D_MODEL), 0.02),
        'w_in': nrm(ks[7], (DEPTH, D_MODEL, IN_COLS), D_MODEL ** -0.5),
        'four_w_out': nrm(ks[8], (DEPTH, FOUR_WIDTH, D_MODEL), FOUR_WIDTH ** -0.5),
        'ret_decay_logit': logit0[None, None, :] + nrm(ks[9], (DEPTH, 2, RET_HEADS), 0.1),
        'ret_gn_gain': 1.0 + nrm(ks[10], (DEPTH, RET_WIDTH), 0.02),
        'ret_w_out': nrm(ks[11], (DEPTH, RET_WIDTH, D_MODEL), RET_WIDTH ** -0.5),
        'w_out': nrm(ks[12], (DEPTH, D_MODEL, D_MODEL), D_MODEL ** -0.5),
        'norm2_gain': 1.0 + nrm(ks[13], (DEPTH, D_MODEL), 0.02),
        'w_mlp1': nrm(ks[14], (DEPTH, D_MODEL, D_FF), D_MODEL ** -0.5),
        'w_mlp2': nrm(ks[15], (DEPTH, D_FF, D_MODEL), D_FF ** -0.5),
        'final_gain': 1.0 + nrm(ks[16], (D_MODEL,), 0.02),
    }


def reference(x, c, ctx, c_ctx, w_ada, b_ada, norm1_gain, w_in, four_w_out, ret_decay_logit,
              ret_gn_gain, ret_w_out, w_out, norm2_gain, w_mlp1, w_mlp2, final_gain):
    for l in range(DEPTH):
        with_ctx_out = l < DEPTH - 1
        mod_x = jax.nn.silu(c) @ w_ada[l] + b_ada[l]
        mod_c = jax.nn.silu(c_ctx) @ w_ada[l] + b_ada[l]
        sh1, sc1, g1, sh2, sc2, g2 = jnp.split(mod_x[:, None, :], N_MOD, axis=-1)
        csh1, csc1, cg1, csh2, csc2, cg2 = jnp.split(mod_c, N_MOD, axis=-1)
        log_gamma = jax.nn.log_sigmoid(ret_decay_logit[l].astype(jnp.float32))

        h_x = _modulate(_rms_norm(x, norm1_gain[l]), sh1, sc1)
        h_c = _modulate(_rms_norm(ctx, norm1_gain[l]), csh1, csc1)
        y_x, y_c = _token_mixer(h_x, h_c, w_in[l], four_w_out[l], log_gamma, ret_gn_gain[l],
                                ret_w_out[l], w_out[l], with_ctx_out)
        x = x + g1 * y_x
        h2 = _modulate(_rms_norm(x, norm2_gain[l]), sh2, sc2)
        x = x + g2 * _sq_relu_mlp(h2, w_mlp1[l], w_mlp2[l])

        if with_ctx_out:
            ctx = ctx + cg1 * y_c
            hc2 = _modulate(_rms_norm(ctx, norm2_gain[l]), csh2, csc2)
            ctx = ctx + cg2 * _sq_relu_mlp(hc2, w_mlp1[l], w_mlp2[l])
    return _rms_norm(x, final_gain)
```

```python
import functools
import math

import jax
import jax.numpy as jnp
import numpy as np
from jax import lax
from jax.experimental import pallas as pl
from jax.experimental.pallas import tpu as pltpu

F32 = jnp.float32
BF16 = jnp.bfloat16

GRID_W = 64
FOUR_GROUPS = 4
RET_HEAD_DIM = 64
N_MOD = 6
ROPE_BASE = 10000.0
EPS = 1e-6

LANES = 128
MXU_DIM = 256
VMEM_LIMIT_BYTES = 56 * 1024 * 1024

RET_CHUNK = 128
HEADS_PER_GROUP = MXU_DIM // RET_HEAD_DIM
GROUP_W = HEADS_PER_GROUP * RET_HEAD_DIM
FFT_L1 = 128
ADA_TN = 1024
TOKEN_TILE = 512


def _dot(a, b):
    return jnp.dot(a, b, preferred_element_type=F32)


def _dot_nt(a, b):
    return lax.dot_general(a, b, (((1,), (1,)), ((), ())), preferred_element_type=F32)


def _dot_tn(a, b):
    return lax.dot_general(a, b, (((0,), (0,)), ((), ())), preferred_element_type=F32)


def _norm_mod(x, gain, shift, scale):
    ms = jnp.mean(x * x, axis=-1, keepdims=True)
    y = x * lax.rsqrt(ms + EPS) * gain
    return y * (1.0 + scale) + shift


def _tile_lanes(t, reps):
    return jnp.concatenate([t] * reps, axis=1) if reps > 1 else t


def _const_spec(shape):
    nd = len(shape)
    return pl.BlockSpec(shape, lambda *_: (0,) * nd, pipeline_mode=pl.Buffered(1))


def _params(*sem):
    return pltpu.CompilerParams(dimension_semantics=sem, vmem_limit_bytes=VMEM_LIMIT_BYTES)


def _ada_kernel(c_ref, w_ref, b_ref, o_ref):
    c = c_ref[...]
    s = c * jax.nn.sigmoid(c)
    o_ref[...] = _dot(s.astype(BF16), w_ref[...].astype(BF16)) + b_ref[...]


def _ada(cvec, w_ada, b_ada):
    rows, d = cvec.shape
    n = w_ada.shape[1]
    return pl.pallas_call(
        _ada_kernel,
        grid=(n // ADA_TN,),
        in_specs=[
            pl.BlockSpec((rows, d), lambda j: (0, 0)),
            pl.BlockSpec((d, ADA_TN), lambda j: (0, j)),
            pl.BlockSpec((1, ADA_TN), lambda j: (0, j)),
        ],
        out_specs=pl.BlockSpec((rows, ADA_TN), lambda j: (0, j)),
        out_shape=jax.ShapeDtypeStruct((rows, n), F32),
        compiler_params=_params("arbitrary"),
        name="ada",
    )(cvec, w_ada, b_ada)


def _proj_kernel(x_ref, mod_ref, gain_ref, w_ref, cs_ref, cos_ref, sin_ref,
                 zr_ref, zi_ref, q_ref, k_ref, v_ref, g_ref, ga_ref, gb_ref,
                 *, fw, rw, d, scale):
    mod = mod_ref[0]
    h = _norm_mod(x_ref[0], gain_ref[...], mod[0:1], mod[1:2]).astype(BF16)
    tm = h.shape[0]

    def proj(lo, width):
        return _dot(h, w_ref[:, lo:lo + width])

    u = proj(0, fw).astype(BF16)
    gd = fw // FOUR_GROUPS
    cs = cs_ref[...]
    for gi in range(FOUR_GROUPS):
        z = _dot(u[:, gi * gd:(gi + 1) * gd], cs)
        zr_ref[0, :, gi * gd:(gi + 1) * gd] = z[:, :gd].astype(BF16)
        zi_ref[0, :, gi * gd:(gi + 1) * gd] = z[:, gd:].astype(BF16)

    reps = rw // cos_ref.shape[1]
    cosf = _tile_lanes(cos_ref[...], reps)
    sinf = _tile_lanes(sin_ref[...], reps)
    lane = lax.broadcasted_iota(jnp.int32, (tm, rw), 1)
    first_half = (lane & (RET_HEAD_DIM // 2)) == 0

    def rope(t):
        rot = jnp.where(first_half,
                        pltpu.roll(t, rw - RET_HEAD_DIM // 2, 1),
                        pltpu.roll(t, RET_HEAD_DIM // 2, 1))
        return t * cosf + rot * sinf

    q_ref[0] = rope(proj(fw, rw)).astype(BF16)
    k_ref[0] = (rope(proj(fw + rw, rw)) * scale).astype(BF16)
    v_ref[0] = proj(fw + 2 * rw, rw).astype(BF16)
    g_ref[0] = proj(fw + 3 * rw, rw).astype(BF16)
    ga_ref[0] = proj(fw + 4 * rw, d).astype(BF16)
    gb_ref[0] = proj(fw + 4 * rw + d, d).astype(BF16)


def _proj(x, mods, gain, w_in, cs, cosf, sinf, fw, rw):
    b, l, d = x.shape
    tm = min(TOKEN_TILE, l)
    gd = fw // FOUR_GROUPS
    tok = lambda width: pl.BlockSpec((1, tm, width), lambda i, j: (i, j, 0))
    out = lambda width: jax.ShapeDtypeStruct((b, l, width), BF16)
    return pl.pallas_call(
        functools.partial(_proj_kernel, fw=fw, rw=rw, d=d, scale=RET_HEAD_DIM ** -0.5),
        grid=(b, l // tm),
        in_specs=[
            tok(d),
            pl.BlockSpec((1, N_MOD, d), lambda i, j: (i, 0, 0)),
            _const_spec((1, d)),
            _const_spec(w_in.shape),
            _const_spec((gd, 2 * gd)),
            pl.BlockSpec((tm, cosf.shape[1]), lambda i, j: (j, 0)),
            pl.BlockSpec((tm, sinf.shape[1]), lambda i, j: (j, 0)),
        ],
        out_specs=[tok(fw), tok(fw), tok(rw), tok(rw), tok(rw), tok(rw), tok(d), tok(d)],
        out_shape=[out(fw), out(fw), out(rw), out(rw), out(rw), out(rw), out(d), out(d)],
        compiler_params=_params("arbitrary", "arbitrary"),
        name="proj",
    )(x, mods, gain, w_in, cs, cosf, sinf)


def _ctxproj_kernel(c_ref, mod_ref, gain_ref, w_ref, k_ref, v_ref, *, rw, scale):
    mod = mod_ref[0]
    h = _norm_mod(c_ref[0], gain_ref[...], mod[0:1], mod[1:2]).astype(BF16)
    k_ref[0] = (_dot(h, w_ref[:, :rw]) * scale).astype(BF16)
    v_ref[0] = _dot(h, w_ref[:, rw:]).astype(BF16)


def _ctxproj(ctx, mod_c, gain, w_in, fw, rw):
    b, lc, d = ctx.shape
    kv_block = (fw + rw) // (2 * rw)
    assert kv_block * 2 * rw == fw + rw
    return pl.pallas_call(
        functools.partial(_ctxproj_kernel, rw=rw, scale=RET_HEAD_DIM ** -0.5),
        grid=(b,),
        in_specs=[
            pl.BlockSpec((1, lc, d), lambda i: (i, 0, 0)),
            pl.BlockSpec((1, N_MOD, d), lambda i: (0, 0, 0)),
            pl.BlockSpec((1, d), lambda i: (0, 0)),
            pl.BlockSpec((d, 2 * rw), lambda i: (0, kv_block)),
        ],
        out_specs=[pl.BlockSpec((1, lc, rw), lambda i: (i, 0, 0))] * 2,
        out_shape=[jax.ShapeDtypeStruct((b, lc, rw), BF16)] * 2,
        compiler_params=_params("arbitrary"),
        name="ctxproj",
    )(ctx, mod_c, gain, w_in)


def _fft1_kernel(zr_ref, zi_ref, w1_ref, twc_ref, tws_ref, tr_ref, ti_ref, *, l1, fw, nl2):
    z = jnp.concatenate([zr_ref[0], zi_ref[0]], axis=0)
    t = _dot(w1_ref[...], z)
    tr, ti = t[:l1], t[l1:]
    reps = fw // twc_ref.shape[2]
    for j in range(nl2):
        ct = _tile_lanes(twc_ref[j], reps)
        st = _tile_lanes(tws_ref[j], reps)
        a = tr[:, j * fw:(j + 1) * fw]
        b = ti[:, j * fw:(j + 1) * fw]
        tr_ref[0, j] = (a * ct + b * st).astype(BF16)
        ti_ref[0, j] = (b * ct - a * st).astype(BF16)


def _fft1(zr, zi, w1, twc, tws, l1, l2):
    b, l, fw = zr.shape
    nl2 = min(8, l2)
    zr2 = zr.reshape(b, l1, l2 * fw)
    zi2 = zi.reshape(b, l1, l2 * fw)
    zspec = pl.BlockSpec((1, l1, nl2 * fw), lambda i, j: (i, 0, j))
    tspec = pl.BlockSpec((nl2, l1, twc.shape[2]), lambda i, j: (j, 0, 0))
    ospec = pl.BlockSpec((1, nl2, l1, fw), lambda i, j: (i, j, 0, 0))
    oshape = jax.ShapeDtypeStruct((b, l2, l1, fw), BF16)
    return pl.pallas_call(
        functools.partial(_fft1_kernel, l1=l1, fw=fw, nl2=nl2),
        grid=(b, l2 // nl2),
        in_specs=[zspec, zspec, _const_spec(w1.shape), tspec, tspec],
        out_specs=[ospec, ospec],
        out_shape=[oshape, oshape],
        compiler_params=_params("arbitrary", "arbitrary"),
        name="fft1",
    )(zr2, zi2, w1, twc, tws)


def _fft2_kernel(tr_ref, ti_ref, w2_ref, o_ref, *, scale):
    t = jnp.concatenate([tr_ref[0], ti_ref[0]], axis=0)
    o_ref[0] = (_dot(w2_ref[...], t) * scale).astype(BF16)


def _fft2(tr, ti, w2, scale):
    b, l2, l1, fw = tr.shape
    n = l1 * fw
    tn = min(8192, n)
    tr2 = tr.reshape(b, l2, n)
    ti2 = ti.reshape(b, l2, n)
    spec = pl.BlockSpec((1, l2, tn), lambda i, j: (i, 0, j))
    out = pl.pallas_call(
        functools.partial(_fft2_kernel, scale=scale),
        grid=(b, n // tn),
        in_specs=[spec, spec, _const_spec(w2.shape)],
        out_specs=spec,
        out_shape=jax.ShapeDtypeStruct((b, l2, n), BF16),
        compiler_params=_params("arbitrary", "arbitrary"),
        name="fft2",
    )(tr2, ti2, w2)
    return out.reshape(b, l2 * l1, fw)


def _ret_kernel(q_ref, k_ref, v_ref, g_ref, kc_ref, vc_ref, dall_ref, qdf_ref, qdb_ref,
                kdf_ref, kdb_ref, cdf_ref, cdb_ref, wcf_ref, wcb_ref, mbd_ref, gain_ref,
                z_ref, sb_ref, st_ref, *, nchunk):
    c = RET_CHUNK
    gw = GROUP_W
    mbd = mbd_ref[...]
    lane = lax.broadcasted_iota(jnp.int32, (c, gw), 1)
    head_masks = [(lane >= h * RET_HEAD_DIM) & (lane < (h + 1) * RET_HEAD_DIM)
                  for h in range(HEADS_PER_GROUP)]

    def stack_heads(t):
        zero = jnp.zeros_like(t)
        return jnp.concatenate([jnp.where(m, t, zero) for m in head_masks], axis=0)

    def weighted(t, w):
        return (t.astype(F32) * w).astype(BF16)

    def chunk(ref, n):
        return ref[0, pl.ds(pl.multiple_of(n * c, c), c), :]

    kc = kc_ref[0]
    vc = vc_ref[0]

    st_ref[...] = mbd * _dot_tn(weighted(kc, wcb_ref[...]), vc)

    def bwd_body(i, carry):
        n = nchunk - 1 - i
        s = st_ref[...]
        sb_ref[n] = s.astype(BF16)
        kv = _dot_tn(weighted(chunk(k_ref, n), kdb_ref[...]), chunk(v_ref, n))
        st_ref[...] = cdb_ref[...] * s + mbd * kv
        return carry

    lax.fori_loop(0, nchunk, bwd_body, 0)

    st_ref[...] = mbd * _dot_tn(weighted(kc, wcf_ref[...]), vc)
    bd_mean = (mbd * (1.0 / RET_HEAD_DIM)).astype(BF16)
    gain = gain_ref[...]

    def fwd_body(n, carry):
        qn = chunk(q_ref, n)
        kn = chunk(k_ref, n)
        vn = chunk(v_ref, n)
        s = st_ref[...]
        a = _dot_nt(qn, stack_heads(kn))
        p = (a * dall_ref[...]).astype(BF16)
        o = _dot(p, stack_heads(vn))
        o = o + _dot(weighted(qn, qdf_ref[...]), s.astype(BF16))
        o = o + _dot(weighted(qn, qdb_ref[...]), sb_ref[n])
        st_ref[...] = cdf_ref[...] * s + mbd * _dot_tn(weighted(kn, kdf_ref[...]), vn)
        o2 = o * o
        hi = o2.astype(BF16)
        lo = (o2 - hi.astype(F32)).astype(BF16)
        ms = _dot(hi, bd_mean) + _dot(lo, bd_mean)
        o_n = o * lax.rsqrt(ms + EPS) * gain
        gg = chunk(g_ref, n).astype(F32)
        z_ref[0, pl.ds(pl.multiple_of(n * c, c), c), :] = (gg * jax.nn.sigmoid(gg) * o_n).astype(BF16)
        return carry

    lax.fori_loop(0, nchunk, fwd_body, 0)


def _retention(q, k, v, g, kc, vc, tabs, gn_gain):
    b, l, rw = q.shape
    lc = kc.shape[1]
    c = RET_CHUNK
    gw = GROUP_W
    ng = rw // gw
    nchunk = l // c
    tok = pl.BlockSpec((1, l, gw), lambda i, j: (i, 0, j))
    ctx = pl.BlockSpec((1, lc, gw), lambda i, j: (i, 0, j))
    grp = lambda rows, width: pl.BlockSpec((None, rows, width), lambda i, j: (j, 0, 0))
    return pl.pallas_call(
        functools.partial(_ret_kernel, nchunk=nchunk),
        grid=(b, ng),
        in_specs=[
            tok, tok, tok, tok, ctx, ctx,
            grp(c, HEADS_PER_GROUP * c),
            grp(c, gw), grp(c, gw), grp(c, gw), grp(c, gw),
            grp(gw, gw), grp(gw, gw),
            grp(lc, gw), grp(lc, gw),
            pl.BlockSpec((gw, gw), lambda i, j: (0, 0)),
            pl.BlockSpec((1, gw), lambda i, j: (0, j)),
        ],
        out_specs=tok,
        out_shape=jax.ShapeDtypeStruct((b, l, rw), BF16),
        scratch_shapes=[pltpu.VMEM((nchunk, gw, gw), BF16), pltpu.VMEM((gw, gw), F32)],
        compiler_params=_params("arbitrary", "arbitrary"),
        name="ret",
    )(q, k, v, g, kc, vc, tabs["dall"], tabs["qdf"], tabs["qdb"], tabs["kdf"], tabs["kdb"],
      tabs["cdf"], tabs["cdb"], tabs["wcf"], tabs["wcb"], tabs["mbd"], gn_gain)


def _retention_tables(log_gamma, lc):
    c = RET_CHUNK
    hpg = HEADS_PER_GROUP
    nh = log_gamma.shape[1]
    ng = nh // hpg
    lgf = log_gamma[0].reshape(ng, hpg)
    lgb = log_gamma[1].reshape(ng, hpg)
    pos = jnp.arange(c, dtype=F32)
    diff = pos[:, None] - pos[None, :]
    df = jnp.where(diff >= 0, jnp.exp(lgf[:, :, None, None] * jnp.maximum(diff, 0.0)), 0.0)
    db = jnp.where(diff <= 0, jnp.exp(lgb[:, :, None, None] * jnp.maximum(-diff, 0.0)), 0.0)
    dall = (df + db).transpose(0, 2, 1, 3).reshape(ng, c, hpg * c)

    def per_lane(lg, expo):
        t = jnp.exp(lg[:, None, :] * expo[None, :, None])
        return jnp.repeat(t, RET_HEAD_DIM, axis=2)

    cpos = jnp.arange(lc, dtype=F32)
    head_of = jnp.arange(GROUP_W) // RET_HEAD_DIM
    mbd = (head_of[:, None] == head_of[None, :]).astype(F32)

    def chunk_decay(lg):
        cd = jnp.repeat(jnp.exp(lg * c), RET_HEAD_DIM, axis=1)
        return cd[:, :, None] * mbd[None]

    return dict(
        dall=dall,
        qdf=per_lane(lgf, pos + 1.0), qdb=per_lane(lgb, c - pos),
        kdf=per_lane(lgf, c - 1.0 - pos), kdb=per_lane(lgb, pos),
        cdf=chunk_decay(lgf), cdb=chunk_decay(lgb),
        wcf=per_lane(lgf, lc - 1.0 - cpos), wcb=per_lane(lgb, cpos),
        mbd=mbd,
    )


def _out_kernel(x_ref, mod_ref, fm_ref, z_ref, ga_ref, gb_ref, w4_ref, wr_ref, wo_ref,
                gain2_ref, w1_ref, w2_ref, fgain_ref, o_ref, *, ff_chunk):
    mod = mod_ref[0]
    g1, sh2, sc2, g2 = mod[2:3], mod[3:4], mod[4:5], mod[5:6]
    y_four = _dot(fm_ref[0], w4_ref[...])
    y_ret = _dot(z_ref[0], wr_ref[...])
    y = (jax.nn.sigmoid(ga_ref[0].astype(F32)) * y_four
         + jax.nn.sigmoid(gb_ref[0].astype(F32)) * y_ret)
    x1 = x_ref[0] + g1 * _dot(y.astype(BF16), wo_ref[...])
    h2 = _norm_mod(x1, gain2_ref[...], sh2, sc2).astype(BF16)
    dff = w1_ref.shape[1]
    acc = None
    for lo in range(0, dff, ff_chunk):
        hid = jnp.maximum(_dot(h2, w1_ref[:, lo:lo + ff_chunk]), 0.0)
        part = _dot((hid * hid).astype(BF16), w2_ref[lo:lo + ff_chunk, :])
        acc = part if acc is None else acc + part
    x2 = x1 + g2 * acc
    ms = jnp.mean(x2 * x2, axis=-1, keepdims=True)
    o_ref[0] = x2 * lax.rsqrt(ms + EPS) * fgain_ref[...]


def _out(x, mods, fm, z, ga, gb, w4, wr, wo, gain2, w1, w2, fgain):
    b, l, d = x.shape
    tm = min(TOKEN_TILE, l)
    tok = lambda width: pl.BlockSpec((1, tm, width), lambda i, j: (i, j, 0))
    return pl.pallas_call(
        functools.partial(_out_kernel, ff_chunk=min(1024, w1.shape[1])),
        grid=(b, l // tm),
        in_specs=[
            tok(d),
            pl.BlockSpec((1, N_MOD, d), lambda i, j: (i, 0, 0)),
            tok(fm.shape[2]), tok(z.shape[2]), tok(d), tok(d),
            _const_spec(w4.shape), _const_spec(wr.shape), _const_spec(wo.shape),
            _const_spec((1, d)), _const_spec(w1.shape), _const_spec(w2.shape), _const_spec((1, d)),
        ],
        out_specs=tok(d),
        out_shape=jax.ShapeDtypeStruct((b, l, d), F32),
        compiler_params=_params("arbitrary", "arbitrary"),
        name="out",
    )(x, mods, fm, z, ga, gb, w4, wr, wo, gain2, w1, w2, fgain)


def _dft_tables(l, gd):
    l1 = FFT_L1
    l2 = l // l1

    def cs(n):
        idx = np.arange(n)
        ang = 2.0 * np.pi * ((idx[:, None] * idx[None, :]) % n) / n
        return np.cos(ang), np.sin(ang)

    cc, sc = cs(gd)
    chan = np.concatenate([cc, -sc], axis=1)
    c1, s1 = cs(l1)
    w1 = np.block([[c1, s1], [-s1, c1]])
    c2, s2 = cs(l2)
    w2 = np.concatenate([c2, s2], axis=1)
    tw = 2.0 * np.pi * (np.arange(l2)[:, None] * np.arange(l1)[None, :]) / l
    twc = np.repeat(np.cos(tw)[:, :, None], LANES, axis=2)
    tws = np.repeat(np.sin(tw)[:, :, None], LANES, axis=2)
    as_bf = lambda a: jnp.asarray(a, dtype=F32).astype(BF16)
    return as_bf(chan), as_bf(w1), as_bf(w2), jnp.asarray(twc, F32), jnp.asarray(tws, F32), l1, l2


def _rope_tables(l):
    nf = RET_HEAD_DIM // 4
    inv = ROPE_BASE ** (-jnp.arange(nf, dtype=F32) / nf)
    rows = l // GRID_W
    r, cc = jnp.meshgrid(jnp.arange(rows, dtype=F32), jnp.arange(GRID_W, dtype=F32), indexing="ij")
    ang = jnp.concatenate([r.reshape(-1)[:, None] * inv, cc.reshape(-1)[:, None] * inv], axis=-1)
    cos, sin = jnp.cos(ang), jnp.sin(ang)
    cos_h = jnp.concatenate([cos, cos], axis=1)
    sin_h = jnp.concatenate([-sin, sin], axis=1)
    reps = LANES // RET_HEAD_DIM
    return jnp.tile(cos_h, (1, reps)), jnp.tile(sin_h, (1, reps))


def kernel(x, c, ctx, c_ctx, w_ada, b_ada, norm1_gain, w_in, four_w_out, ret_decay_logit,
           ret_gn_gain, ret_w_out, w_out, norm2_gain, w_mlp1, w_mlp2, final_gain):
    assert w_ada.shape[0] == 1, "single-layer block"
    b, l, d = x.shape
    lc = ctx.shape[1]
    fw = four_w_out.shape[1]
    rw = ret_w_out.shape[1]
    gd = fw // FOUR_GROUPS
    assert l % (FFT_L1 * 8) == 0 or l // FFT_L1 < 8
    assert l % RET_CHUNK == 0 and rw % GROUP_W == 0

    pad = (-(b + 1)) % 8
    cvec = jnp.concatenate([c, c_ctx[None, :], jnp.zeros((pad, d), F32)], axis=0)
    mod = _ada(cvec, w_ada[0], b_ada)
    mods_x = mod[:b].reshape(b, N_MOD, d)
    mod_c = mod[b:b + 1].reshape(1, N_MOD, d)

    chan, w1, w2, twc, tws, l1, l2 = _dft_tables(l, gd)
    cosf, sinf = _rope_tables(l)
    w_in_b = w_in[0].astype(BF16)

    zr, zi, q, k, v, g, ga, gb = _proj(x, mods_x, norm1_gain, w_in_b, chan, cosf, sinf, fw, rw)
    kc, vc = _ctxproj(ctx, mod_c, norm1_gain, w_in_b, fw, rw)

    tr, ti = _fft1(zr, zi, w1, twc, tws, l1, l2)
    fm = _fft2(tr, ti, w2, 1.0 / math.sqrt(l * gd))

    log_gamma = jax.nn.log_sigmoid(ret_decay_logit[0].astype(F32))
    tabs = _retention_tables(log_gamma, lc)
    z = _retention(q, k, v, g, kc, vc, tabs, ret_gn_gain)

    return _out(x, mods_x, fm, z, ga, gb,
                four_w_out[0].astype(BF16), ret_w_out[0].astype(BF16), w_out[0].astype(BF16),
                norm2_gain, w_mlp1[0].astype(BF16), w_mlp2[0].astype(BF16), final_gain[None, :])
```

```python
import functools
import math

import jax
import jax.numpy as jnp
import numpy as np
from jax import lax
from jax.experimental import pallas as pl
from jax.experimental.pallas import tpu as pltpu

F32 = jnp.float32
BF16 = jnp.bfloat16

GRID_W = 64
FOUR_GROUPS = 4
RET_HEAD_DIM = 64
N_MOD = 6
ROPE_BASE = 10000.0
EPS = 1e-6

LANES = 128
MXU_DIM = 256
BF16_SUBLANES = 16
VMEM_LIMIT_BYTES = 56 * 1024 * 1024

RET_CHUNK = 128
RET_UNROLL = 4
HEADS_PER_GROUP = MXU_DIM // RET_HEAD_DIM
GROUP_W = HEADS_PER_GROUP * RET_HEAD_DIM
FFT_LA = 128
FFT_LA_TILE = BF16_SUBLANES
ADA_TN = 1024
TOKEN_TILE = 512


def _dot(a, b):
    return jnp.dot(a, b, preferred_element_type=F32)


def _norm_mod(x, gain, shift, scale):
    ms = jnp.mean(x * x, axis=-1, keepdims=True)
    y = x * lax.rsqrt(ms + EPS) * gain
    return y * (1.0 + scale) + shift


def _tile_lanes(t, reps):
    return jnp.concatenate([t] * reps, axis=1) if reps > 1 else t


def _const_spec(shape):
    nd = len(shape)
    return pl.BlockSpec(shape, lambda *_: (0,) * nd, pipeline_mode=pl.Buffered(1))


def _params(*sem):
    return pltpu.CompilerParams(dimension_semantics=sem, vmem_limit_bytes=VMEM_LIMIT_BYTES)


def _ada_kernel(c_ref, w_ref, b_ref, o_ref):
    c = c_ref[...]
    s = c * jax.nn.sigmoid(c)
    o_ref[...] = _dot(s.astype(BF16), w_ref[...].astype(BF16)) + b_ref[...]


def _ada(cvec, w_ada, b_ada):
    rows, d = cvec.shape
    n = w_ada.shape[1]
    return pl.pallas_call(
        _ada_kernel,
        grid=(n // ADA_TN,),
        in_specs=[
            pl.BlockSpec((rows, d), lambda j: (0, 0)),
            pl.BlockSpec((d, ADA_TN), lambda j: (0, j)),
            pl.BlockSpec((1, ADA_TN), lambda j: (0, j)),
        ],
        out_specs=pl.BlockSpec((rows, ADA_TN), lambda j: (0, j)),
        out_shape=jax.ShapeDtypeStruct((rows, n), F32),
        compiler_params=_params("arbitrary"),
        name="ada",
    )(cvec, w_ada, b_ada)


def _proj_kernel(x_ref, mod_ref, gain_ref, w_ref, cos_ref, sin_ref,
                 u_ref, q_ref, kt_ref, v_ref, g_ref, ga_ref, gb_ref, *, fw, rw, d, scale):
    mod = mod_ref[0]
    h = _norm_mod(x_ref[0], gain_ref[...], mod[0:1], mod[1:2]).astype(BF16)
    tm = h.shape[0]

    def proj(lo, width):
        return _dot(h, w_ref[:, lo:lo + width])

    u_ref[0] = proj(0, fw).astype(BF16)

    reps = rw // cos_ref.shape[1]
    cosf = _tile_lanes(cos_ref[...], reps)
    sinf = _tile_lanes(sin_ref[...], reps)
    lane = lax.broadcasted_iota(jnp.int32, (tm, rw), 1)
    first_half = (lane & (RET_HEAD_DIM // 2)) == 0

    def rope(t):
        rot = jnp.where(first_half,
                        pltpu.roll(t, rw - RET_HEAD_DIM // 2, 1),
                        pltpu.roll(t, RET_HEAD_DIM // 2, 1))
        return t * cosf + rot * sinf

    q_ref[0] = rope(proj(fw, rw)).astype(BF16)
    kt_ref[0] = (rope(proj(fw + rw, rw)) * scale).T.astype(BF16)
    v_ref[0] = proj(fw + 2 * rw, rw).astype(BF16)
    g_ref[0] = proj(fw + 3 * rw, rw).astype(BF16)
    ga_ref[0] = proj(fw + 4 * rw, d).astype(BF16)
    gb_ref[0] = proj(fw + 4 * rw + d, d).astype(BF16)


def _proj(x, mods, gain, w_in, cosf, sinf, fw, rw):
    b, l, d = x.shape
    tm = min(TOKEN_TILE, l)
    tok = lambda width: pl.BlockSpec((1, tm, width), lambda i, j: (i, j, 0))
    out = lambda width: jax.ShapeDtypeStruct((b, l, width), BF16)
    return pl.pallas_call(
        functools.partial(_proj_kernel, fw=fw, rw=rw, d=d, scale=RET_HEAD_DIM ** -0.5),
        grid=(b, l // tm),
        in_specs=[
            tok(d),
            pl.BlockSpec((1, N_MOD, d), lambda i, j: (i, 0, 0)),
            _const_spec((1, d)),
            _const_spec(w_in.shape),
            pl.BlockSpec((tm, cosf.shape[1]), lambda i, j: (j, 0)),
            pl.BlockSpec((tm, sinf.shape[1]), lambda i, j: (j, 0)),
        ],
        out_specs=[tok(fw), tok(rw), pl.BlockSpec((1, rw, tm), lambda i, j: (i, 0, j)),
                   tok(rw), tok(rw), tok(d), tok(d)],
        out_shape=[out(fw), out(rw), jax.ShapeDtypeStruct((b, rw, l), BF16),
                   out(rw), out(rw), out(d), out(d)],
        compiler_params=_params("arbitrary", "arbitrary"),
        name="proj",
    )(x, mods, gain, w_in, cosf, sinf)


def _ctxproj_kernel(c_ref, mod_ref, gain_ref, w_ref, kt_ref, v_ref, *, rw, scale):
    mod = mod_ref[0]
    h = _norm_mod(c_ref[0], gain_ref[...], mod[0:1], mod[1:2]).astype(BF16)
    kt_ref[0] = (_dot(h, w_ref[:, :rw]) * scale).T.astype(BF16)
    v_ref[0] = _dot(h, w_ref[:, rw:]).astype(BF16)


def _ctxproj(ctx, mod_c, gain, w_in, fw, rw):
    b, lc, d = ctx.shape
    kv_block = (fw + rw) // (2 * rw)
    assert kv_block * 2 * rw == fw + rw
    return pl.pallas_call(
        functools.partial(_ctxproj_kernel, rw=rw, scale=RET_HEAD_DIM ** -0.5),
        grid=(b,),
        in_specs=[
            pl.BlockSpec((1, lc, d), lambda i: (i, 0, 0)),
            pl.BlockSpec((1, N_MOD, d), lambda i: (0, 0, 0)),
            pl.BlockSpec((1, d), lambda i: (0, 0)),
            pl.BlockSpec((d, 2 * rw), lambda i: (0, kv_block)),
        ],
        out_specs=[pl.BlockSpec((1, rw, lc), lambda i: (i, 0, 0)),
                   pl.BlockSpec((1, lc, rw), lambda i: (i, 0, 0))],
        out_shape=[jax.ShapeDtypeStruct((b, rw, lc), BF16), jax.ShapeDtypeStruct((b, lc, rw), BF16)],
        compiler_params=_params("arbitrary"),
        name="ctxproj",
    )(ctx, mod_c, gain, w_in)


def _fft1_kernel(u_ref, kw_ref, twc_ref, tws_ref, tr_ref, ti_ref, *, lb, nsub):
    r = FFT_LA_TILE
    rows = lb * r
    for s in range(nsub):
        u = u_ref[0, :, s * r:(s + 1) * r, :].reshape(rows, u_ref.shape[3])
        t = _dot(kw_ref[...], u)
        a, b = t[:rows], t[rows:]
        reps = u.shape[1] // twc_ref.shape[2]
        ct = _tile_lanes(twc_ref[:, s * r:(s + 1) * r, :].reshape(rows, twc_ref.shape[2]), reps)
        st = _tile_lanes(tws_ref[:, s * r:(s + 1) * r, :].reshape(rows, tws_ref.shape[2]), reps)
        tr_ref[0, :, s * r:(s + 1) * r, :] = (a * ct + b * st).astype(BF16).reshape(lb, r, u.shape[1])
        ti_ref[0, :, s * r:(s + 1) * r, :] = (b * ct - a * st).astype(BF16).reshape(lb, r, u.shape[1])


def _fft1(u, kw, twc, tws, la, lb):
    b, l, fw = u.shape
    nsub = 2 if la % (2 * FFT_LA_TILE) == 0 else 1
    rr = nsub * FFT_LA_TILE
    u4 = u.reshape(b, lb, la, fw)
    blk = pl.BlockSpec((1, lb, rr, fw), lambda i, j: (i, 0, j, 0))
    tw = pl.BlockSpec((lb, rr, twc.shape[2]), lambda i, j: (0, j, 0))
    oshape = jax.ShapeDtypeStruct((b, lb, la, fw), BF16)
    tr, ti = pl.pallas_call(
        functools.partial(_fft1_kernel, lb=lb, nsub=nsub),
        grid=(b, la // rr),
        in_specs=[blk, _const_spec(kw.shape), tw, tw],
        out_specs=[blk, blk],
        out_shape=[oshape, oshape],
        compiler_params=_params("arbitrary", "arbitrary"),
        name="fft1",
    )(u4, kw, twc, tws)
    return tr.reshape(b, l, fw), ti.reshape(b, l, fw)


def _fft2_kernel(tr_ref, ti_ref, w2_ref, cs_ref, o_ref, scr_ref, *, la, npb, gd, scale):
    fw = tr_ref.shape[2]
    for p in range(npb):
        t = jnp.concatenate([tr_ref[0, p * la:(p + 1) * la, :], ti_ref[0, p * la:(p + 1) * la, :]], axis=0)
        x = _dot(w2_ref[...], t).astype(BF16)
        for gi in range(fw // gd):
            xg = jnp.concatenate([x[:la, gi * gd:(gi + 1) * gd], x[la:, gi * gd:(gi + 1) * gd]], axis=1)
            scr_ref[gi, pl.ds(p, la, stride=npb), :] = _dot(xg, cs_ref[...]) * scale
    full = jnp.concatenate([scr_ref[gi] for gi in range(fw // gd)], axis=1)
    o_ref[0] = full.reshape(la, npb, fw).astype(BF16)


def _fft2(tr, ti, w2, cs, la, lb, gd, scale):
    b, l, fw = tr.shape
    npb = min(BF16_SUBLANES, lb)
    inb = pl.BlockSpec((1, npb * la, fw), lambda i, j: (i, j, 0))
    out = pl.pallas_call(
        functools.partial(_fft2_kernel, la=la, npb=npb, gd=gd, scale=scale),
        grid=(b, lb // npb),
        in_specs=[inb, inb, _const_spec(w2.shape), _const_spec(cs.shape)],
        out_specs=pl.BlockSpec((1, la, npb, fw), lambda i, j: (i, 0, j, 0)),
        out_shape=jax.ShapeDtypeStruct((b, la, lb, fw), BF16),
        scratch_shapes=[pltpu.VMEM((fw // gd, la * npb, gd), F32)],
        compiler_params=_params("arbitrary", "arbitrary"),
        name="fft2",
    )(tr, ti, w2, cs)
    return out.reshape(b, l, fw)


def _ret_kernel(q_ref, kt_ref, v_ref, g_ref, kct_ref, vc_ref, dall_ref, qdf_ref, qdb_ref,
                kdf_ref, kdb_ref, cdf_ref, cdb_ref, wcf_ref, wcb_ref, mbd_ref, gain_ref,
                z_ref, sf_ref, sb_ref, stf_ref, stb_ref, p_ref, o_ref, *, nchunk, unroll):
    c = RET_CHUNK
    gw = GROUP_W
    mbd = mbd_ref[...]

    def weighted(t, w):
        return (t.astype(F32) * w).astype(BF16)

    def rows(ref, n):
        return ref[0, pl.ds(pl.multiple_of(n * c, c), c), :]

    def kt_chunk(n):
        return kt_ref[0, :, pl.ds(pl.multiple_of(n * c, c), c)]

    kct = kct_ref[0]
    vc = vc_ref[0]
    stf_ref[...] = mbd * _dot(kct, weighted(vc, wcf_ref[...]))
    stb_ref[...] = mbd * _dot(kct, weighted(vc, wcb_ref[...]))

    def scan_body(i, carry):
        nf = i
        nb = nchunk - 1 - i
        s = stf_ref[...]
        sf_ref[nf] = s.astype(BF16)
        stf_ref[...] = cdf_ref[...] * s + mbd * _dot(kt_chunk(nf), weighted(rows(v_ref, nf), kdf_ref[...]))
        s = stb_ref[...]
        sb_ref[nb] = s.astype(BF16)
        stb_ref[...] = cdb_ref[...] * s + mbd * _dot(kt_chunk(nb), weighted(rows(v_ref, nb), kdb_ref[...]))
        return carry

    lax.fori_loop(0, nchunk, scan_body, 0, unroll=unroll)

    lane = lax.broadcasted_iota(jnp.int32, (c, gw), 1)
    sub = lax.broadcasted_iota(jnp.int32, (gw, c), 0)
    lane_masks = [(lane >= h * RET_HEAD_DIM) & (lane < (h + 1) * RET_HEAD_DIM)
                  for h in range(HEADS_PER_GROUP)]
    sub_masks = [(sub >= h * RET_HEAD_DIM) & (sub < (h + 1) * RET_HEAD_DIM)
                 for h in range(HEADS_PER_GROUP)]
    bd_mean = (mbd * (1.0 / RET_HEAD_DIM)).astype(BF16)
    gain = gain_ref[...]

    def score_body(n, carry):
        ktn = kt_chunk(n)
        zk = jnp.zeros_like(ktn)
        kbd = jnp.concatenate([jnp.where(m, ktn, zk) for m in sub_masks], axis=1)
        p_ref[n] = (_dot(rows(q_ref, n), kbd) * dall_ref[...]).astype(BF16)
        return carry

    lax.fori_loop(0, nchunk, score_body, 0, unroll=unroll)

    def mix_body(n, carry):
        qn = rows(q_ref, n)
        vn = rows(v_ref, n)
        zv = jnp.zeros_like(vn)
        vbd = jnp.concatenate([jnp.where(m, vn, zv) for m in lane_masks], axis=0)
        qfb = jnp.concatenate([weighted(qn, qdf_ref[...]), weighted(qn, qdb_ref[...])], axis=1)
        o_ref[n] = _dot(p_ref[n], vbd) + _dot(qfb, jnp.concatenate([sf_ref[n], sb_ref[n]], axis=0))
        return carry

    lax.fori_loop(0, nchunk, mix_body, 0, unroll=unroll)

    def norm_body(n, carry):
        o = o_ref[n]
        o2 = o * o
        hi = o2.astype(BF16)
        lo = (o2 - hi.astype(F32)).astype(BF16)
        ms = _dot(jnp.concatenate([hi, lo], axis=1), jnp.concatenate([bd_mean, bd_mean], axis=0))
        o_n = o * lax.rsqrt(ms + EPS) * gain
        gg = rows(g_ref, n).astype(F32)
        z_ref[0, pl.ds(pl.multiple_of(n * c, c), c), :] = (gg * jax.nn.sigmoid(gg) * o_n).astype(BF16)
        return carry

    lax.fori_loop(0, nchunk, norm_body, 0, unroll=unroll)


def _retention(q, kt, v, g, kct, vc, tabs, gn_gain):
    b, l, rw = q.shape
    lc = vc.shape[1]
    c = RET_CHUNK
    gw = GROUP_W
    ng = rw // gw
    nchunk = l // c
    tok = pl.BlockSpec((1, l, gw), lambda i, j: (i, 0, j))
    tokt = pl.BlockSpec((1, gw, l), lambda i, j: (i, j, 0))
    grp = lambda r, width: pl.BlockSpec((None, r, width), lambda i, j: (j, 0, 0))
    return pl.pallas_call(
        functools.partial(_ret_kernel, nchunk=nchunk, unroll=math.gcd(nchunk, RET_UNROLL)),
        grid=(b, ng),
        in_specs=[
            tok, tokt, tok, tok,
            pl.BlockSpec((1, gw, lc), lambda i, j: (i, j, 0)),
            pl.BlockSpec((1, lc, gw), lambda i, j: (i, 0, j)),
            grp(c, HEADS_PER_GROUP * c),
            grp(c, gw), grp(c, gw), grp(c, gw), grp(c, gw),
            grp(gw, gw), grp(gw, gw),
            grp(lc, gw), grp(lc, gw),
            pl.BlockSpec((gw, gw), lambda i, j: (0, 0)),
            pl.BlockSpec((1, gw), lambda i, j: (0, j)),
        ],
        out_specs=tok,
        out_shape=jax.ShapeDtypeStruct((b, l, rw), BF16),
        scratch_shapes=[pltpu.VMEM((nchunk, gw, gw), BF16), pltpu.VMEM((nchunk, gw, gw), BF16),
                        pltpu.VMEM((gw, gw), F32), pltpu.VMEM((gw, gw), F32),
                        pltpu.VMEM((nchunk, c, HEADS_PER_GROUP * c), BF16),
                        pltpu.VMEM((nchunk, c, gw), F32)],
        compiler_params=_params("arbitrary", "arbitrary"),
        name="ret",
    )(q, kt, v, g, kct, vc, tabs["dall"], tabs["qdf"], tabs["qdb"], tabs["kdf"], tabs["kdb"],
      tabs["cdf"], tabs["cdb"], tabs["wcf"], tabs["wcb"], tabs["mbd"], gn_gain)


def _retention_tables(log_gamma, lc):
    c = RET_CHUNK
    hpg = HEADS_PER_GROUP
    nh = log_gamma.shape[1]
    ng = nh // hpg
    lgf = log_gamma[0].reshape(ng, hpg)
    lgb = log_gamma[1].reshape(ng, hpg)
    pos = jnp.arange(c, dtype=F32)
    diff = pos[:, None] - pos[None, :]
    df = jnp.where(diff >= 0, jnp.exp(lgf[:, :, None, None] * jnp.maximum(diff, 0.0)), 0.0)
    db = jnp.where(diff <= 0, jnp.exp(lgb[:, :, None, None] * jnp.maximum(-diff, 0.0)), 0.0)
    dall = (df + db).transpose(0, 2, 1, 3).reshape(ng, c, hpg * c)

    def per_lane(lg, expo):
        t = jnp.exp(lg[:, None, :] * expo[None, :, None])
        return jnp.repeat(t, RET_HEAD_DIM, axis=2)

    cpos = jnp.arange(lc, dtype=F32)
    head_of = jnp.arange(GROUP_W) // RET_HEAD_DIM
    mbd = (head_of[:, None] == head_of[None, :]).astype(F32)

    def chunk_decay(lg):
        cd = jnp.repeat(jnp.exp(lg * c), RET_HEAD_DIM, axis=1)
        return cd[:, :, None] * mbd[None]

    return dict(
        dall=dall,
        qdf=per_lane(lgf, pos + 1.0), qdb=per_lane(lgb, c - pos),
        kdf=per_lane(lgf, c - 1.0 - pos), kdb=per_lane(lgb, pos),
        cdf=chunk_decay(lgf), cdb=chunk_decay(lgb),
        wcf=per_lane(lgf, lc - 1.0 - cpos), wcb=per_lane(lgb, cpos),
        mbd=mbd,
    )


def _out_kernel(x_ref, mod_ref, fm_ref, z_ref, ga_ref, gb_ref, w4_ref, wr_ref, wo_ref,
                gain2_ref, w1_ref, w2_ref, fgain_ref, o_ref, *, ff_chunk):
    mod = mod_ref[0]
    g1, sh2, sc2, g2 = mod[2:3], mod[3:4], mod[4:5], mod[5:6]
    y_four = _dot(fm_ref[0], w4_ref[...])
    y_ret = _dot(z_ref[0], wr_ref[...])
    y = (jax.nn.sigmoid(ga_ref[0].astype(F32)) * y_four
         + jax.nn.sigmoid(gb_ref[0].astype(F32)) * y_ret)
    x1 = x_ref[0] + g1 * _dot(y.astype(BF16), wo_ref[...])
    h2 = _norm_mod(x1, gain2_ref[...], sh2, sc2).astype(BF16)
    dff = w1_ref.shape[1]
    acc = None
    for lo in range(0, dff, ff_chunk):
        hid = jnp.maximum(_dot(h2, w1_ref[:, lo:lo + ff_chunk]), 0.0)
        part = _dot((hid * hid).astype(BF16), w2_ref[lo:lo + ff_chunk, :])
        acc = part if acc is None else acc + part
    x2 = x1 + g2 * acc
    ms = jnp.mean(x2 * x2, axis=-1, keepdims=True)
    o_ref[0] = x2 * lax.rsqrt(ms + EPS) * fgain_ref[...]


def _out(x, mods, fm, z, ga, gb, w4, wr, wo, gain2, w1, w2, fgain):
    b, l, d = x.shape
    tm = min(TOKEN_TILE, l)
    tok = lambda width: pl.BlockSpec((1, tm, width), lambda i, j: (i, j, 0))
    return pl.pallas_call(
        functools.partial(_out_kernel, ff_chunk=min(1024, w1.shape[1])),
        grid=(b, l // tm),
        in_specs=[
            tok(d),
            pl.BlockSpec((1, N_MOD, d), lambda i, j: (i, 0, 0)),
            tok(fm.shape[2]), tok(z.shape[2]), tok(d), tok(d),
            _const_spec(w4.shape), _const_spec(wr.shape), _const_spec(wo.shape),
            _const_spec((1, d)), _const_spec(w1.shape), _const_spec(w2.shape), _const_spec((1, d)),
        ],
        out_specs=tok(d),
        out_shape=jax.ShapeDtypeStruct((b, l, d), F32),
        compiler_params=_params("arbitrary", "arbitrary"),
        name="out",
    )(x, mods, fm, z, ga, gb, w4, wr, wo, gain2, w1, w2, fgain)


def _dft_tables(l, gd):
    la = FFT_LA
    lb = l // la

    def cs(n):
        idx = np.arange(n)
        ang = 2.0 * np.pi * ((idx[:, None] * idx[None, :]) % n) / n
        return np.cos(ang), np.sin(ang)

    cb, sb = cs(lb)
    eye = np.eye(FFT_LA_TILE)
    kw = np.concatenate([np.kron(cb, eye), -np.kron(sb, eye)], axis=0)
    ca, sa = cs(la)
    w2 = np.block([[ca, sa], [-sa, ca]])
    cc, sc = cs(gd)
    chan = np.concatenate([cc, sc], axis=0)
    tw = 2.0 * np.pi * (np.arange(lb)[:, None] * np.arange(la)[None, :]) / l
    twc = np.repeat(np.cos(tw)[:, :, None], LANES, axis=2)
    tws = np.repeat(np.sin(tw)[:, :, None], LANES, axis=2)
    as_bf = lambda a: jnp.asarray(a, dtype=F32).astype(BF16)
    return as_bf(kw), as_bf(w2), as_bf(chan), jnp.asarray(twc, F32), jnp.asarray(tws, F32), la, lb


def _rope_tables(l):
    nf = RET_HEAD_DIM // 4
    inv = ROPE_BASE ** (-jnp.arange(nf, dtype=F32) / nf)
    rows = l // GRID_W
    r, cc = jnp.meshgrid(jnp.arange(rows, dtype=F32), jnp.arange(GRID_W, dtype=F32), indexing="ij")
    ang = jnp.concatenate([r.reshape(-1)[:, None] * inv, cc.reshape(-1)[:, None] * inv], axis=-1)
    cos, sin = jnp.cos(ang), jnp.sin(ang)
    cos_h = jnp.concatenate([cos, cos], axis=1)
    sin_h = jnp.concatenate([-sin, sin], axis=1)
    reps = LANES // RET_HEAD_DIM
    return jnp.tile(cos_h, (1, reps)), jnp.tile(sin_h, (1, reps))


def kernel(x, c, ctx, c_ctx, w_ada, b_ada, norm1_gain, w_in, four_w_out, ret_decay_logit,
           ret_gn_gain, ret_w_out, w_out, norm2_gain, w_mlp1, w_mlp2, final_gain):
    assert w_ada.shape[0] == 1, "single-layer block"
    b, l, d = x.shape
    lc = ctx.shape[1]
    fw = four_w_out.shape[1]
    rw = ret_w_out.shape[1]
    gd = fw // FOUR_GROUPS
    assert l % FFT_LA == 0 and l % RET_CHUNK == 0 and rw % GROUP_W == 0

    pad = (-(b + 1)) % 8
    cvec = jnp.concatenate([c, c_ctx[None, :], jnp.zeros((pad, d), F32)], axis=0)
    mod = _ada(cvec, w_ada[0], b_ada)
    mods_x = mod[:b].reshape(b, N_MOD, d)
    mod_c = mod[b:b + 1].reshape(1, N_MOD, d)

    kw, w2, chan, twc, tws, la, lb = _dft_tables(l, gd)
    cosf, sinf = _rope_tables(l)
    w_in_b = w_in[0].astype(BF16)

    u, q, kt, v, g, ga, gb = _proj(x, mods_x, norm1_gain, w_in_b, cosf, sinf, fw, rw)
    kct, vc = _ctxproj(ctx, mod_c, norm1_gain, w_in_b, fw, rw)

    tr, ti = _fft1(u, kw, twc, tws, la, lb)
    fm = _fft2(tr, ti, w2, chan, la, lb, gd, 1.0 / math.sqrt(l * gd))

    log_gamma = jax.nn.log_sigmoid(ret_decay_logit[0].astype(F32))
    tabs = _retention_tables(log_gamma, lc)
    z = _retention(q, kt, v, g, kct, vc, tabs, ret_gn_gain)

    return _out(x, mods_x, fm, z, ga, gb,
                four_w_out[0].astype(BF16), ret_w_out[0].astype(BF16), w_out[0].astype(BF16),
                norm2_gain, w_mlp1[0].astype(BF16), w_mlp2[0].astype(BF16), final_gain[None, :])
```

```python
import functools
import math

import jax
import jax.numpy as jnp
import numpy as np
from jax import lax
from jax.experimental import pallas as pl
from jax.experimental.pallas import tpu as pltpu

F32 = jnp.float32
BF16 = jnp.bfloat16

GRID_W = 64
FOUR_GROUPS = 4
RET_HEAD_DIM = 64
N_MOD = 6
ROPE_BASE = 10000.0
EPS = 1e-6

LANES = 128
MXU_DIM = 256
BF16_SUBLANES = 16
VMEM_LIMIT_BYTES = 56 * 1024 * 1024

RET_CHUNK = 128
RET_UNROLL = 32
HEADS_PER_GROUP = MXU_DIM // RET_HEAD_DIM
GROUP_W = HEADS_PER_GROUP * RET_HEAD_DIM
FFT_LA = 128
FFT_LA_TILE = BF16_SUBLANES
ADA_TN = 1024
TOKEN_TILE = 512


def _dot(a, b):
    return jnp.dot(a, b, preferred_element_type=F32)


def _norm_mod(x, gain, shift, scale):
    ms = jnp.mean(x * x, axis=-1, keepdims=True)
    y = x * lax.rsqrt(ms + EPS) * gain
    return y * (1.0 + scale) + shift


def _tile_lanes(t, reps):
    return jnp.concatenate([t] * reps, axis=1) if reps > 1 else t


def _const_spec(shape):
    nd = len(shape)
    return pl.BlockSpec(shape, lambda *_: (0,) * nd, pipeline_mode=pl.Buffered(1))


def _params(*sem):
    return pltpu.CompilerParams(dimension_semantics=sem, vmem_limit_bytes=VMEM_LIMIT_BYTES)


def _ada_kernel(c_ref, w_ref, b_ref, o_ref):
    c = c_ref[...]
    s = c * jax.nn.sigmoid(c)
    o_ref[...] = _dot(s.astype(BF16), w_ref[...].astype(BF16)) + b_ref[...]


def _ada(cvec, w_ada, b_ada):
    rows, d = cvec.shape
    n = w_ada.shape[1]
    return pl.pallas_call(
        _ada_kernel,
        grid=(n // ADA_TN,),
        in_specs=[
            pl.BlockSpec((rows, d), lambda j: (0, 0)),
            pl.BlockSpec((d, ADA_TN), lambda j: (0, j)),
            pl.BlockSpec((1, ADA_TN), lambda j: (0, j)),
        ],
        out_specs=pl.BlockSpec((rows, ADA_TN), lambda j: (0, j)),
        out_shape=jax.ShapeDtypeStruct((rows, n), F32),
        compiler_params=_params("arbitrary"),
        name="ada",
    )(cvec, w_ada, b_ada)


def _proj_kernel(x_ref, mod_ref, gain_ref, w_ref, cos_ref, sin_ref, gng_ref,
                 u_ref, q_ref, kt_ref, v_ref, sg_ref, ga_ref, gb_ref, *, fw, rw, d, scale):
    mod = mod_ref[0]
    h = _norm_mod(x_ref[0], gain_ref[...], mod[0:1], mod[1:2]).astype(BF16)
    tm = h.shape[0]

    def proj(lo, width):
        return _dot(h, w_ref[:, lo:lo + width])

    u_ref[0] = proj(0, fw).astype(BF16)

    reps = rw // cos_ref.shape[1]
    cosf = _tile_lanes(cos_ref[...], reps)
    sinf = _tile_lanes(sin_ref[...], reps)
    lane = lax.broadcasted_iota(jnp.int32, (tm, rw), 1)
    first_half = (lane & (RET_HEAD_DIM // 2)) == 0

    def rope(t):
        rot = jnp.where(first_half,
                        pltpu.roll(t, rw - RET_HEAD_DIM // 2, 1),
                        pltpu.roll(t, RET_HEAD_DIM // 2, 1))
        return t * cosf + rot * sinf

    q_ref[0] = rope(proj(fw, rw)).astype(BF16)
    kt_ref[0] = (rope(proj(fw + rw, rw)) * scale).T.astype(BF16)
    v_ref[0] = proj(fw + 2 * rw, rw).astype(BF16)
    g = proj(fw + 3 * rw, rw)
    sg_ref[0] = (g * jax.nn.sigmoid(g) * gng_ref[...]).astype(BF16)
    ga_ref[0] = proj(fw + 4 * rw, d).astype(BF16)
    gb_ref[0] = proj(fw + 4 * rw + d, d).astype(BF16)


def _proj(x, mods, gain, w_in, cosf, sinf, gn_gain, fw, rw):
    b, l, d = x.shape
    tm = min(TOKEN_TILE, l)
    tok = lambda width: pl.BlockSpec((1, tm, width), lambda i, j: (i, j, 0))
    out = lambda width: jax.ShapeDtypeStruct((b, l, width), BF16)
    return pl.pallas_call(
        functools.partial(_proj_kernel, fw=fw, rw=rw, d=d, scale=RET_HEAD_DIM ** -0.5),
        grid=(b, l // tm),
        in_specs=[
            tok(d),
            pl.BlockSpec((1, N_MOD, d), lambda i, j: (i, 0, 0)),
            _const_spec((1, d)),
            _const_spec(w_in.shape),
            pl.BlockSpec((tm, cosf.shape[1]), lambda i, j: (j, 0)),
            pl.BlockSpec((tm, sinf.shape[1]), lambda i, j: (j, 0)),
            _const_spec((1, rw)),
        ],
        out_specs=[tok(fw), tok(rw), pl.BlockSpec((1, rw, tm), lambda i, j: (i, 0, j)),
                   tok(rw), tok(rw), tok(d), tok(d)],
        out_shape=[out(fw), out(rw), jax.ShapeDtypeStruct((b, rw, l), BF16),
                   out(rw), out(rw), out(d), out(d)],
        compiler_params=_params("arbitrary", "arbitrary"),
        name="proj",
    )(x, mods, gain, w_in, cosf, sinf, gn_gain)


def _ctxproj_kernel(c_ref, mod_ref, gain_ref, w_ref, kt_ref, v_ref, *, rw, scale):
    mod = mod_ref[0]
    h = _norm_mod(c_ref[0], gain_ref[...], mod[0:1], mod[1:2]).astype(BF16)
    kt_ref[0] = (_dot(h, w_ref[:, :rw]) * scale).T.astype(BF16)
    v_ref[0] = _dot(h, w_ref[:, rw:]).astype(BF16)


def _ctxproj(ctx, mod_c, gain, w_in, fw, rw):
    b, lc, d = ctx.shape
    kv_block = (fw + rw) // (2 * rw)
    assert kv_block * 2 * rw == fw + rw
    return pl.pallas_call(
        functools.partial(_ctxproj_kernel, rw=rw, scale=RET_HEAD_DIM ** -0.5),
        grid=(b,),
        in_specs=[
            pl.BlockSpec((1, lc, d), lambda i: (i, 0, 0)),
            pl.BlockSpec((1, N_MOD, d), lambda i: (0, 0, 0)),
            pl.BlockSpec((1, d), lambda i: (0, 0)),
            pl.BlockSpec((d, 2 * rw), lambda i: (0, kv_block)),
        ],
        out_specs=[pl.BlockSpec((1, rw, lc), lambda i: (i, 0, 0)),
                   pl.BlockSpec((1, lc, rw), lambda i: (i, 0, 0))],
        out_shape=[jax.ShapeDtypeStruct((b, rw, lc), BF16), jax.ShapeDtypeStruct((b, lc, rw), BF16)],
        compiler_params=_params("arbitrary"),
        name="ctxproj",
    )(ctx, mod_c, gain, w_in)


def _fft_kernel(u_ref, kw_ref, twc_ref, tws_ref, w2_ref, cs_ref, o_ref, tr_ref, ti_ref, scr_ref,
                *, la, lb, gd, npb, scale):
    r = FFT_LA_TILE
    rows = lb * r
    fw = u_ref.shape[2]
    ngroups = fw // gd
    reps = fw // twc_ref.shape[2]

    for j in range(la // r):
        u = jnp.concatenate([u_ref[0, k * la + j * r:k * la + (j + 1) * r, :] for k in range(lb)], axis=0)
        t = _dot(kw_ref[...], u)
        a, b = t[:rows], t[rows:]
        ct = _tile_lanes(twc_ref[j], reps)
        st = _tile_lanes(tws_ref[j], reps)
        tr = (a * ct + b * st).astype(BF16)
        ti = (b * ct - a * st).astype(BF16)
        for k in range(lb):
            tr_ref[k * la + j * r:k * la + (j + 1) * r, :] = tr[k * r:(k + 1) * r]
            ti_ref[k * la + j * r:k * la + (j + 1) * r, :] = ti[k * r:(k + 1) * r]

    for pb in range(lb // npb):
        xs = []
        for p in range(npb):
            lo = (pb * npb + p) * la
            t = jnp.concatenate([tr_ref[lo:lo + la, :], ti_ref[lo:lo + la, :]], axis=0)
            xs.append(_dot(w2_ref[...], t).astype(BF16))
        for gi in range(ngroups):
            cols = slice(gi * gd, (gi + 1) * gd)
            lhs = jnp.concatenate([jnp.concatenate([x[:la, cols], x[la:, cols]], axis=1) for x in xs], axis=0)
            fg = _dot(lhs, cs_ref[...]) * scale
            for p in range(npb):
                scr_ref[gi, pl.ds(p, la, stride=npb), :] = fg[p * la:(p + 1) * la]
        full = jnp.concatenate([scr_ref[gi] for gi in range(ngroups)], axis=1)
        o_ref[0, :, pb * npb:(pb + 1) * npb, :] = full.reshape(la, npb, fw).astype(BF16)


def _fft(u, kw, twc, tws, w2, cs, la, lb, gd, scale):
    b, l, fw = u.shape
    npb = min(BF16_SUBLANES, lb)
    out = pl.pallas_call(
        functools.partial(_fft_kernel, la=la, lb=lb, gd=gd, npb=npb, scale=scale),
        grid=(b,),
        in_specs=[pl.BlockSpec((1, l, fw), lambda i: (i, 0, 0)),
                  _const_spec(kw.shape), _const_spec(twc.shape), _const_spec(tws.shape),
                  _const_spec(w2.shape), _const_spec(cs.shape)],
        out_specs=pl.BlockSpec((1, la, lb, fw), lambda i: (i, 0, 0, 0)),
        out_shape=jax.ShapeDtypeStruct((b, la, lb, fw), BF16),
        scratch_shapes=[pltpu.VMEM((l, fw), BF16), pltpu.VMEM((l, fw), BF16),
                        pltpu.VMEM((fw // gd, la * npb, gd), F32)],
        compiler_params=_params("arbitrary"),
        name="fft",
    )(u, kw, twc, tws, w2, cs)
    return out.reshape(b, l, fw)


def _ret_kernel(q_ref, kt_ref, v_ref, sg_ref, kct_ref, vc_ref, dall_ref, qdf_ref, qdb_ref,
                kdf_ref, kdb_ref, cdf_ref, cdb_ref, wcf_ref, wcb_ref,
                z_ref, sf_ref, sb_ref, stf_ref, stb_ref, p_ref, o_ref, *, nchunk, unroll):
    c = RET_CHUNK
    gw = GROUP_W
    same_head = (lax.broadcasted_iota(jnp.int32, (gw, gw), 0) // RET_HEAD_DIM
                 == lax.broadcasted_iota(jnp.int32, (gw, gw), 1) // RET_HEAD_DIM)

    def diag_blocks(t):
        return jnp.where(same_head, t, 0.0)

    def weighted(t, w):
        return (t.astype(F32) * w).astype(BF16)

    def rows(ref, n):
        return ref[0, pl.ds(pl.multiple_of(n * c, c), c), :]

    def kt_chunk(n):
        return kt_ref[0, :, pl.ds(pl.multiple_of(n * c, c), c)]

    kct = kct_ref[0]
    vc = vc_ref[0]
    stf_ref[...] = diag_blocks(_dot(kct, weighted(vc, wcf_ref[...])))
    stb_ref[...] = diag_blocks(_dot(kct, weighted(vc, wcb_ref[...])))

    def scan_body(i, carry):
        nf = i
        nb = nchunk - 1 - i
        s = stf_ref[...]
        sf_ref[nf] = s.astype(BF16)
        stf_ref[...] = s * cdf_ref[...] + diag_blocks(_dot(kt_chunk(nf), weighted(rows(v_ref, nf), kdf_ref[...])))
        s = stb_ref[...]
        sb_ref[nb] = s.astype(BF16)
        stb_ref[...] = s * cdb_ref[...] + diag_blocks(_dot(kt_chunk(nb), weighted(rows(v_ref, nb), kdb_ref[...])))
        return carry

    lax.fori_loop(0, nchunk, scan_body, 0, unroll=unroll)

    lane = lax.broadcasted_iota(jnp.int32, (c, gw), 1)
    sub = lax.broadcasted_iota(jnp.int32, (gw, c), 0)
    lane_masks = [(lane >= h * RET_HEAD_DIM) & (lane < (h + 1) * RET_HEAD_DIM)
                  for h in range(HEADS_PER_GROUP)]
    sub_masks = [(sub >= h * RET_HEAD_DIM) & (sub < (h + 1) * RET_HEAD_DIM)
                 for h in range(HEADS_PER_GROUP)]
    bd_mean = jnp.where(same_head, 1.0 / RET_HEAD_DIM, 0.0).astype(BF16)

    def score_body(n, carry):
        ktn = kt_chunk(n)
        zk = jnp.zeros_like(ktn)
        kbd = jnp.concatenate([jnp.where(m, ktn, zk) for m in sub_masks], axis=1)
        p_ref[n] = (_dot(rows(q_ref, n), kbd) * dall_ref[...]).astype(BF16)
        return carry

    lax.fori_loop(0, nchunk, score_body, 0, unroll=unroll)

    def mix_body(n, carry):
        qn = rows(q_ref, n)
        vn = rows(v_ref, n)
        zv = jnp.zeros_like(vn)
        vbd = jnp.concatenate([jnp.where(m, vn, zv) for m in lane_masks], axis=0)
        qfb = jnp.concatenate([weighted(qn, qdf_ref[...]), weighted(qn, qdb_ref[...])], axis=1)
        o_ref[n] = _dot(p_ref[n], vbd) + _dot(qfb, jnp.concatenate([sf_ref[n], sb_ref[n]], axis=0))
        return carry

    lax.fori_loop(0, nchunk, mix_body, 0, unroll=unroll)

    def norm_body(n, carry):
        o = o_ref[n]
        o2 = o * o
        hi = o2.astype(BF16)
        lo = (o2 - hi.astype(F32)).astype(BF16)
        ms = _dot(jnp.concatenate([hi, lo], axis=1), jnp.concatenate([bd_mean, bd_mean], axis=0))
        z = rows(sg_ref, n).astype(F32) * (o * lax.rsqrt(ms + EPS))
        z_ref[0, pl.ds(pl.multiple_of(n * c, c), c), :] = z.astype(BF16)
        return carry

    lax.fori_loop(0, nchunk, norm_body, 0, unroll=unroll)


def _retention(q, kt, v, sg, kct, vc, tabs):
    b, l, rw = q.shape
    lc = vc.shape[1]
    c = RET_CHUNK
    gw = GROUP_W
    ng = rw // gw
    nchunk = l // c
    tok = pl.BlockSpec((1, l, gw), lambda i, j: (i, 0, j))
    tokt = pl.BlockSpec((1, gw, l), lambda i, j: (i, j, 0))
    grp = lambda r, width: pl.BlockSpec((None, r, width), lambda i, j: (j, 0, 0))
    return pl.pallas_call(
        functools.partial(_ret_kernel, nchunk=nchunk, unroll=math.gcd(nchunk, RET_UNROLL)),
        grid=(b, ng),
        in_specs=[
            tok, tokt, tok, tok,
            pl.BlockSpec((1, gw, lc), lambda i, j: (i, j, 0)),
            pl.BlockSpec((1, lc, gw), lambda i, j: (i, 0, j)),
            grp(c, HEADS_PER_GROUP * c),
            grp(c, gw), grp(c, gw), grp(c, gw), grp(c, gw),
            grp(1, gw), grp(1, gw),
            grp(lc, gw), grp(lc, gw),
        ],
        out_specs=tok,
        out_shape=jax.ShapeDtypeStruct((b, l, rw), BF16),
        scratch_shapes=[pltpu.VMEM((nchunk, gw, gw), BF16), pltpu.VMEM((nchunk, gw, gw), BF16),
                        pltpu.VMEM((gw, gw), F32), pltpu.VMEM((gw, gw), F32),
                        pltpu.VMEM((nchunk, c, HEADS_PER_GROUP * c), BF16),
                        pltpu.VMEM((nchunk, c, gw), F32)],
        compiler_params=_params("arbitrary", "arbitrary"),
        name="ret",
    )(q, kt, v, sg, kct, vc, tabs["dall"], tabs["qdf"], tabs["qdb"], tabs["kdf"], tabs["kdb"],
      tabs["cdf"], tabs["cdb"], tabs["wcf"], tabs["wcb"])


def _retention_tables(log_gamma, lc):
    c = RET_CHUNK
    hpg = HEADS_PER_GROUP
    nh = log_gamma.shape[1]
    ng = nh // hpg
    lgf = log_gamma[0].reshape(ng, hpg)
    lgb = log_gamma[1].reshape(ng, hpg)
    pos = jnp.arange(c, dtype=F32)
    diff = pos[:, None] - pos[None, :]
    df = jnp.where(diff >= 0, jnp.exp(lgf[:, :, None, None] * jnp.maximum(diff, 0.0)), 0.0)
    db = jnp.where(diff <= 0, jnp.exp(lgb[:, :, None, None] * jnp.maximum(-diff, 0.0)), 0.0)
    dall = (df + db).transpose(0, 2, 1, 3).reshape(ng, c, hpg * c)

    def per_lane(lg, expo):
        t = jnp.exp(lg[:, None, :] * expo[None, :, None])
        return jnp.repeat(t, RET_HEAD_DIM, axis=2)

    cpos = jnp.arange(lc, dtype=F32)
    chunk_len = jnp.full((1,), float(c), F32)

    return dict(
        dall=dall,
        qdf=per_lane(lgf, pos + 1.0), qdb=per_lane(lgb, c - pos),
        kdf=per_lane(lgf, c - 1.0 - pos), kdb=per_lane(lgb, pos),
        cdf=per_lane(lgf, chunk_len), cdb=per_lane(lgb, chunk_len),
        wcf=per_lane(lgf, lc - 1.0 - cpos), wcb=per_lane(lgb, cpos),
    )


def _out_kernel(x_ref, mod_ref, fm_ref, z_ref, ga_ref, gb_ref, w4_ref, wr_ref, wo_ref,
                gain2_ref, w1_ref, w2_ref, fgain_ref, o_ref, *, ff_chunk):
    mod = mod_ref[0]
    g1, sh2, sc2, g2 = mod[2:3], mod[3:4], mod[4:5], mod[5:6]
    y_four = _dot(fm_ref[0], w4_ref[...])
    y_ret = _dot(z_ref[0], wr_ref[...])
    y = (jax.nn.sigmoid(ga_ref[0].astype(F32)) * y_four
         + jax.nn.sigmoid(gb_ref[0].astype(F32)) * y_ret)
    x1 = x_ref[0] + g1 * _dot(y.astype(BF16), wo_ref[...])
    h2 = _norm_mod(x1, gain2_ref[...], sh2, sc2).astype(BF16)
    dff = w1_ref.shape[1]
    acc = None
    for lo in range(0, dff, ff_chunk):
        hid = jnp.maximum(_dot(h2, w1_ref[:, lo:lo + ff_chunk]), 0.0)
        part = _dot((hid * hid).astype(BF16), w2_ref[lo:lo + ff_chunk, :])
        acc = part if acc is None else acc + part
    x2 = x1 + g2 * acc
    ms = jnp.mean(x2 * x2, axis=-1, keepdims=True)
    o_ref[0] = x2 * lax.rsqrt(ms + EPS) * fgain_ref[...]


def _out(x, mods, fm, z, ga, gb, w4, wr, wo, gain2, w1, w2, fgain):
    b, l, d = x.shape
    tm = min(TOKEN_TILE, l)
    tok = lambda width: pl.BlockSpec((1, tm, width), lambda i, j: (i, j, 0))
    return pl.pallas_call(
        functools.partial(_out_kernel, ff_chunk=min(1024, w1.shape[1])),
        grid=(b, l // tm),
        in_specs=[
            tok(d),
            pl.BlockSpec((1, N_MOD, d), lambda i, j: (i, 0, 0)),
            tok(fm.shape[2]), tok(z.shape[2]), tok(d), tok(d),
            _const_spec(w4.shape), _const_spec(wr.shape), _const_spec(wo.shape),
            _const_spec((1, d)), _const_spec(w1.shape), _const_spec(w2.shape), _const_spec((1, d)),
        ],
        out_specs=tok(d),
        out_shape=jax.ShapeDtypeStruct((b, l, d), F32),
        compiler_params=_params("arbitrary", "arbitrary"),
        name="out",
    )(x, mods, fm, z, ga, gb, w4, wr, wo, gain2, w1, w2, fgain)


def _dft_tables(l, gd):
    la = FFT_LA
    lb = l // la

    def cs(n):
        idx = np.arange(n)
        ang = 2.0 * np.pi * ((idx[:, None] * idx[None, :]) % n) / n
        return np.cos(ang), np.sin(ang)

    cb, sb = cs(lb)
    eye = np.eye(FFT_LA_TILE)
    kw = np.concatenate([np.kron(cb, eye), -np.kron(sb, eye)], axis=0)
    ca, sa = cs(la)
    w2 = np.block([[ca, sa], [-sa, ca]])
    cc, sc = cs(gd)
    chan = np.concatenate([cc, sc], axis=0)
    tw = 2.0 * np.pi * (np.arange(lb)[:, None] * np.arange(la)[None, :]) / l
    tw = tw.reshape(lb, la // FFT_LA_TILE, FFT_LA_TILE).transpose(1, 0, 2).reshape(la // FFT_LA_TILE, -1)
    twc = np.repeat(np.cos(tw)[:, :, None], LANES, axis=2)
    tws = np.repeat(np.sin(tw)[:, :, None], LANES, axis=2)
    as_bf = lambda a: jnp.asarray(a, dtype=F32).astype(BF16)
    return as_bf(kw), as_bf(w2), as_bf(chan), jnp.asarray(twc, F32), jnp.asarray(tws, F32), la, lb


def _rope_tables(l):
    nf = RET_HEAD_DIM // 4
    inv = ROPE_BASE ** (-jnp.arange(nf, dtype=F32) / nf)
    rows = l // GRID_W
    r, cc = jnp.meshgrid(jnp.arange(rows, dtype=F32), jnp.arange(GRID_W, dtype=F32), indexing="ij")
    ang = jnp.concatenate([r.reshape(-1)[:, None] * inv, cc.reshape(-1)[:, None] * inv], axis=-1)
    cos, sin = jnp.cos(ang), jnp.sin(ang)
    cos_h = jnp.concatenate([cos, cos], axis=1)
    sin_h = jnp.concatenate([-sin, sin], axis=1)
    reps = LANES // RET_HEAD_DIM
    return jnp.tile(cos_h, (1, reps)), jnp.tile(sin_h, (1, reps))


def kernel(x, c, ctx, c_ctx, w_ada, b_ada, norm1_gain, w_in, four_w_out, ret_decay_logit,
           ret_gn_gain, ret_w_out, w_out, norm2_gain, w_mlp1, w_mlp2, final_gain):
    assert w_ada.shape[0] == 1, "single-layer block"
    b, l, d = x.shape
    lc = ctx.shape[1]
    fw = four_w_out.shape[1]
    rw = ret_w_out.shape[1]
    gd = fw // FOUR_GROUPS
    assert l % FFT_LA == 0 and l % RET_CHUNK == 0 and rw % GROUP_W == 0

    pad = (-(b + 1)) % 8
    cvec = jnp.concatenate([c, c_ctx[None, :], jnp.zeros((pad, d), F32)], axis=0)
    mod = _ada(cvec, w_ada[0], b_ada)
    mods_x = mod[:b].reshape(b, N_MOD, d)
    mod_c = mod[b:b + 1].reshape(1, N_MOD, d)

    kw, w2, chan, twc, tws, la, lb = _dft_tables(l, gd)
    cosf, sinf = _rope_tables(l)
    w_in_b = w_in[0].astype(BF16)

    u, q, kt, v, sg, ga, gb = _proj(x, mods_x, norm1_gain, w_in_b, cosf, sinf, ret_gn_gain, fw, rw)
    kct, vc = _ctxproj(ctx, mod_c, norm1_gain, w_in_b, fw, rw)

    fm = _fft(u, kw, twc, tws, w2, chan, la, lb, gd, 1.0 / math.sqrt(l * gd))

    log_gamma = jax.nn.log_sigmoid(ret_decay_logit[0].astype(F32))
    tabs = _retention_tables(log_gamma, lc)
    z = _retention(q, kt, v, sg, kct, vc, tabs)

    return _out(x, mods_x, fm, z, ga, gb,
                four_w_out[0].astype(BF16), ret_w_out[0].astype(BF16), w_out[0].astype(BF16),
                norm2_gain, w_mlp1[0].astype(BF16), w_mlp2[0].astype(BF16), final_gain[None, :])
```

```python
import functools
import math

import jax
import jax.numpy as jnp
import numpy as np
from jax import lax
from jax.experimental import pallas as pl
from jax.experimental.pallas import tpu as pltpu

F32 = jnp.float32
BF16 = jnp.bfloat16

GRID_W = 64
FOUR_GROUPS = 4
RET_HEAD_DIM = 64
N_MOD = 6
ROPE_BASE = 10000.0
EPS = 1e-6

LANES = 128
MXU_DIM = 256
BF16_SUBLANES = 16
VMEM_LIMIT_BYTES = 56 * 1024 * 1024

RET_CHUNK = 128
RET_UNROLL = 32
HEADS_PER_GROUP = MXU_DIM // RET_HEAD_DIM
GROUP_W = HEADS_PER_GROUP * RET_HEAD_DIM
FFT_LA = 128
FFT_LA_TILE = BF16_SUBLANES
ADA_TN = 1024
TOKEN_TILE = 512
WEIGHT_STAGE_CHUNK = (512, 512)


def _dot(a, b):
    return jnp.dot(a, b, preferred_element_type=F32)


def _norm_mod(x, gain, shift, scale):
    ms = jnp.mean(x * x, axis=-1, keepdims=True)
    y = x * lax.rsqrt(ms + EPS) * gain
    return y * (1.0 + scale) + shift


def _tile_lanes(t, reps):
    return jnp.concatenate([t] * reps, axis=1) if reps > 1 else t


def _const_spec(shape):
    nd = len(shape)
    return pl.BlockSpec(shape, lambda *_: (0,) * nd, pipeline_mode=pl.Buffered(1))


def _params(*sem):
    return pltpu.CompilerParams(dimension_semantics=sem, vmem_limit_bytes=VMEM_LIMIT_BYTES)


def _load_weights_as_bf16(pairs, stage_ref, sem_ref):
    _, sr, sc = stage_ref.shape
    chunks = [(src, dst, r0, c0)
              for src, dst in pairs
              for r0 in range(0, src.shape[0], sr)
              for c0 in range(0, src.shape[1], sc)]

    def copy(k):
        src, _, r0, c0 = chunks[k]
        return pltpu.make_async_copy(src.at[pl.ds(r0, sr), pl.ds(c0, sc)], stage_ref.at[k % 2], sem_ref.at[k % 2])

    copy(0).start()
    for k, (_, dst, r0, c0) in enumerate(chunks):
        if k + 1 < len(chunks):
            copy(k + 1).start()
        copy(k).wait()
        dst[r0:r0 + sr, c0:c0 + sc] = stage_ref[k % 2].astype(BF16)


def _is_first_step(grid_rank):
    first = pl.program_id(0) == 0
    for axis in range(1, grid_rank):
        first = first & (pl.program_id(axis) == 0)
    return first


def _ada_kernel(c_ref, w_ref, b_ref, o_ref):
    c = c_ref[...]
    s = c * jax.nn.sigmoid(c)
    o_ref[...] = _dot(s.astype(BF16), w_ref[...].astype(BF16)) + b_ref[...]


def _ada(cvec, w_ada, b_ada):
    rows, d = cvec.shape
    n = w_ada.shape[1]
    return pl.pallas_call(
        _ada_kernel,
        grid=(n // ADA_TN,),
        in_specs=[
            pl.BlockSpec((rows, d), lambda j: (0, 0)),
            pl.BlockSpec((d, ADA_TN), lambda j: (0, j)),
            pl.BlockSpec((1, ADA_TN), lambda j: (0, j)),
        ],
        out_specs=pl.BlockSpec((rows, ADA_TN), lambda j: (0, j)),
        out_shape=jax.ShapeDtypeStruct((rows, n), F32),
        compiler_params=_params("arbitrary"),
        name="ada",
    )(cvec, w_ada, b_ada)


def _proj_kernel(x_ref, mod_ref, gain_ref, cos_ref, sin_ref, gng_ref, w_hbm,
                 u_ref, q_ref, kt_ref, v_ref, sg_ref, ga_ref, gb_ref,
                 w_ref, stage_ref, sem_ref, *, fw, rw, d, scale):
    @pl.when(_is_first_step(2))
    def _():
        _load_weights_as_bf16([(w_hbm, w_ref)], stage_ref, sem_ref)

    mod = mod_ref[0]
    h = _norm_mod(x_ref[0], gain_ref[...], mod[0:1], mod[1:2]).astype(BF16)
    tm = h.shape[0]

    def proj(lo, width):
        return _dot(h, w_ref[:, lo:lo + width])

    u_ref[0] = proj(0, fw).astype(BF16)

    reps = rw // cos_ref.shape[1]
    cosf = _tile_lanes(cos_ref[...], reps)
    sinf = _tile_lanes(sin_ref[...], reps)
    lane = lax.broadcasted_iota(jnp.int32, (tm, rw), 1)
    first_half = (lane & (RET_HEAD_DIM // 2)) == 0

    def rope(t):
        rot = jnp.where(first_half,
                        pltpu.roll(t, rw - RET_HEAD_DIM // 2, 1),
                        pltpu.roll(t, RET_HEAD_DIM // 2, 1))
        return t * cosf + rot * sinf

    q_ref[0] = rope(proj(fw, rw)).astype(BF16)
    kt_ref[0] = (rope(proj(fw + rw, rw)) * scale).T.astype(BF16)
    v_ref[0] = proj(fw + 2 * rw, rw).astype(BF16)
    g = proj(fw + 3 * rw, rw)
    sg_ref[0] = (g * jax.nn.sigmoid(g) * gng_ref[...]).astype(BF16)
    ga_ref[0] = proj(fw + 4 * rw, d).astype(BF16)
    gb_ref[0] = proj(fw + 4 * rw + d, d).astype(BF16)


def _proj(x, mods, gain, w_in, cosf, sinf, gn_gain, fw, rw):
    b, l, d = x.shape
    tm = min(TOKEN_TILE, l)
    tok = lambda width: pl.BlockSpec((1, tm, width), lambda i, j: (i, j, 0))
    out = lambda width: jax.ShapeDtypeStruct((b, l, width), BF16)
    return pl.pallas_call(
        functools.partial(_proj_kernel, fw=fw, rw=rw, d=d, scale=RET_HEAD_DIM ** -0.5),
        grid=(b, l // tm),
        in_specs=[
            tok(d),
            pl.BlockSpec((1, N_MOD, d), lambda i, j: (i, 0, 0)),
            _const_spec((1, d)),
            pl.BlockSpec((tm, cosf.shape[1]), lambda i, j: (j, 0)),
            pl.BlockSpec((tm, sinf.shape[1]), lambda i, j: (j, 0)),
            _const_spec((1, rw)),
            pl.BlockSpec(memory_space=pl.ANY),
        ],
        out_specs=[tok(fw), tok(rw), pl.BlockSpec((1, rw, tm), lambda i, j: (i, 0, j)),
                   tok(rw), tok(rw), tok(d), tok(d)],
        out_shape=[out(fw), out(rw), jax.ShapeDtypeStruct((b, rw, l), BF16),
                   out(rw), out(rw), out(d), out(d)],
        scratch_shapes=[pltpu.VMEM(w_in.shape, BF16),
                        pltpu.VMEM((2,) + WEIGHT_STAGE_CHUNK, F32), pltpu.SemaphoreType.DMA((2,))],
        compiler_params=_params("arbitrary", "arbitrary"),
        name="proj",
    )(x, mods, gain, cosf, sinf, gn_gain, w_in)


def _ctxproj_kernel(c_ref, mod_ref, gain_ref, w_ref, kt_ref, v_ref, *, rw, scale):
    mod = mod_ref[0]
    h = _norm_mod(c_ref[0], gain_ref[...], mod[0:1], mod[1:2]).astype(BF16)
    kt_ref[0] = (_dot(h, w_ref[:, :rw].astype(BF16)) * scale).T.astype(BF16)
    v_ref[0] = _dot(h, w_ref[:, rw:].astype(BF16)).astype(BF16)


def _ctxproj(ctx, mod_c, gain, w_in, fw, rw):
    b, lc, d = ctx.shape
    kv_block = (fw + rw) // (2 * rw)
    assert kv_block * 2 * rw == fw + rw
    return pl.pallas_call(
        functools.partial(_ctxproj_kernel, rw=rw, scale=RET_HEAD_DIM ** -0.5),
        grid=(b,),
        in_specs=[
            pl.BlockSpec((1, lc, d), lambda i: (i, 0, 0)),
            pl.BlockSpec((1, N_MOD, d), lambda i: (0, 0, 0)),
            pl.BlockSpec((1, d), lambda i: (0, 0)),
            pl.BlockSpec((d, 2 * rw), lambda i: (0, kv_block)),
        ],
        out_specs=[pl.BlockSpec((1, rw, lc), lambda i: (i, 0, 0)),
                   pl.BlockSpec((1, lc, rw), lambda i: (i, 0, 0))],
        out_shape=[jax.ShapeDtypeStruct((b, rw, lc), BF16), jax.ShapeDtypeStruct((b, lc, rw), BF16)],
        compiler_params=_params("arbitrary"),
        name="ctxproj",
    )(ctx, mod_c, gain, w_in)


def _fft_kernel(u_ref, kw_ref, twc_ref, tws_ref, w2_ref, cs_ref, o_ref, tr_ref, ti_ref, scr_ref,
                *, la, lb, gd, npb, scale):
    r = FFT_LA_TILE
    rows = lb * r
    fw = u_ref.shape[2]
    ngroups = fw // gd
    reps = fw // twc_ref.shape[2]

    for j in range(la // r):
        u = jnp.concatenate([u_ref[0, k * la + j * r:k * la + (j + 1) * r, :] for k in range(lb)], axis=0)
        t = _dot(kw_ref[...], u)
        a, b = t[:rows], t[rows:]
        ct = _tile_lanes(twc_ref[j], reps)
        st = _tile_lanes(tws_ref[j], reps)
        tr = (a * ct + b * st).astype(BF16)
        ti = (b * ct - a * st).astype(BF16)
        for k in range(lb):
            tr_ref[k * la + j * r:k * la + (j + 1) * r, :] = tr[k * r:(k + 1) * r]
            ti_ref[k * la + j * r:k * la + (j + 1) * r, :] = ti[k * r:(k + 1) * r]

    for pb in range(lb // npb):
        xs = []
        for p in range(npb):
            lo = (pb * npb + p) * la
            t = jnp.concatenate([tr_ref[lo:lo + la, :], ti_ref[lo:lo + la, :]], axis=0)
            xs.append(_dot(w2_ref[...], t).astype(BF16))
        for gi in range(ngroups):
            cols = slice(gi * gd, (gi + 1) * gd)
            lhs = jnp.concatenate([jnp.concatenate([x[:la, cols], x[la:, cols]], axis=1) for x in xs], axis=0)
            fg = _dot(lhs, cs_ref[...]) * scale
            for p in range(npb):
                scr_ref[gi, pl.ds(p, la, stride=npb), :] = fg[p * la:(p + 1) * la]
        full = jnp.concatenate([scr_ref[gi] for gi in range(ngroups)], axis=1)
        o_ref[0, :, pb * npb:(pb + 1) * npb, :] = full.reshape(la, npb, fw).astype(BF16)


def _fft(u, kw, twc, tws, w2, cs, la, lb, gd, scale):
    b, l, fw = u.shape
    npb = min(BF16_SUBLANES, lb)
    out = pl.pallas_call(
        functools.partial(_fft_kernel, la=la, lb=lb, gd=gd, npb=npb, scale=scale),
        grid=(b,),
        in_specs=[pl.BlockSpec((1, l, fw), lambda i: (i, 0, 0)),
                  _const_spec(kw.shape), _const_spec(twc.shape), _const_spec(tws.shape),
                  _const_spec(w2.shape), _const_spec(cs.shape)],
        out_specs=pl.BlockSpec((1, la, lb, fw), lambda i: (i, 0, 0, 0)),
        out_shape=jax.ShapeDtypeStruct((b, la, lb, fw), BF16),
        scratch_shapes=[pltpu.VMEM((l, fw), BF16), pltpu.VMEM((l, fw), BF16),
                        pltpu.VMEM((fw // gd, la * npb, gd), F32)],
        compiler_params=_params("arbitrary"),
        name="fft",
    )(u, kw, twc, tws, w2, cs)
    return out.reshape(b, l, fw)


def _ret_kernel(q_ref, kt_ref, v_ref, sg_ref, kct_ref, vc_ref, dall_ref, qdf_ref, qdb_ref,
                kdf_ref, kdb_ref, cdf_ref, cdb_ref, wcf_ref, wcb_ref,
                z_ref, sf_ref, sb_ref, stf_ref, stb_ref, p_ref, o_ref, *, nchunk, unroll):
    c = RET_CHUNK
    gw = GROUP_W
    same_head = (lax.broadcasted_iota(jnp.int32, (gw, gw), 0) // RET_HEAD_DIM
                 == lax.broadcasted_iota(jnp.int32, (gw, gw), 1) // RET_HEAD_DIM)

    def diag_blocks(t):
        return jnp.where(same_head, t, 0.0)

    def weighted(t, w):
        return (t.astype(F32) * w).astype(BF16)

    def rows(ref, n):
        return ref[0, pl.ds(pl.multiple_of(n * c, c), c), :]

    def kt_chunk(n):
        return kt_ref[0, :, pl.ds(pl.multiple_of(n * c, c), c)]

    kct = kct_ref[0]
    vc = vc_ref[0]
    stf_ref[...] = diag_blocks(_dot(kct, weighted(vc, wcf_ref[...])))
    stb_ref[...] = diag_blocks(_dot(kct, weighted(vc, wcb_ref[...])))

    def scan_body(i, carry):
        nf = i
        nb = nchunk - 1 - i
        s = stf_ref[...]
        sf_ref[nf] = s.astype(BF16)
        stf_ref[...] = s * cdf_ref[...] + diag_blocks(_dot(kt_chunk(nf), weighted(rows(v_ref, nf), kdf_ref[...])))
        s = stb_ref[...]
        sb_ref[nb] = s.astype(BF16)
        stb_ref[...] = s * cdb_ref[...] + diag_blocks(_dot(kt_chunk(nb), weighted(rows(v_ref, nb), kdb_ref[...])))
        return carry

    lax.fori_loop(0, nchunk, scan_body, 0, unroll=unroll)

    lane = lax.broadcasted_iota(jnp.int32, (c, gw), 1)
    sub = lax.broadcasted_iota(jnp.int32, (gw, c), 0)
    lane_masks = [(lane >= h * RET_HEAD_DIM) & (lane < (h + 1) * RET_HEAD_DIM)
                  for h in range(HEADS_PER_GROUP)]
    sub_masks = [(sub >= h * RET_HEAD_DIM) & (sub < (h + 1) * RET_HEAD_DIM)
                 for h in range(HEADS_PER_GROUP)]
    bd_mean = jnp.where(same_head, 1.0 / RET_HEAD_DIM, 0.0).astype(BF16)

    def score_body(n, carry):
        ktn = kt_chunk(n)
        zk = jnp.zeros_like(ktn)
        kbd = jnp.concatenate([jnp.where(m, ktn, zk) for m in sub_masks], axis=1)
        p_ref[n] = (_dot(rows(q_ref, n), kbd) * dall_ref[...]).astype(BF16)
        return carry

    lax.fori_loop(0, nchunk, score_body, 0, unroll=unroll)

    def mix_body(n, carry):
        qn = rows(q_ref, n)
        vn = rows(v_ref, n)
        zv = jnp.zeros_like(vn)
        vbd = jnp.concatenate([jnp.where(m, vn, zv) for m in lane_masks], axis=0)
        qfb = jnp.concatenate([weighted(qn, qdf_ref[...]), weighted(qn, qdb_ref[...])], axis=1)
        o_ref[n] = _dot(p_ref[n], vbd) + _dot(qfb, jnp.concatenate([sf_ref[n], sb_ref[n]], axis=0))
        return carry

    lax.fori_loop(0, nchunk, mix_body, 0, unroll=unroll)

    def norm_body(n, carry):
        o = o_ref[n]
        o2 = o * o
        hi = o2.astype(BF16)
        lo = (o2 - hi.astype(F32)).astype(BF16)
        ms = _dot(jnp.concatenate([hi, lo], axis=1), jnp.concatenate([bd_mean, bd_mean], axis=0))
        z = rows(sg_ref, n).astype(F32) * (o * lax.rsqrt(ms + EPS))
        z_ref[0, pl.ds(pl.multiple_of(n * c, c), c), :] = z.astype(BF16)
        return carry

    lax.fori_loop(0, nchunk, norm_body, 0, unroll=unroll)


def _retention(q, kt, v, sg, kct, vc, tabs):
    b, l, rw = q.shape
    lc = vc.shape[1]
    c = RET_CHUNK
    gw = GROUP_W
    ng = rw // gw
    nchunk = l // c
    tok = pl.BlockSpec((1, l, gw), lambda i, j: (i, 0, j))
    tokt = pl.BlockSpec((1, gw, l), lambda i, j: (i, j, 0))
    grp = lambda r, width: pl.BlockSpec((None, r, width), lambda i, j: (j, 0, 0))
    return pl.pallas_call(
        functools.partial(_ret_kernel, nchunk=nchunk, unroll=math.gcd(nchunk, RET_UNROLL)),
        grid=(b, ng),
        in_specs=[
            tok, tokt, tok, tok,
            pl.BlockSpec((1, gw, lc), lambda i, j: (i, j, 0)),
            pl.BlockSpec((1, lc, gw), lambda i, j: (i, 0, j)),
            grp(c, HEADS_PER_GROUP * c),
            grp(c, gw), grp(c, gw), grp(c, gw), grp(c, gw),
            grp(1, gw), grp(1, gw),
            grp(lc, gw), grp(lc, gw),
        ],
        out_specs=tok,
        out_shape=jax.ShapeDtypeStruct((b, l, rw), BF16),
        scratch_shapes=[pltpu.VMEM((nchunk, gw, gw), BF16), pltpu.VMEM((nchunk, gw, gw), BF16),
                        pltpu.VMEM((gw, gw), F32), pltpu.VMEM((gw, gw), F32),
                        pltpu.VMEM((nchunk, c, HEADS_PER_GROUP * c), BF16),
                        pltpu.VMEM((nchunk, c, gw), F32)],
        compiler_params=_params("arbitrary", "arbitrary"),
        name="ret",
    )(q, kt, v, sg, kct, vc, tabs["dall"], tabs["qdf"], tabs["qdb"], tabs["kdf"], tabs["kdb"],
      tabs["cdf"], tabs["cdb"], tabs["wcf"], tabs["wcb"])


def _retention_tables(log_gamma, lc):
    c = RET_CHUNK
    hpg = HEADS_PER_GROUP
    nh = log_gamma.shape[1]
    ng = nh // hpg
    lgf = log_gamma[0].reshape(ng, hpg)
    lgb = log_gamma[1].reshape(ng, hpg)
    pos = jnp.arange(c, dtype=F32)
    diff = pos[:, None] - pos[None, :]
    df = jnp.where(diff >= 0, jnp.exp(lgf[:, :, None, None] * jnp.maximum(diff, 0.0)), 0.0)
    db = jnp.where(diff <= 0, jnp.exp(lgb[:, :, None, None] * jnp.maximum(-diff, 0.0)), 0.0)
    dall = (df + db).transpose(0, 2, 1, 3).reshape(ng, c, hpg * c)

    def per_lane(lg, expo):
        t = jnp.exp(lg[:, None, :] * expo[None, :, None])
        return jnp.repeat(t, RET_HEAD_DIM, axis=2)

    cpos = jnp.arange(lc, dtype=F32)
    chunk_len = jnp.full((1,), float(c), F32)

    return dict(
        dall=dall,
        qdf=per_lane(lgf, pos + 1.0), qdb=per_lane(lgb, c - pos),
        kdf=per_lane(lgf, c - 1.0 - pos), kdb=per_lane(lgb, pos),
        cdf=per_lane(lgf, chunk_len), cdb=per_lane(lgb, chunk_len),
        wcf=per_lane(lgf, lc - 1.0 - cpos), wcb=per_lane(lgb, cpos),
    )


def _out_kernel(x_ref, mod_ref, fm_ref, z_ref, ga_ref, gb_ref, gain2_ref, fgain_ref,
                w4_hbm, wr_hbm, wo_hbm, w1_hbm, w2_hbm, o_ref,
                w4_ref, wr_ref, wo_ref, w1_ref, w2_ref, stage_ref, sem_ref, *, ff_chunk):
    @pl.when(_is_first_step(2))
    def _():
        _load_weights_as_bf16([(w4_hbm, w4_ref), (wr_hbm, wr_ref), (wo_hbm, wo_ref),
                               (w1_hbm, w1_ref), (w2_hbm, w2_ref)], stage_ref, sem_ref)

    mod = mod_ref[0]
    g1, sh2, sc2, g2 = mod[2:3], mod[3:4], mod[4:5], mod[5:6]
    y_four = _dot(fm_ref[0], w4_ref[...])
    y_ret = _dot(z_ref[0], wr_ref[...])
    y = (jax.nn.sigmoid(ga_ref[0].astype(F32)) * y_four
         + jax.nn.sigmoid(gb_ref[0].astype(F32)) * y_ret)
    x1 = x_ref[0] + g1 * _dot(y.astype(BF16), wo_ref[...])
    h2 = _norm_mod(x1, gain2_ref[...], sh2, sc2).astype(BF16)
    dff = w1_ref.shape[1]
    acc = None
    for lo in range(0, dff, ff_chunk):
        hid = jnp.maximum(_dot(h2, w1_ref[:, lo:lo + ff_chunk]), 0.0)
        part = _dot((hid * hid).astype(BF16), w2_ref[lo:lo + ff_chunk, :])
        acc = part if acc is None else acc + part
    x2 = x1 + g2 * acc
    ms = jnp.mean(x2 * x2, axis=-1, keepdims=True)
    o_ref[0] = x2 * lax.rsqrt(ms + EPS) * fgain_ref[...]


def _out(x, mods, fm, z, ga, gb, w4, wr, wo, gain2, w1, w2, fgain):
    b, l, d = x.shape
    tm = min(TOKEN_TILE, l)
    tok = lambda width: pl.BlockSpec((1, tm, width), lambda i, j: (i, j, 0))
    weights = (w4, wr, wo, w1, w2)
    for w in weights:
        assert w.shape[0] % WEIGHT_STAGE_CHUNK[0] == 0 and w.shape[1] % WEIGHT_STAGE_CHUNK[1] == 0
    return pl.pallas_call(
        functools.partial(_out_kernel, ff_chunk=min(1024, w1.shape[1])),
        grid=(b, l // tm),
        in_specs=[
            tok(d),
            pl.BlockSpec((1, N_MOD, d), lambda i, j: (i, 0, 0)),
            tok(fm.shape[2]), tok(z.shape[2]), tok(d), tok(d),
            _const_spec((1, d)), _const_spec((1, d)),
        ] + [pl.BlockSpec(memory_space=pl.ANY)] * len(weights),
        out_specs=tok(d),
        out_shape=jax.ShapeDtypeStruct((b, l, d), F32),
        scratch_shapes=[pltpu.VMEM(w.shape, BF16) for w in weights]
        + [pltpu.VMEM((2,) + WEIGHT_STAGE_CHUNK, F32), pltpu.SemaphoreType.DMA((2,))],
        compiler_params=_params("arbitrary", "arbitrary"),
        name="out",
    )(x, mods, fm, z, ga, gb, gain2, fgain, *weights)


def _dft_tables(l, gd):
    la = FFT_LA
    lb = l // la

    def cs(n):
        idx = np.arange(n)
        ang = 2.0 * np.pi * ((idx[:, None] * idx[None, :]) % n) / n
        return np.cos(ang), np.sin(ang)

    cb, sb = cs(lb)
    eye = np.eye(FFT_LA_TILE)
    kw = np.concatenate([np.kron(cb, eye), -np.kron(sb, eye)], axis=0)
    ca, sa = cs(la)
    w2 = np.block([[ca, sa], [-sa, ca]])
    cc, sc = cs(gd)
    chan = np.concatenate([cc, sc], axis=0)
    tw = 2.0 * np.pi * (np.arange(lb)[:, None] * np.arange(la)[None, :]) / l
    tw = tw.reshape(lb, la // FFT_LA_TILE, FFT_LA_TILE).transpose(1, 0, 2).reshape(la // FFT_LA_TILE, -1)
    twc = np.repeat(np.cos(tw)[:, :, None], LANES, axis=2)
    tws = np.repeat(np.sin(tw)[:, :, None], LANES, axis=2)
    as_bf = lambda a: jnp.asarray(a, dtype=F32).astype(BF16)
    return as_bf(kw), as_bf(w2), as_bf(chan), jnp.asarray(twc, F32), jnp.asarray(tws, F32), la, lb


def _rope_tables(l):
    nf = RET_HEAD_DIM // 4
    inv = ROPE_BASE ** (-jnp.arange(nf, dtype=F32) / nf)
    rows = l // GRID_W
    r, cc = jnp.meshgrid(jnp.arange(rows, dtype=F32), jnp.arange(GRID_W, dtype=F32), indexing="ij")
    ang = jnp.concatenate([r.reshape(-1)[:, None] * inv, cc.reshape(-1)[:, None] * inv], axis=-1)
    cos, sin = jnp.cos(ang), jnp.sin(ang)
    cos_h = jnp.concatenate([cos, cos], axis=1)
    sin_h = jnp.concatenate([-sin, sin], axis=1)
    reps = LANES // RET_HEAD_DIM
    return jnp.tile(cos_h, (1, reps)), jnp.tile(sin_h, (1, reps))


def kernel(x, c, ctx, c_ctx, w_ada, b_ada, norm1_gain, w_in, four_w_out, ret_decay_logit,
           ret_gn_gain, ret_w_out, w_out, norm2_gain, w_mlp1, w_mlp2, final_gain):
    assert w_ada.shape[0] == 1, "single-layer block"
    b, l, d = x.shape
    lc = ctx.shape[1]
    fw = four_w_out.shape[1]
    rw = ret_w_out.shape[1]
    gd = fw // FOUR_GROUPS
    assert l % FFT_LA == 0 and l % RET_CHUNK == 0 and rw % GROUP_W == 0

    pad = (-(b + 1)) % 8
    cvec = jnp.concatenate([c, c_ctx[None, :], jnp.zeros((pad, d), F32)], axis=0)
    mod = _ada(cvec, w_ada[0], b_ada)
    mods_x = mod[:b].reshape(b, N_MOD, d)
    mod_c = mod[b:b + 1].reshape(1, N_MOD, d)

    kw, w2, chan, twc, tws, la, lb = _dft_tables(l, gd)
    cosf, sinf = _rope_tables(l)

    u, q, kt, v, sg, ga, gb = _proj(x, mods_x, norm1_gain, w_in[0], cosf, sinf, ret_gn_gain, fw, rw)
    kct, vc = _ctxproj(ctx, mod_c, norm1_gain, w_in[0], fw, rw)

    fm = _fft(u, kw, twc, tws, w2, chan, la, lb, gd, 1.0 / math.sqrt(l * gd))

    log_gamma = jax.nn.log_sigmoid(ret_decay_logit[0].astype(F32))
    tabs = _retention_tables(log_gamma, lc)
    z = _retention(q, kt, v, sg, kct, vc, tabs)

    return _out(x, mods_x, fm, z, ga, gb, four_w_out[0], ret_w_out[0], w_out[0],
                norm2_gain, w_mlp1[0], w_mlp2[0], final_gain[None, :])
```

```python
import functools
import math

import jax
import jax.numpy as jnp
import numpy as np
from jax import lax
from jax.experimental import pallas as pl
from jax.experimental.pallas import tpu as pltpu

F32 = jnp.float32
BF16 = jnp.bfloat16

GRID_W = 64
FOUR_GROUPS = 4
RET_HEAD_DIM = 64
N_MOD = 6
ROPE_BASE = 10000.0
EPS = 1e-6

LANES = 128
MXU_DIM = 256
BF16_SUBLANES = 16
VMEM_LIMIT_BYTES = 56 * 1024 * 1024

RET_CHUNK = 128
RET_UNROLL = 32
HEADS_PER_GROUP = MXU_DIM // RET_HEAD_DIM
GROUP_W = HEADS_PER_GROUP * RET_HEAD_DIM
FFT_LA = 128
FFT_LA_TILE = BF16_SUBLANES
ADA_TN = 1024
TOKEN_TILE = 512
WEIGHT_STAGE_BYTES = 2 * 1024 * 1024
WEIGHT_STAGE_SLOTS = 4


def _dot(a, b):
    return jnp.dot(a, b, preferred_element_type=F32)


def _norm_mod(x, gain, shift, scale):
    ms = jnp.mean(x * x, axis=-1, keepdims=True)
    y = x * lax.rsqrt(ms + EPS) * gain
    return y * (1.0 + scale) + shift


def _tile_lanes(t, reps):
    return jnp.concatenate([t] * reps, axis=1) if reps > 1 else t


def _const_spec(shape):
    nd = len(shape)
    return pl.BlockSpec(shape, lambda *_: (0,) * nd, pipeline_mode=pl.Buffered(1))


def _params(*sem):
    return pltpu.CompilerParams(dimension_semantics=sem, vmem_limit_bytes=VMEM_LIMIT_BYTES)


def _load_weights_as_bf16(pairs, stage_ref, sem_ref):
    slots, sr, sc = stage_ref.shape
    chunks = [(src, dst, r0, c0)
              for src, dst in pairs
              for r0 in range(0, src.shape[0], sr)
              for c0 in range(0, src.shape[1], sc)]

    def copy(k):
        src, _, r0, c0 = chunks[k]
        slot = k % slots
        return pltpu.make_async_copy(src.at[pl.ds(r0, sr), pl.ds(c0, sc)], stage_ref.at[slot], sem_ref.at[slot])

    for k in range(min(slots - 1, len(chunks))):
        copy(k).start()
    for k, (_, dst, r0, c0) in enumerate(chunks):
        if k + slots - 1 < len(chunks):
            copy(k + slots - 1).start()
        copy(k).wait()
        dst[r0:r0 + sr, c0:c0 + sc] = stage_ref[k % slots].astype(BF16)


def _is_first_step(grid_rank):
    first = pl.program_id(0) == 0
    for axis in range(1, grid_rank):
        first = first & (pl.program_id(axis) == 0)
    return first


def _ada_kernel(c_ref, w_ref, b_ref, o_ref):
    c = c_ref[...]
    s = c * jax.nn.sigmoid(c)
    o_ref[...] = _dot(s.astype(BF16), w_ref[...].astype(BF16)) + b_ref[...]


def _ada(cvec, w_ada, b_ada):
    rows, d = cvec.shape
    n = w_ada.shape[1]
    return pl.pallas_call(
        _ada_kernel,
        grid=(n // ADA_TN,),
        in_specs=[
            pl.BlockSpec((rows, d), lambda j: (0, 0)),
            pl.BlockSpec((d, ADA_TN), lambda j: (0, j)),
            pl.BlockSpec((1, ADA_TN), lambda j: (0, j)),
        ],
        out_specs=pl.BlockSpec((rows, ADA_TN), lambda j: (0, j)),
        out_shape=jax.ShapeDtypeStruct((rows, n), F32),
        compiler_params=_params("arbitrary"),
        name="ada",
    )(cvec, w_ada, b_ada)


def _proj_kernel(*refs, fw, rw, d, scale, ncast):
    x_ref, mod_ref, gain_ref, cos_ref, sin_ref, gng_ref, w_hbm = refs[:7]
    cast_in = refs[7:7 + ncast]
    u_ref, q_ref, kt_ref, v_ref, sg_ref, ga_ref, gb_ref = refs[7 + ncast:14 + ncast]
    cast_out = refs[14 + ncast:14 + 2 * ncast]
    w_ref, stage_ref, sem_ref = refs[14 + 2 * ncast:]

    @pl.when(_is_first_step(2))
    def _():
        _load_weights_as_bf16([(w_hbm, w_ref)], stage_ref, sem_ref)

    for src, dst in zip(cast_in, cast_out):
        dst[...] = src[...].astype(BF16)

    mod = mod_ref[0]
    h = _norm_mod(x_ref[0], gain_ref[...], mod[0:1], mod[1:2]).astype(BF16)
    tm = h.shape[0]

    def proj(lo, width):
        return _dot(h, w_ref[:, lo:lo + width])

    u_ref[0] = proj(0, fw).astype(BF16)

    reps = rw // cos_ref.shape[1]
    cosf = _tile_lanes(cos_ref[...], reps)
    sinf = _tile_lanes(sin_ref[...], reps)
    lane = lax.broadcasted_iota(jnp.int32, (tm, rw), 1)
    first_half = (lane & (RET_HEAD_DIM // 2)) == 0

    def rope(t):
        rot = jnp.where(first_half,
                        pltpu.roll(t, rw - RET_HEAD_DIM // 2, 1),
                        pltpu.roll(t, RET_HEAD_DIM // 2, 1))
        return t * cosf + rot * sinf

    q_ref[0] = rope(proj(fw, rw)).astype(BF16)
    kt_ref[0] = (rope(proj(fw + rw, rw)) * scale).T.astype(BF16)
    v_ref[0] = proj(fw + 2 * rw, rw).astype(BF16)
    g = proj(fw + 3 * rw, rw)
    sg_ref[0] = (g * jax.nn.sigmoid(g) * gng_ref[...]).astype(BF16)
    ga_ref[0] = proj(fw + 4 * rw, d).astype(BF16)
    gb_ref[0] = proj(fw + 4 * rw + d, d).astype(BF16)


def _proj(x, mods, gain, w_in, cosf, sinf, gn_gain, later_weights, fw, rw):
    b, l, d = x.shape
    tm = min(TOKEN_TILE, l)
    nj = l // tm
    steps = b * nj
    tok = lambda width: pl.BlockSpec((1, tm, width), lambda i, j: (i, j, 0))
    out = lambda width: jax.ShapeDtypeStruct((b, l, width), BF16)
    sliced = [w.reshape(steps, w.shape[0] // steps, w.shape[1]) for w in later_weights]
    for w in sliced:
        assert w.shape[1] % BF16_SUBLANES == 0
    cast_spec = lambda w: pl.BlockSpec((1,) + w.shape[1:], lambda i, j: (i * nj + j, 0, 0))
    stage_rows = max(r for r in range(8, w_in.shape[0] + 1, 8)
                     if w_in.shape[0] % r == 0 and r * w_in.shape[1] * 4 <= WEIGHT_STAGE_BYTES)
    res = pl.pallas_call(
        functools.partial(_proj_kernel, fw=fw, rw=rw, d=d, scale=RET_HEAD_DIM ** -0.5, ncast=len(sliced)),
        grid=(b, nj),
        in_specs=[
            tok(d),
            pl.BlockSpec((1, N_MOD, d), lambda i, j: (i, 0, 0)),
            _const_spec((1, d)),
            pl.BlockSpec((tm, cosf.shape[1]), lambda i, j: (j, 0)),
            pl.BlockSpec((tm, sinf.shape[1]), lambda i, j: (j, 0)),
            _const_spec((1, rw)),
            pl.BlockSpec(memory_space=pl.ANY),
        ] + [cast_spec(w) for w in sliced],
        out_specs=[tok(fw), tok(rw), pl.BlockSpec((1, rw, tm), lambda i, j: (i, 0, j)),
                   tok(rw), tok(rw), tok(d), tok(d)] + [cast_spec(w) for w in sliced],
        out_shape=[out(fw), out(rw), jax.ShapeDtypeStruct((b, rw, l), BF16),
                   out(rw), out(rw), out(d), out(d)]
        + [jax.ShapeDtypeStruct(w.shape, BF16) for w in sliced],
        scratch_shapes=[pltpu.VMEM(w_in.shape, BF16),
                        pltpu.VMEM((WEIGHT_STAGE_SLOTS, stage_rows, w_in.shape[1]), F32),
                        pltpu.SemaphoreType.DMA((WEIGHT_STAGE_SLOTS,))],
        compiler_params=_params("arbitrary", "arbitrary"),
        name="proj",
    )(x, mods, gain, cosf, sinf, gn_gain, w_in, *sliced)
    casts = [c.reshape(w.shape) for c, w in zip(res[7:], later_weights)]
    return res[:7], casts


def _ctxproj_kernel(c_ref, mod_ref, gain_ref, w_ref, kt_ref, v_ref, *, rw, scale):
    mod = mod_ref[0]
    h = _norm_mod(c_ref[0], gain_ref[...], mod[0:1], mod[1:2]).astype(BF16)
    kt_ref[0] = (_dot(h, w_ref[:, :rw].astype(BF16)) * scale).T.astype(BF16)
    v_ref[0] = _dot(h, w_ref[:, rw:].astype(BF16)).astype(BF16)


def _ctxproj(ctx, mod_c, gain, w_in, fw, rw):
    b, lc, d = ctx.shape
    kv_block = (fw + rw) // (2 * rw)
    assert kv_block * 2 * rw == fw + rw
    return pl.pallas_call(
        functools.partial(_ctxproj_kernel, rw=rw, scale=RET_HEAD_DIM ** -0.5),
        grid=(b,),
        in_specs=[
            pl.BlockSpec((1, lc, d), lambda i: (i, 0, 0)),
            pl.BlockSpec((1, N_MOD, d), lambda i: (0, 0, 0)),
            pl.BlockSpec((1, d), lambda i: (0, 0)),
            pl.BlockSpec((d, 2 * rw), lambda i: (0, kv_block)),
        ],
        out_specs=[pl.BlockSpec((1, rw, lc), lambda i: (i, 0, 0)),
                   pl.BlockSpec((1, lc, rw), lambda i: (i, 0, 0))],
        out_shape=[jax.ShapeDtypeStruct((b, rw, lc), BF16), jax.ShapeDtypeStruct((b, lc, rw), BF16)],
        compiler_params=_params("arbitrary"),
        name="ctxproj",
    )(ctx, mod_c, gain, w_in)


def _fft_kernel(u_ref, kw_ref, twc_ref, tws_ref, w2_ref, cs_ref, o_ref, tr_ref, ti_ref, scr_ref,
                *, la, lb, gd, npb, scale):
    r = FFT_LA_TILE
    rows = lb * r
    fw = u_ref.shape[2]
    ngroups = fw // gd
    reps = fw // twc_ref.shape[2]

    for j in range(la // r):
        u = jnp.concatenate([u_ref[0, k * la + j * r:k * la + (j + 1) * r, :] for k in range(lb)], axis=0)
        t = _dot(kw_ref[...], u)
        a, b = t[:rows], t[rows:]
        ct = _tile_lanes(twc_ref[j], reps)
        st = _tile_lanes(tws_ref[j], reps)
        tr = (a * ct + b * st).astype(BF16)
        ti = (b * ct - a * st).astype(BF16)
        for k in range(lb):
            tr_ref[k * la + j * r:k * la + (j + 1) * r, :] = tr[k * r:(k + 1) * r]
            ti_ref[k * la + j * r:k * la + (j + 1) * r, :] = ti[k * r:(k + 1) * r]

    for pb in range(lb // npb):
        xs = []
        for p in range(npb):
            lo = (pb * npb + p) * la
            t = jnp.concatenate([tr_ref[lo:lo + la, :], ti_ref[lo:lo + la, :]], axis=0)
            xs.append(_dot(w2_ref[...], t).astype(BF16))
        for gi in range(ngroups):
            cols = slice(gi * gd, (gi + 1) * gd)
            lhs = jnp.concatenate([jnp.concatenate([x[:la, cols], x[la:, cols]], axis=1) for x in xs], axis=0)
            fg = _dot(lhs, cs_ref[...]) * scale
            for p in range(npb):
                scr_ref[gi, pl.ds(p, la, stride=npb), :] = fg[p * la:(p + 1) * la]
        full = jnp.concatenate([scr_ref[gi] for gi in range(ngroups)], axis=1)
        o_ref[0, :, pb * npb:(pb + 1) * npb, :] = full.reshape(la, npb, fw).astype(BF16)


def _fft(u, kw, twc, tws, w2, cs, la, lb, gd, scale):
    b, l, fw = u.shape
    npb = min(BF16_SUBLANES, lb)
    out = pl.pallas_call(
        functools.partial(_fft_kernel, la=la, lb=lb, gd=gd, npb=npb, scale=scale),
        grid=(b,),
        in_specs=[pl.BlockSpec((1, l, fw), lambda i: (i, 0, 0)),
                  _const_spec(kw.shape), _const_spec(twc.shape), _const_spec(tws.shape),
                  _const_spec(w2.shape), _const_spec(cs.shape)],
        out_specs=pl.BlockSpec((1, la, lb, fw), lambda i: (i, 0, 0, 0)),
        out_shape=jax.ShapeDtypeStruct((b, la, lb, fw), BF16),
        scratch_shapes=[pltpu.VMEM((l, fw), BF16), pltpu.VMEM((l, fw), BF16),
                        pltpu.VMEM((fw // gd, la * npb, gd), F32)],
        compiler_params=_params("arbitrary"),
        name="fft",
    )(u, kw, twc, tws, w2, cs)
    return out.reshape(b, l, fw)


def _ret_kernel(q_ref, kt_ref, v_ref, sg_ref, kct_ref, vc_ref, dall_ref, qdf_ref, qdb_ref,
                kdf_ref, kdb_ref, cdf_ref, cdb_ref, wcf_ref, wcb_ref,
                z_ref, sf_ref, sb_ref, stf_ref, stb_ref, p_ref, o_ref, *, nchunk, unroll):
    c = RET_CHUNK
    gw = GROUP_W
    same_head = (lax.broadcasted_iota(jnp.int32, (gw, gw), 0) // RET_HEAD_DIM
                 == lax.broadcasted_iota(jnp.int32, (gw, gw), 1) // RET_HEAD_DIM)

    def diag_blocks(t):
        return jnp.where(same_head, t, 0.0)

    def weighted(t, w):
        return (t.astype(F32) * w).astype(BF16)

    def rows(ref, n):
        return ref[0, pl.ds(pl.multiple_of(n * c, c), c), :]

    def kt_chunk(n):
        return kt_ref[0, :, pl.ds(pl.multiple_of(n * c, c), c)]

    kct = kct_ref[0]
    vc = vc_ref[0]
    stf_ref[...] = diag_blocks(_dot(kct, weighted(vc, wcf_ref[...])))
    stb_ref[...] = diag_blocks(_dot(kct, weighted(vc, wcb_ref[...])))

    def scan_body(i, carry):
        nf = i
        nb = nchunk - 1 - i
        s = stf_ref[...]
        sf_ref[nf] = s.astype(BF16)
        stf_ref[...] = s * cdf_ref[...] + diag_blocks(_dot(kt_chunk(nf), weighted(rows(v_ref, nf), kdf_ref[...])))
        s = stb_ref[...]
        sb_ref[nb] = s.astype(BF16)
        stb_ref[...] = s * cdb_ref[...] + diag_blocks(_dot(kt_chunk(nb), weighted(rows(v_ref, nb), kdb_ref[...])))
        return carry

    lax.fori_loop(0, nchunk, scan_body, 0, unroll=unroll)

    lane = lax.broadcasted_iota(jnp.int32, (c, gw), 1)
    sub = lax.broadcasted_iota(jnp.int32, (gw, c), 0)
    lane_masks = [(lane >= h * RET_HEAD_DIM) & (lane < (h + 1) * RET_HEAD_DIM)
                  for h in range(HEADS_PER_GROUP)]
    sub_masks = [(sub >= h * RET_HEAD_DIM) & (sub < (h + 1) * RET_HEAD_DIM)
                 for h in range(HEADS_PER_GROUP)]
    bd_mean = jnp.where(same_head, 1.0 / RET_HEAD_DIM, 0.0).astype(BF16)

    def score_body(n, carry):
        ktn = kt_chunk(n)
        zk = jnp.zeros_like(ktn)
        kbd = jnp.concatenate([jnp.where(m, ktn, zk) for m in sub_masks], axis=1)
        p_ref[n] = (_dot(rows(q_ref, n), kbd) * dall_ref[...]).astype(BF16)
        return carry

    lax.fori_loop(0, nchunk, score_body, 0, unroll=unroll)

    def mix_body(n, carry):
        qn = rows(q_ref, n)
        vn = rows(v_ref, n)
        zv = jnp.zeros_like(vn)
        vbd = jnp.concatenate([jnp.where(m, vn, zv) for m in lane_masks], axis=0)
        qfb = jnp.concatenate([weighted(qn, qdf_ref[...]), weighted(qn, qdb_ref[...])], axis=1)
        o_ref[n] = _dot(p_ref[n], vbd) + _dot(qfb, jnp.concatenate([sf_ref[n], sb_ref[n]], axis=0))
        return carry

    lax.fori_loop(0, nchunk, mix_body, 0, unroll=unroll)

    def norm_body(n, carry):
        o = o_ref[n]
        o2 = o * o
        hi = o2.astype(BF16)
        lo = (o2 - hi.astype(F32)).astype(BF16)
        ms = _dot(jnp.concatenate([hi, lo], axis=1), jnp.concatenate([bd_mean, bd_mean], axis=0))
        z = rows(sg_ref, n).astype(F32) * (o * lax.rsqrt(ms + EPS))
        z_ref[0, pl.ds(pl.multiple_of(n * c, c), c), :] = z.astype(BF16)
        return carry

    lax.fori_loop(0, nchunk, norm_body, 0, unroll=unroll)


def _retention(q, kt, v, sg, kct, vc, tabs):
    b, l, rw = q.shape
    lc = vc.shape[1]
    c = RET_CHUNK
    gw = GROUP_W
    ng = rw // gw
    nchunk = l // c
    tok = pl.BlockSpec((1, l, gw), lambda i, j: (i, 0, j))
    tokt = pl.BlockSpec((1, gw, l), lambda i, j: (i, j, 0))
    grp = lambda r, width: pl.BlockSpec((None, r, width), lambda i, j: (j, 0, 0))
    return pl.pallas_call(
        functools.partial(_ret_kernel, nchunk=nchunk, unroll=math.gcd(nchunk, RET_UNROLL)),
        grid=(b, ng),
        in_specs=[
            tok, tokt, tok, tok,
            pl.BlockSpec((1, gw, lc), lambda i, j: (i, j, 0)),
            pl.BlockSpec((1, lc, gw), lambda i, j: (i, 0, j)),
            grp(c, HEADS_PER_GROUP * c),
            grp(c, gw), grp(c, gw), grp(c, gw), grp(c, gw),
            grp(1, gw), grp(1, gw),
            grp(lc, gw), grp(lc, gw),
        ],
        out_specs=tok,
        out_shape=jax.ShapeDtypeStruct((b, l, rw), BF16),
        scratch_shapes=[pltpu.VMEM((nchunk, gw, gw), BF16), pltpu.VMEM((nchunk, gw, gw), BF16),
                        pltpu.VMEM((gw, gw), F32), pltpu.VMEM((gw, gw), F32),
                        pltpu.VMEM((nchunk, c, HEADS_PER_GROUP * c), BF16),
                        pltpu.VMEM((nchunk, c, gw), F32)],
        compiler_params=_params("arbitrary", "arbitrary"),
        name="ret",
    )(q, kt, v, sg, kct, vc, tabs["dall"], tabs["qdf"], tabs["qdb"], tabs["kdf"], tabs["kdb"],
      tabs["cdf"], tabs["cdb"], tabs["wcf"], tabs["wcb"])


def _retention_tables(log_gamma, lc):
    c = RET_CHUNK
    hpg = HEADS_PER_GROUP
    nh = log_gamma.shape[1]
    ng = nh // hpg
    lgf = log_gamma[0].reshape(ng, hpg)
    lgb = log_gamma[1].reshape(ng, hpg)
    pos = jnp.arange(c, dtype=F32)
    diff = pos[:, None] - pos[None, :]
    df = jnp.where(diff >= 0, jnp.exp(lgf[:, :, None, None] * jnp.maximum(diff, 0.0)), 0.0)
    db = jnp.where(diff <= 0, jnp.exp(lgb[:, :, None, None] * jnp.maximum(-diff, 0.0)), 0.0)
    dall = (df + db).transpose(0, 2, 1, 3).reshape(ng, c, hpg * c)

    def per_lane(lg, expo):
        t = jnp.exp(lg[:, None, :] * expo[None, :, None])
        return jnp.repeat(t, RET_HEAD_DIM, axis=2)

    cpos = jnp.arange(lc, dtype=F32)
    chunk_len = jnp.full((1,), float(c), F32)

    return dict(
        dall=dall,
        qdf=per_lane(lgf, pos + 1.0), qdb=per_lane(lgb, c - pos),
        kdf=per_lane(lgf, c - 1.0 - pos), kdb=per_lane(lgb, pos),
        cdf=per_lane(lgf, chunk_len), cdb=per_lane(lgb, chunk_len),
        wcf=per_lane(lgf, lc - 1.0 - cpos), wcb=per_lane(lgb, cpos),
    )


def _out_kernel(x_ref, mod_ref, fm_ref, z_ref, ga_ref, gb_ref, gain2_ref, fgain_ref,
                w4_ref, wr_ref, wo_ref, w1_ref, w2_ref, o_ref, *, ff_chunk):
    mod = mod_ref[0]
    g1, sh2, sc2, g2 = mod[2:3], mod[3:4], mod[4:5], mod[5:6]
    y_four = _dot(fm_ref[0], w4_ref[...])
    y_ret = _dot(z_ref[0], wr_ref[...])
    y = (jax.nn.sigmoid(ga_ref[0].astype(F32)) * y_four
         + jax.nn.sigmoid(gb_ref[0].astype(F32)) * y_ret)
    x1 = x_ref[0] + g1 * _dot(y.astype(BF16), wo_ref[...])
    h2 = _norm_mod(x1, gain2_ref[...], sh2, sc2).astype(BF16)
    dff = w1_ref.shape[1]
    acc = None
    for lo in range(0, dff, ff_chunk):
        hid = jnp.maximum(_dot(h2, w1_ref[:, lo:lo + ff_chunk]), 0.0)
        part = _dot((hid * hid).astype(BF16), w2_ref[lo:lo + ff_chunk, :])
        acc = part if acc is None else acc + part
    x2 = x1 + g2 * acc
    ms = jnp.mean(x2 * x2, axis=-1, keepdims=True)
    o_ref[0] = x2 * lax.rsqrt(ms + EPS) * fgain_ref[...]


def _out(x, mods, fm, z, ga, gb, w4, wr, wo, gain2, w1, w2, fgain):
    b, l, d = x.shape
    tm = min(TOKEN_TILE, l)
    tok = lambda width: pl.BlockSpec((1, tm, width), lambda i, j: (i, j, 0))
    weights = (w4, wr, wo, w1, w2)
    return pl.pallas_call(
        functools.partial(_out_kernel, ff_chunk=min(1024, w1.shape[1])),
        grid=(b, l // tm),
        in_specs=[
            tok(d),
            pl.BlockSpec((1, N_MOD, d), lambda i, j: (i, 0, 0)),
            tok(fm.shape[2]), tok(z.shape[2]), tok(d), tok(d),
            _const_spec((1, d)), _const_spec((1, d)),
        ] + [_const_spec(w.shape) for w in weights],
        out_specs=tok(d),
        out_shape=jax.ShapeDtypeStruct((b, l, d), F32),
        compiler_params=_params("arbitrary", "arbitrary"),
        name="out",
    )(x, mods, fm, z, ga, gb, gain2, fgain, *weights)


def _dft_tables(l, gd):
    la = FFT_LA
    lb = l // la

    def cs(n):
        idx = np.arange(n)
        ang = 2.0 * np.pi * ((idx[:, None] * idx[None, :]) % n) / n
        return np.cos(ang), np.sin(ang)

    cb, sb = cs(lb)
    eye = np.eye(FFT_LA_TILE)
    kw = np.concatenate([np.kron(cb, eye), -np.kron(sb, eye)], axis=0)
    ca, sa = cs(la)
    w2 = np.block([[ca, sa], [-sa, ca]])
    cc, sc = cs(gd)
    chan = np.concatenate([cc, sc], axis=0)
    tw = 2.0 * np.pi * (np.arange(lb)[:, None] * np.arange(la)[None, :]) / l
    tw = tw.reshape(lb, la // FFT_LA_TILE, FFT_LA_TILE).transpose(1, 0, 2).reshape(la // FFT_LA_TILE, -1)
    twc = np.repeat(np.cos(tw)[:, :, None], LANES, axis=2)
    tws = np.repeat(np.sin(tw)[:, :, None], LANES, axis=2)
    as_bf = lambda a: jnp.asarray(a, dtype=F32).astype(BF16)
    return as_bf(kw), as_bf(w2), as_bf(chan), jnp.asarray(twc, F32), jnp.asarray(tws, F32), la, lb


def _rope_tables(l):
    nf = RET_HEAD_DIM // 4
    inv = ROPE_BASE ** (-jnp.arange(nf, dtype=F32) / nf)
    rows = l // GRID_W
    r, cc = jnp.meshgrid(jnp.arange(rows, dtype=F32), jnp.arange(GRID_W, dtype=F32), indexing="ij")
    ang = jnp.concatenate([r.reshape(-1)[:, None] * inv, cc.reshape(-1)[:, None] * inv], axis=-1)
    cos, sin = jnp.cos(ang), jnp.sin(ang)
    cos_h = jnp.concatenate([cos, cos], axis=1)
    sin_h = jnp.concatenate([-sin, sin], axis=1)
    reps = LANES // RET_HEAD_DIM
    return jnp.tile(cos_h, (1, reps)), jnp.tile(sin_h, (1, reps))


def kernel(x, c, ctx, c_ctx, w_ada, b_ada, norm1_gain, w_in, four_w_out, ret_decay_logit,
           ret_gn_gain, ret_w_out, w_out, norm2_gain, w_mlp1, w_mlp2, final_gain):
    assert w_ada.shape[0] == 1, "single-layer block"
    b, l, d = x.shape
    lc = ctx.shape[1]
    fw = four_w_out.shape[1]
    rw = ret_w_out.shape[1]
    gd = fw // FOUR_GROUPS
    assert l % FFT_LA == 0 and l % RET_CHUNK == 0 and rw % GROUP_W == 0

    pad = (-(b + 1)) % 8
    cvec = jnp.concatenate([c, c_ctx[None, :], jnp.zeros((pad, d), F32)], axis=0)
    mod = _ada(cvec, w_ada[0], b_ada)
    mods_x = mod[:b].reshape(b, N_MOD, d)
    mod_c = mod[b:b + 1].reshape(1, N_MOD, d)

    kw, w2, chan, twc, tws, la, lb = _dft_tables(l, gd)
    cosf, sinf = _rope_tables(l)

    later = [four_w_out[0], ret_w_out[0], w_out[0], w_mlp1[0], w_mlp2[0]]
    (u, q, kt, v, sg, ga, gb), later_b = _proj(x, mods_x, norm1_gain, w_in[0], cosf, sinf, ret_gn_gain,
                                                later, fw, rw)
    kct, vc = _ctxproj(ctx, mod_c, norm1_gain, w_in[0], fw, rw)

    fm = _fft(u, kw, twc, tws, w2, chan, la, lb, gd, 1.0 / math.sqrt(l * gd))

    log_gamma = jax.nn.log_sigmoid(ret_decay_logit[0].astype(F32))
    tabs = _retention_tables(log_gamma, lc)
    z = _retention(q, kt, v, sg, kct, vc, tabs)

    w4, wr, wo, w1, w2 = later_b
    return _out(x, mods_x, fm, z, ga, gb, w4, wr, wo, norm2_gain, w1, w2, final_gain[None, :])
```

```python
import functools
import math

import jax
import jax.numpy as jnp
import numpy as np
from jax import lax
from jax.experimental import pallas as pl
from jax.experimental.pallas import tpu as pltpu

F32 = jnp.float32
BF16 = jnp.bfloat16

GRID_W = 64
FOUR_GROUPS = 4
RET_HEAD_DIM = 64
N_MOD = 6
ROPE_BASE = 10000.0
EPS = 1e-6

LANES = 128
MXU_DIM = 256
F32_SUBLANES = 8
BF16_SUBLANES = 16
VMEM_LIMIT_BYTES = 56 * 1024 * 1024

RET_CHUNK = 128
RET_UNROLL = 32
HEADS_PER_GROUP = MXU_DIM // RET_HEAD_DIM
GROUP_W = HEADS_PER_GROUP * RET_HEAD_DIM
FFT_LA = 128
FFT_LA_TILE = BF16_SUBLANES
FFT_SCATTER_PAD = F32_SUBLANES
TOKEN_TILE = 512
WEIGHT_STAGE_BYTES = 2 * 1024 * 1024
WEIGHT_STAGE_SLOTS = 4


def _dot(a, b):
    return jnp.dot(a, b, preferred_element_type=F32)


def _norm_mod(x, gain, shift, scale):
    ms = jnp.mean(x * x, axis=-1, keepdims=True)
    y = x * lax.rsqrt(ms + EPS) * gain
    return y * (1.0 + scale) + shift


def _tile_lanes(t, reps):
    return jnp.concatenate([t] * reps, axis=1) if reps > 1 else t


def _const_spec(shape):
    nd = len(shape)
    return pl.BlockSpec(shape, lambda *_: (0,) * nd, pipeline_mode=pl.Buffered(1))


def _params(*sem):
    return pltpu.CompilerParams(dimension_semantics=sem, vmem_limit_bytes=VMEM_LIMIT_BYTES)


def _load_weights_as_bf16(pairs, stage_ref, sem_ref):
    slots, sr, sc = stage_ref.shape
    chunks = [(src, dst, r0, c0)
              for src, dst in pairs
              for r0 in range(0, src.shape[0], sr)
              for c0 in range(0, src.shape[1], sc)]

    def copy(k):
        src, _, r0, c0 = chunks[k]
        slot = k % slots
        return pltpu.make_async_copy(src.at[pl.ds(r0, sr), pl.ds(c0, sc)], stage_ref.at[slot], sem_ref.at[slot])

    for k in range(min(slots - 1, len(chunks))):
        copy(k).start()
    for k, (_, dst, r0, c0) in enumerate(chunks):
        if k + slots - 1 < len(chunks):
            copy(k + slots - 1).start()
        copy(k).wait()
        dst[r0:r0 + sr, c0:c0 + sc] = stage_ref[k % slots].astype(BF16)


def _is_first_step(grid_rank):
    first = pl.program_id(0) == 0
    for axis in range(1, grid_rank):
        first = first & (pl.program_id(axis) == 0)
    return first


def _ada_kernel(c_ref, cctx_ref, w_ref, b_ref, o_ref, cv_ref):
    b = c_ref.shape[0]
    cv_ref[...] = jnp.zeros_like(cv_ref)
    cv_ref[0:b, :] = c_ref[...]
    cv_ref[b:b + 1, :] = cctx_ref[...]
    c = cv_ref[...]
    s = c * jax.nn.sigmoid(c)
    o_ref[0] = _dot(s.astype(BF16), w_ref[...].astype(BF16)) + b_ref[...]


def _ada(c, c_ctx, w_ada, b_ada):
    b, d = c.shape
    rows = -(-(b + 1) // F32_SUBLANES) * F32_SUBLANES
    assert w_ada.shape[1] == N_MOD * d
    return pl.pallas_call(
        _ada_kernel,
        grid=(N_MOD,),
        in_specs=[
            pl.BlockSpec((b, d), lambda j: (0, 0)),
            pl.BlockSpec((1, d), lambda j: (0, 0)),
            pl.BlockSpec((d, d), lambda j: (0, j)),
            pl.BlockSpec((1, d), lambda j: (0, j)),
        ],
        out_specs=pl.BlockSpec((1, rows, d), lambda j: (j, 0, 0)),
        out_shape=jax.ShapeDtypeStruct((N_MOD, rows, d), F32),
        scratch_shapes=[pltpu.VMEM((rows, d), F32)],
        compiler_params=_params("arbitrary"),
        name="ada",
    )(c, c_ctx[None, :], w_ada, b_ada)


def _proj_kernel(*refs, fw, rw, d, scale, ncast):
    x_ref, mod_ref, gain_ref, cos_ref, sin_ref, gng_ref, w_hbm = refs[:7]
    cast_in = refs[7:7 + ncast]
    u_ref, q_ref, kt_ref, v_ref, sg_ref, ga_ref, gb_ref = refs[7 + ncast:14 + ncast]
    cast_out = refs[14 + ncast:14 + 2 * ncast]
    w_ref, stage_ref, sem_ref = refs[14 + 2 * ncast:]

    @pl.when(_is_first_step(2))
    def _():
        _load_weights_as_bf16([(w_hbm, w_ref)], stage_ref, sem_ref)

    for src, dst in zip(cast_in, cast_out):
        dst[...] = src[...].astype(BF16)

    mod = mod_ref[:, pl.ds(pl.program_id(0), 1), :]
    h = _norm_mod(x_ref[0], gain_ref[...], mod[0], mod[1]).astype(BF16)
    tm = h.shape[0]

    def proj(lo, width):
        return _dot(h, w_ref[:, lo:lo + width])

    u_ref[0] = proj(0, fw).astype(BF16)

    reps = rw // cos_ref.shape[1]
    cosf = _tile_lanes(cos_ref[...], reps)
    sinf = _tile_lanes(sin_ref[...], reps)
    lane = lax.broadcasted_iota(jnp.int32, (tm, rw), 1)
    first_half = (lane & (RET_HEAD_DIM // 2)) == 0

    def rope(t):
        rot = jnp.where(first_half,
                        pltpu.roll(t, rw - RET_HEAD_DIM // 2, 1),
                        pltpu.roll(t, RET_HEAD_DIM // 2, 1))
        return t * cosf + rot * sinf

    q_ref[0] = rope(proj(fw, rw)).astype(BF16)
    kt_ref[0] = (rope(proj(fw + rw, rw)) * scale).T.astype(BF16)
    v_ref[0] = proj(fw + 2 * rw, rw).astype(BF16)
    g = proj(fw + 3 * rw, rw)
    sg_ref[0] = (g * jax.nn.sigmoid(g) * gng_ref[...]).astype(BF16)
    ga_ref[0] = proj(fw + 4 * rw, d).astype(BF16)
    gb_ref[0] = proj(fw + 4 * rw + d, d).astype(BF16)


def _proj(x, mods, gain, w_in, cosf, sinf, gn_gain, later_weights, fw, rw):
    b, l, d = x.shape
    tm = min(TOKEN_TILE, l)
    nj = l // tm
    steps = b * nj
    tok = lambda width: pl.BlockSpec((1, tm, width), lambda i, j: (i, j, 0))
    out = lambda width: jax.ShapeDtypeStruct((b, l, width), BF16)
    sliced = [w.reshape(steps, w.shape[0] // steps, w.shape[1]) for w in later_weights]
    for w in sliced:
        assert w.shape[1] % BF16_SUBLANES == 0
    cast_spec = lambda w: pl.BlockSpec((1,) + w.shape[1:], lambda i, j: (i * nj + j, 0, 0))
    stage_rows = max(r for r in range(8, w_in.shape[0] + 1, 8)
                     if w_in.shape[0] % r == 0 and r * w_in.shape[1] * 4 <= WEIGHT_STAGE_BYTES)
    res = pl.pallas_call(
        functools.partial(_proj_kernel, fw=fw, rw=rw, d=d, scale=RET_HEAD_DIM ** -0.5, ncast=len(sliced)),
        grid=(b, nj),
        in_specs=[
            tok(d),
            _const_spec(mods.shape),
            _const_spec((1, d)),
            pl.BlockSpec((tm, cosf.shape[1]), lambda i, j: (j, 0)),
            pl.BlockSpec((tm, sinf.shape[1]), lambda i, j: (j, 0)),
            _const_spec((1, rw)),
            pl.BlockSpec(memory_space=pl.ANY),
        ] + [cast_spec(w) for w in sliced],
        out_specs=[tok(fw), tok(rw), pl.BlockSpec((1, rw, tm), lambda i, j: (i, 0, j)),
                   tok(rw), tok(rw), tok(d), tok(d)] + [cast_spec(w) for w in sliced],
        out_shape=[out(fw), out(rw), jax.ShapeDtypeStruct((b, rw, l), BF16),
                   out(rw), out(rw), out(d), out(d)]
        + [jax.ShapeDtypeStruct(w.shape, BF16) for w in sliced],
        scratch_shapes=[pltpu.VMEM(w_in.shape, BF16),
                        pltpu.VMEM((WEIGHT_STAGE_SLOTS, stage_rows, w_in.shape[1]), F32),
                        pltpu.SemaphoreType.DMA((WEIGHT_STAGE_SLOTS,))],
        compiler_params=_params("arbitrary", "arbitrary"),
        name="proj",
    )(x, mods, gain, cosf, sinf, gn_gain, w_in, *sliced)
    casts = [c.reshape(w.shape) for c, w in zip(res[7:], later_weights)]
    return res[:7], casts


def _ctxproj_kernel(c_ref, mod_ref, gain_ref, w_ref, kt_ref, v_ref, *, rw, scale, mod_row):
    mod = mod_ref[:, mod_row:mod_row + 1, :]
    h = _norm_mod(c_ref[0], gain_ref[...], mod[0], mod[1]).astype(BF16)
    kt_ref[0] = (_dot(h, w_ref[:, :rw].astype(BF16)) * scale).T.astype(BF16)
    v_ref[0] = _dot(h, w_ref[:, rw:].astype(BF16)).astype(BF16)


def _ctxproj(ctx, mods, gain, w_in, fw, rw):
    b, lc, d = ctx.shape
    kv_block = (fw + rw) // (2 * rw)
    assert kv_block * 2 * rw == fw + rw
    return pl.pallas_call(
        functools.partial(_ctxproj_kernel, rw=rw, scale=RET_HEAD_DIM ** -0.5, mod_row=b),
        grid=(b,),
        in_specs=[
            pl.BlockSpec((1, lc, d), lambda i: (i, 0, 0)),
            _const_spec(mods.shape),
            pl.BlockSpec((1, d), lambda i: (0, 0)),
            pl.BlockSpec((d, 2 * rw), lambda i: (0, kv_block)),
        ],
        out_specs=[pl.BlockSpec((1, rw, lc), lambda i: (i, 0, 0)),
                   pl.BlockSpec((1, lc, rw), lambda i: (i, 0, 0))],
        out_shape=[jax.ShapeDtypeStruct((b, rw, lc), BF16), jax.ShapeDtypeStruct((b, lc, rw), BF16)],
        compiler_params=_params("arbitrary"),
        name="ctxproj",
    )(ctx, mods, gain, w_in)


def _fft_kernel(u_ref, kw_ref, twc_ref, tws_ref, w2_ref, cs_ref, o_ref, tr_ref, ti_ref, scr_ref,
                *, la, lb, gd, npb, scale):
    r = FFT_LA_TILE
    rows = lb * r
    fw = u_ref.shape[2]
    ngroups = fw // gd
    reps = fw // twc_ref.shape[2]
    pitch = scr_ref.shape[1] // la

    @pl.when(pl.program_id(0) == 0)
    def _():
        scr_ref[...] = jnp.zeros_like(scr_ref)

    for j in range(la // r):
        u = jnp.concatenate([u_ref[0, k * la + j * r:k * la + (j + 1) * r, :] for k in range(lb)], axis=0)
        t = _dot(kw_ref[...], u)
        a, b = t[:rows], t[rows:]
        ct = _tile_lanes(twc_ref[j], reps)
        st = _tile_lanes(tws_ref[j], reps)
        tr = (a * ct + b * st).astype(BF16)
        ti = (b * ct - a * st).astype(BF16)
        for k in range(lb):
            tr_ref[k * la + j * r:k * la + (j + 1) * r, :] = tr[k * r:(k + 1) * r]
            ti_ref[k * la + j * r:k * la + (j + 1) * r, :] = ti[k * r:(k + 1) * r]

    for pb in range(lb // npb):
        xs = []
        for p in range(npb):
            lo = (pb * npb + p) * la
            t = jnp.concatenate([tr_ref[lo:lo + la, :], ti_ref[lo:lo + la, :]], axis=0)
            xs.append(_dot(w2_ref[...], t).astype(BF16))
        for gi in range(ngroups):
            cols = slice(gi * gd, (gi + 1) * gd)
            lhs = jnp.concatenate([jnp.concatenate([x[:la, cols], x[la:, cols]], axis=1) for x in xs], axis=0)
            fg = _dot(lhs, cs_ref[...]) * scale
            for p in range(npb):
                scr_ref[gi, pl.ds(p, la, stride=pitch), :] = fg[p * la:(p + 1) * la]
        full = jnp.concatenate([scr_ref[gi].reshape(la, pitch, gd)[:, :npb, :] for gi in range(ngroups)], axis=2)
        o_ref[0, :, pb * npb:(pb + 1) * npb, :] = full.astype(BF16)


def _fft(u, kw, twc, tws, w2, cs, la, lb, gd, scale):
    b, l, fw = u.shape
    npb = min(BF16_SUBLANES, lb)
    out = pl.pallas_call(
        functools.partial(_fft_kernel, la=la, lb=lb, gd=gd, npb=npb, scale=scale),
        grid=(b,),
        in_specs=[pl.BlockSpec((1, l, fw), lambda i: (i, 0, 0)),
                  _const_spec(kw.shape), _const_spec(twc.shape), _const_spec(tws.shape),
                  _const_spec(w2.shape), _const_spec(cs.shape)],
        out_specs=pl.BlockSpec((1, la, lb, fw), lambda i: (i, 0, 0, 0)),
        out_shape=jax.ShapeDtypeStruct((b, la, lb, fw), BF16),
        scratch_shapes=[pltpu.VMEM((l, fw), BF16), pltpu.VMEM((l, fw), BF16),
                        pltpu.VMEM((fw // gd, la * (npb + FFT_SCATTER_PAD), gd), F32)],
        compiler_params=_params("arbitrary"),
        name="fft",
    )(u, kw, twc, tws, w2, cs)
    return out.reshape(b, l, fw)


def _ret_kernel(q_ref, kt_ref, v_ref, sg_ref, kct_ref, vc_ref, lgl_ref, lgc_ref,
                z_ref, sf_ref, sb_ref, stf_ref, stb_ref, p_ref, o_ref,
                dall_ref, qdf_ref, qdb_ref, kdf_ref, kdb_ref, *, nchunk, unroll):
    c = RET_CHUNK
    gw = GROUP_W
    lc = vc_ref.shape[1]

    lgf, lgb = lgl_ref[0:1, :], lgl_ref[1:2, :]
    pos = lax.broadcasted_iota(jnp.int32, (c, gw), 0).astype(F32)
    qdf_ref[...] = jnp.exp(lgf * (pos + 1.0))
    qdb_ref[...] = jnp.exp(lgb * (c - pos))
    kdf_ref[...] = jnp.exp(lgf * (c - 1.0 - pos))
    kdb_ref[...] = jnp.exp(lgb * pos)
    cdf = jnp.exp(lgf * c)
    cdb = jnp.exp(lgb * c)
    cpos = lax.broadcasted_iota(jnp.int32, (lc, gw), 0).astype(F32)
    wcf = jnp.exp(lgf * (lc - 1.0 - cpos))
    wcb = jnp.exp(lgb * cpos)
    si = lax.broadcasted_iota(jnp.int32, (c, HEADS_PER_GROUP * c), 0)
    sj = lax.broadcasted_iota(jnp.int32, (c, HEADS_PER_GROUP * c), 1) & (c - 1)
    diff = (si - sj).astype(F32)
    dall_ref[...] = (jnp.where(diff >= 0, jnp.exp(lgc_ref[0:1, :] * jnp.maximum(diff, 0.0)), 0.0)
                     + jnp.where(diff <= 0, jnp.exp(lgc_ref[1:2, :] * jnp.maximum(-diff, 0.0)), 0.0))

    same_head = (lax.broadcasted_iota(jnp.int32, (gw, gw), 0) // RET_HEAD_DIM
                 == lax.broadcasted_iota(jnp.int32, (gw, gw), 1) // RET_HEAD_DIM)

    def diag_blocks(t):
        return jnp.where(same_head, t, 0.0)

    def weighted(t, w):
        return (t.astype(F32) * w).astype(BF16)

    def rows(ref, n):
        return ref[0, pl.ds(pl.multiple_of(n * c, c), c), :]

    def kt_chunk(n):
        return kt_ref[0, :, pl.ds(pl.multiple_of(n * c, c), c)]

    kct = kct_ref[0]
    vc = vc_ref[0]
    stf_ref[...] = diag_blocks(_dot(kct, weighted(vc, wcf)))
    stb_ref[...] = diag_blocks(_dot(kct, weighted(vc, wcb)))

    def scan_body(i, carry):
        nf = i
        nb = nchunk - 1 - i
        s = stf_ref[...]
        sf_ref[nf] = s.astype(BF16)
        stf_ref[...] = s * cdf + diag_blocks(_dot(kt_chunk(nf), weighted(rows(v_ref, nf), kdf_ref[...])))
        s = stb_ref[...]
        sb_ref[nb] = s.astype(BF16)
        stb_ref[...] = s * cdb + diag_blocks(_dot(kt_chunk(nb), weighted(rows(v_ref, nb), kdb_ref[...])))
        return carry

    lax.fori_loop(0, nchunk, scan_body, 0, unroll=unroll)

    lane = lax.broadcasted_iota(jnp.int32, (c, gw), 1)
    sub = lax.broadcasted_iota(jnp.int32, (gw, c), 0)
    lane_masks = [(lane >= h * RET_HEAD_DIM) & (lane < (h + 1) * RET_HEAD_DIM)
                  for h in range(HEADS_PER_GROUP)]
    sub_masks = [(sub >= h * RET_HEAD_DIM) & (sub < (h + 1) * RET_HEAD_DIM)
                 for h in range(HEADS_PER_GROUP)]
    bd_mean = jnp.where(same_head, 1.0 / RET_HEAD_DIM, 0.0).astype(BF16)

    def score_body(n, carry):
        ktn = kt_chunk(n)
        zk = jnp.zeros_like(ktn)
        kbd = jnp.concatenate([jnp.where(m, ktn, zk) for m in sub_masks], axis=1)
        p_ref[n] = (_dot(rows(q_ref, n), kbd) * dall_ref[...]).astype(BF16)
        return carry

    lax.fori_loop(0, nchunk, score_body, 0, unroll=unroll)

    def mix_body(n, carry):
        qn = rows(q_ref, n)
        vn = rows(v_ref, n)
        zv = jnp.zeros_like(vn)
        vbd = jnp.concatenate([jnp.where(m, vn, zv) for m in lane_masks], axis=0)
        qfb = jnp.concatenate([weighted(qn, qdf_ref[...]), weighted(qn, qdb_ref[...])], axis=1)
        o_ref[n] = _dot(p_ref[n], vbd) + _dot(qfb, jnp.concatenate([sf_ref[n], sb_ref[n]], axis=0))
        return carry

    lax.fori_loop(0, nchunk, mix_body, 0, unroll=unroll)

    def norm_body(n, carry):
        o = o_ref[n]
        o2 = o * o
        hi = o2.astype(BF16)
        lo = (o2 - hi.astype(F32)).astype(BF16)
        ms = _dot(jnp.concatenate([hi, lo], axis=1), jnp.concatenate([bd_mean, bd_mean], axis=0))
        z = rows(sg_ref, n).astype(F32) * (o * lax.rsqrt(ms + EPS))
        z_ref[0, pl.ds(pl.multiple_of(n * c, c), c), :] = z.astype(BF16)
        return carry

    lax.fori_loop(0, nchunk, norm_body, 0, unroll=unroll)


def _retention(q, kt, v, sg, kct, vc, log_gamma):
    b, l, rw = q.shape
    lc = vc.shape[1]
    c = RET_CHUNK
    gw = GROUP_W
    hpg = HEADS_PER_GROUP
    ng = rw // gw
    nchunk = l // c
    assert c & (c - 1) == 0
    lg = log_gamma.reshape(2, ng, hpg).transpose(1, 0, 2)
    lg_lane = jnp.repeat(lg, RET_HEAD_DIM, axis=2)
    lg_col = jnp.repeat(lg, c, axis=2)
    tok = pl.BlockSpec((1, l, gw), lambda i, j: (i, 0, j))
    tokt = pl.BlockSpec((1, gw, l), lambda i, j: (i, j, 0))
    grp = lambda r, width: pl.BlockSpec((None, r, width), lambda i, j: (j, 0, 0))
    return pl.pallas_call(
        functools.partial(_ret_kernel, nchunk=nchunk, unroll=math.gcd(nchunk, RET_UNROLL)),
        grid=(b, ng),
        in_specs=[
            tok, tokt, tok, tok,
            pl.BlockSpec((1, gw, lc), lambda i, j: (i, j, 0)),
            pl.BlockSpec((1, lc, gw), lambda i, j: (i, 0, j)),
            grp(2, gw), grp(2, hpg * c),
        ],
        out_specs=tok,
        out_shape=jax.ShapeDtypeStruct((b, l, rw), BF16),
        scratch_shapes=[pltpu.VMEM((nchunk, gw, gw), BF16), pltpu.VMEM((nchunk, gw, gw), BF16),
                        pltpu.VMEM((gw, gw), F32), pltpu.VMEM((gw, gw), F32),
                        pltpu.VMEM((nchunk, c, hpg * c), BF16),
                        pltpu.VMEM((nchunk, c, gw), F32),
                        pltpu.VMEM((c, hpg * c), F32)] + [pltpu.VMEM((c, gw), F32)] * 4,
        compiler_params=_params("arbitrary", "arbitrary"),
        name="ret",
    )(q, kt, v, sg, kct, vc, lg_lane, lg_col)


def _out_kernel(x_ref, mod_ref, fm_ref, z_ref, ga_ref, gb_ref, gain2_ref, fgain_ref,
                w4_ref, wr_ref, wo_ref, w1_ref, w2_ref, o_ref, *, ff_chunk):
    mod = mod_ref[:, pl.ds(pl.program_id(0), 1), :]
    g1, sh2, sc2, g2 = mod[2], mod[3], mod[4], mod[5]
    y_four = _dot(fm_ref[0], w4_ref[...])
    y_ret = _dot(z_ref[0], wr_ref[...])
    y = (jax.nn.sigmoid(ga_ref[0].astype(F32)) * y_four
         + jax.nn.sigmoid(gb_ref[0].astype(F32)) * y_ret)
    x1 = x_ref[0] + g1 * _dot(y.astype(BF16), wo_ref[...])
    h2 = _norm_mod(x1, gain2_ref[...], sh2, sc2).astype(BF16)
    dff = w1_ref.shape[1]
    acc = None
    for lo in range(0, dff, ff_chunk):
        hid = jnp.maximum(_dot(h2, w1_ref[:, lo:lo + ff_chunk]), 0.0)
        part = _dot((hid * hid).astype(BF16), w2_ref[lo:lo + ff_chunk, :])
        acc = part if acc is None else acc + part
    x2 = x1 + g2 * acc
    ms = jnp.mean(x2 * x2, axis=-1, keepdims=True)
    o_ref[0] = x2 * lax.rsqrt(ms + EPS) * fgain_ref[...]


def _out(x, mods, fm, z, ga, gb, w4, wr, wo, gain2, w1, w2, fgain):
    b, l, d = x.shape
    tm = min(TOKEN_TILE, l)
    tok = lambda width: pl.BlockSpec((1, tm, width), lambda i, j: (i, j, 0))
    weights = (w4, wr, wo, w1, w2)
    return pl.pallas_call(
        functools.partial(_out_kernel, ff_chunk=min(1024, w1.shape[1])),
        grid=(b, l // tm),
        in_specs=[
            tok(d),
            _const_spec(mods.shape),
            tok(fm.shape[2]), tok(z.shape[2]), tok(d), tok(d),
            _const_spec((1, d)), _const_spec((1, d)),
        ] + [_const_spec(w.shape) for w in weights],
        out_specs=tok(d),
        out_shape=jax.ShapeDtypeStruct((b, l, d), F32),
        compiler_params=_params("arbitrary", "arbitrary"),
        name="out",
    )(x, mods, fm, z, ga, gb, gain2, fgain, *weights)


def _dft_tables(l, gd):
    la = FFT_LA
    lb = l // la

    def cs(n):
        idx = np.arange(n)
        ang = 2.0 * np.pi * ((idx[:, None] * idx[None, :]) % n) / n
        return np.cos(ang), np.sin(ang)

    cb, sb = cs(lb)
    eye = np.eye(FFT_LA_TILE)
    kw = np.concatenate([np.kron(cb, eye), -np.kron(sb, eye)], axis=0)
    ca, sa = cs(la)
    w2 = np.block([[ca, sa], [-sa, ca]])
    cc, sc = cs(gd)
    chan = np.concatenate([cc, sc], axis=0)
    tw = 2.0 * np.pi * (np.arange(lb)[:, None] * np.arange(la)[None, :]) / l
    tw = tw.reshape(lb, la // FFT_LA_TILE, FFT_LA_TILE).transpose(1, 0, 2).reshape(la // FFT_LA_TILE, -1)
    twc = np.repeat(np.cos(tw)[:, :, None], LANES, axis=2)
    tws = np.repeat(np.sin(tw)[:, :, None], LANES, axis=2)
    as_bf = lambda a: jnp.asarray(a, dtype=F32).astype(BF16)
    return as_bf(kw), as_bf(w2), as_bf(chan), jnp.asarray(twc, F32), jnp.asarray(tws, F32), la, lb


def _rope_tables(l):
    f32 = np.float32
    nf = RET_HEAD_DIM // 4
    inv = np.power(f32(ROPE_BASE), -np.arange(nf, dtype=f32) / f32(nf)).astype(f32)
    rows = l // GRID_W
    r, cc = np.meshgrid(np.arange(rows, dtype=f32), np.arange(GRID_W, dtype=f32), indexing="ij")
    ang = np.concatenate([r.reshape(-1)[:, None] * inv, cc.reshape(-1)[:, None] * inv], axis=-1).astype(f32)
    cos, sin = np.cos(ang).astype(f32), np.sin(ang).astype(f32)
    cos_h = np.concatenate([cos, cos], axis=1)
    sin_h = np.concatenate([-sin, sin], axis=1)
    reps = LANES // RET_HEAD_DIM
    return jnp.asarray(np.tile(cos_h, (1, reps))), jnp.asarray(np.tile(sin_h, (1, reps)))


def kernel(x, c, ctx, c_ctx, w_ada, b_ada, norm1_gain, w_in, four_w_out, ret_decay_logit,
           ret_gn_gain, ret_w_out, w_out, norm2_gain, w_mlp1, w_mlp2, final_gain):
    assert w_ada.shape[0] == 1, "single-layer block"
    b, l, d = x.shape
    lc = ctx.shape[1]
    fw = four_w_out.shape[1]
    rw = ret_w_out.shape[1]
    gd = fw // FOUR_GROUPS
    assert l % FFT_LA == 0 and l % RET_CHUNK == 0 and rw % GROUP_W == 0

    mods = _ada(c, c_ctx, w_ada[0], b_ada)

    kw, w2, chan, twc, tws, la, lb = _dft_tables(l, gd)
    cosf, sinf = _rope_tables(l)

    later = [four_w_out[0], ret_w_out[0], w_out[0], w_mlp1[0], w_mlp2[0]]
    (u, q, kt, v, sg, ga, gb), later_b = _proj(x, mods, norm1_gain, w_in[0], cosf, sinf, ret_gn_gain,
                                                later, fw, rw)
    kct, vc = _ctxproj(ctx, mods, norm1_gain, w_in[0], fw, rw)

    fm = _fft(u, kw, twc, tws, w2, chan, la, lb, gd, 1.0 / math.sqrt(l * gd))

    log_gamma = jax.nn.log_sigmoid(ret_decay_logit[0].astype(F32))
    z = _retention(q, kt, v, sg, kct, vc, log_gamma)

    w4, wr, wo, w1, w2 = later_b
    return _out(x, mods, fm, z, ga, gb, w4, wr, wo, norm2_gain, w1, w2, final_gain[None, :])
```

```python
import functools
import math

import jax
import jax.numpy as jnp
import numpy as np
from jax import lax
from jax.experimental import pallas as pl
from jax.experimental.pallas import tpu as pltpu

F32 = jnp.float32
BF16 = jnp.bfloat16

GRID_W = 64
FOUR_GROUPS = 4
RET_HEAD_DIM = 64
N_MOD = 6
ROPE_BASE = 10000.0
EPS = 1e-6

LANES = 128
MXU_DIM = 256
F32_SUBLANES = 8
BF16_SUBLANES = 16
VMEM_LIMIT_BYTES = 56 * 1024 * 1024

RET_CHUNK = 128
RET_UNROLL = 32
HEADS_PER_GROUP = MXU_DIM // RET_HEAD_DIM
GROUP_W = HEADS_PER_GROUP * RET_HEAD_DIM
FFT_LA = 128
FFT_LA_TILE = BF16_SUBLANES
FFT_SCATTER_PAD = F32_SUBLANES
TOKEN_TILE = 512
PROJ_TOKEN_TILE = 1024
WEIGHT_STAGE_BYTES = 2 * 1024 * 1024
WEIGHT_STAGE_SLOTS = 4


def _dot(a, b):
    return jnp.dot(a, b, preferred_element_type=F32)


def _norm_mod(x, gain, shift, scale):
    ms = jnp.mean(x * x, axis=-1, keepdims=True)
    y = x * lax.rsqrt(ms + EPS) * gain
    return y * (1.0 + scale) + shift


def _tile_lanes(t, reps):
    return jnp.concatenate([t] * reps, axis=1) if reps > 1 else t


def _const_spec(shape):
    nd = len(shape)
    return pl.BlockSpec(shape, lambda *_: (0,) * nd, pipeline_mode=pl.Buffered(1))


def _params(*sem):
    return pltpu.CompilerParams(dimension_semantics=sem, vmem_limit_bytes=VMEM_LIMIT_BYTES)


def _load_weights_as_bf16(pairs, stage_ref, sem_ref):
    slots, sr, sc = stage_ref.shape
    chunks = [(src, dst, r0, c0)
              for src, dst in pairs
              for r0 in range(0, src.shape[0], sr)
              for c0 in range(0, src.shape[1], sc)]

    def copy(k):
        src, _, r0, c0 = chunks[k]
        slot = k % slots
        return pltpu.make_async_copy(src.at[pl.ds(r0, sr), pl.ds(c0, sc)], stage_ref.at[slot], sem_ref.at[slot])

    for k in range(min(slots - 1, len(chunks))):
        copy(k).start()
    for k, (_, dst, r0, c0) in enumerate(chunks):
        if k + slots - 1 < len(chunks):
            copy(k + slots - 1).start()
        copy(k).wait()
        dst[r0:r0 + sr, c0:c0 + sc] = stage_ref[k % slots].astype(BF16)


def _is_first_step(grid_rank):
    first = pl.program_id(0) == 0
    for axis in range(1, grid_rank):
        first = first & (pl.program_id(axis) == 0)
    return first


def _ada_kernel(c_ref, cctx_ref, w_ref, b_ref, o_ref, cv_ref):
    b = c_ref.shape[0]
    cv_ref[...] = jnp.zeros_like(cv_ref)
    cv_ref[0:b, :] = c_ref[...]
    cv_ref[b:b + 1, :] = cctx_ref[...]
    c = cv_ref[...]
    s = c * jax.nn.sigmoid(c)
    o_ref[0] = _dot(s.astype(BF16), w_ref[...].astype(BF16)) + b_ref[...]


def _ada(c, c_ctx, w_ada, b_ada):
    b, d = c.shape
    rows = -(-(b + 1) // F32_SUBLANES) * F32_SUBLANES
    assert w_ada.shape[1] == N_MOD * d
    return pl.pallas_call(
        _ada_kernel,
        grid=(N_MOD,),
        in_specs=[
            pl.BlockSpec((b, d), lambda j: (0, 0)),
            pl.BlockSpec((1, d), lambda j: (0, 0)),
            pl.BlockSpec((d, d), lambda j: (0, j)),
            pl.BlockSpec((1, d), lambda j: (0, j)),
        ],
        out_specs=pl.BlockSpec((1, rows, d), lambda j: (j, 0, 0)),
        out_shape=jax.ShapeDtypeStruct((N_MOD, rows, d), F32),
        scratch_shapes=[pltpu.VMEM((rows, d), F32)],
        compiler_params=_params("arbitrary"),
        name="ada",
    )(c, c_ctx[None, :], w_ada, b_ada)


def _proj_kernel(*refs, fw, rw, d, scale, ncast):
    x_ref, mod_ref, gain_ref, cos_ref, sin_ref, gng_ref, w_hbm = refs[:7]
    cast_in = refs[7:7 + ncast]
    u_ref, q_ref, kt_ref, v_ref, sg_ref, ga_ref, gb_ref = refs[7 + ncast:14 + ncast]
    cast_out = refs[14 + ncast:14 + 2 * ncast]
    w_ref, stage_ref, sem_ref = refs[14 + 2 * ncast:]

    @pl.when(_is_first_step(2))
    def _():
        _load_weights_as_bf16([(w_hbm, w_ref)], stage_ref, sem_ref)

    for src, dst in zip(cast_in, cast_out):
        dst[...] = src[...].astype(BF16)

    mod = mod_ref[:, pl.ds(pl.program_id(0), 1), :]
    h = _norm_mod(x_ref[0], gain_ref[...], mod[0], mod[1]).astype(BF16)
    tm = h.shape[0]

    def proj(lo, width):
        return _dot(h, w_ref[:, lo:lo + width])

    u_ref[0] = proj(0, fw).astype(BF16)

    reps = rw // cos_ref.shape[1]
    cosf = _tile_lanes(cos_ref[...], reps)
    sinf = _tile_lanes(sin_ref[...], reps)
    lane = lax.broadcasted_iota(jnp.int32, (tm, rw), 1)
    first_half = (lane & (RET_HEAD_DIM // 2)) == 0

    def rope(t):
        rot = jnp.where(first_half,
                        pltpu.roll(t, rw - RET_HEAD_DIM // 2, 1),
                        pltpu.roll(t, RET_HEAD_DIM // 2, 1))
        return t * cosf + rot * sinf

    q_ref[0] = rope(proj(fw, rw)).astype(BF16)
    kt_ref[0] = (rope(proj(fw + rw, rw)) * scale).T.astype(BF16)
    v_ref[0] = proj(fw + 2 * rw, rw).astype(BF16)
    g = proj(fw + 3 * rw, rw)
    sg_ref[0] = (g * jax.nn.sigmoid(g) * gng_ref[...]).astype(BF16)
    ga_ref[0] = proj(fw + 4 * rw, d).astype(BF16)
    gb_ref[0] = proj(fw + 4 * rw + d, d).astype(BF16)


def _proj(x, mods, gain, w_in, cosf, sinf, gn_gain, later_weights, fw, rw):
    b, l, d = x.shape
    tm = min(PROJ_TOKEN_TILE, l)
    nj = l // tm
    steps = b * nj
    tok = lambda width: pl.BlockSpec((1, tm, width), lambda i, j: (i, j, 0))
    out = lambda width: jax.ShapeDtypeStruct((b, l, width), BF16)
    sliced = [w.reshape(steps, w.shape[0] // steps, w.shape[1]) for w in later_weights]
    for w in sliced:
        assert w.shape[1] % BF16_SUBLANES == 0
    cast_spec = lambda w: pl.BlockSpec((1,) + w.shape[1:], lambda i, j: (i * nj + j, 0, 0))
    stage_rows = max(r for r in range(8, w_in.shape[0] + 1, 8)
                     if w_in.shape[0] % r == 0 and r * w_in.shape[1] * 4 <= WEIGHT_STAGE_BYTES)
    res = pl.pallas_call(
        functools.partial(_proj_kernel, fw=fw, rw=rw, d=d, scale=RET_HEAD_DIM ** -0.5, ncast=len(sliced)),
        grid=(b, nj),
        in_specs=[
            tok(d),
            _const_spec(mods.shape),
            _const_spec((1, d)),
            pl.BlockSpec((tm, cosf.shape[1]), lambda i, j: (j, 0)),
            pl.BlockSpec((tm, sinf.shape[1]), lambda i, j: (j, 0)),
            _const_spec((1, rw)),
            pl.BlockSpec(memory_space=pl.ANY),
        ] + [cast_spec(w) for w in sliced],
        out_specs=[tok(fw), tok(rw), pl.BlockSpec((1, rw, tm), lambda i, j: (i, 0, j)),
                   tok(rw), tok(rw), tok(d), tok(d)] + [cast_spec(w) for w in sliced],
        out_shape=[out(fw), out(rw), jax.ShapeDtypeStruct((b, rw, l), BF16),
                   out(rw), out(rw), out(d), out(d)]
        + [jax.ShapeDtypeStruct(w.shape, BF16) for w in sliced],
        scratch_shapes=[pltpu.VMEM(w_in.shape, BF16),
                        pltpu.VMEM((WEIGHT_STAGE_SLOTS, stage_rows, w_in.shape[1]), F32),
                        pltpu.SemaphoreType.DMA((WEIGHT_STAGE_SLOTS,))],
        compiler_params=_params("arbitrary", "arbitrary"),
        name="proj",
    )(x, mods, gain, cosf, sinf, gn_gain, w_in, *sliced)
    casts = [c.reshape(w.shape) for c, w in zip(res[7:], later_weights)]
    return res[:7], casts


def _ctxproj_kernel(c_ref, mod_ref, gain_ref, w_ref, kt_ref, v_ref, *, rw, scale, mod_row):
    mod = mod_ref[:, mod_row:mod_row + 1, :]
    h = _norm_mod(c_ref[0], gain_ref[...], mod[0], mod[1]).astype(BF16)
    kt_ref[0] = (_dot(h, w_ref[:, :rw].astype(BF16)) * scale).T.astype(BF16)
    v_ref[0] = _dot(h, w_ref[:, rw:].astype(BF16)).astype(BF16)


def _ctxproj(ctx, mods, gain, w_in, fw, rw):
    b, lc, d = ctx.shape
    kv_block = (fw + rw) // (2 * rw)
    assert kv_block * 2 * rw == fw + rw
    return pl.pallas_call(
        functools.partial(_ctxproj_kernel, rw=rw, scale=RET_HEAD_DIM ** -0.5, mod_row=b),
        grid=(b,),
        in_specs=[
            pl.BlockSpec((1, lc, d), lambda i: (i, 0, 0)),
            _const_spec(mods.shape),
            pl.BlockSpec((1, d), lambda i: (0, 0)),
            pl.BlockSpec((d, 2 * rw), lambda i: (0, kv_block)),
        ],
        out_specs=[pl.BlockSpec((1, rw, lc), lambda i: (i, 0, 0)),
                   pl.BlockSpec((1, lc, rw), lambda i: (i, 0, 0))],
        out_shape=[jax.ShapeDtypeStruct((b, rw, lc), BF16), jax.ShapeDtypeStruct((b, lc, rw), BF16)],
        compiler_params=_params("arbitrary"),
        name="ctxproj",
    )(ctx, mods, gain, w_in)


def _fft_kernel(u_ref, kw_ref, twc_ref, tws_ref, w2_ref, cs_ref, o_ref, tr_ref, ti_ref, scr_ref,
                *, la, lb, gd, npb, scale):
    r = FFT_LA_TILE
    rows = lb * r
    fw = u_ref.shape[2]
    ngroups = fw // gd
    reps = fw // twc_ref.shape[2]
    pitch = scr_ref.shape[1] // la

    @pl.when(pl.program_id(0) == 0)
    def _():
        scr_ref[...] = jnp.zeros_like(scr_ref)

    for j in range(la // r):
        u = jnp.concatenate([u_ref[0, k * la + j * r:k * la + (j + 1) * r, :] for k in range(lb)], axis=0)
        half = lb // 2 + 1
        t = _dot(kw_ref[...], u)
        mirror = [lb - k for k in range(half, lb)]
        a = jnp.concatenate([t[:half * r]] + [t[k * r:(k + 1) * r] for k in mirror], axis=0)
        b = jnp.concatenate([t[half * r:]] + [-t[(half + k) * r:(half + k + 1) * r] for k in mirror], axis=0)
        ct = _tile_lanes(twc_ref[j], reps)
        st = _tile_lanes(tws_ref[j], reps)
        tr = (a * ct + b * st).astype(BF16)
        ti = (b * ct - a * st).astype(BF16)
        for k in range(lb):
            tr_ref[k * la + j * r:k * la + (j + 1) * r, :] = tr[k * r:(k + 1) * r]
            ti_ref[k * la + j * r:k * la + (j + 1) * r, :] = ti[k * r:(k + 1) * r]

    for pb in range(lb // npb):
        xs = []
        for p in range(npb):
            lo = (pb * npb + p) * la
            t = jnp.concatenate([tr_ref[lo:lo + la, :], ti_ref[lo:lo + la, :]], axis=0)
            xs.append(_dot(w2_ref[...], t).astype(BF16))
        for gi in range(ngroups):
            cols = slice(gi * gd, (gi + 1) * gd)
            lhs = jnp.concatenate([jnp.concatenate([x[:la, cols], x[la:, cols]], axis=1) for x in xs], axis=0)
            fg = _dot(lhs, cs_ref[...]) * scale
            for p in range(npb):
                scr_ref[gi, pl.ds(p, la, stride=pitch), :] = fg[p * la:(p + 1) * la]
        full = jnp.concatenate([scr_ref[gi].reshape(la, pitch, gd)[:, :npb, :] for gi in range(ngroups)], axis=2)
        o_ref[0, :, pb * npb:(pb + 1) * npb, :] = full.astype(BF16)


def _fft(u, kw, twc, tws, w2, cs, la, lb, gd, scale):
    b, l, fw = u.shape
    npb = min(BF16_SUBLANES, lb)
    out = pl.pallas_call(
        functools.partial(_fft_kernel, la=la, lb=lb, gd=gd, npb=npb, scale=scale),
        grid=(b,),
        in_specs=[pl.BlockSpec((1, l, fw), lambda i: (i, 0, 0)),
                  _const_spec(kw.shape), _const_spec(twc.shape), _const_spec(tws.shape),
                  _const_spec(w2.shape), _const_spec(cs.shape)],
        out_specs=pl.BlockSpec((1, la, lb, fw), lambda i: (i, 0, 0, 0)),
        out_shape=jax.ShapeDtypeStruct((b, la, lb, fw), BF16),
        scratch_shapes=[pltpu.VMEM((l, fw), BF16), pltpu.VMEM((l, fw), BF16),
                        pltpu.VMEM((fw // gd, la * (npb + FFT_SCATTER_PAD), gd), F32)],
        compiler_params=_params("arbitrary"),
        name="fft",
    )(u, kw, twc, tws, w2, cs)
    return out.reshape(b, l, fw)


def _ret_kernel(q_ref, kt_ref, v_ref, sg_ref, kct_ref, vc_ref, lgl_ref, lgc_ref,
                z_ref, sf_ref, sb_ref, stf_ref, stb_ref, p_ref, o_ref,
                dall_ref, qdf_ref, qdb_ref, kdf_ref, kdb_ref, *, nchunk, unroll):
    c = RET_CHUNK
    gw = GROUP_W
    lc = vc_ref.shape[1]

    lgf, lgb = lgl_ref[0:1, :], lgl_ref[1:2, :]
    pos = lax.broadcasted_iota(jnp.int32, (c, gw), 0).astype(F32)
    qdf_ref[...] = jnp.exp(lgf * (pos + 1.0))
    qdb_ref[...] = jnp.exp(lgb * (c - pos))
    kdf_ref[...] = jnp.exp(lgf * (c - 1.0 - pos))
    kdb_ref[...] = jnp.exp(lgb * pos)
    cdf = jnp.exp(lgf * c)
    cdb = jnp.exp(lgb * c)
    cpos = lax.broadcasted_iota(jnp.int32, (lc, gw), 0).astype(F32)
    wcf = jnp.exp(lgf * (lc - 1.0 - cpos))
    wcb = jnp.exp(lgb * cpos)
    si = lax.broadcasted_iota(jnp.int32, (c, HEADS_PER_GROUP * c), 0)
    sj = lax.broadcasted_iota(jnp.int32, (c, HEADS_PER_GROUP * c), 1) & (c - 1)
    diff = (si - sj).astype(F32)
    dall_ref[...] = (jnp.where(diff >= 0, jnp.exp(lgc_ref[0:1, :] * jnp.maximum(diff, 0.0)), 0.0)
                     + jnp.where(diff <= 0, jnp.exp(lgc_ref[1:2, :] * jnp.maximum(-diff, 0.0)), 0.0))

    same_head = (lax.broadcasted_iota(jnp.int32, (gw, gw), 0) // RET_HEAD_DIM
                 == lax.broadcasted_iota(jnp.int32, (gw, gw), 1) // RET_HEAD_DIM)

    def diag_blocks(t):
        return jnp.where(same_head, t, 0.0)

    def weighted(t, w):
        return (t.astype(F32) * w).astype(BF16)

    def rows(ref, n):
        return ref[0, pl.ds(pl.multiple_of(n * c, c), c), :]

    def kt_chunk(n):
        return kt_ref[0, :, pl.ds(pl.multiple_of(n * c, c), c)]

    kct = kct_ref[0]
    vc = vc_ref[0]
    stf_ref[...] = diag_blocks(_dot(kct, weighted(vc, wcf)))
    stb_ref[...] = diag_blocks(_dot(kct, weighted(vc, wcb)))

    def scan_body(i, carry):
        nf = i
        nb = nchunk - 1 - i
        s = stf_ref[...]
        sf_ref[nf] = s.astype(BF16)
        stf_ref[...] = s * cdf + diag_blocks(_dot(kt_chunk(nf), weighted(rows(v_ref, nf), kdf_ref[...])))
        s = stb_ref[...]
        sb_ref[nb] = s.astype(BF16)
        stb_ref[...] = s * cdb + diag_blocks(_dot(kt_chunk(nb), weighted(rows(v_ref, nb), kdb_ref[...])))
        return carry

    lax.fori_loop(0, nchunk, scan_body, 0, unroll=unroll)

    lane = lax.broadcasted_iota(jnp.int32, (c, gw), 1)
    sub = lax.broadcasted_iota(jnp.int32, (gw, c), 0)
    lane_masks = [(lane >= h * RET_HEAD_DIM) & (lane < (h + 1) * RET_HEAD_DIM)
                  for h in range(HEADS_PER_GROUP)]
    sub_masks = [(sub >= h * RET_HEAD_DIM) & (sub < (h + 1) * RET_HEAD_DIM)
                 for h in range(HEADS_PER_GROUP)]
    bd_mean = jnp.where(same_head, 1.0 / RET_HEAD_DIM, 0.0).astype(BF16)
    bd_mean2 = jnp.concatenate([bd_mean, bd_mean], axis=0)

    def score_body(n, carry):
        ktn = kt_chunk(n)
        zk = jnp.zeros_like(ktn)
        kbd = jnp.concatenate([jnp.where(m, ktn, zk) for m in sub_masks], axis=1)
        p_ref[n] = (_dot(rows(q_ref, n), kbd) * dall_ref[...]).astype(BF16)
        return carry

    lax.fori_loop(0, nchunk, score_body, 0, unroll=unroll)

    def mix_body(n, carry):
        qn = rows(q_ref, n)
        vn = rows(v_ref, n)
        zv = jnp.zeros_like(vn)
        vbd = jnp.concatenate([jnp.where(m, vn, zv) for m in lane_masks], axis=0)
        qfb = jnp.concatenate([weighted(qn, qdf_ref[...]), weighted(qn, qdb_ref[...])], axis=1)
        o_ref[n] = _dot(p_ref[n], vbd) + _dot(qfb, jnp.concatenate([sf_ref[n], sb_ref[n]], axis=0))
        return carry

    lax.fori_loop(0, nchunk, mix_body, 0, unroll=unroll)

    def norm_body(n, carry):
        o = o_ref[n]
        o2 = o * o
        hi = o2.astype(BF16)
        lo = (o2 - hi.astype(F32)).astype(BF16)
        ms = _dot(jnp.concatenate([hi, lo], axis=1), bd_mean2)
        z = rows(sg_ref, n).astype(F32) * (o * lax.rsqrt(ms + EPS))
        z_ref[0, pl.ds(pl.multiple_of(n * c, c), c), :] = z.astype(BF16)
        return carry

    lax.fori_loop(0, nchunk, norm_body, 0, unroll=unroll)


def _retention(q, kt, v, sg, kct, vc, log_gamma):
    b, l, rw = q.shape
    lc = vc.shape[1]
    c = RET_CHUNK
    gw = GROUP_W
    hpg = HEADS_PER_GROUP
    ng = rw // gw
    nchunk = l // c
    assert c & (c - 1) == 0
    lg = log_gamma.reshape(2, ng, hpg).transpose(1, 0, 2)
    lg_lane = jnp.repeat(lg, RET_HEAD_DIM, axis=2)
    lg_col = jnp.repeat(lg, c, axis=2)
    tok = pl.BlockSpec((1, l, gw), lambda i, j: (i, 0, j))
    tokt = pl.BlockSpec((1, gw, l), lambda i, j: (i, j, 0))
    grp = lambda r, width: pl.BlockSpec((None, r, width), lambda i, j: (j, 0, 0))
    return pl.pallas_call(
        functools.partial(_ret_kernel, nchunk=nchunk, unroll=math.gcd(nchunk, RET_UNROLL)),
        grid=(b, ng),
        in_specs=[
            tok, tokt, tok, tok,
            pl.BlockSpec((1, gw, lc), lambda i, j: (i, j, 0)),
            pl.BlockSpec((1, lc, gw), lambda i, j: (i, 0, j)),
            grp(2, gw), grp(2, hpg * c),
        ],
        out_specs=tok,
        out_shape=jax.ShapeDtypeStruct((b, l, rw), BF16),
        scratch_shapes=[pltpu.VMEM((nchunk, gw, gw), BF16), pltpu.VMEM((nchunk, gw, gw), BF16),
                        pltpu.VMEM((gw, gw), F32), pltpu.VMEM((gw, gw), F32),
                        pltpu.VMEM((nchunk, c, hpg * c), BF16),
                        pltpu.VMEM((nchunk, c, gw), F32),
                        pltpu.VMEM((c, hpg * c), F32)] + [pltpu.VMEM((c, gw), F32)] * 4,
        compiler_params=_params("arbitrary", "arbitrary"),
        name="ret",
    )(q, kt, v, sg, kct, vc, lg_lane, lg_col)


def _out_kernel(x_ref, mod_ref, fm_ref, z_ref, ga_ref, gb_ref, gain2_ref, fgain_ref,
                w4_ref, wr_ref, wo_ref, w1_ref, w2_ref, o_ref, *, ff_chunk):
    mod = mod_ref[:, pl.ds(pl.program_id(0), 1), :]
    g1, sh2, sc2, g2 = mod[2], mod[3], mod[4], mod[5]
    y_four = _dot(fm_ref[0], w4_ref[...])
    y_ret = _dot(z_ref[0], wr_ref[...])
    y = (jax.nn.sigmoid(ga_ref[0].astype(F32)) * y_four
         + jax.nn.sigmoid(gb_ref[0].astype(F32)) * y_ret)
    x1 = x_ref[0] + g1 * _dot(y.astype(BF16), wo_ref[...])
    h2 = _norm_mod(x1, gain2_ref[...], sh2, sc2).astype(BF16)
    dff = w1_ref.shape[1]
    acc = None
    for lo in range(0, dff, ff_chunk):
        hid = jnp.maximum(_dot(h2, w1_ref[:, lo:lo + ff_chunk]), 0.0)
        part = _dot((hid * hid).astype(BF16), w2_ref[lo:lo + ff_chunk, :])
        acc = part if acc is None else acc + part
    x2 = x1 + g2 * acc
    ms = jnp.mean(x2 * x2, axis=-1, keepdims=True)
    o_ref[0] = x2 * lax.rsqrt(ms + EPS) * fgain_ref[...]


def _out(x, mods, fm, z, ga, gb, w4, wr, wo, gain2, w1, w2, fgain):
    b, l, d = x.shape
    tm = min(TOKEN_TILE, l)
    tok = lambda width: pl.BlockSpec((1, tm, width), lambda i, j: (i, j, 0))
    weights = (w4, wr, wo, w1, w2)
    return pl.pallas_call(
        functools.partial(_out_kernel, ff_chunk=min(1024, w1.shape[1])),
        grid=(b, l // tm),
        in_specs=[
            tok(d),
            _const_spec(mods.shape),
            tok(fm.shape[2]), tok(z.shape[2]), tok(d), tok(d),
            _const_spec((1, d)), _const_spec((1, d)),
        ] + [_const_spec(w.shape) for w in weights],
        out_specs=tok(d),
        out_shape=jax.ShapeDtypeStruct((b, l, d), F32),
        compiler_params=_params("arbitrary", "arbitrary"),
        name="out",
    )(x, mods, fm, z, ga, gb, gain2, fgain, *weights)


def _dft_tables(l, gd):
    la = FFT_LA
    lb = l // la

    def cs(n):
        idx = np.arange(n)
        ang = 2.0 * np.pi * ((idx[:, None] * idx[None, :]) % n) / n
        return np.cos(ang), np.sin(ang)

    cb, sb = cs(lb)
    eye = np.eye(FFT_LA_TILE)
    half = lb // 2 + 1
    kw = np.concatenate([np.kron(cb[:half], eye), -np.kron(sb[:half], eye)], axis=0)
    ca, sa = cs(la)
    w2 = np.block([[ca, sa], [-sa, ca]])
    cc, sc = cs(gd)
    chan = np.concatenate([cc, sc], axis=0)
    tw = 2.0 * np.pi * (np.arange(lb)[:, None] * np.arange(la)[None, :]) / l
    tw = tw.reshape(lb, la // FFT_LA_TILE, FFT_LA_TILE).transpose(1, 0, 2).reshape(la // FFT_LA_TILE, -1)
    twc = np.repeat(np.cos(tw)[:, :, None], LANES, axis=2)
    tws = np.repeat(np.sin(tw)[:, :, None], LANES, axis=2)
    as_bf = lambda a: jnp.asarray(a, dtype=F32).astype(BF16)
    return as_bf(kw), as_bf(w2), as_bf(chan), jnp.asarray(twc, F32), jnp.asarray(tws, F32), la, lb


def _rope_tables(l):
    f32 = np.float32
    nf = RET_HEAD_DIM // 4
    inv = np.power(f32(ROPE_BASE), -np.arange(nf, dtype=f32) / f32(nf)).astype(f32)
    rows = l // GRID_W
    r, cc = np.meshgrid(np.arange(rows, dtype=f32), np.arange(GRID_W, dtype=f32), indexing="ij")
    ang = np.concatenate([r.reshape(-1)[:, None] * inv, cc.reshape(-1)[:, None] * inv], axis=-1).astype(f32)
    cos, sin = np.cos(ang).astype(f32), np.sin(ang).astype(f32)
    cos_h = np.concatenate([cos, cos], axis=1)
    sin_h = np.concatenate([-sin, sin], axis=1)
    reps = LANES // RET_HEAD_DIM
    return jnp.asarray(np.tile(cos_h, (1, reps))), jnp.asarray(np.tile(sin_h, (1, reps)))


def kernel(x, c, ctx, c_ctx, w_ada, b_ada, norm1_gain, w_in, four_w_out, ret_decay_logit,
           ret_gn_gain, ret_w_out, w_out, norm2_gain, w_mlp1, w_mlp2, final_gain):
    assert w_ada.shape[0] == 1, "single-layer block"
    b, l, d = x.shape
    lc = ctx.shape[1]
    fw = four_w_out.shape[1]
    rw = ret_w_out.shape[1]
    gd = fw // FOUR_GROUPS
    assert l % FFT_LA == 0 and l % RET_CHUNK == 0 and rw % GROUP_W == 0

    mods = _ada(c, c_ctx, w_ada[0], b_ada)

    kw, w2, chan, twc, tws, la, lb = _dft_tables(l, gd)
    cosf, sinf = _rope_tables(l)

    later = [four_w_out[0], ret_w_out[0], w_out[0], w_mlp1[0], w_mlp2[0]]
    (u, q, kt, v, sg, ga, gb), later_b = _proj(x, mods, norm1_gain, w_in[0], cosf, sinf, ret_gn_gain,
                                                later, fw, rw)
    kct, vc = _ctxproj(ctx, mods, norm1_gain, w_in[0], fw, rw)

    fm = _fft(u, kw, twc, tws, w2, chan, la, lb, gd, 1.0 / math.sqrt(l * gd))

    log_gamma = jax.nn.log_sigmoid(ret_decay_logit[0].astype(F32))
    z = _retention(q, kt, v, sg, kct, vc, log_gamma)

    w4, wr, wo, w1, w2 = later_b
    return _out(x, mods, fm, z, ga, gb, w4, wr, wo, norm2_gain, w1, w2, final_gain[None, :])
```

```python
import functools
import math

import jax
import jax.numpy as jnp
import numpy as np
from jax import lax
from jax.experimental import pallas as pl
from jax.experimental.pallas import tpu as pltpu

F32 = jnp.float32
BF16 = jnp.bfloat16

GRID_W = 64
FOUR_GROUPS = 4
RET_HEAD_DIM = 64
N_MOD = 6
CTX_FIRST_STEP = 2
ROPE_BASE = 10000.0
EPS = 1e-6

LANES = 128
MXU_DIM = 256
F32_SUBLANES = 8
BF16_SUBLANES = 16
VMEM_LIMIT_BYTES = 56 * 1024 * 1024

RET_CHUNK = 128
RET_UNROLL = 32
HEADS_PER_GROUP = MXU_DIM // RET_HEAD_DIM
GROUP_W = HEADS_PER_GROUP * RET_HEAD_DIM
FFT_LA = 128
FFT_LA_TILE = BF16_SUBLANES
FFT_SCATTER_PAD = F32_SUBLANES
TOKEN_TILE = 512
PROJ_TOKEN_TILE = 1024
WEIGHT_STAGE_BYTES = 2 * 1024 * 1024
WEIGHT_STAGE_SLOTS = 4


def _dot(a, b):
    return jnp.dot(a, b, preferred_element_type=F32)


def _norm_mod(x, gain, shift, scale):
    ms = jnp.mean(x * x, axis=-1, keepdims=True)
    y = x * lax.rsqrt(ms + EPS) * gain
    return y * (1.0 + scale) + shift


def _tile_lanes(t, reps):
    return jnp.concatenate([t] * reps, axis=1) if reps > 1 else t


def _const_spec(shape):
    nd = len(shape)
    return pl.BlockSpec(shape, lambda *_: (0,) * nd, pipeline_mode=pl.Buffered(1))


def _params(*sem):
    return pltpu.CompilerParams(dimension_semantics=sem, vmem_limit_bytes=VMEM_LIMIT_BYTES)


def _load_weights_as_bf16(pairs, stage_ref, sem_ref):
    slots, sr, sc = stage_ref.shape
    chunks = [(src, dst, r0, c0)
              for src, dst in pairs
              for r0 in range(0, src.shape[0], sr)
              for c0 in range(0, src.shape[1], sc)]

    def copy(k):
        src, _, r0, c0 = chunks[k]
        slot = k % slots
        return pltpu.make_async_copy(src.at[pl.ds(r0, sr), pl.ds(c0, sc)], stage_ref.at[slot], sem_ref.at[slot])

    for k in range(min(slots - 1, len(chunks))):
        copy(k).start()
    for k, (_, dst, r0, c0) in enumerate(chunks):
        if k + slots - 1 < len(chunks):
            copy(k + slots - 1).start()
        copy(k).wait()
        dst[r0:r0 + sr, c0:c0 + sc] = stage_ref[k % slots].astype(BF16)


def _is_first_step(grid_rank):
    first = pl.program_id(0) == 0
    for axis in range(1, grid_rank):
        first = first & (pl.program_id(axis) == 0)
    return first


def _ada_kernel(c_ref, cctx_ref, w_ref, b_ref, ctx_ref, gain_ref, wkv_ref,
                o_ref, kt_ref, v_ref, cv_ref, modc_ref, wkv_bf_ref, *, rw, scale):
    j = pl.program_id(0)
    b = c_ref.shape[0]
    cv_ref[...] = jnp.zeros_like(cv_ref)
    cv_ref[0:b, :] = c_ref[...]
    cv_ref[b:b + 1, :] = cctx_ref[...]
    c = cv_ref[...]
    s = c * jax.nn.sigmoid(c)
    mod = _dot(s.astype(BF16), w_ref[...].astype(BF16)) + b_ref[...]
    o_ref[0] = mod

    @pl.when(j < CTX_FIRST_STEP)
    def _():
        modc_ref[pl.ds(j, 1), :] = mod[b:b + 1, :]

    @pl.when(j == CTX_FIRST_STEP)
    def _():
        wkv_bf_ref[...] = wkv_ref[...].astype(BF16)

    @pl.when(j >= CTX_FIRST_STEP)
    def _():
        h = _norm_mod(ctx_ref[0], gain_ref[...], modc_ref[0:1, :], modc_ref[1:2, :]).astype(BF16)
        kt_ref[0] = (_dot(h, wkv_bf_ref[:, :rw]) * scale).T.astype(BF16)
        v_ref[0] = _dot(h, wkv_bf_ref[:, rw:]).astype(BF16)


def _ada(c, c_ctx, w_ada, b_ada, ctx, gain, w_in, fw, rw):
    b, d = c.shape
    lc = ctx.shape[1]
    rows = -(-(b + 1) // F32_SUBLANES) * F32_SUBLANES
    assert w_ada.shape[1] == N_MOD * d and b <= N_MOD - CTX_FIRST_STEP
    kv_block = (fw + rw) // (2 * rw)
    assert kv_block * 2 * rw == fw + rw
    sample = lambda j: jnp.clip(j - CTX_FIRST_STEP, 0, b - 1)
    return pl.pallas_call(
        functools.partial(_ada_kernel, rw=rw, scale=RET_HEAD_DIM ** -0.5),
        grid=(N_MOD,),
        in_specs=[
            pl.BlockSpec((b, d), lambda j: (0, 0)),
            pl.BlockSpec((1, d), lambda j: (0, 0)),
            pl.BlockSpec((d, d), lambda j: (0, j)),
            pl.BlockSpec((1, d), lambda j: (0, j)),
            pl.BlockSpec((1, lc, d), lambda j: (sample(j), 0, 0)),
            pl.BlockSpec((1, d), lambda j: (0, 0)),
            pl.BlockSpec((d, 2 * rw), lambda j: (0, kv_block)),
        ],
        out_specs=[pl.BlockSpec((1, rows, d), lambda j: (j, 0, 0)),
                   pl.BlockSpec((1, rw, lc), lambda j: (sample(j), 0, 0)),
                   pl.BlockSpec((1, lc, rw), lambda j: (sample(j), 0, 0))],
        out_shape=[jax.ShapeDtypeStruct((N_MOD, rows, d), F32),
                   jax.ShapeDtypeStruct((b, rw, lc), BF16), jax.ShapeDtypeStruct((b, lc, rw), BF16)],
        scratch_shapes=[pltpu.VMEM((rows, d), F32), pltpu.VMEM((CTX_FIRST_STEP, d), F32),
                        pltpu.VMEM((d, 2 * rw), BF16)],
        compiler_params=_params("arbitrary"),
        name="ada",
    )(c, c_ctx[None, :], w_ada, b_ada, ctx, gain, w_in)


def _proj_kernel(*refs, fw, rw, d, scale, ncast):
    x_ref, mod_ref, gain_ref, cos_ref, sin_ref, gng_ref, w_hbm = refs[:7]
    cast_in = refs[7:7 + ncast]
    u_ref, q_ref, kt_ref, v_ref, sg_ref, gab_ref = refs[7 + ncast:13 + ncast]
    cast_out = refs[13 + ncast:13 + 2 * ncast]
    w_ref, stage_ref, sem_ref = refs[13 + 2 * ncast:]

    @pl.when(_is_first_step(2))
    def _():
        _load_weights_as_bf16([(w_hbm, w_ref)], stage_ref, sem_ref)

    for src, dst in zip(cast_in, cast_out):
        dst[...] = src[...].astype(BF16)

    mod = mod_ref[:, pl.ds(pl.program_id(0), 1), :]
    h = _norm_mod(x_ref[0], gain_ref[...], mod[0], mod[1]).astype(BF16)
    tm = h.shape[0]

    def proj(lo, width):
        return _dot(h, w_ref[:, lo:lo + width])

    u_ref[0] = proj(0, fw).astype(BF16)

    reps = rw // cos_ref.shape[1]
    cosf = _tile_lanes(cos_ref[...], reps)
    sinf = _tile_lanes(sin_ref[...], reps)
    lane = lax.broadcasted_iota(jnp.int32, (tm, rw), 1)
    first_half = (lane & (RET_HEAD_DIM // 2)) == 0

    def rope(t):
        rot = jnp.where(first_half,
                        pltpu.roll(t, rw - RET_HEAD_DIM // 2, 1),
                        pltpu.roll(t, RET_HEAD_DIM // 2, 1))
        return t * cosf + rot * sinf

    q_ref[0] = rope(proj(fw, rw)).astype(BF16)
    kt_ref[0] = (rope(proj(fw + rw, rw)) * scale).T.astype(BF16)
    v_ref[0] = proj(fw + 2 * rw, rw).astype(BF16)
    g = proj(fw + 3 * rw, rw)
    sg_ref[0] = (g * jax.nn.sigmoid(g) * gng_ref[...]).astype(BF16)
    gab_ref[0, :, :d] = proj(fw + 4 * rw, d).astype(BF16)
    gab_ref[0, :, d:] = proj(fw + 4 * rw + d, d).astype(BF16)


def _proj(x, mods, gain, w_in, cosf, sinf, gn_gain, later_weights, fw, rw):
    b, l, d = x.shape
    tm = min(PROJ_TOKEN_TILE, l)
    nj = l // tm
    steps = b * nj
    tok = lambda width: pl.BlockSpec((1, tm, width), lambda i, j: (i, j, 0))
    out = lambda width: jax.ShapeDtypeStruct((b, l, width), BF16)
    sliced = [w.reshape(steps, w.shape[0] // steps, w.shape[1]) for w in later_weights]
    for w in sliced:
        assert w.shape[1] % BF16_SUBLANES == 0
    cast_spec = lambda w: pl.BlockSpec((1,) + w.shape[1:], lambda i, j: (i * nj + j, 0, 0))
    stage_rows = max(r for r in range(8, w_in.shape[0] + 1, 8)
                     if w_in.shape[0] % r == 0 and r * w_in.shape[1] * 4 <= WEIGHT_STAGE_BYTES)
    res = pl.pallas_call(
        functools.partial(_proj_kernel, fw=fw, rw=rw, d=d, scale=RET_HEAD_DIM ** -0.5, ncast=len(sliced)),
        grid=(b, nj),
        in_specs=[
            tok(d),
            _const_spec(mods.shape),
            _const_spec((1, d)),
            pl.BlockSpec((tm, cosf.shape[1]), lambda i, j: (j, 0)),
            pl.BlockSpec((tm, sinf.shape[1]), lambda i, j: (j, 0)),
            _const_spec((1, rw)),
            pl.BlockSpec(memory_space=pl.ANY),
        ] + [cast_spec(w) for w in sliced],
        out_specs=[tok(fw), tok(rw), pl.BlockSpec((1, rw, tm), lambda i, j: (i, 0, j)),
                   tok(rw), tok(rw), tok(2 * d)] + [cast_spec(w) for w in sliced],
        out_shape=[out(fw), out(rw), jax.ShapeDtypeStruct((b, rw, l), BF16),
                   out(rw), out(rw), out(2 * d)]
        + [jax.ShapeDtypeStruct(w.shape, BF16) for w in sliced],
        scratch_shapes=[pltpu.VMEM(w_in.shape, BF16),
                        pltpu.VMEM((WEIGHT_STAGE_SLOTS, stage_rows, w_in.shape[1]), F32),
                        pltpu.SemaphoreType.DMA((WEIGHT_STAGE_SLOTS,))],
        compiler_params=_params("arbitrary", "arbitrary"),
        name="proj",
    )(x, mods, gain, cosf, sinf, gn_gain, w_in, *sliced)
    casts = [c.reshape(w.shape) for c, w in zip(res[6:], later_weights)]
    return res[:6], casts


def _fft_kernel(u_ref, kw_ref, twc_ref, tws_ref, w2_ref, cs_ref, o_ref, tr_ref, ti_ref, scr_ref,
                *, la, lb, gd, npb, scale):
    r = FFT_LA_TILE
    rows = lb * r
    fw = u_ref.shape[2]
    ngroups = fw // gd
    reps = fw // twc_ref.shape[2]
    pitch = scr_ref.shape[1] // la

    @pl.when(pl.program_id(0) == 0)
    def _():
        scr_ref[...] = jnp.zeros_like(scr_ref)

    for j in range(la // r):
        u = jnp.concatenate([u_ref[0, k * la + j * r:k * la + (j + 1) * r, :] for k in range(lb)], axis=0)
        half = lb // 2 + 1
        t = _dot(kw_ref[...], u)
        mirror = [lb - k for k in range(half, lb)]
        a = jnp.concatenate([t[:half * r]] + [t[k * r:(k + 1) * r] for k in mirror], axis=0)
        b = jnp.concatenate([t[half * r:]] + [-t[(half + k) * r:(half + k + 1) * r] for k in mirror], axis=0)
        ct = _tile_lanes(twc_ref[j], reps)
        st = _tile_lanes(tws_ref[j], reps)
        tr = (a * ct + b * st).astype(BF16)
        ti = (b * ct - a * st).astype(BF16)
        for k in range(lb):
            tr_ref[k * la + j * r:k * la + (j + 1) * r, :] = tr[k * r:(k + 1) * r]
            ti_ref[k * la + j * r:k * la + (j + 1) * r, :] = ti[k * r:(k + 1) * r]

    for pb in range(lb // npb):
        xs = []
        for p in range(npb):
            lo = (pb * npb + p) * la
            t = jnp.concatenate([tr_ref[lo:lo + la, :], ti_ref[lo:lo + la, :]], axis=0)
            xs.append(_dot(w2_ref[...], t).astype(BF16))
        for gi in range(ngroups):
            cols = slice(gi * gd, (gi + 1) * gd)
            lhs = jnp.concatenate([jnp.concatenate([x[:la, cols], x[la:, cols]], axis=1) for x in xs], axis=0)
            fg = _dot(lhs, cs_ref[...]) * scale
            for p in range(npb):
                scr_ref[gi, pl.ds(p, la, stride=pitch), :] = fg[p * la:(p + 1) * la]
        full = jnp.concatenate([scr_ref[gi].reshape(la, pitch, gd)[:, :npb, :] for gi in range(ngroups)], axis=2)
        o_ref[0, :, pb * npb:(pb + 1) * npb, :] = full.astype(BF16)


def _fft(u, kw, twc, tws, w2, cs, la, lb, gd, scale):
    b, l, fw = u.shape
    npb = min(BF16_SUBLANES, lb)
    out = pl.pallas_call(
        functools.partial(_fft_kernel, la=la, lb=lb, gd=gd, npb=npb, scale=scale),
        grid=(b,),
        in_specs=[pl.BlockSpec((1, l, fw), lambda i: (i, 0, 0)),
                  _const_spec(kw.shape), _const_spec(twc.shape), _const_spec(tws.shape),
                  _const_spec(w2.shape), _const_spec(cs.shape)],
        out_specs=pl.BlockSpec((1, la, lb, fw), lambda i: (i, 0, 0, 0)),
        out_shape=jax.ShapeDtypeStruct((b, la, lb, fw), BF16),
        scratch_shapes=[pltpu.VMEM((l, fw), BF16), pltpu.VMEM((l, fw), BF16),
                        pltpu.VMEM((fw // gd, la * (npb + FFT_SCATTER_PAD), gd), F32)],
        compiler_params=_params("arbitrary"),
        name="fft",
    )(u, kw, twc, tws, w2, cs)
    return out.reshape(b, l, fw)


def _ret_kernel(q_ref, kt_ref, v_ref, sg_ref, kct_ref, vc_ref, lgl_ref, lgc_ref,
                z_ref, sf_ref, sb_ref, stf_ref, stb_ref, p_ref, o_ref,
                dall_ref, qdf_ref, qdb_ref, kdf_ref, kdb_ref, *, nchunk, unroll):
    c = RET_CHUNK
    gw = GROUP_W
    lc = vc_ref.shape[1]

    lgf, lgb = lgl_ref[0:1, :], lgl_ref[1:2, :]
    pos = lax.broadcasted_iota(jnp.int32, (c, gw), 0).astype(F32)
    qdf_ref[...] = jnp.exp(lgf * (pos + 1.0))
    qdb_ref[...] = jnp.exp(lgb * (c - pos))
    kdf_ref[...] = jnp.exp(lgf * (c - 1.0 - pos))
    kdb_ref[...] = jnp.exp(lgb * pos)
    cdf = jnp.exp(lgf * c)
    cdb = jnp.exp(lgb * c)
    cpos = lax.broadcasted_iota(jnp.int32, (lc, gw), 0).astype(F32)
    wcf = jnp.exp(lgf * (lc - 1.0 - cpos))
    wcb = jnp.exp(lgb * cpos)
    si = lax.broadcasted_iota(jnp.int32, (c, HEADS_PER_GROUP * c), 0)
    sj = lax.broadcasted_iota(jnp.int32, (c, HEADS_PER_GROUP * c), 1) & (c - 1)
    diff = (si - sj).astype(F32)
    dall_ref[...] = (jnp.where(diff >= 0, jnp.exp(lgc_ref[0:1, :] * jnp.maximum(diff, 0.0)), 0.0)
                     + jnp.where(diff <= 0, jnp.exp(lgc_ref[1:2, :] * jnp.maximum(-diff, 0.0)), 0.0))

    same_head = (lax.broadcasted_iota(jnp.int32, (gw, gw), 0) // RET_HEAD_DIM
                 == lax.broadcasted_iota(jnp.int32, (gw, gw), 1) // RET_HEAD_DIM)

    def diag_blocks(t):
        return jnp.where(same_head, t, 0.0)

    def weighted(t, w):
        return (t.astype(F32) * w).astype(BF16)

    def rows(ref, n):
        return ref[0, pl.ds(pl.multiple_of(n * c, c), c), :]

    def kt_chunk(n):
        return kt_ref[0, :, pl.ds(pl.multiple_of(n * c, c), c)]

    kct = kct_ref[0]
    vc = vc_ref[0]
    stf_ref[...] = diag_blocks(_dot(kct, weighted(vc, wcf)))
    stb_ref[...] = diag_blocks(_dot(kct, weighted(vc, wcb)))

    def scan_body(i, carry):
        nf = i
        nb = nchunk - 1 - i
        s = stf_ref[...]
        sf_ref[nf] = s.astype(BF16)
        stf_ref[...] = s * cdf + diag_blocks(_dot(kt_chunk(nf), weighted(rows(v_ref, nf), kdf_ref[...])))
        s = stb_ref[...]
        sb_ref[nb] = s.astype(BF16)
        stb_ref[...] = s * cdb + diag_blocks(_dot(kt_chunk(nb), weighted(rows(v_ref, nb), kdb_ref[...])))
        return carry

    lax.fori_loop(0, nchunk, scan_body, 0, unroll=unroll)

    lane = lax.broadcasted_iota(jnp.int32, (c, gw), 1)
    sub = lax.broadcasted_iota(jnp.int32, (gw, c), 0)
    lane_masks = [(lane >= h * RET_HEAD_DIM) & (lane < (h + 1) * RET_HEAD_DIM)
                  for h in range(HEADS_PER_GROUP)]
    sub_masks = [(sub >= h * RET_HEAD_DIM) & (sub < (h + 1) * RET_HEAD_DIM)
                 for h in range(HEADS_PER_GROUP)]
    bd_mean = jnp.where(same_head, 1.0 / RET_HEAD_DIM, 0.0).astype(BF16)

    def score_body(n, carry):
        ktn = kt_chunk(n)
        zk = jnp.zeros_like(ktn)
        kbd = jnp.concatenate([jnp.where(m, ktn, zk) for m in sub_masks], axis=1)
        p_ref[n] = (_dot(rows(q_ref, n), kbd) * dall_ref[...]).astype(BF16)
        return carry

    lax.fori_loop(0, nchunk, score_body, 0, unroll=unroll)

    def mix_body(n, carry):
        qn = rows(q_ref, n)
        vn = rows(v_ref, n)
        zv = jnp.zeros_like(vn)
        vbd = jnp.concatenate([jnp.where(m, vn, zv) for m in lane_masks], axis=0)
        qfb = jnp.concatenate([weighted(qn, qdf_ref[...]), weighted(qn, qdb_ref[...])], axis=1)
        o_ref[n] = _dot(p_ref[n], vbd) + _dot(qfb, jnp.concatenate([sf_ref[n], sb_ref[n]], axis=0))
        return carry

    lax.fori_loop(0, nchunk, mix_body, 0, unroll=unroll)

    def norm_body(n, carry):
        o = o_ref[n]
        ms = _dot((o * o).astype(BF16), bd_mean)
        z = rows(sg_ref, n).astype(F32) * (o * lax.rsqrt(ms + EPS))
        z_ref[0, pl.ds(pl.multiple_of(n * c, c), c), :] = z.astype(BF16)
        return carry

    lax.fori_loop(0, nchunk, norm_body, 0, unroll=unroll)


def _retention(q, kt, v, sg, kct, vc, log_gamma):
    b, l, rw = q.shape
    lc = vc.shape[1]
    c = RET_CHUNK
    gw = GROUP_W
    hpg = HEADS_PER_GROUP
    ng = rw // gw
    nchunk = l // c
    assert c & (c - 1) == 0
    lg = log_gamma.reshape(2, ng, hpg).transpose(1, 0, 2)
    lg_lane = jnp.repeat(lg, RET_HEAD_DIM, axis=2)
    lg_col = jnp.repeat(lg, c, axis=2)
    tok = pl.BlockSpec((1, l, gw), lambda i, j: (i, 0, j))
    tokt = pl.BlockSpec((1, gw, l), lambda i, j: (i, j, 0))
    grp = lambda r, width: pl.BlockSpec((None, r, width), lambda i, j: (j, 0, 0))
    return pl.pallas_call(
        functools.partial(_ret_kernel, nchunk=nchunk, unroll=math.gcd(nchunk, RET_UNROLL)),
        grid=(b, ng),
        in_specs=[
            tok, tokt, tok, tok,
            pl.BlockSpec((1, gw, lc), lambda i, j: (i, j, 0)),
            pl.BlockSpec((1, lc, gw), lambda i, j: (i, 0, j)),
            grp(2, gw), grp(2, hpg * c),
        ],
        out_specs=tok,
        out_shape=jax.ShapeDtypeStruct((b, l, rw), BF16),
        scratch_shapes=[pltpu.VMEM((nchunk, gw, gw), BF16), pltpu.VMEM((nchunk, gw, gw), BF16),
                        pltpu.VMEM((gw, gw), F32), pltpu.VMEM((gw, gw), F32),
                        pltpu.VMEM((nchunk, c, hpg * c), BF16),
                        pltpu.VMEM((nchunk, c, gw), F32),
                        pltpu.VMEM((c, hpg * c), F32)] + [pltpu.VMEM((c, gw), F32)] * 4,
        compiler_params=_params("arbitrary", "arbitrary"),
        name="ret",
    )(q, kt, v, sg, kct, vc, lg_lane, lg_col)


def _out_kernel(x_ref, mod_ref, fm_ref, z_ref, gab_ref, gain2_ref, fgain_ref,
                w4_ref, wr_ref, wo_ref, w1_ref, w2_ref, o_ref, *, ff_chunk):
    mod = mod_ref[:, pl.ds(pl.program_id(0), 1), :]
    g1, sh2, sc2, g2 = mod[2], mod[3], mod[4], mod[5]
    y_four = _dot(fm_ref[0], w4_ref[...])
    y_ret = _dot(z_ref[0], wr_ref[...])
    d = x_ref.shape[2]
    y = (jax.nn.sigmoid(gab_ref[0, :, :d].astype(F32)) * y_four
         + jax.nn.sigmoid(gab_ref[0, :, d:].astype(F32)) * y_ret)
    x1 = x_ref[0] + g1 * _dot(y.astype(BF16), wo_ref[...])
    h2 = _norm_mod(x1, gain2_ref[...], sh2, sc2).astype(BF16)
    dff = w1_ref.shape[1]
    acc = None
    for lo in range(0, dff, ff_chunk):
        hid = jnp.maximum(_dot(h2, w1_ref[:, lo:lo + ff_chunk]), 0.0)
        part = _dot((hid * hid).astype(BF16), w2_ref[lo:lo + ff_chunk, :])
        acc = part if acc is None else acc + part
    x2 = x1 + g2 * acc
    ms = jnp.mean(x2 * x2, axis=-1, keepdims=True)
    o_ref[0] = x2 * lax.rsqrt(ms + EPS) * fgain_ref[...]


def _out(x, mods, fm, z, gab, w4, wr, wo, gain2, w1, w2, fgain):
    b, l, d = x.shape
    tm = min(TOKEN_TILE, l)
    tok = lambda width: pl.BlockSpec((1, tm, width), lambda i, j: (i, j, 0))
    weights = (w4, wr, wo, w1, w2)
    return pl.pallas_call(
        functools.partial(_out_kernel, ff_chunk=min(1024, w1.shape[1])),
        grid=(b, l // tm),
        in_specs=[
            tok(d),
            _const_spec(mods.shape),
            tok(fm.shape[2]), tok(z.shape[2]), tok(2 * d),
            _const_spec((1, d)), _const_spec((1, d)),
        ] + [_const_spec(w.shape) for w in weights],
        out_specs=tok(d),
        out_shape=jax.ShapeDtypeStruct((b, l, d), F32),
        compiler_params=_params("arbitrary", "arbitrary"),
        name="out",
    )(x, mods, fm, z, gab, gain2, fgain, *weights)


def _dft_tables(l, gd):
    la = FFT_LA
    lb = l // la

    def cs(n):
        idx = np.arange(n)
        ang = 2.0 * np.pi * ((idx[:, None] * idx[None, :]) % n) / n
        return np.cos(ang), np.sin(ang)

    cb, sb = cs(lb)
    eye = np.eye(FFT_LA_TILE)
    half = lb // 2 + 1
    kw = np.concatenate([np.kron(cb[:half], eye), -np.kron(sb[:half], eye)], axis=0)
    ca, sa = cs(la)
    w2 = np.block([[ca, sa], [-sa, ca]])
    cc, sc = cs(gd)
    chan = np.concatenate([cc, sc], axis=0)
    tw = 2.0 * np.pi * (np.arange(lb)[:, None] * np.arange(la)[None, :]) / l
    tw = tw.reshape(lb, la // FFT_LA_TILE, FFT_LA_TILE).transpose(1, 0, 2).reshape(la // FFT_LA_TILE, -1)
    twc = np.repeat(np.cos(tw)[:, :, None], LANES, axis=2)
    tws = np.repeat(np.sin(tw)[:, :, None], LANES, axis=2)
    as_bf = lambda a: jnp.asarray(a, dtype=F32).astype(BF16)
    return as_bf(kw), as_bf(w2), as_bf(chan), jnp.asarray(twc, F32), jnp.asarray(tws, F32), la, lb


def _rope_tables(l):
    f32 = np.float32
    nf = RET_HEAD_DIM // 4
    inv = np.power(f32(ROPE_BASE), -np.arange(nf, dtype=f32) / f32(nf)).astype(f32)
    rows = l // GRID_W
    r, cc = np.meshgrid(np.arange(rows, dtype=f32), np.arange(GRID_W, dtype=f32), indexing="ij")
    ang = np.concatenate([r.reshape(-1)[:, None] * inv, cc.reshape(-1)[:, None] * inv], axis=-1).astype(f32)
    cos, sin = np.cos(ang).astype(f32), np.sin(ang).astype(f32)
    cos_h = np.concatenate([cos, cos], axis=1)
    sin_h = np.concatenate([-sin, sin], axis=1)
    reps = LANES // RET_HEAD_DIM
    return jnp.asarray(np.tile(cos_h, (1, reps))), jnp.asarray(np.tile(sin_h, (1, reps)))


def kernel(x, c, ctx, c_ctx, w_ada, b_ada, norm1_gain, w_in, four_w_out, ret_decay_logit,
           ret_gn_gain, ret_w_out, w_out, norm2_gain, w_mlp1, w_mlp2, final_gain):
    assert w_ada.shape[0] == 1, "single-layer block"
    b, l, d = x.shape
    lc = ctx.shape[1]
    fw = four_w_out.shape[1]
    rw = ret_w_out.shape[1]
    gd = fw // FOUR_GROUPS
    assert l % FFT_LA == 0 and l % RET_CHUNK == 0 and rw % GROUP_W == 0

    mods, kct, vc = _ada(c, c_ctx, w_ada[0], b_ada, ctx, norm1_gain, w_in[0], fw, rw)

    kw, w2, chan, twc, tws, la, lb = _dft_tables(l, gd)
    cosf, sinf = _rope_tables(l)

    later = [four_w_out[0], ret_w_out[0], w_out[0], w_mlp1[0], w_mlp2[0]]
    (u, q, kt, v, sg, gab), later_b = _proj(x, mods, norm1_gain, w_in[0], cosf, sinf, ret_gn_gain,
                                             later, fw, rw)

    fm = _fft(u, kw, twc, tws, w2, chan, la, lb, gd, 1.0 / math.sqrt(l * gd))

    log_gamma = jax.nn.log_sigmoid(ret_decay_logit[0].astype(F32))
    z = _retention(q, kt, v, sg, kct, vc, log_gamma)

    w4, wr, wo, w1, w2 = later_b
    return _out(x, mods, fm, z, gab, w4, wr, wo, norm2_gain, w1, w2, final_gain[None, :])
```

```python
import functools
import math

import jax
import jax.numpy as jnp
import numpy as np
from jax import lax
from jax.experimental import pallas as pl
from jax.experimental.pallas import tpu as pltpu

F32 = jnp.float32
BF16 = jnp.bfloat16

GRID_W = 64
FOUR_GROUPS = 4
RET_HEAD_DIM = 64
N_MOD = 6
CTX_FIRST_STEP = 2
ROPE_BASE = 10000.0
EPS = 1e-6

LANES = 128
MXU_DIM = 256
F32_SUBLANES = 8
BF16_SUBLANES = 16
VMEM_LIMIT_BYTES = 56 * 1024 * 1024

RET_CHUNK = 128
RET_UNROLL = 32
HEADS_PER_GROUP = MXU_DIM // RET_HEAD_DIM
GROUP_W = HEADS_PER_GROUP * RET_HEAD_DIM
STATE_BLOCK_W = LANES
FFT_LA = 128
FFT_LA_TILE = BF16_SUBLANES
FFT_SCATTER_PAD = F32_SUBLANES
TOKEN_TILE = 512
PROJ_TOKEN_TILE = 1024
WEIGHT_STAGE_BYTES = 2 * 1024 * 1024
WEIGHT_STAGE_SLOTS = 4


def _dot(a, b):
    return jnp.dot(a, b, preferred_element_type=F32)


def _norm_mod(x, gain, shift, scale):
    ms = jnp.mean(x * x, axis=-1, keepdims=True)
    y = x * lax.rsqrt(ms + EPS) * gain
    return y * (1.0 + scale) + shift


def _tile_lanes(t, reps):
    return jnp.concatenate([t] * reps, axis=1) if reps > 1 else t


def _const_spec(shape):
    nd = len(shape)
    return pl.BlockSpec(shape, lambda *_: (0,) * nd, pipeline_mode=pl.Buffered(1))


def _params(*sem):
    return pltpu.CompilerParams(dimension_semantics=sem, vmem_limit_bytes=VMEM_LIMIT_BYTES)


def _load_weights_as_bf16(pairs, stage_ref, sem_ref):
    slots, sr, sc = stage_ref.shape
    chunks = [(src, dst, r0, c0)
              for src, dst in pairs
              for r0 in range(0, src.shape[0], sr)
              for c0 in range(0, src.shape[1], sc)]

    def copy(k):
        src, _, r0, c0 = chunks[k]
        slot = k % slots
        return pltpu.make_async_copy(src.at[pl.ds(r0, sr), pl.ds(c0, sc)], stage_ref.at[slot], sem_ref.at[slot])

    for k in range(min(slots - 1, len(chunks))):
        copy(k).start()
    for k, (_, dst, r0, c0) in enumerate(chunks):
        if k + slots - 1 < len(chunks):
            copy(k + slots - 1).start()
        copy(k).wait()
        dst[r0:r0 + sr, c0:c0 + sc] = stage_ref[k % slots].astype(BF16)


def _is_first_step(grid_rank):
    first = pl.program_id(0) == 0
    for axis in range(1, grid_rank):
        first = first & (pl.program_id(axis) == 0)
    return first


def _ada_kernel(c_ref, cctx_ref, w_ref, b_ref, ctx_ref, gain_ref, wkv_ref,
                o_ref, kt_ref, v_ref, cv_ref, modc_ref, wkv_bf_ref, *, rw, scale):
    j = pl.program_id(0)
    b = c_ref.shape[0]
    cv_ref[...] = jnp.zeros_like(cv_ref)
    cv_ref[0:b, :] = c_ref[...]
    cv_ref[b:b + 1, :] = cctx_ref[...]
    c = cv_ref[...]
    s = c * jax.nn.sigmoid(c)
    mod = _dot(s.astype(BF16), w_ref[...].astype(BF16)) + b_ref[...]
    o_ref[0] = mod

    @pl.when(j < CTX_FIRST_STEP)
    def _():
        modc_ref[pl.ds(j, 1), :] = mod[b:b + 1, :]

    @pl.when(j == CTX_FIRST_STEP)
    def _():
        wkv_bf_ref[...] = wkv_ref[...].astype(BF16)

    @pl.when(j >= CTX_FIRST_STEP)
    def _():
        h = _norm_mod(ctx_ref[0], gain_ref[...], modc_ref[0:1, :], modc_ref[1:2, :]).astype(BF16)
        kt_ref[0] = (_dot(h, wkv_bf_ref[:, :rw]) * scale).T.astype(BF16)
        v_ref[0] = _dot(h, wkv_bf_ref[:, rw:]).astype(BF16)


def _ada(c, c_ctx, w_ada, b_ada, ctx, gain, w_in, fw, rw):
    b, d = c.shape
    lc = ctx.shape[1]
    rows = -(-(b + 1) // F32_SUBLANES) * F32_SUBLANES
    assert w_ada.shape[1] == N_MOD * d and b <= N_MOD - CTX_FIRST_STEP
    kv_block = (fw + rw) // (2 * rw)
    assert kv_block * 2 * rw == fw + rw
    sample = lambda j: jnp.clip(j - CTX_FIRST_STEP, 0, b - 1)
    return pl.pallas_call(
        functools.partial(_ada_kernel, rw=rw, scale=RET_HEAD_DIM ** -0.5),
        grid=(N_MOD,),
        in_specs=[
            pl.BlockSpec((b, d), lambda j: (0, 0)),
            pl.BlockSpec((1, d), lambda j: (0, 0)),
            pl.BlockSpec((d, d), lambda j: (0, j)),
            pl.BlockSpec((1, d), lambda j: (0, j)),
            pl.BlockSpec((1, lc, d), lambda j: (sample(j), 0, 0)),
            pl.BlockSpec((1, d), lambda j: (0, 0)),
            pl.BlockSpec((d, 2 * rw), lambda j: (0, kv_block)),
        ],
        out_specs=[pl.BlockSpec((1, rows, d), lambda j: (j, 0, 0)),
                   pl.BlockSpec((1, rw, lc), lambda j: (sample(j), 0, 0)),
                   pl.BlockSpec((1, lc, rw), lambda j: (sample(j), 0, 0))],
        out_shape=[jax.ShapeDtypeStruct((N_MOD, rows, d), F32),
                   jax.ShapeDtypeStruct((b, rw, lc), BF16), jax.ShapeDtypeStruct((b, lc, rw), BF16)],
        scratch_shapes=[pltpu.VMEM((rows, d), F32), pltpu.VMEM((CTX_FIRST_STEP, d), F32),
                        pltpu.VMEM((d, 2 * rw), BF16)],
        compiler_params=_params("arbitrary"),
        name="ada",
    )(c, c_ctx[None, :], w_ada, b_ada, ctx, gain, w_in)


def _proj_kernel(*refs, fw, rw, d, scale, ncast):
    x_ref, mod_ref, gain_ref, cos_ref, sin_ref, gng_ref, w_hbm = refs[:7]
    cast_in = refs[7:7 + ncast]
    u_ref, q_ref, kt_ref, v_ref, sg_ref, gab_ref = refs[7 + ncast:13 + ncast]
    cast_out = refs[13 + ncast:13 + 2 * ncast]
    w_ref, stage_ref, sem_ref = refs[13 + 2 * ncast:]

    @pl.when(_is_first_step(2))
    def _():
        _load_weights_as_bf16([(w_hbm, w_ref)], stage_ref, sem_ref)

    for src, dst in zip(cast_in, cast_out):
        dst[...] = src[...].astype(BF16)

    mod = mod_ref[:, pl.ds(pl.program_id(0), 1), :]
    h = _norm_mod(x_ref[0], gain_ref[...], mod[0], mod[1]).astype(BF16)
    tm = h.shape[0]

    def proj(lo, width):
        return _dot(h, w_ref[:, lo:lo + width])

    u_ref[0] = proj(0, fw).astype(BF16)

    reps = rw // cos_ref.shape[1]
    cosf = _tile_lanes(cos_ref[...], reps)
    sinf = _tile_lanes(sin_ref[...], reps)
    lane = lax.broadcasted_iota(jnp.int32, (tm, rw), 1)
    first_half = (lane & (RET_HEAD_DIM // 2)) == 0

    def rope(t):
        rot = jnp.where(first_half,
                        pltpu.roll(t, rw - RET_HEAD_DIM // 2, 1),
                        pltpu.roll(t, RET_HEAD_DIM // 2, 1))
        return t * cosf + rot * sinf

    q_ref[0] = rope(proj(fw, rw)).astype(BF16)
    kt_ref[0] = (rope(proj(fw + rw, rw)) * scale).T.astype(BF16)
    v_ref[0] = proj(fw + 2 * rw, rw).astype(BF16)
    g = proj(fw + 3 * rw, rw)
    sg_ref[0] = (g * jax.nn.sigmoid(g) * gng_ref[...]).astype(BF16)
    gab_ref[0, :, :d] = proj(fw + 4 * rw, d).astype(BF16)
    gab_ref[0, :, d:] = proj(fw + 4 * rw + d, d).astype(BF16)


def _proj(x, mods, gain, w_in, cosf, sinf, gn_gain, later_weights, fw, rw):
    b, l, d = x.shape
    tm = min(PROJ_TOKEN_TILE, l)
    nj = l // tm
    steps = b * nj
    tok = lambda width: pl.BlockSpec((1, tm, width), lambda i, j: (i, j, 0))
    out = lambda width: jax.ShapeDtypeStruct((b, l, width), BF16)
    sliced = [w.reshape(steps, w.shape[0] // steps, w.shape[1]) for w in later_weights]
    for w in sliced:
        assert w.shape[1] % BF16_SUBLANES == 0
    cast_spec = lambda w: pl.BlockSpec((1,) + w.shape[1:], lambda i, j: (i * nj + j, 0, 0))
    stage_rows = max(r for r in range(8, w_in.shape[0] + 1, 8)
                     if w_in.shape[0] % r == 0 and r * w_in.shape[1] * 4 <= WEIGHT_STAGE_BYTES)
    res = pl.pallas_call(
        functools.partial(_proj_kernel, fw=fw, rw=rw, d=d, scale=RET_HEAD_DIM ** -0.5, ncast=len(sliced)),
        grid=(b, nj),
        in_specs=[
            tok(d),
            _const_spec(mods.shape),
            _const_spec((1, d)),
            pl.BlockSpec((tm, cosf.shape[1]), lambda i, j: (j, 0)),
            pl.BlockSpec((tm, sinf.shape[1]), lambda i, j: (j, 0)),
            _const_spec((1, rw)),
            pl.BlockSpec(memory_space=pl.ANY),
        ] + [cast_spec(w) for w in sliced],
        out_specs=[tok(fw), tok(rw), pl.BlockSpec((1, rw, tm), lambda i, j: (i, 0, j)),
                   tok(rw), tok(rw), tok(2 * d)] + [cast_spec(w) for w in sliced],
        out_shape=[out(fw), out(rw), jax.ShapeDtypeStruct((b, rw, l), BF16),
                   out(rw), out(rw), out(2 * d)]
        + [jax.ShapeDtypeStruct(w.shape, BF16) for w in sliced],
        scratch_shapes=[pltpu.VMEM(w_in.shape, BF16),
                        pltpu.VMEM((WEIGHT_STAGE_SLOTS, stage_rows, w_in.shape[1]), F32),
                        pltpu.SemaphoreType.DMA((WEIGHT_STAGE_SLOTS,))],
        compiler_params=_params("arbitrary", "arbitrary"),
        name="proj",
    )(x, mods, gain, cosf, sinf, gn_gain, w_in, *sliced)
    casts = [c.reshape(w.shape) for c, w in zip(res[6:], later_weights)]
    return res[:6], casts


def _fft_kernel(u_ref, kw_ref, twc_ref, tws_ref, w2_ref, cs_ref, o_ref, tr_ref, ti_ref, scr_ref,
                *, la, lb, gd, npb, scale):
    r = FFT_LA_TILE
    rows = lb * r
    fw = u_ref.shape[2]
    ngroups = fw // gd
    reps = fw // twc_ref.shape[2]
    pitch = scr_ref.shape[1] // la

    @pl.when(pl.program_id(0) == 0)
    def _():
        scr_ref[...] = jnp.zeros_like(scr_ref)

    for j in range(la // r):
        u = jnp.concatenate([u_ref[0, k * la + j * r:k * la + (j + 1) * r, :] for k in range(lb)], axis=0)
        half = lb // 2 + 1
        t = _dot(kw_ref[...], u)
        mirror = [lb - k for k in range(half, lb)]
        a = jnp.concatenate([t[:half * r]] + [t[k * r:(k + 1) * r] for k in mirror], axis=0)
        b = jnp.concatenate([t[half * r:]] + [-t[(half + k) * r:(half + k + 1) * r] for k in mirror], axis=0)
        ct = _tile_lanes(twc_ref[j], reps)
        st = _tile_lanes(tws_ref[j], reps)
        tr = (a * ct + b * st).astype(BF16)
        ti = (b * ct - a * st).astype(BF16)
        for k in range(lb):
            tr_ref[k * la + j * r:k * la + (j + 1) * r, :] = tr[k * r:(k + 1) * r]
            ti_ref[k * la + j * r:k * la + (j + 1) * r, :] = ti[k * r:(k + 1) * r]

    for pb in range(lb // npb):
        xs = []
        for p in range(npb):
            lo = (pb * npb + p) * la
            t = jnp.concatenate([tr_ref[lo:lo + la, :], ti_ref[lo:lo + la, :]], axis=0)
            xs.append(_dot(w2_ref[...], t).astype(BF16))
        for gi in range(ngroups):
            cols = slice(gi * gd, (gi + 1) * gd)
            lhs = jnp.concatenate([jnp.concatenate([x[:la, cols], x[la:, cols]], axis=1) for x in xs], axis=0)
            fg = _dot(lhs, cs_ref[...]) * scale
            for p in range(npb):
                scr_ref[gi, pl.ds(p, la, stride=pitch), :] = fg[p * la:(p + 1) * la]
        full = jnp.concatenate([scr_ref[gi].reshape(la, pitch, gd)[:, :npb, :] for gi in range(ngroups)], axis=2)
        o_ref[0, :, pb * npb:(pb + 1) * npb, :] = full.astype(BF16)


def _fft(u, kw, twc, tws, w2, cs, la, lb, gd, scale):
    b, l, fw = u.shape
    npb = min(BF16_SUBLANES, lb)
    out = pl.pallas_call(
        functools.partial(_fft_kernel, la=la, lb=lb, gd=gd, npb=npb, scale=scale),
        grid=(b,),
        in_specs=[pl.BlockSpec((1, l, fw), lambda i: (i, 0, 0)),
                  _const_spec(kw.shape), _const_spec(twc.shape), _const_spec(tws.shape),
                  _const_spec(w2.shape), _const_spec(cs.shape)],
        out_specs=pl.BlockSpec((1, la, lb, fw), lambda i: (i, 0, 0, 0)),
        out_shape=jax.ShapeDtypeStruct((b, la, lb, fw), BF16),
        scratch_shapes=[pltpu.VMEM((l, fw), BF16), pltpu.VMEM((l, fw), BF16),
                        pltpu.VMEM((fw // gd, la * (npb + FFT_SCATTER_PAD), gd), F32)],
        compiler_params=_params("arbitrary"),
        name="fft",
    )(u, kw, twc, tws, w2, cs)
    return out.reshape(b, l, fw)


def _ret_kernel(q_ref, kt_ref, v_ref, sg_ref, kct_ref, vc_ref, lgl_ref, lgc_ref,
                z_ref, sf_ref, sb_ref, stf_ref, stb_ref, p_ref, o_ref,
                dall_ref, qdf_ref, qdb_ref, kdf_ref, kdb_ref, *, nchunk, unroll):
    c = RET_CHUNK
    gw = GROUP_W
    lc = vc_ref.shape[1]

    lgf, lgb = lgl_ref[0:1, :], lgl_ref[1:2, :]
    pos = lax.broadcasted_iota(jnp.int32, (c, gw), 0).astype(F32)
    qdf_ref[...] = jnp.exp(lgf * (pos + 1.0))
    qdb_ref[...] = jnp.exp(lgb * (c - pos))
    kdf_ref[...] = jnp.exp(lgf * (c - 1.0 - pos))
    kdb_ref[...] = jnp.exp(lgb * pos)
    cdf = jnp.exp(lgf * c)
    cdb = jnp.exp(lgb * c)
    cpos = lax.broadcasted_iota(jnp.int32, (lc, gw), 0).astype(F32)
    wcf = jnp.exp(lgf * (lc - 1.0 - cpos))
    wcb = jnp.exp(lgb * cpos)
    si = lax.broadcasted_iota(jnp.int32, (c, HEADS_PER_GROUP * c), 0)
    sj = lax.broadcasted_iota(jnp.int32, (c, HEADS_PER_GROUP * c), 1) & (c - 1)
    diff = (si - sj).astype(F32)
    dall_ref[...] = (jnp.where(diff >= 0, jnp.exp(lgc_ref[0:1, :] * jnp.maximum(diff, 0.0)), 0.0)
                     + jnp.where(diff <= 0, jnp.exp(lgc_ref[1:2, :] * jnp.maximum(-diff, 0.0)), 0.0))

    pw = STATE_BLOCK_W
    npair = gw // pw
    same_head_p = (lax.broadcasted_iota(jnp.int32, (pw, pw), 0) // RET_HEAD_DIM
                   == lax.broadcasted_iota(jnp.int32, (pw, pw), 1) // RET_HEAD_DIM)

    def diag_blocks(t):
        return jnp.where(same_head_p, t, 0.0)

    def weighted(t, w):
        return (t.astype(F32) * w).astype(BF16)

    def rows(ref, n):
        return ref[0, pl.ds(pl.multiple_of(n * c, c), c), :]

    def kt_chunk(n):
        return kt_ref[0, :, pl.ds(pl.multiple_of(n * c, c), c)]

    def state_update(st_ref, kt, vw, cd):
        for p in range(npair):
            cols = slice(p * pw, (p + 1) * pw)
            st_ref[p] = st_ref[p] * cd[:, cols] + diag_blocks(_dot(kt[cols, :], vw[:, cols]))

    def full_state(s_ref, n):
        zero = jnp.zeros((pw, pw), BF16)
        return jnp.concatenate(
            [jnp.concatenate([s_ref[n, p] if q == p else zero for q in range(npair)], axis=1)
             for p in range(npair)], axis=0)

    kct = kct_ref[0]
    vc = vc_ref[0]
    stf_ref[...] = jnp.zeros_like(stf_ref)
    stb_ref[...] = jnp.zeros_like(stb_ref)
    state_update(stf_ref, kct, weighted(vc, wcf), cdf)
    state_update(stb_ref, kct, weighted(vc, wcb), cdb)

    def scan_body(i, carry):
        nf = i
        nb = nchunk - 1 - i
        sf_ref[nf] = stf_ref[...].astype(BF16)
        state_update(stf_ref, kt_chunk(nf), weighted(rows(v_ref, nf), kdf_ref[...]), cdf)
        sb_ref[nb] = stb_ref[...].astype(BF16)
        state_update(stb_ref, kt_chunk(nb), weighted(rows(v_ref, nb), kdb_ref[...]), cdb)
        return carry

    lax.fori_loop(0, nchunk, scan_body, 0, unroll=unroll)

    lane = lax.broadcasted_iota(jnp.int32, (c, gw), 1)
    sub = lax.broadcasted_iota(jnp.int32, (gw, c), 0)
    lane_masks = [(lane >= h * RET_HEAD_DIM) & (lane < (h + 1) * RET_HEAD_DIM)
                  for h in range(HEADS_PER_GROUP)]
    sub_masks = [(sub >= h * RET_HEAD_DIM) & (sub < (h + 1) * RET_HEAD_DIM)
                 for h in range(HEADS_PER_GROUP)]
    same_head = (lax.broadcasted_iota(jnp.int32, (gw, gw), 0) // RET_HEAD_DIM
                 == lax.broadcasted_iota(jnp.int32, (gw, gw), 1) // RET_HEAD_DIM)
    bd_mean = jnp.where(same_head, 1.0 / RET_HEAD_DIM, 0.0).astype(BF16)

    def score_body(n, carry):
        ktn = kt_chunk(n)
        zk = jnp.zeros_like(ktn)
        kbd = jnp.concatenate([jnp.where(m, ktn, zk) for m in sub_masks], axis=1)
        p_ref[n] = (_dot(rows(q_ref, n), kbd) * dall_ref[...]).astype(BF16)
        return carry

    lax.fori_loop(0, nchunk, score_body, 0, unroll=unroll)

    def mix_body(n, carry):
        qn = rows(q_ref, n)
        vn = rows(v_ref, n)
        zv = jnp.zeros_like(vn)
        vbd = jnp.concatenate([jnp.where(m, vn, zv) for m in lane_masks], axis=0)
        o_ref[n] = (_dot(p_ref[n], vbd) + qdf_ref[...] * _dot(qn, full_state(sf_ref, n))
                    + qdb_ref[...] * _dot(qn, full_state(sb_ref, n)))
        return carry

    lax.fori_loop(0, nchunk, mix_body, 0, unroll=unroll)

    def norm_body(n, carry):
        o = o_ref[n]
        ms = _dot((o * o).astype(BF16), bd_mean)
        z = rows(sg_ref, n).astype(F32) * (o * lax.rsqrt(ms + EPS))
        z_ref[0, pl.ds(pl.multiple_of(n * c, c), c), :] = z.astype(BF16)
        return carry

    lax.fori_loop(0, nchunk, norm_body, 0, unroll=unroll)


def _retention(q, kt, v, sg, kct, vc, log_gamma):
    b, l, rw = q.shape
    lc = vc.shape[1]
    c = RET_CHUNK
    gw = GROUP_W
    hpg = HEADS_PER_GROUP
    ng = rw // gw
    nchunk = l // c
    pw = STATE_BLOCK_W
    npair = gw // pw
    assert c & (c - 1) == 0
    lg = log_gamma.reshape(2, ng, hpg).transpose(1, 0, 2)
    lg_lane = jnp.repeat(lg, RET_HEAD_DIM, axis=2)
    lg_col = jnp.repeat(lg, c, axis=2)
    tok = pl.BlockSpec((1, l, gw), lambda i, j: (i, 0, j))
    tokt = pl.BlockSpec((1, gw, l), lambda i, j: (i, j, 0))
    grp = lambda r, width: pl.BlockSpec((None, r, width), lambda i, j: (j, 0, 0))
    return pl.pallas_call(
        functools.partial(_ret_kernel, nchunk=nchunk, unroll=math.gcd(nchunk, RET_UNROLL)),
        grid=(b, ng),
        in_specs=[
            tok, tokt, tok, tok,
            pl.BlockSpec((1, gw, lc), lambda i, j: (i, j, 0)),
            pl.BlockSpec((1, lc, gw), lambda i, j: (i, 0, j)),
            grp(2, gw), grp(2, hpg * c),
        ],
        out_specs=tok,
        out_shape=jax.ShapeDtypeStruct((b, l, rw), BF16),
        scratch_shapes=[pltpu.VMEM((nchunk, npair, pw, pw), BF16), pltpu.VMEM((nchunk, npair, pw, pw), BF16),
                        pltpu.VMEM((npair, pw, pw), F32), pltpu.VMEM((npair, pw, pw), F32),
                        pltpu.VMEM((nchunk, c, hpg * c), BF16),
                        pltpu.VMEM((nchunk, c, gw), F32),
                        pltpu.VMEM((c, hpg * c), F32)] + [pltpu.VMEM((c, gw), F32)] * 4,
        compiler_params=_params("arbitrary", "arbitrary"),
        name="ret",
    )(q, kt, v, sg, kct, vc, lg_lane, lg_col)


def _out_kernel(x_ref, mod_ref, fm_ref, z_ref, gab_ref, gain2_ref, fgain_ref,
                w4_ref, wr_ref, wo_ref, w1_ref, w2_ref, o_ref, *, ff_chunk):
    mod = mod_ref[:, pl.ds(pl.program_id(0), 1), :]
    g1, sh2, sc2, g2 = mod[2], mod[3], mod[4], mod[5]
    y_four = _dot(fm_ref[0], w4_ref[...])
    y_ret = _dot(z_ref[0], wr_ref[...])
    d = x_ref.shape[2]
    y = (jax.nn.sigmoid(gab_ref[0, :, :d].astype(F32)) * y_four
         + jax.nn.sigmoid(gab_ref[0, :, d:].astype(F32)) * y_ret)
    x1 = x_ref[0] + g1 * _dot(y.astype(BF16), wo_ref[...])
    h2 = _norm_mod(x1, gain2_ref[...], sh2, sc2).astype(BF16)
    dff = w1_ref.shape[1]
    acc = None
    for lo in range(0, dff, ff_chunk):
        hid = jnp.maximum(_dot(h2, w1_ref[:, lo:lo + ff_chunk]), 0.0)
        part = _dot((hid * hid).astype(BF16), w2_ref[lo:lo + ff_chunk, :])
        acc = part if acc is None else acc + part
    x2 = x1 + g2 * acc
    ms = jnp.mean(x2 * x2, axis=-1, keepdims=True)
    o_ref[0] = x2 * lax.rsqrt(ms + EPS) * fgain_ref[...]


def _out(x, mods, fm, z, gab, w4, wr, wo, gain2, w1, w2, fgain):
    b, l, d = x.shape
    tm = min(TOKEN_TILE, l)
    tok = lambda width: pl.BlockSpec((1, tm, width), lambda i, j: (i, j, 0))
    weights = (w4, wr, wo, w1, w2)
    return pl.pallas_call(
        functools.partial(_out_kernel, ff_chunk=min(1024, w1.shape[1])),
        grid=(b, l // tm),
        in_specs=[
            tok(d),
            _const_spec(mods.shape),
            tok(fm.shape[2]), tok(z.shape[2]), tok(2 * d),
            _const_spec((1, d)), _const_spec((1, d)),
        ] + [_const_spec(w.shape) for w in weights],
        out_specs=tok(d),
        out_shape=jax.ShapeDtypeStruct((b, l, d), F32),
        compiler_params=_params("arbitrary", "arbitrary"),
        name="out",
    )(x, mods, fm, z, gab, gain2, fgain, *weights)


def _dft_tables(l, gd):
    la = FFT_LA
    lb = l // la

    def cs(n):
        idx = np.arange(n)
        ang = 2.0 * np.pi * ((idx[:, None] * idx[None, :]) % n) / n
        return np.cos(ang), np.sin(ang)

    cb, sb = cs(lb)
    eye = np.eye(FFT_LA_TILE)
    half = lb // 2 + 1
    kw = np.concatenate([np.kron(cb[:half], eye), -np.kron(sb[:half], eye)], axis=0)
    ca, sa = cs(la)
    w2 = np.block([[ca, sa], [-sa, ca]])
    cc, sc = cs(gd)
    chan = np.concatenate([cc, sc], axis=0)
    tw = 2.0 * np.pi * (np.arange(lb)[:, None] * np.arange(la)[None, :]) / l
    tw = tw.reshape(lb, la // FFT_LA_TILE, FFT_LA_TILE).transpose(1, 0, 2).reshape(la // FFT_LA_TILE, -1)
    twc = np.repeat(np.cos(tw)[:, :, None], LANES, axis=2)
    tws = np.repeat(np.sin(tw)[:, :, None], LANES, axis=2)
    as_bf = lambda a: jnp.asarray(a, dtype=F32).astype(BF16)
    return as_bf(kw), as_bf(w2), as_bf(chan), jnp.asarray(twc, F32), jnp.asarray(tws, F32), la, lb


def _rope_tables(l):
    f32 = np.float32
    nf = RET_HEAD_DIM // 4
    inv = np.power(f32(ROPE_BASE), -np.arange(nf, dtype=f32) / f32(nf)).astype(f32)
    rows = l // GRID_W
    r, cc = np.meshgrid(np.arange(rows, dtype=f32), np.arange(GRID_W, dtype=f32), indexing="ij")
    ang = np.concatenate([r.reshape(-1)[:, None] * inv, cc.reshape(-1)[:, None] * inv], axis=-1).astype(f32)
    cos, sin = np.cos(ang).astype(f32), np.sin(ang).astype(f32)
    cos_h = np.concatenate([cos, cos], axis=1)
    sin_h = np.concatenate([-sin, sin], axis=1)
    reps = LANES // RET_HEAD_DIM
    return jnp.asarray(np.tile(cos_h, (1, reps))), jnp.asarray(np.tile(sin_h, (1, reps)))


def kernel(x, c, ctx, c_ctx, w_ada, b_ada, norm1_gain, w_in, four_w_out, ret_decay_logit,
           ret_gn_gain, ret_w_out, w_out, norm2_gain, w_mlp1, w_mlp2, final_gain):
    assert w_ada.shape[0] == 1, "single-layer block"
    b, l, d = x.shape
    lc = ctx.shape[1]
    fw = four_w_out.shape[1]
    rw = ret_w_out.shape[1]
    gd = fw // FOUR_GROUPS
    assert l % FFT_LA == 0 and l % RET_CHUNK == 0 and rw % GROUP_W == 0

    mods, kct, vc = _ada(c, c_ctx, w_ada[0], b_ada, ctx, norm1_gain, w_in[0], fw, rw)

    kw, w2, chan, twc, tws, la, lb = _dft_tables(l, gd)
    cosf, sinf = _rope_tables(l)

    later = [four_w_out[0], ret_w_out[0], w_out[0], w_mlp1[0], w_mlp2[0]]
    (u, q, kt, v, sg, gab), later_b = _proj(x, mods, norm1_gain, w_in[0], cosf, sinf, ret_gn_gain,
                                             later, fw, rw)

    fm = _fft(u, kw, twc, tws, w2, chan, la, lb, gd, 1.0 / math.sqrt(l * gd))

    log_gamma = jax.nn.log_sigmoid(ret_decay_logit[0].astype(F32))
    z = _retention(q, kt, v, sg, kct, vc, log_gamma)

    w4, wr, wo, w1, w2 = later_b
    return _out(x, mods, fm, z, gab, w4, wr, wo, norm2_gain, w1, w2, final_gain[None, :])
```

```python
import functools
import math

import jax
import jax.numpy as jnp
import numpy as np
from jax import lax
from jax.experimental import pallas as pl
from jax.experimental.pallas import tpu as pltpu

F32 = jnp.float32
BF16 = jnp.bfloat16

GRID_W = 64
FOUR_GROUPS = 4
RET_HEAD_DIM = 64
N_MOD = 6
CTX_FIRST_STEP = 2
ROPE_BASE = 10000.0
EPS = 1e-6

LANES = 128
MXU_DIM = 256
F32_SUBLANES = 8
BF16_SUBLANES = 16
VMEM_LIMIT_BYTES = 56 * 1024 * 1024

RET_CHUNK = 128
RET_UNROLL = 32
HEADS_PER_GROUP = MXU_DIM // RET_HEAD_DIM
GROUP_W = HEADS_PER_GROUP * RET_HEAD_DIM
STATE_BLOCK_W = LANES
FFT_LA = 128
FFT_LA_TILE = BF16_SUBLANES
TOKEN_TILE = 512
PROJ_TOKEN_TILE = 1024
WEIGHT_STAGE_BYTES = 2 * 1024 * 1024
WEIGHT_STAGE_SLOTS = 4


def _dot(a, b):
    return jnp.dot(a, b, preferred_element_type=F32)


def _norm_mod(x, gain, shift, scale):
    ms = jnp.mean(x * x, axis=-1, keepdims=True)
    y = x * lax.rsqrt(ms + EPS) * gain
    return y * (1.0 + scale) + shift


def _tile_lanes(t, reps):
    return jnp.concatenate([t] * reps, axis=1) if reps > 1 else t


def _const_spec(shape):
    nd = len(shape)
    return pl.BlockSpec(shape, lambda *_: (0,) * nd, pipeline_mode=pl.Buffered(1))


def _params(*sem):
    return pltpu.CompilerParams(dimension_semantics=sem, vmem_limit_bytes=VMEM_LIMIT_BYTES)


def _load_weights_as_bf16(pairs, stage_ref, sem_ref):
    slots, sr, sc = stage_ref.shape
    chunks = [(src, dst, r0, c0)
              for src, dst in pairs
              for r0 in range(0, src.shape[0], sr)
              for c0 in range(0, src.shape[1], sc)]

    def copy(k):
        src, _, r0, c0 = chunks[k]
        slot = k % slots
        return pltpu.make_async_copy(src.at[pl.ds(r0, sr), pl.ds(c0, sc)], stage_ref.at[slot], sem_ref.at[slot])

    for k in range(min(slots - 1, len(chunks))):
        copy(k).start()
    for k, (_, dst, r0, c0) in enumerate(chunks):
        if k + slots - 1 < len(chunks):
            copy(k + slots - 1).start()
        copy(k).wait()
        dst[r0:r0 + sr, c0:c0 + sc] = stage_ref[k % slots].astype(BF16)


def _is_first_step(grid_rank):
    first = pl.program_id(0) == 0
    for axis in range(1, grid_rank):
        first = first & (pl.program_id(axis) == 0)
    return first


def _ada_kernel(c_ref, cctx_ref, w_ref, b_ref, ctx_ref, gain_ref, wkv_ref,
                o_ref, kt_ref, v_ref, cv_ref, modc_ref, wkv_bf_ref, *, rw, scale):
    j = pl.program_id(0)
    b = c_ref.shape[0]
    cv_ref[...] = jnp.zeros_like(cv_ref)
    cv_ref[0:b, :] = c_ref[...]
    cv_ref[b:b + 1, :] = cctx_ref[...]
    c = cv_ref[...]
    s = c * jax.nn.sigmoid(c)
    mod = _dot(s.astype(BF16), w_ref[...].astype(BF16)) + b_ref[...]
    o_ref[0] = mod

    @pl.when(j < CTX_FIRST_STEP)
    def _():
        modc_ref[pl.ds(j, 1), :] = mod[b:b + 1, :]

    @pl.when(j == CTX_FIRST_STEP)
    def _():
        wkv_bf_ref[...] = wkv_ref[...].astype(BF16)

    @pl.when(j >= CTX_FIRST_STEP)
    def _():
        h = _norm_mod(ctx_ref[0], gain_ref[...], modc_ref[0:1, :], modc_ref[1:2, :]).astype(BF16)
        kt_ref[0] = (_dot(h, wkv_bf_ref[:, :rw]) * scale).T.astype(BF16)
        v_ref[0] = _dot(h, wkv_bf_ref[:, rw:]).astype(BF16)


def _ada(c, c_ctx, w_ada, b_ada, ctx, gain, w_in, fw, rw):
    b, d = c.shape
    lc = ctx.shape[1]
    rows = -(-(b + 1) // F32_SUBLANES) * F32_SUBLANES
    assert w_ada.shape[1] == N_MOD * d and b <= N_MOD - CTX_FIRST_STEP
    kv_block = (fw + rw) // (2 * rw)
    assert kv_block * 2 * rw == fw + rw
    sample = lambda j: jnp.clip(j - CTX_FIRST_STEP, 0, b - 1)
    return pl.pallas_call(
        functools.partial(_ada_kernel, rw=rw, scale=RET_HEAD_DIM ** -0.5),
        grid=(N_MOD,),
        in_specs=[
            pl.BlockSpec((b, d), lambda j: (0, 0)),
            pl.BlockSpec((1, d), lambda j: (0, 0)),
            pl.BlockSpec((d, d), lambda j: (0, j)),
            pl.BlockSpec((1, d), lambda j: (0, j)),
            pl.BlockSpec((1, lc, d), lambda j: (sample(j), 0, 0)),
            pl.BlockSpec((1, d), lambda j: (0, 0)),
            pl.BlockSpec((d, 2 * rw), lambda j: (0, kv_block)),
        ],
        out_specs=[pl.BlockSpec((1, rows, d), lambda j: (j, 0, 0)),
                   pl.BlockSpec((1, rw, lc), lambda j: (sample(j), 0, 0)),
                   pl.BlockSpec((1, lc, rw), lambda j: (sample(j), 0, 0))],
        out_shape=[jax.ShapeDtypeStruct((N_MOD, rows, d), F32),
                   jax.ShapeDtypeStruct((b, rw, lc), BF16), jax.ShapeDtypeStruct((b, lc, rw), BF16)],
        scratch_shapes=[pltpu.VMEM((rows, d), F32), pltpu.VMEM((CTX_FIRST_STEP, d), F32),
                        pltpu.VMEM((d, 2 * rw), BF16)],
        compiler_params=_params("arbitrary"),
        name="ada",
    )(c, c_ctx[None, :], w_ada, b_ada, ctx, gain, w_in)


def _proj_kernel(*refs, fw, rw, d, scale, ncast):
    x_ref, mod_ref, gain_ref, cos_ref, sin_ref, gng_ref, w_hbm = refs[:7]
    cast_in = refs[7:7 + ncast]
    u_ref, q_ref, kt_ref, v_ref, sg_ref, gab_ref = refs[7 + ncast:13 + ncast]
    cast_out = refs[13 + ncast:13 + 2 * ncast]
    w_ref, stage_ref, sem_ref = refs[13 + 2 * ncast:]

    @pl.when(_is_first_step(2))
    def _():
        _load_weights_as_bf16([(w_hbm, w_ref)], stage_ref, sem_ref)

    for src, dst in zip(cast_in, cast_out):
        dst[...] = src[...].astype(BF16)

    mod = mod_ref[:, pl.ds(pl.program_id(0), 1), :]
    h = _norm_mod(x_ref[0], gain_ref[...], mod[0], mod[1]).astype(BF16)
    tm = h.shape[0]

    def proj(lo, width):
        return _dot(h, w_ref[:, lo:lo + width])

    u_ref[0] = proj(0, fw).astype(BF16)

    reps = rw // cos_ref.shape[1]
    cosf = _tile_lanes(cos_ref[...], reps)
    sinf = _tile_lanes(sin_ref[...], reps)
    lane = lax.broadcasted_iota(jnp.int32, (tm, rw), 1)
    first_half = (lane & (RET_HEAD_DIM // 2)) == 0

    def rope(t):
        rot = jnp.where(first_half,
                        pltpu.roll(t, rw - RET_HEAD_DIM // 2, 1),
                        pltpu.roll(t, RET_HEAD_DIM // 2, 1))
        return t * cosf + rot * sinf

    q_ref[0] = rope(proj(fw, rw)).astype(BF16)
    kt_ref[0] = (rope(proj(fw + rw, rw)) * scale).T.astype(BF16)
    v_ref[0] = proj(fw + 2 * rw, rw).astype(BF16)
    g = proj(fw + 3 * rw, rw)
    sg_ref[0] = (g * jax.nn.sigmoid(g) * gng_ref[...]).astype(BF16)
    gab_ref[0, :, :d] = proj(fw + 4 * rw, d).astype(BF16)
    gab_ref[0, :, d:] = proj(fw + 4 * rw + d, d).astype(BF16)


def _proj(x, mods, gain, w_in, cosf, sinf, gn_gain, later_weights, fw, rw):
    b, l, d = x.shape
    tm = min(PROJ_TOKEN_TILE, l)
    nj = l // tm
    steps = b * nj
    tok = lambda width: pl.BlockSpec((1, tm, width), lambda i, j: (i, j, 0))
    out = lambda width: jax.ShapeDtypeStruct((b, l, width), BF16)
    sliced = [w.reshape(steps, w.shape[0] // steps, w.shape[1]) for w in later_weights]
    for w in sliced:
        assert w.shape[1] % BF16_SUBLANES == 0
    cast_spec = lambda w: pl.BlockSpec((1,) + w.shape[1:], lambda i, j: (i * nj + j, 0, 0))
    stage_rows = max(r for r in range(8, w_in.shape[0] + 1, 8)
                     if w_in.shape[0] % r == 0 and r * w_in.shape[1] * 4 <= WEIGHT_STAGE_BYTES)
    res = pl.pallas_call(
        functools.partial(_proj_kernel, fw=fw, rw=rw, d=d, scale=RET_HEAD_DIM ** -0.5, ncast=len(sliced)),
        grid=(b, nj),
        in_specs=[
            tok(d),
            _const_spec(mods.shape),
            _const_spec((1, d)),
            pl.BlockSpec((tm, cosf.shape[1]), lambda i, j: (j, 0)),
            pl.BlockSpec((tm, sinf.shape[1]), lambda i, j: (j, 0)),
            _const_spec((1, rw)),
            pl.BlockSpec(memory_space=pl.ANY),
        ] + [cast_spec(w) for w in sliced],
        out_specs=[tok(fw), tok(rw), pl.BlockSpec((1, rw, tm), lambda i, j: (i, 0, j)),
                   tok(rw), tok(rw), tok(2 * d)] + [cast_spec(w) for w in sliced],
        out_shape=[out(fw), out(rw), jax.ShapeDtypeStruct((b, rw, l), BF16),
                   out(rw), out(rw), out(2 * d)]
        + [jax.ShapeDtypeStruct(w.shape, BF16) for w in sliced],
        scratch_shapes=[pltpu.VMEM(w_in.shape, BF16),
                        pltpu.VMEM((WEIGHT_STAGE_SLOTS, stage_rows, w_in.shape[1]), F32),
                        pltpu.SemaphoreType.DMA((WEIGHT_STAGE_SLOTS,))],
        compiler_params=_params("arbitrary", "arbitrary"),
        name="proj",
    )(x, mods, gain, cosf, sinf, gn_gain, w_in, *sliced)
    casts = [c.reshape(w.shape) for c, w in zip(res[6:], later_weights)]
    return res[:6], casts


def _fft_kernel(u_ref, kw_ref, twc_ref, tws_ref, w2_ref, cs_ref, o_ref, tr_ref, ti_ref,
                *, la, lb, gd, npb, scale):
    r = FFT_LA_TILE
    rows = lb * r
    fw = u_ref.shape[2]
    ngroups = fw // gd
    reps = fw // twc_ref.shape[2]

    for j in range(la // r):
        u = jnp.concatenate([u_ref[0, k * la + j * r:k * la + (j + 1) * r, :] for k in range(lb)], axis=0)
        half = lb // 2 + 1
        t = _dot(kw_ref[...], u)
        mirror = [lb - k for k in range(half, lb)]
        a = jnp.concatenate([t[:half * r]] + [t[k * r:(k + 1) * r] for k in mirror], axis=0)
        b = jnp.concatenate([t[half * r:]] + [-t[(half + k) * r:(half + k + 1) * r] for k in mirror], axis=0)
        ct = _tile_lanes(twc_ref[j], reps)
        st = _tile_lanes(tws_ref[j], reps)
        tr = (a * ct + b * st).astype(BF16)
        ti = (b * ct - a * st).astype(BF16)
        for k in range(lb):
            tr_ref[k * la + j * r:k * la + (j + 1) * r, :] = tr[k * r:(k + 1) * r]
            ti_ref[k * la + j * r:k * la + (j + 1) * r, :] = ti[k * r:(k + 1) * r]

    for pb in range(lb // npb):
        xs = []
        for p in range(npb):
            lo = (pb * npb + p) * la
            t = jnp.concatenate([tr_ref[lo:lo + la, :], ti_ref[lo:lo + la, :]], axis=0)
            xs.append(_dot(w2_ref[...], t).astype(BF16))
        for gi in range(ngroups):
            cols = slice(gi * gd, (gi + 1) * gd)
            lhs = jnp.concatenate([jnp.concatenate([x[:la, cols], x[la:, cols]], axis=1) for x in xs], axis=0)
            fg = _dot(lhs, cs_ref[...]) * scale
            for p in range(npb):
                o_ref[0, pb * npb + p, :, cols] = fg[p * la:(p + 1) * la].astype(BF16)


def _fft(u, kw, twc, tws, w2, cs, la, lb, gd, scale):
    b, l, fw = u.shape
    npb = min(BF16_SUBLANES, lb)
    out = pl.pallas_call(
        functools.partial(_fft_kernel, la=la, lb=lb, gd=gd, npb=npb, scale=scale),
        grid=(b,),
        in_specs=[pl.BlockSpec((1, l, fw), lambda i: (i, 0, 0)),
                  _const_spec(kw.shape), _const_spec(twc.shape), _const_spec(tws.shape),
                  _const_spec(w2.shape), _const_spec(cs.shape)],
        out_specs=pl.BlockSpec((1, lb, la, fw), lambda i: (i, 0, 0, 0)),
        out_shape=jax.ShapeDtypeStruct((b, lb, la, fw), BF16),
        scratch_shapes=[pltpu.VMEM((l, fw), BF16), pltpu.VMEM((l, fw), BF16)],
        compiler_params=_params("arbitrary"),
        name="fft",
    )(u, kw, twc, tws, w2, cs)
    return out.transpose(0, 2, 1, 3).reshape(b, l, fw)


def _ret_kernel(q_ref, kt_ref, v_ref, sg_ref, kct_ref, vc_ref, lgl_ref, lgc_ref,
                z_ref, sf_ref, sb_ref, stf_ref, stb_ref, p_ref, o_ref,
                dall_ref, qdf_ref, qdb_ref, kdf_ref, kdb_ref, *, nchunk, unroll):
    c = RET_CHUNK
    gw = GROUP_W
    lc = vc_ref.shape[1]

    lgf, lgb = lgl_ref[0:1, :], lgl_ref[1:2, :]
    pos = lax.broadcasted_iota(jnp.int32, (c, gw), 0).astype(F32)
    qdf_ref[...] = jnp.exp(lgf * (pos + 1.0))
    qdb_ref[...] = jnp.exp(lgb * (c - pos))
    kdf_ref[...] = jnp.exp(lgf * (c - 1.0 - pos))
    kdb_ref[...] = jnp.exp(lgb * pos)
    cdf = jnp.exp(lgf * c)
    cdb = jnp.exp(lgb * c)
    cpos = lax.broadcasted_iota(jnp.int32, (lc, gw), 0).astype(F32)
    wcf = jnp.exp(lgf * (lc - 1.0 - cpos))
    wcb = jnp.exp(lgb * cpos)
    si = lax.broadcasted_iota(jnp.int32, (c, HEADS_PER_GROUP * c), 0)
    sj = lax.broadcasted_iota(jnp.int32, (c, HEADS_PER_GROUP * c), 1) & (c - 1)
    diff = (si - sj).astype(F32)
    dall_ref[...] = (jnp.where(diff >= 0, jnp.exp(lgc_ref[0:1, :] * jnp.maximum(diff, 0.0)), 0.0)
                     + jnp.where(diff <= 0, jnp.exp(lgc_ref[1:2, :] * jnp.maximum(-diff, 0.0)), 0.0))

    pw = STATE_BLOCK_W
    npair = gw // pw
    same_head_p = (lax.broadcasted_iota(jnp.int32, (pw, pw), 0) // RET_HEAD_DIM
                   == lax.broadcasted_iota(jnp.int32, (pw, pw), 1) // RET_HEAD_DIM)

    def diag_blocks(t):
        return jnp.where(same_head_p, t, 0.0)

    def weighted(t, w):
        return (t.astype(F32) * w).astype(BF16)

    def rows(ref, n):
        return ref[0, pl.ds(pl.multiple_of(n * c, c), c), :]

    def kt_chunk(n):
        return kt_ref[0, :, pl.ds(pl.multiple_of(n * c, c), c)]

    def state_update(st_ref, kt, vw, cd):
        for p in range(npair):
            cols = slice(p * pw, (p + 1) * pw)
            st_ref[p] = st_ref[p] * cd[:, cols] + diag_blocks(_dot(kt[cols, :], vw[:, cols]))

    def full_state(s_ref, n):
        zero = jnp.zeros((pw, pw), BF16)
        return jnp.concatenate(
            [jnp.concatenate([s_ref[n, p] if q == p else zero for q in range(npair)], axis=1)
             for p in range(npair)], axis=0)

    kct = kct_ref[0]
    vc = vc_ref[0]
    stf_ref[...] = jnp.zeros_like(stf_ref)
    stb_ref[...] = jnp.zeros_like(stb_ref)
    state_update(stf_ref, kct, weighted(vc, wcf), cdf)
    state_update(stb_ref, kct, weighted(vc, wcb), cdb)

    def scan_body(i, carry):
        nf = i
        nb = nchunk - 1 - i
        sf_ref[nf] = stf_ref[...].astype(BF16)
        state_update(stf_ref, kt_chunk(nf), weighted(rows(v_ref, nf), kdf_ref[...]), cdf)
        sb_ref[nb] = stb_ref[...].astype(BF16)
        state_update(stb_ref, kt_chunk(nb), weighted(rows(v_ref, nb), kdb_ref[...]), cdb)
        return carry

    lax.fori_loop(0, nchunk, scan_body, 0, unroll=unroll)

    lane = lax.broadcasted_iota(jnp.int32, (c, gw), 1)
    sub = lax.broadcasted_iota(jnp.int32, (gw, c), 0)
    lane_masks = [(lane >= h * RET_HEAD_DIM) & (lane < (h + 1) * RET_HEAD_DIM)
                  for h in range(HEADS_PER_GROUP)]
    sub_masks = [(sub >= h * RET_HEAD_DIM) & (sub < (h + 1) * RET_HEAD_DIM)
                 for h in range(HEADS_PER_GROUP)]
    same_head = (lax.broadcasted_iota(jnp.int32, (gw, gw), 0) // RET_HEAD_DIM
                 == lax.broadcasted_iota(jnp.int32, (gw, gw), 1) // RET_HEAD_DIM)
    bd_mean = jnp.where(same_head, 1.0 / RET_HEAD_DIM, 0.0).astype(BF16)

    def score_body(n, carry):
        ktn = kt_chunk(n)
        zk = jnp.zeros_like(ktn)
        kbd = jnp.concatenate([jnp.where(m, ktn, zk) for m in sub_masks], axis=1)
        p_ref[n] = (_dot(rows(q_ref, n), kbd) * dall_ref[...]).astype(BF16)
        return carry

    lax.fori_loop(0, nchunk, score_body, 0, unroll=unroll)

    def mix_body(n, carry):
        qn = rows(q_ref, n)
        vn = rows(v_ref, n)
        zv = jnp.zeros_like(vn)
        vbd = jnp.concatenate([jnp.where(m, vn, zv) for m in lane_masks], axis=0)
        o_ref[n] = (_dot(p_ref[n], vbd) + qdf_ref[...] * _dot(qn, full_state(sf_ref, n))
                    + qdb_ref[...] * _dot(qn, full_state(sb_ref, n)))
        return carry

    lax.fori_loop(0, nchunk, mix_body, 0, unroll=unroll)

    def norm_body(n, carry):
        o = o_ref[n]
        ms = _dot((o * o).astype(BF16), bd_mean)
        z = rows(sg_ref, n).astype(F32) * (o * lax.rsqrt(ms + EPS))
        z_ref[0, pl.ds(pl.multiple_of(n * c, c), c), :] = z.astype(BF16)
        return carry

    lax.fori_loop(0, nchunk, norm_body, 0, unroll=unroll)


def _retention(q, kt, v, sg, kct, vc, log_gamma):
    b, l, rw = q.shape
    lc = vc.shape[1]
    c = RET_CHUNK
    gw = GROUP_W
    hpg = HEADS_PER_GROUP
    ng = rw // gw
    nchunk = l // c
    pw = STATE_BLOCK_W
    npair = gw // pw
    assert c & (c - 1) == 0
    lg = log_gamma.reshape(2, ng, hpg).transpose(1, 0, 2)
    lg_lane = jnp.repeat(lg, RET_HEAD_DIM, axis=2)
    lg_col = jnp.repeat(lg, c, axis=2)
    tok = pl.BlockSpec((1, l, gw), lambda i, j: (i, 0, j))
    tokt = pl.BlockSpec((1, gw, l), lambda i, j: (i, j, 0))
    grp = lambda r, width: pl.BlockSpec((None, r, width), lambda i, j: (j, 0, 0))
    return pl.pallas_call(
        functools.partial(_ret_kernel, nchunk=nchunk, unroll=math.gcd(nchunk, RET_UNROLL)),
        grid=(b, ng),
        in_specs=[
            tok, tokt, tok, tok,
            pl.BlockSpec((1, gw, lc), lambda i, j: (i, j, 0)),
            pl.BlockSpec((1, lc, gw), lambda i, j: (i, 0, j)),
            grp(2, gw), grp(2, hpg * c),
        ],
        out_specs=tok,
        out_shape=jax.ShapeDtypeStruct((b, l, rw), BF16),
        scratch_shapes=[pltpu.VMEM((nchunk, npair, pw, pw), BF16), pltpu.VMEM((nchunk, npair, pw, pw), BF16),
                        pltpu.VMEM((npair, pw, pw), F32), pltpu.VMEM((npair, pw, pw), F32),
                        pltpu.VMEM((nchunk, c, hpg * c), BF16),
                        pltpu.VMEM((nchunk, c, gw), F32),
                        pltpu.VMEM((c, hpg * c), F32)] + [pltpu.VMEM((c, gw), F32)] * 4,
        compiler_params=_params("arbitrary", "arbitrary"),
        name="ret",
    )(q, kt, v, sg, kct, vc, lg_lane, lg_col)


def _out_kernel(x_ref, mod_ref, fm_ref, z_ref, gab_ref, gain2_ref, fgain_ref,
                w4_ref, wr_ref, wo_ref, w1_ref, w2_ref, o_ref, *, ff_chunk):
    mod = mod_ref[:, pl.ds(pl.program_id(0), 1), :]
    g1, sh2, sc2, g2 = mod[2], mod[3], mod[4], mod[5]
    y_four = _dot(fm_ref[0], w4_ref[...])
    y_ret = _dot(z_ref[0], wr_ref[...])
    d = x_ref.shape[2]
    y = (jax.nn.sigmoid(gab_ref[0, :, :d].astype(F32)) * y_four
         + jax.nn.sigmoid(gab_ref[0, :, d:].astype(F32)) * y_ret)
    x1 = x_ref[0] + g1 * _dot(y.astype(BF16), wo_ref[...])
    h2 = _norm_mod(x1, gain2_ref[...], sh2, sc2).astype(BF16)
    dff = w1_ref.shape[1]
    acc = None
    for lo in range(0, dff, ff_chunk):
        hid = jnp.maximum(_dot(h2, w1_ref[:, lo:lo + ff_chunk]), 0.0)
        part = _dot((hid * hid).astype(BF16), w2_ref[lo:lo + ff_chunk, :])
        acc = part if acc is None else acc + part
    x2 = x1 + g2 * acc
    ms = jnp.mean(x2 * x2, axis=-1, keepdims=True)
    o_ref[0] = x2 * lax.rsqrt(ms + EPS) * fgain_ref[...]


def _out(x, mods, fm, z, gab, w4, wr, wo, gain2, w1, w2, fgain):
    b, l, d = x.shape
    tm = min(TOKEN_TILE, l)
    tok = lambda width: pl.BlockSpec((1, tm, width), lambda i, j: (i, j, 0))
    weights = (w4, wr, wo, w1, w2)
    return pl.pallas_call(
        functools.partial(_out_kernel, ff_chunk=min(1024, w1.shape[1])),
        grid=(b, l // tm),
        in_specs=[
            tok(d),
            _const_spec(mods.shape),
            tok(fm.shape[2]), tok(z.shape[2]), tok(2 * d),
            _const_spec((1, d)), _const_spec((1, d)),
        ] + [_const_spec(w.shape) for w in weights],
        out_specs=tok(d),
        out_shape=jax.ShapeDtypeStruct((b, l, d), F32),
        compiler_params=_params("arbitrary", "arbitrary"),
        name="out",
    )(x, mods, fm, z, gab, gain2, fgain, *weights)


def _dft_tables(l, gd):
    la = FFT_LA
    lb = l // la

    def cs(n):
        idx = np.arange(n)
        ang = 2.0 * np.pi * ((idx[:, None] * idx[None, :]) % n) / n
        return np.cos(ang), np.sin(ang)

    cb, sb = cs(lb)
    eye = np.eye(FFT_LA_TILE)
    half = lb // 2 + 1
    kw = np.concatenate([np.kron(cb[:half], eye), -np.kron(sb[:half], eye)], axis=0)
    ca, sa = cs(la)
    w2 = np.block([[ca, sa], [-sa, ca]])
    cc, sc = cs(gd)
    chan = np.concatenate([cc, sc], axis=0)
    tw = 2.0 * np.pi * (np.arange(lb)[:, None] * np.arange(la)[None, :]) / l
    tw = tw.reshape(lb, la // FFT_LA_TILE, FFT_LA_TILE).transpose(1, 0, 2).reshape(la // FFT_LA_TILE, -1)
    twc = np.repeat(np.cos(tw)[:, :, None], LANES, axis=2)
    tws = np.repeat(np.sin(tw)[:, :, None], LANES, axis=2)
    as_bf = lambda a: jnp.asarray(a, dtype=F32).astype(BF16)
    return as_bf(kw), as_bf(w2), as_bf(chan), jnp.asarray(twc, F32), jnp.asarray(tws, F32), la, lb


def _rope_tables(l):
    f32 = np.float32
    nf = RET_HEAD_DIM // 4
    inv = np.power(f32(ROPE_BASE), -np.arange(nf, dtype=f32) / f32(nf)).astype(f32)
    rows = l // GRID_W
    r, cc = np.meshgrid(np.arange(rows, dtype=f32), np.arange(GRID_W, dtype=f32), indexing="ij")
    ang = np.concatenate([r.reshape(-1)[:, None] * inv, cc.reshape(-1)[:, None] * inv], axis=-1).astype(f32)
    cos, sin = np.cos(ang).astype(f32), np.sin(ang).astype(f32)
    cos_h = np.concatenate([cos, cos], axis=1)
    sin_h = np.concatenate([-sin, sin], axis=1)
    reps = LANES // RET_HEAD_DIM
    return jnp.asarray(np.tile(cos_h, (1, reps))), jnp.asarray(np.tile(sin_h, (1, reps)))


def kernel(x, c, ctx, c_ctx, w_ada, b_ada, norm1_gain, w_in, four_w_out, ret_decay_logit,
           ret_gn_gain, ret_w_out, w_out, norm2_gain, w_mlp1, w_mlp2, final_gain):
    assert w_ada.shape[0] == 1, "single-layer block"
    b, l, d = x.shape
    lc = ctx.shape[1]
    fw = four_w_out.shape[1]
    rw = ret_w_out.shape[1]
    gd = fw // FOUR_GROUPS
    assert l % FFT_LA == 0 and l % RET_CHUNK == 0 and rw % GROUP_W == 0

    mods, kct, vc = _ada(c, c_ctx, w_ada[0], b_ada, ctx, norm1_gain, w_in[0], fw, rw)

    kw, w2, chan, twc, tws, la, lb = _dft_tables(l, gd)
    cosf, sinf = _rope_tables(l)

    later = [four_w_out[0], ret_w_out[0], w_out[0], w_mlp1[0], w_mlp2[0]]
    (u, q, kt, v, sg, gab), later_b = _proj(x, mods, norm1_gain, w_in[0], cosf, sinf, ret_gn_gain,
                                             later, fw, rw)

    fm = _fft(u, kw, twc, tws, w2, chan, la, lb, gd, 1.0 / math.sqrt(l * gd))

    log_gamma = jax.nn.log_sigmoid(ret_decay_logit[0].astype(F32))
    z = _retention(q, kt, v, sg, kct, vc, log_gamma)

    w4, wr, wo, w1, w2 = later_b
    return _out(x, mods, fm, z, gab, w4, wr, wo, norm2_gain, w1, w2, final_gain[None, :])
```

```python
import functools
import math

import jax
import jax.numpy as jnp
import numpy as np
from jax import lax
from jax.experimental import pallas as pl
from jax.experimental.pallas import tpu as pltpu

F32 = jnp.float32
BF16 = jnp.bfloat16

GRID_W = 64
FOUR_GROUPS = 4
RET_HEAD_DIM = 64
N_MOD = 6
CTX_FIRST_STEP = 2
ROPE_BASE = 10000.0
EPS = 1e-6

LANES = 128
MXU_DIM = 256
F32_SUBLANES = 8
BF16_SUBLANES = 16
VMEM_LIMIT_BYTES = 56 * 1024 * 1024

RET_CHUNK = 128
RET_UNROLL = 32
HEADS_PER_GROUP = MXU_DIM // RET_HEAD_DIM
GROUP_W = HEADS_PER_GROUP * RET_HEAD_DIM
STATE_BLOCK_W = LANES
FFT_LA = 128
FFT_LA_TILE = BF16_SUBLANES
FFT_SCATTER_PAD = F32_SUBLANES
TOKEN_TILE = 512
PROJ_TOKEN_TILE = 1024
WEIGHT_STAGE_BYTES = 2 * 1024 * 1024
WEIGHT_STAGE_SLOTS = 4


def _dot(a, b):
    return jnp.dot(a, b, preferred_element_type=F32)


def _norm_mod(x, gain, shift, scale):
    ms = jnp.mean(x * x, axis=-1, keepdims=True)
    y = x * lax.rsqrt(ms + EPS) * gain
    return y * (1.0 + scale) + shift


def _tile_lanes(t, reps):
    return jnp.concatenate([t] * reps, axis=1) if reps > 1 else t


def _const_spec(shape):
    nd = len(shape)
    return pl.BlockSpec(shape, lambda *_: (0,) * nd, pipeline_mode=pl.Buffered(1))


def _params(*sem):
    return pltpu.CompilerParams(dimension_semantics=sem, vmem_limit_bytes=VMEM_LIMIT_BYTES)


def _load_weights_as_bf16(pairs, stage_ref, sem_ref):
    slots, sr, sc = stage_ref.shape
    chunks = [(src, dst, r0, c0)
              for src, dst in pairs
              for r0 in range(0, src.shape[0], sr)
              for c0 in range(0, src.shape[1], sc)]

    def copy(k):
        src, _, r0, c0 = chunks[k]
        slot = k % slots
        return pltpu.make_async_copy(src.at[pl.ds(r0, sr), pl.ds(c0, sc)], stage_ref.at[slot], sem_ref.at[slot])

    for k in range(min(slots - 1, len(chunks))):
        copy(k).start()
    for k, (_, dst, r0, c0) in enumerate(chunks):
        if k + slots - 1 < len(chunks):
            copy(k + slots - 1).start()
        copy(k).wait()
        dst[r0:r0 + sr, c0:c0 + sc] = stage_ref[k % slots].astype(BF16)


def _is_first_step(grid_rank):
    first = pl.program_id(0) == 0
    for axis in range(1, grid_rank):
        first = first & (pl.program_id(axis) == 0)
    return first


def _ada_kernel(c_ref, cctx_ref, w_ref, b_ref, ctx_ref, gain_ref, wkv_ref,
                o_ref, kt_ref, v_ref, cv_ref, modc_ref, wkv_bf_ref, *, rw, scale):
    j = pl.program_id(0)
    b = c_ref.shape[0]
    cv_ref[...] = jnp.zeros_like(cv_ref)
    cv_ref[0:b, :] = c_ref[...]
    cv_ref[b:b + 1, :] = cctx_ref[...]
    c = cv_ref[...]
    s = c * jax.nn.sigmoid(c)
    mod = _dot(s.astype(BF16), w_ref[...].astype(BF16)) + b_ref[...]
    o_ref[0] = mod

    @pl.when(j < CTX_FIRST_STEP)
    def _():
        modc_ref[pl.ds(j, 1), :] = mod[b:b + 1, :]

    @pl.when(j == CTX_FIRST_STEP)
    def _():
        wkv_bf_ref[...] = wkv_ref[...].astype(BF16)

    @pl.when(j >= CTX_FIRST_STEP)
    def _():
        h = _norm_mod(ctx_ref[0], gain_ref[...], modc_ref[0:1, :], modc_ref[1:2, :]).astype(BF16)
        kt_ref[0] = (_dot(h, wkv_bf_ref[:, :rw]) * scale).T.astype(BF16)
        v_ref[0] = _dot(h, wkv_bf_ref[:, rw:]).astype(BF16)


def _ada(c, c_ctx, w_ada, b_ada, ctx, gain, w_in, fw, rw):
    b, d = c.shape
    lc = ctx.shape[1]
    rows = -(-(b + 1) // F32_SUBLANES) * F32_SUBLANES
    assert w_ada.shape[1] == N_MOD * d and b <= N_MOD - CTX_FIRST_STEP
    kv_block = (fw + rw) // (2 * rw)
    assert kv_block * 2 * rw == fw + rw
    sample = lambda j: jnp.clip(j - CTX_FIRST_STEP, 0, b - 1)
    return pl.pallas_call(
        functools.partial(_ada_kernel, rw=rw, scale=RET_HEAD_DIM ** -0.5),
        grid=(N_MOD,),
        in_specs=[
            pl.BlockSpec((b, d), lambda j: (0, 0)),
            pl.BlockSpec((1, d), lambda j: (0, 0)),
            pl.BlockSpec((d, d), lambda j: (0, j)),
            pl.BlockSpec((1, d), lambda j: (0, j)),
            pl.BlockSpec((1, lc, d), lambda j: (sample(j), 0, 0)),
            pl.BlockSpec((1, d), lambda j: (0, 0)),
            pl.BlockSpec((d, 2 * rw), lambda j: (0, kv_block)),
        ],
        out_specs=[pl.BlockSpec((1, rows, d), lambda j: (j, 0, 0)),
                   pl.BlockSpec((1, rw, lc), lambda j: (sample(j), 0, 0)),
                   pl.BlockSpec((1, lc, rw), lambda j: (sample(j), 0, 0))],
        out_shape=[jax.ShapeDtypeStruct((N_MOD, rows, d), F32),
                   jax.ShapeDtypeStruct((b, rw, lc), BF16), jax.ShapeDtypeStruct((b, lc, rw), BF16)],
        scratch_shapes=[pltpu.VMEM((rows, d), F32), pltpu.VMEM((CTX_FIRST_STEP, d), F32),
                        pltpu.VMEM((d, 2 * rw), BF16)],
        compiler_params=_params("arbitrary"),
        name="ada",
    )(c, c_ctx[None, :], w_ada, b_ada, ctx, gain, w_in)


def _proj_kernel(*refs, fw, rw, d, scale, ncast):
    x_ref, mod_ref, gain_ref, cos_ref, sin_ref, gng_ref, w_hbm = refs[:7]
    cast_in = refs[7:7 + ncast]
    u_ref, q_ref, kt_ref, v_ref, sg_ref, gab_ref = refs[7 + ncast:13 + ncast]
    cast_out = refs[13 + ncast:13 + 2 * ncast]
    w_ref, stage_ref, sem_ref = refs[13 + 2 * ncast:]

    @pl.when(_is_first_step(2))
    def _():
        _load_weights_as_bf16([(w_hbm, w_ref)], stage_ref, sem_ref)

    for src, dst in zip(cast_in, cast_out):
        dst[...] = src[...].astype(BF16)

    mod = mod_ref[:, pl.ds(pl.program_id(0), 1), :]
    h = _norm_mod(x_ref[0], gain_ref[...], mod[0], mod[1]).astype(BF16)
    tm = h.shape[0]

    def proj(lo, width):
        return _dot(h, w_ref[:, lo:lo + width])

    u_ref[0] = proj(0, fw).astype(BF16)

    reps = rw // cos_ref.shape[1]
    cosf = _tile_lanes(cos_ref[...], reps)
    sinf = _tile_lanes(sin_ref[...], reps)
    lane = lax.broadcasted_iota(jnp.int32, (tm, rw), 1)
    first_half = (lane & (RET_HEAD_DIM // 2)) == 0

    def rope(t):
        rot = jnp.where(first_half,
                        pltpu.roll(t, rw - RET_HEAD_DIM // 2, 1),
                        pltpu.roll(t, RET_HEAD_DIM // 2, 1))
        return t * cosf + rot * sinf

    q_ref[0] = rope(proj(fw, rw)).astype(BF16)
    kt_ref[0] = (rope(proj(fw + rw, rw)) * scale).T.astype(BF16)
    v_ref[0] = proj(fw + 2 * rw, rw).astype(BF16)
    g = proj(fw + 3 * rw, rw)
    sg_ref[0] = (g * jax.nn.sigmoid(g) * gng_ref[...]).astype(BF16)
    gab_ref[0, :, :d] = proj(fw + 4 * rw, d).astype(BF16)
    gab_ref[0, :, d:] = proj(fw + 4 * rw + d, d).astype(BF16)


def _proj(x, mods, gain, w_in, cosf, sinf, gn_gain, later_weights, fw, rw):
    b, l, d = x.shape
    tm = min(PROJ_TOKEN_TILE, l)
    nj = l // tm
    steps = b * nj
    tok = lambda width: pl.BlockSpec((1, tm, width), lambda i, j: (i, j, 0))
    out = lambda width: jax.ShapeDtypeStruct((b, l, width), BF16)
    sliced = [w.reshape(steps, w.shape[0] // steps, w.shape[1]) for w in later_weights]
    for w in sliced:
        assert w.shape[1] % BF16_SUBLANES == 0
    cast_spec = lambda w: pl.BlockSpec((1,) + w.shape[1:], lambda i, j: (i * nj + j, 0, 0))
    stage_rows = max(r for r in range(8, w_in.shape[0] + 1, 8)
                     if w_in.shape[0] % r == 0 and r * w_in.shape[1] * 4 <= WEIGHT_STAGE_BYTES)
    res = pl.pallas_call(
        functools.partial(_proj_kernel, fw=fw, rw=rw, d=d, scale=RET_HEAD_DIM ** -0.5, ncast=len(sliced)),
        grid=(b, nj),
        in_specs=[
            tok(d),
            _const_spec(mods.shape),
            _const_spec((1, d)),
            pl.BlockSpec((tm, cosf.shape[1]), lambda i, j: (j, 0)),
            pl.BlockSpec((tm, sinf.shape[1]), lambda i, j: (j, 0)),
            _const_spec((1, rw)),
            pl.BlockSpec(memory_space=pl.ANY),
        ] + [cast_spec(w) for w in sliced],
        out_specs=[tok(fw), tok(rw), pl.BlockSpec((1, rw, tm), lambda i, j: (i, 0, j)),
                   tok(rw), tok(rw), tok(2 * d)] + [cast_spec(w) for w in sliced],
        out_shape=[out(fw), out(rw), jax.ShapeDtypeStruct((b, rw, l), BF16),
                   out(rw), out(rw), out(2 * d)]
        + [jax.ShapeDtypeStruct(w.shape, BF16) for w in sliced],
        scratch_shapes=[pltpu.VMEM(w_in.shape, BF16),
                        pltpu.VMEM((WEIGHT_STAGE_SLOTS, stage_rows, w_in.shape[1]), F32),
                        pltpu.SemaphoreType.DMA((WEIGHT_STAGE_SLOTS,))],
        compiler_params=_params("arbitrary", "arbitrary"),
        name="proj",
    )(x, mods, gain, cosf, sinf, gn_gain, w_in, *sliced)
    casts = [c.reshape(w.shape) for c, w in zip(res[6:], later_weights)]
    return res[:6], casts


def _fft_kernel(u_ref, kw_ref, twc_ref, tws_ref, w2_ref, cs_ref, o_ref, tr_ref, ti_ref, scr_ref,
                *, la, lb, gd, npb, scale):
    r = FFT_LA_TILE
    rows = lb * r
    fw = u_ref.shape[2]
    ngroups = fw // gd
    reps = fw // twc_ref.shape[2]
    pitch = scr_ref.shape[1] // la

    @pl.when(pl.program_id(0) == 0)
    def _():
        scr_ref[...] = jnp.zeros_like(scr_ref)

    for j in range(la // r):
        u = jnp.concatenate([u_ref[0, k * la + j * r:k * la + (j + 1) * r, :] for k in range(lb)], axis=0)
        half = lb // 2 + 1
        t = _dot(kw_ref[...], u)
        mirror = [lb - k for k in range(half, lb)]
        a = jnp.concatenate([t[:half * r]] + [t[k * r:(k + 1) * r] for k in mirror], axis=0)
        b = jnp.concatenate([t[half * r:]] + [-t[(half + k) * r:(half + k + 1) * r] for k in mirror], axis=0)
        ct = _tile_lanes(twc_ref[j], reps)
        st = _tile_lanes(tws_ref[j], reps)
        tr = (a * ct + b * st).astype(BF16)
        ti = (b * ct - a * st).astype(BF16)
        for k in range(lb):
            tr_ref[k * la + j * r:k * la + (j + 1) * r, :] = tr[k * r:(k + 1) * r]
            ti_ref[k * la + j * r:k * la + (j + 1) * r, :] = ti[k * r:(k + 1) * r]

    for pb in range(lb // npb):
        xs = []
        for p in range(npb):
            lo = (pb * npb + p) * la
            t = jnp.concatenate([tr_ref[lo:lo + la, :], ti_ref[lo:lo + la, :]], axis=0)
            xs.append(_dot(w2_ref[...], t).astype(BF16))
        for gi in range(ngroups):
            cols = slice(gi * gd, (gi + 1) * gd)
            lhs = jnp.concatenate([jnp.concatenate([x[:la, cols], x[la:, cols]], axis=1) for x in xs], axis=0)
            fg = _dot(lhs, cs_ref[...]) * scale
            for p in range(npb):
                scr_ref[gi, pl.ds(p, la, stride=pitch), :] = fg[p * la:(p + 1) * la]
        full = jnp.concatenate([scr_ref[gi].reshape(la, pitch, gd)[:, :npb, :] for gi in range(ngroups)], axis=2)
        o_ref[0, :, pb * npb:(pb + 1) * npb, :] = full.astype(BF16)


def _fft(u, kw, twc, tws, w2, cs, la, lb, gd, scale):
    b, l, fw = u.shape
    npb = min(BF16_SUBLANES, lb)
    out = pl.pallas_call(
        functools.partial(_fft_kernel, la=la, lb=lb, gd=gd, npb=npb, scale=scale),
        grid=(b,),
        in_specs=[pl.BlockSpec((1, l, fw), lambda i: (i, 0, 0)),
                  _const_spec(kw.shape), _const_spec(twc.shape), _const_spec(tws.shape),
                  _const_spec(w2.shape), _const_spec(cs.shape)],
        out_specs=pl.BlockSpec((1, la, lb, fw), lambda i: (i, 0, 0, 0)),
        out_shape=jax.ShapeDtypeStruct((b, la, lb, fw), BF16),
        scratch_shapes=[pltpu.VMEM((l, fw), BF16), pltpu.VMEM((l, fw), BF16),
                        pltpu.VMEM((fw // gd, la * (npb + FFT_SCATTER_PAD), gd), F32)],
        compiler_params=_params("arbitrary"),
        name="fft",
    )(u, kw, twc, tws, w2, cs)
    return out.reshape(b, l, fw)


def _ret_kernel(q_ref, kt_ref, v_ref, sg_ref, kct_ref, vc_ref, lgl_ref, lgc_ref,
                z_ref, sf_ref, sb_ref, stf_ref, stb_ref, p_ref, o_ref,
                dall_ref, qdf_ref, qdb_ref, kdf_ref, kdb_ref, *, nchunk, unroll):
    c = RET_CHUNK
    gw = GROUP_W
    lc = vc_ref.shape[1]

    lgf, lgb = lgl_ref[0:1, :], lgl_ref[1:2, :]
    pos = lax.broadcasted_iota(jnp.int32, (c, gw), 0).astype(F32)
    qdf_ref[...] = jnp.exp(lgf * (pos + 1.0))
    qdb_ref[...] = jnp.exp(lgb * (c - pos))
    kdf_ref[...] = jnp.exp(lgf * (c - 1.0 - pos))
    kdb_ref[...] = jnp.exp(lgb * pos)
    cdf = jnp.exp(lgf * c)
    cdb = jnp.exp(lgb * c)
    cpos = lax.broadcasted_iota(jnp.int32, (lc, gw), 0).astype(F32)
    wcf = jnp.exp(lgf * (lc - 1.0 - cpos))
    wcb = jnp.exp(lgb * cpos)
    si = lax.broadcasted_iota(jnp.int32, (c, HEADS_PER_GROUP * c), 0)
    sj = lax.broadcasted_iota(jnp.int32, (c, HEADS_PER_GROUP * c), 1) & (c - 1)
    diff = (si - sj).astype(F32)
    dall_ref[...] = (jnp.where(diff >= 0, jnp.exp(lgc_ref[0:1, :] * jnp.maximum(diff, 0.0)), 0.0)
                     + jnp.where(diff <= 0, jnp.exp(lgc_ref[1:2, :] * jnp.maximum(-diff, 0.0)), 0.0))

    pw = STATE_BLOCK_W
    npair = gw // pw
    same_head_p = (lax.broadcasted_iota(jnp.int32, (pw, pw), 0) // RET_HEAD_DIM
                   == lax.broadcasted_iota(jnp.int32, (pw, pw), 1) // RET_HEAD_DIM)

    def diag_blocks(t):
        return jnp.where(same_head_p, t, 0.0)

    def weighted(t, w):
        return (t.astype(F32) * w).astype(BF16)

    def rows(ref, n):
        return ref[0, pl.ds(pl.multiple_of(n * c, c), c), :]

    def kt_chunk(n):
        return kt_ref[0, :, pl.ds(pl.multiple_of(n * c, c), c)]

    def state_update(st_ref, kt, vw, cd):
        for p in range(npair):
            cols = slice(p * pw, (p + 1) * pw)
            st_ref[p] = st_ref[p] * cd[:, cols] + diag_blocks(_dot(kt[cols, :], vw[:, cols]))

    def full_state(s_ref, n):
        zero = jnp.zeros((pw, pw), BF16)
        return jnp.concatenate(
            [jnp.concatenate([s_ref[n, p] if q == p else zero for q in range(npair)], axis=1)
             for p in range(npair)], axis=0)

    kct = kct_ref[0]
    vc = vc_ref[0]
    stf_ref[...] = jnp.zeros_like(stf_ref)
    stb_ref[...] = jnp.zeros_like(stb_ref)
    state_update(stf_ref, kct, weighted(vc, wcf), cdf)
    state_update(stb_ref, kct, weighted(vc, wcb), cdb)

    def scan_body(i, carry):
        nf = i
        nb = nchunk - 1 - i
        sf_ref[nf] = stf_ref[...].astype(BF16)
        state_update(stf_ref, kt_chunk(nf), weighted(rows(v_ref, nf), kdf_ref[...]), cdf)
        sb_ref[nb] = stb_ref[...].astype(BF16)
        state_update(stb_ref, kt_chunk(nb), weighted(rows(v_ref, nb), kdb_ref[...]), cdb)
        return carry

    lax.fori_loop(0, nchunk, scan_body, 0, unroll=unroll)

    lane = lax.broadcasted_iota(jnp.int32, (c, gw), 1)
    sub = lax.broadcasted_iota(jnp.int32, (gw, c), 0)
    lane_masks = [(lane >= h * RET_HEAD_DIM) & (lane < (h + 1) * RET_HEAD_DIM)
                  for h in range(HEADS_PER_GROUP)]
    sub_masks = [(sub >= h * RET_HEAD_DIM) & (sub < (h + 1) * RET_HEAD_DIM)
                 for h in range(HEADS_PER_GROUP)]
    same_head = (lax.broadcasted_iota(jnp.int32, (gw, gw), 0) // RET_HEAD_DIM
                 == lax.broadcasted_iota(jnp.int32, (gw, gw), 1) // RET_HEAD_DIM)
    bd_mean = jnp.where(same_head, 1.0 / RET_HEAD_DIM, 0.0).astype(BF16)

    def score_body(n, carry):
        ktn = kt_chunk(n)
        zk = jnp.zeros_like(ktn)
        kbd = jnp.concatenate([jnp.where(m, ktn, zk) for m in sub_masks], axis=1)
        p_ref[n] = (_dot(rows(q_ref, n), kbd) * dall_ref[...]).astype(BF16)
        return carry

    lax.fori_loop(0, nchunk, score_body, 0, unroll=unroll)

    def mix_body(n, carry):
        qn = rows(q_ref, n)
        vn = rows(v_ref, n)
        zv = jnp.zeros_like(vn)
        vbd = jnp.concatenate([jnp.where(m, vn, zv) for m in lane_masks], axis=0)
        o_ref[n] = (_dot(p_ref[n], vbd) + qdf_ref[...] * _dot(qn, full_state(sf_ref, n))
                    + qdb_ref[...] * _dot(qn, full_state(sb_ref, n)))
        return carry

    lax.fori_loop(0, nchunk, mix_body, 0, unroll=unroll)

    def norm_body(n, carry):
        o = o_ref[n]
        ms = _dot((o * o).astype(BF16), bd_mean)
        z = rows(sg_ref, n).astype(F32) * (o * lax.rsqrt(ms + EPS))
        z_ref[0, pl.ds(pl.multiple_of(n * c, c), c), :] = z.astype(BF16)
        return carry

    lax.fori_loop(0, nchunk, norm_body, 0, unroll=unroll)


def _retention(q, kt, v, sg, kct, vc, log_gamma):
    b, l, rw = q.shape
    lc = vc.shape[1]
    c = RET_CHUNK
    gw = GROUP_W
    hpg = HEADS_PER_GROUP
    ng = rw // gw
    nchunk = l // c
    pw = STATE_BLOCK_W
    npair = gw // pw
    assert c & (c - 1) == 0
    lg = log_gamma.reshape(2, ng, hpg).transpose(1, 0, 2)
    lg_lane = jnp.repeat(lg, RET_HEAD_DIM, axis=2)
    lg_col = jnp.repeat(lg, c, axis=2)
    tok = pl.BlockSpec((1, l, gw), lambda i, j: (i, 0, j))
    tokt = pl.BlockSpec((1, gw, l), lambda i, j: (i, j, 0))
    grp = lambda r, width: pl.BlockSpec((None, r, width), lambda i, j: (j, 0, 0))
    return pl.pallas_call(
        functools.partial(_ret_kernel, nchunk=nchunk, unroll=math.gcd(nchunk, RET_UNROLL)),
        grid=(b, ng),
        in_specs=[
            tok, tokt, tok, tok,
            pl.BlockSpec((1, gw, lc), lambda i, j: (i, j, 0)),
            pl.BlockSpec((1, lc, gw), lambda i, j: (i, 0, j)),
            grp(2, gw), grp(2, hpg * c),
        ],
        out_specs=tok,
        out_shape=jax.ShapeDtypeStruct((b, l, rw), BF16),
        scratch_shapes=[pltpu.VMEM((nchunk, npair, pw, pw), BF16), pltpu.VMEM((nchunk, npair, pw, pw), BF16),
                        pltpu.VMEM((npair, pw, pw), F32), pltpu.VMEM((npair, pw, pw), F32),
                        pltpu.VMEM((nchunk, c, hpg * c), BF16),
                        pltpu.VMEM((nchunk, c, gw), F32),
                        pltpu.VMEM((c, hpg * c), F32)] + [pltpu.VMEM((c, gw), F32)] * 4,
        compiler_params=_params("arbitrary", "arbitrary"),
        name="ret",
    )(q, kt, v, sg, kct, vc, lg_lane, lg_col)


def _out_kernel(x_ref, mod_ref, fm_ref, z_ref, gab_ref, gain2_ref, fgain_ref,
                w4_hbm, wr_hbm, wo_hbm, w1_hbm, w2_hbm, o_ref,
                w4_ref, wr_ref, wo_ref, w1_ref, w2_ref, sem_ref, *, ff_chunk):
    dff = w1_ref.shape[1]
    chunks = list(range(0, dff, ff_chunk))
    pieces = [("w4", w4_hbm, w4_ref), ("wr", wr_hbm, wr_ref), ("wo", wo_hbm, wo_ref)]
    for lo in chunks:
        pieces.append((("w1", lo), w1_hbm.at[:, pl.ds(lo, ff_chunk)], w1_ref.at[:, pl.ds(lo, ff_chunk)]))
        pieces.append((("w2", lo), w2_hbm.at[pl.ds(lo, ff_chunk), :], w2_ref.at[pl.ds(lo, ff_chunk), :]))
    copies = {name: pltpu.make_async_copy(src, dst, sem_ref.at[k]) for k, (name, src, dst) in enumerate(pieces)}

    def body(ready):
        mod = mod_ref[:, pl.ds(pl.program_id(0), 1), :]
        g1, sh2, sc2, g2 = mod[2], mod[3], mod[4], mod[5]
        d = x_ref.shape[2]
        ready("w4")
        y_four = _dot(fm_ref[0], w4_ref[...])
        ready("wr")
        y_ret = _dot(z_ref[0], wr_ref[...])
        y = (jax.nn.sigmoid(gab_ref[0, :, :d].astype(F32)) * y_four
             + jax.nn.sigmoid(gab_ref[0, :, d:].astype(F32)) * y_ret)
        ready("wo")
        x1 = x_ref[0] + g1 * _dot(y.astype(BF16), wo_ref[...])
        h2 = _norm_mod(x1, gain2_ref[...], sh2, sc2).astype(BF16)
        acc = None
        for lo in chunks:
            ready(("w1", lo))
            hid = jnp.maximum(_dot(h2, w1_ref[:, lo:lo + ff_chunk]), 0.0)
            ready(("w2", lo))
            part = _dot((hid * hid).astype(BF16), w2_ref[lo:lo + ff_chunk, :])
            acc = part if acc is None else acc + part
        x2 = x1 + g2 * acc
        ms = jnp.mean(x2 * x2, axis=-1, keepdims=True)
        o_ref[0] = x2 * lax.rsqrt(ms + EPS) * fgain_ref[...]

    first = _is_first_step(2)

    @pl.when(first)
    def _():
        for cp in copies.values():
            cp.start()
        body(lambda name: copies[name].wait())

    @pl.when(jnp.logical_not(first))
    def _():
        body(lambda name: None)


def _out(x, mods, fm, z, gab, w4, wr, wo, gain2, w1, w2, fgain):
    b, l, d = x.shape
    tm = min(TOKEN_TILE, l)
    tok = lambda width: pl.BlockSpec((1, tm, width), lambda i, j: (i, j, 0))
    weights = (w4, wr, wo, w1, w2)
    ff_chunk = min(1024, w1.shape[1])
    npieces = 3 + 2 * (w1.shape[1] // ff_chunk)
    return pl.pallas_call(
        functools.partial(_out_kernel, ff_chunk=ff_chunk),
        grid=(b, l // tm),
        in_specs=[
            tok(d),
            _const_spec(mods.shape),
            tok(fm.shape[2]), tok(z.shape[2]), tok(2 * d),
            _const_spec((1, d)), _const_spec((1, d)),
        ] + [pl.BlockSpec(memory_space=pl.ANY)] * len(weights),
        out_specs=tok(d),
        out_shape=jax.ShapeDtypeStruct((b, l, d), F32),
        scratch_shapes=[pltpu.VMEM(w.shape, BF16) for w in weights] + [pltpu.SemaphoreType.DMA((npieces,))],
        compiler_params=_params("arbitrary", "arbitrary"),
        name="out",
    )(x, mods, fm, z, gab, gain2, fgain, *weights)


def _dft_tables(l, gd):
    la = FFT_LA
    lb = l // la

    def cs(n):
        idx = np.arange(n)
        ang = 2.0 * np.pi * ((idx[:, None] * idx[None, :]) % n) / n
        return np.cos(ang), np.sin(ang)

    cb, sb = cs(lb)
    eye = np.eye(FFT_LA_TILE)
    half = lb // 2 + 1
    kw = np.concatenate([np.kron(cb[:half], eye), -np.kron(sb[:half], eye)], axis=0)
    ca, sa = cs(la)
    w2 = np.block([[ca, sa], [-sa, ca]])
    cc, sc = cs(gd)
    chan = np.concatenate([cc, sc], axis=0)
    tw = 2.0 * np.pi * (np.arange(lb)[:, None] * np.arange(la)[None, :]) / l
    tw = tw.reshape(lb, la // FFT_LA_TILE, FFT_LA_TILE).transpose(1, 0, 2).reshape(la // FFT_LA_TILE, -1)
    twc = np.repeat(np.cos(tw)[:, :, None], LANES, axis=2)
    tws = np.repeat(np.sin(tw)[:, :, None], LANES, axis=2)
    as_bf = lambda a: jnp.asarray(a, dtype=F32).astype(BF16)
    return as_bf(kw), as_bf(w2), as_bf(chan), jnp.asarray(twc, F32), jnp.asarray(tws, F32), la, lb


def _rope_tables(l):
    f32 = np.float32
    nf = RET_HEAD_DIM // 4
    inv = np.power(f32(ROPE_BASE), -np.arange(nf, dtype=f32) / f32(nf)).astype(f32)
    rows = l // GRID_W
    r, cc = np.meshgrid(np.arange(rows, dtype=f32), np.arange(GRID_W, dtype=f32), indexing="ij")
    ang = np.concatenate([r.reshape(-1)[:, None] * inv, cc.reshape(-1)[:, None] * inv], axis=-1).astype(f32)
    cos, sin = np.cos(ang).astype(f32), np.sin(ang).astype(f32)
    cos_h = np.concatenate([cos, cos], axis=1)
    sin_h = np.concatenate([-sin, sin], axis=1)
    reps = LANES // RET_HEAD_DIM
    return jnp.asarray(np.tile(cos_h, (1, reps))), jnp.asarray(np.tile(sin_h, (1, reps)))


def kernel(x, c, ctx, c_ctx, w_ada, b_ada, norm1_gain, w_in, four_w_out, ret_decay_logit,
           ret_gn_gain, ret_w_out, w_out, norm2_gain, w_mlp1, w_mlp2, final_gain):
    assert w_ada.shape[0] == 1, "single-layer block"
    b, l, d = x.shape
    lc = ctx.shape[1]
    fw = four_w_out.shape[1]
    rw = ret_w_out.shape[1]
    gd = fw // FOUR_GROUPS
    assert l % FFT_LA == 0 and l % RET_CHUNK == 0 and rw % GROUP_W == 0

    mods, kct, vc = _ada(c, c_ctx, w_ada[0], b_ada, ctx, norm1_gain, w_in[0], fw, rw)

    kw, w2, chan, twc, tws, la, lb = _dft_tables(l, gd)
    cosf, sinf = _rope_tables(l)

    later = [four_w_out[0], ret_w_out[0], w_out[0], w_mlp1[0], w_mlp2[0]]
    (u, q, kt, v, sg, gab), later_b = _proj(x, mods, norm1_gain, w_in[0], cosf, sinf, ret_gn_gain,
                                             later, fw, rw)

    fm = _fft(u, kw, twc, tws, w2, chan, la, lb, gd, 1.0 / math.sqrt(l * gd))

    log_gamma = jax.nn.log_sigmoid(ret_decay_logit[0].astype(F32))
    z = _retention(q, kt, v, sg, kct, vc, log_gamma)

    w4, wr, wo, w1, w2 = later_b
    return _out(x, mods, fm, z, gab, w4, wr, wo, norm2_gain, w1, w2, final_gain[None, :])
```

```python
import functools
import math

import jax
import jax.numpy as jnp
import numpy as np
from jax import lax
from jax.experimental import pallas as pl
from jax.experimental.pallas import tpu as pltpu

F32 = jnp.float32
BF16 = jnp.bfloat16

GRID_W = 64
FOUR_GROUPS = 4
RET_HEAD_DIM = 64
N_MOD = 6
CTX_FIRST_STEP = 2
ROPE_BASE = 10000.0
EPS = 1e-6

LANES = 128
MXU_DIM = 256
F32_SUBLANES = 8
BF16_SUBLANES = 16
VMEM_LIMIT_BYTES = 56 * 1024 * 1024

RET_CHUNK = 128
RET_UNROLL = 32
HEADS_PER_GROUP = MXU_DIM // RET_HEAD_DIM
GROUP_W = HEADS_PER_GROUP * RET_HEAD_DIM
STATE_BLOCK_W = LANES
FFT_LA = 128
FFT_LA_TILE = BF16_SUBLANES
FFT_SCATTER_PAD = F32_SUBLANES
TOKEN_TILE = 512
PROJ_TOKEN_TILE = 1024
WEIGHT_STAGE_BYTES = 2 * 1024 * 1024
WEIGHT_STAGE_SLOTS = 4


def _dot(a, b):
    return jnp.dot(a, b, preferred_element_type=F32)


def _norm_mod(x, gain, shift, scale):
    ms = jnp.mean(x * x, axis=-1, keepdims=True)
    y = x * lax.rsqrt(ms + EPS) * gain
    return y * (1.0 + scale) + shift


def _tile_lanes(t, reps):
    return jnp.concatenate([t] * reps, axis=1) if reps > 1 else t


def _const_spec(shape):
    nd = len(shape)
    return pl.BlockSpec(shape, lambda *_: (0,) * nd, pipeline_mode=pl.Buffered(1))


def _params(*sem):
    return pltpu.CompilerParams(dimension_semantics=sem, vmem_limit_bytes=VMEM_LIMIT_BYTES)


def _load_weights_as_bf16(pairs, stage_ref, sem_ref):
    slots, sr, sc = stage_ref.shape
    chunks = [(src, dst, r0, c0)
              for src, dst in pairs
              for r0 in range(0, src.shape[0], sr)
              for c0 in range(0, src.shape[1], sc)]

    def copy(k):
        src, _, r0, c0 = chunks[k]
        slot = k % slots
        return pltpu.make_async_copy(src.at[pl.ds(r0, sr), pl.ds(c0, sc)], stage_ref.at[slot], sem_ref.at[slot])

    for k in range(min(slots - 1, len(chunks))):
        copy(k).start()
    for k, (_, dst, r0, c0) in enumerate(chunks):
        if k + slots - 1 < len(chunks):
            copy(k + slots - 1).start()
        copy(k).wait()
        dst[r0:r0 + sr, c0:c0 + sc] = stage_ref[k % slots].astype(BF16)


def _is_first_step(grid_rank):
    first = pl.program_id(0) == 0
    for axis in range(1, grid_rank):
        first = first & (pl.program_id(axis) == 0)
    return first


def _ada_kernel(c_ref, cctx_ref, w_ref, b_ref, ctx_ref, gain_ref, wkv_ref,
                o_ref, kt_ref, v_ref, cv_ref, modc_ref, wkv_bf_ref, *, rw, scale):
    j = pl.program_id(0)
    b = c_ref.shape[0]
    cv_ref[...] = jnp.zeros_like(cv_ref)
    cv_ref[0:b, :] = c_ref[...]
    cv_ref[b:b + 1, :] = cctx_ref[...]
    c = cv_ref[...]
    s = c * jax.nn.sigmoid(c)
    mod = _dot(s.astype(BF16), w_ref[...].astype(BF16)) + b_ref[...]
    o_ref[0] = mod

    @pl.when(j < CTX_FIRST_STEP)
    def _():
        modc_ref[pl.ds(j, 1), :] = mod[b:b + 1, :]

    @pl.when(j == CTX_FIRST_STEP)
    def _():
        wkv_bf_ref[...] = wkv_ref[...].astype(BF16)

    @pl.when(j >= CTX_FIRST_STEP)
    def _():
        h = _norm_mod(ctx_ref[0], gain_ref[...], modc_ref[0:1, :], modc_ref[1:2, :]).astype(BF16)
        kt_ref[0] = (_dot(h, wkv_bf_ref[:, :rw]) * scale).T.astype(BF16)
        v_ref[0] = _dot(h, wkv_bf_ref[:, rw:]).astype(BF16)


def _ada(c, c_ctx, w_ada, b_ada, ctx, gain, w_in, fw, rw):
    b, d = c.shape
    lc = ctx.shape[1]
    rows = -(-(b + 1) // F32_SUBLANES) * F32_SUBLANES
    assert w_ada.shape[1] == N_MOD * d and b <= N_MOD - CTX_FIRST_STEP
    kv_block = (fw + rw) // (2 * rw)
    assert kv_block * 2 * rw == fw + rw
    sample = lambda j: jnp.clip(j - CTX_FIRST_STEP, 0, b - 1)
    return pl.pallas_call(
        functools.partial(_ada_kernel, rw=rw, scale=RET_HEAD_DIM ** -0.5),
        grid=(N_MOD,),
        in_specs=[
            pl.BlockSpec((b, d), lambda j: (0, 0)),
            pl.BlockSpec((1, d), lambda j: (0, 0)),
            pl.BlockSpec((d, d), lambda j: (0, j)),
            pl.BlockSpec((1, d), lambda j: (0, j)),
            pl.BlockSpec((1, lc, d), lambda j: (sample(j), 0, 0)),
            pl.BlockSpec((1, d), lambda j: (0, 0)),
            pl.BlockSpec((d, 2 * rw), lambda j: (0, kv_block)),
        ],
        out_specs=[pl.BlockSpec((1, rows, d), lambda j: (j, 0, 0)),
                   pl.BlockSpec((1, rw, lc), lambda j: (sample(j), 0, 0)),
                   pl.BlockSpec((1, lc, rw), lambda j: (sample(j), 0, 0))],
        out_shape=[jax.ShapeDtypeStruct((N_MOD, rows, d), F32),
                   jax.ShapeDtypeStruct((b, rw, lc), BF16), jax.ShapeDtypeStruct((b, lc, rw), BF16)],
        scratch_shapes=[pltpu.VMEM((rows, d), F32), pltpu.VMEM((CTX_FIRST_STEP, d), F32),
                        pltpu.VMEM((d, 2 * rw), BF16)],
        compiler_params=_params("arbitrary"),
        name="ada",
    )(c, c_ctx[None, :], w_ada, b_ada, ctx, gain, w_in)


def _proj_kernel(*refs, fw, rw, d, scale, ncast):
    x_ref, mod_ref, cos_ref, sin_ref, w_hbm = refs[:5]
    cast_in = refs[5:5 + ncast]
    gain_ref, gng_ref = refs[5 + ncast:7 + ncast]
    u_ref, q_ref, kt_ref, v_ref, sg_ref, gab_ref = refs[7 + ncast:13 + ncast]
    cast_out = refs[13 + ncast:13 + 2 * ncast]
    w_ref, stage_ref, sem_ref = refs[13 + 2 * ncast:]

    @pl.when(_is_first_step(2))
    def _():
        _load_weights_as_bf16([(w_hbm, w_ref)], stage_ref, sem_ref)

    for src, dst in zip(cast_in, cast_out):
        dst[...] = src[...].astype(BF16)

    mod = mod_ref[:, pl.ds(pl.program_id(0), 1), :]
    h = _norm_mod(x_ref[0], gain_ref[...], mod[0], mod[1]).astype(BF16)
    tm = h.shape[0]

    def proj(lo, width):
        return _dot(h, w_ref[:, lo:lo + width])

    u_ref[0] = proj(0, fw).astype(BF16)

    reps = rw // cos_ref.shape[1]
    cosf = _tile_lanes(cos_ref[...], reps)
    sinf = _tile_lanes(sin_ref[...], reps)
    lane = lax.broadcasted_iota(jnp.int32, (tm, rw), 1)
    first_half = (lane & (RET_HEAD_DIM // 2)) == 0

    def rope(t):
        rot = jnp.where(first_half,
                        pltpu.roll(t, rw - RET_HEAD_DIM // 2, 1),
                        pltpu.roll(t, RET_HEAD_DIM // 2, 1))
        return t * cosf + rot * sinf

    q_ref[0] = rope(proj(fw, rw)).astype(BF16)
    kt_ref[0] = (rope(proj(fw + rw, rw)) * scale).T.astype(BF16)
    v_ref[0] = proj(fw + 2 * rw, rw).astype(BF16)
    g = proj(fw + 3 * rw, rw)
    sg_ref[0] = (g * jax.nn.sigmoid(g) * gng_ref[...]).astype(BF16)
    gab_ref[0, :, :d] = proj(fw + 4 * rw, d).astype(BF16)
    gab_ref[0, :, d:] = proj(fw + 4 * rw + d, d).astype(BF16)


def _proj(x, mods, gain, w_in, cosf, sinf, gn_gain, later_weights, fw, rw):
    b, l, d = x.shape
    tm = min(PROJ_TOKEN_TILE, l)
    nj = l // tm
    steps = b * nj
    tok = lambda width: pl.BlockSpec((1, tm, width), lambda i, j: (i, j, 0))
    out = lambda width: jax.ShapeDtypeStruct((b, l, width), BF16)
    sliced = [w.reshape(steps, w.shape[0] // steps, w.shape[1]) for w in later_weights]
    for w in sliced:
        assert w.shape[1] % BF16_SUBLANES == 0
    cast_spec = lambda w: pl.BlockSpec((1,) + w.shape[1:], lambda i, j: (i * nj + j, 0, 0))
    stage_rows = max(r for r in range(8, w_in.shape[0] + 1, 8)
                     if w_in.shape[0] % r == 0 and r * w_in.shape[1] * 4 <= WEIGHT_STAGE_BYTES)
    res = pl.pallas_call(
        functools.partial(_proj_kernel, fw=fw, rw=rw, d=d, scale=RET_HEAD_DIM ** -0.5, ncast=len(sliced)),
        grid=(b, nj),
        in_specs=[
            tok(d),
            _const_spec(mods.shape),
            pl.BlockSpec((tm, cosf.shape[1]), lambda i, j: (j, 0)),
            pl.BlockSpec((tm, sinf.shape[1]), lambda i, j: (j, 0)),
            pl.BlockSpec(memory_space=pl.ANY),
        ] + [cast_spec(w) for w in sliced] + [_const_spec((1, d)), _const_spec((1, rw))],
        out_specs=[tok(fw), tok(rw), pl.BlockSpec((1, rw, tm), lambda i, j: (i, 0, j)),
                   tok(rw), tok(rw), tok(2 * d)] + [cast_spec(w) for w in sliced],
        out_shape=[out(fw), out(rw), jax.ShapeDtypeStruct((b, rw, l), BF16),
                   out(rw), out(rw), out(2 * d)]
        + [jax.ShapeDtypeStruct(w.shape, BF16) for w in sliced],
        scratch_shapes=[pltpu.VMEM(w_in.shape, BF16),
                        pltpu.VMEM((WEIGHT_STAGE_SLOTS, stage_rows, w_in.shape[1]), F32),
                        pltpu.SemaphoreType.DMA((WEIGHT_STAGE_SLOTS,))],
        compiler_params=_params("arbitrary", "arbitrary"),
        name="proj",
    )(x, mods, cosf, sinf, w_in, *sliced, gain, gn_gain)
    casts = [c.reshape(w.shape) for c, w in zip(res[6:], later_weights)]
    return res[:6], casts


def _fft_kernel(u_ref, kw_ref, twc_ref, tws_ref, w2_ref, cs_ref, o_ref, tr_ref, ti_ref, scr_ref,
                *, la, lb, gd, npb, scale):
    r = FFT_LA_TILE
    rows = lb * r
    fw = u_ref.shape[2]
    ngroups = fw // gd
    reps = fw // twc_ref.shape[2]
    pitch = scr_ref.shape[1] // la

    @pl.when(pl.program_id(0) == 0)
    def _():
        scr_ref[...] = jnp.zeros_like(scr_ref)

    for j in range(la // r):
        u = jnp.concatenate([u_ref[0, k * la + j * r:k * la + (j + 1) * r, :] for k in range(lb)], axis=0)
        half = lb // 2 + 1
        t = _dot(kw_ref[...], u)
        mirror = [lb - k for k in range(half, lb)]
        a = jnp.concatenate([t[:half * r]] + [t[k * r:(k + 1) * r] for k in mirror], axis=0)
        b = jnp.concatenate([t[half * r:]] + [-t[(half + k) * r:(half + k + 1) * r] for k in mirror], axis=0)
        ct = _tile_lanes(twc_ref[j], reps)
        st = _tile_lanes(tws_ref[j], reps)
        tr = (a * ct + b * st).astype(BF16)
        ti = (b * ct - a * st).astype(BF16)
        for k in range(lb):
            tr_ref[k * la + j * r:k * la + (j + 1) * r, :] = tr[k * r:(k + 1) * r]
            ti_ref[k * la + j * r:k * la + (j + 1) * r, :] = ti[k * r:(k + 1) * r]

    for pb in range(lb // npb):
        xs = []
        for p in range(npb):
            lo = (pb * npb + p) * la
            t = jnp.concatenate([tr_ref[lo:lo + la, :], ti_ref[lo:lo + la, :]], axis=0)
            xs.append(_dot(w2_ref[...], t).astype(BF16))
        for gi in range(ngroups):
            cols = slice(gi * gd, (gi + 1) * gd)
            lhs = jnp.concatenate([jnp.concatenate([x[:la, cols], x[la:, cols]], axis=1) for x in xs], axis=0)
            fg = _dot(lhs, cs_ref[...]) * scale
            for p in range(npb):
                scr_ref[gi, pl.ds(p, la, stride=pitch), :] = fg[p * la:(p + 1) * la]
        full = jnp.concatenate([scr_ref[gi].reshape(la, pitch, gd)[:, :npb, :] for gi in range(ngroups)], axis=2)
        o_ref[0, :, pb * npb:(pb + 1) * npb, :] = full.astype(BF16)


def _fft(u, kw, twc, tws, w2, cs, la, lb, gd, scale):
    b, l, fw = u.shape
    npb = min(BF16_SUBLANES, lb)
    out = pl.pallas_call(
        functools.partial(_fft_kernel, la=la, lb=lb, gd=gd, npb=npb, scale=scale),
        grid=(b,),
        in_specs=[pl.BlockSpec((1, l, fw), lambda i: (i, 0, 0)),
                  _const_spec(kw.shape), _const_spec(twc.shape), _const_spec(tws.shape),
                  _const_spec(w2.shape), _const_spec(cs.shape)],
        out_specs=pl.BlockSpec((1, la, lb, fw), lambda i: (i, 0, 0, 0)),
        out_shape=jax.ShapeDtypeStruct((b, la, lb, fw), BF16),
        scratch_shapes=[pltpu.VMEM((l, fw), BF16), pltpu.VMEM((l, fw), BF16),
                        pltpu.VMEM((fw // gd, la * (npb + FFT_SCATTER_PAD), gd), F32)],
        compiler_params=_params("arbitrary"),
        name="fft",
    )(u, kw, twc, tws, w2, cs)
    return out.reshape(b, l, fw)


def _ret_kernel(q_ref, kt_ref, v_ref, sg_ref, kct_ref, vc_ref, lgl_ref, lgc_ref,
                z_ref, sf_ref, sb_ref, stf_ref, stb_ref, p_ref, o_ref,
                dall_ref, qdf_ref, qdb_ref, kdf_ref, kdb_ref, *, nchunk, unroll):
    c = RET_CHUNK
    gw = GROUP_W
    lc = vc_ref.shape[1]

    lgf, lgb = lgl_ref[0:1, :], lgl_ref[1:2, :]
    pos = lax.broadcasted_iota(jnp.int32, (c, gw), 0).astype(F32)
    qdf_ref[...] = jnp.exp(lgf * (pos + 1.0))
    qdb_ref[...] = jnp.exp(lgb * (c - pos))
    kdf_ref[...] = jnp.exp(lgf * (c - 1.0 - pos))
    kdb_ref[...] = jnp.exp(lgb * pos)
    cdf = jnp.exp(lgf * c)
    cdb = jnp.exp(lgb * c)
    cpos = lax.broadcasted_iota(jnp.int32, (lc, gw), 0).astype(F32)
    wcf = jnp.exp(lgf * (lc - 1.0 - cpos))
    wcb = jnp.exp(lgb * cpos)
    si = lax.broadcasted_iota(jnp.int32, (c, HEADS_PER_GROUP * c), 0)
    sj = lax.broadcasted_iota(jnp.int32, (c, HEADS_PER_GROUP * c), 1) & (c - 1)
    diff = (si - sj).astype(F32)
    dall_ref[...] = (jnp.where(diff >= 0, jnp.exp(lgc_ref[0:1, :] * jnp.maximum(diff, 0.0)), 0.0)
                     + jnp.where(diff <= 0, jnp.exp(lgc_ref[1:2, :] * jnp.maximum(-diff, 0.0)), 0.0))

    pw = STATE_BLOCK_W
    npair = gw // pw
    same_head_p = (lax.broadcasted_iota(jnp.int32, (pw, pw), 0) // RET_HEAD_DIM
                   == lax.broadcasted_iota(jnp.int32, (pw, pw), 1) // RET_HEAD_DIM)

    def diag_blocks(t):
        return jnp.where(same_head_p, t, 0.0)

    def weighted(t, w):
        return (t.astype(F32) * w).astype(BF16)

    def rows(ref, n):
        return ref[0, pl.ds(pl.multiple_of(n * c, c), c), :]

    def kt_chunk(n):
        return kt_ref[0, :, pl.ds(pl.multiple_of(n * c, c), c)]

    def state_update(st_ref, kt, vw, cd):
        for p in range(npair):
            cols = slice(p * pw, (p + 1) * pw)
            st_ref[p] = st_ref[p] * cd[:, cols] + diag_blocks(_dot(kt[cols, :], vw[:, cols]))

    def full_state(s_ref, n):
        zero = jnp.zeros((pw, pw), BF16)
        return jnp.concatenate(
            [jnp.concatenate([s_ref[n, p] if q == p else zero for q in range(npair)], axis=1)
             for p in range(npair)], axis=0)

    kct = kct_ref[0]
    vc = vc_ref[0]
    stf_ref[...] = jnp.zeros_like(stf_ref)
    stb_ref[...] = jnp.zeros_like(stb_ref)
    state_update(stf_ref, kct, weighted(vc, wcf), cdf)
    state_update(stb_ref, kct, weighted(vc, wcb), cdb)

    def scan_body(i, carry):
        nf = i
        nb = nchunk - 1 - i
        sf_ref[nf] = stf_ref[...].astype(BF16)
        state_update(stf_ref, kt_chunk(nf), weighted(rows(v_ref, nf), kdf_ref[...]), cdf)
        sb_ref[nb] = stb_ref[...].astype(BF16)
        state_update(stb_ref, kt_chunk(nb), weighted(rows(v_ref, nb), kdb_ref[...]), cdb)
        return carry

    lax.fori_loop(0, nchunk, scan_body, 0, unroll=unroll)

    lane = lax.broadcasted_iota(jnp.int32, (c, gw), 1)
    sub = lax.broadcasted_iota(jnp.int32, (gw, c), 0)
    lane_masks = [(lane >= h * RET_HEAD_DIM) & (lane < (h + 1) * RET_HEAD_DIM)
                  for h in range(HEADS_PER_GROUP)]
    sub_masks = [(sub >= h * RET_HEAD_DIM) & (sub < (h + 1) * RET_HEAD_DIM)
                 for h in range(HEADS_PER_GROUP)]
    same_head = (lax.broadcasted_iota(jnp.int32, (gw, gw), 0) // RET_HEAD_DIM
                 == lax.broadcasted_iota(jnp.int32, (gw, gw), 1) // RET_HEAD_DIM)
    bd_mean = jnp.where(same_head, 1.0 / RET_HEAD_DIM, 0.0).astype(BF16)

    def score_body(n, carry):
        ktn = kt_chunk(n)
        zk = jnp.zeros_like(ktn)
        kbd = jnp.concatenate([jnp.where(m, ktn, zk) for m in sub_masks], axis=1)
        p_ref[n] = (_dot(rows(q_ref, n), kbd) * dall_ref[...]).astype(BF16)
        return carry

    lax.fori_loop(0, nchunk, score_body, 0, unroll=unroll)

    def mix_body(n, carry):
        qn = rows(q_ref, n)
        vn = rows(v_ref, n)
        zv = jnp.zeros_like(vn)
        vbd = jnp.concatenate([jnp.where(m, vn, zv) for m in lane_masks], axis=0)
        o_ref[n] = (_dot(p_ref[n], vbd) + qdf_ref[...] * _dot(qn, full_state(sf_ref, n))
                    + qdb_ref[...] * _dot(qn, full_state(sb_ref, n)))
        return carry

    lax.fori_loop(0, nchunk, mix_body, 0, unroll=unroll)

    def norm_body(n, carry):
        o = o_ref[n]
        ms = _dot((o * o).astype(BF16), bd_mean)
        z = rows(sg_ref, n).astype(F32) * (o * lax.rsqrt(ms + EPS))
        z_ref[0, pl.ds(pl.multiple_of(n * c, c), c), :] = z.astype(BF16)
        return carry

    lax.fori_loop(0, nchunk, norm_body, 0, unroll=unroll)


def _retention(q, kt, v, sg, kct, vc, log_gamma):
    b, l, rw = q.shape
    lc = vc.shape[1]
    c = RET_CHUNK
    gw = GROUP_W
    hpg = HEADS_PER_GROUP
    ng = rw // gw
    nchunk = l // c
    pw = STATE_BLOCK_W
    npair = gw // pw
    assert c & (c - 1) == 0
    lg = log_gamma.reshape(2, ng, hpg).transpose(1, 0, 2)
    lg_lane = jnp.repeat(lg, RET_HEAD_DIM, axis=2)
    lg_col = jnp.repeat(lg, c, axis=2)
    tok = pl.BlockSpec((1, l, gw), lambda i, j: (i, 0, j))
    tokt = pl.BlockSpec((1, gw, l), lambda i, j: (i, j, 0))
    grp = lambda r, width: pl.BlockSpec((None, r, width), lambda i, j: (j, 0, 0))
    return pl.pallas_call(
        functools.partial(_ret_kernel, nchunk=nchunk, unroll=math.gcd(nchunk, RET_UNROLL)),
        grid=(b, ng),
        in_specs=[
            tok, tokt, tok, tok,
            pl.BlockSpec((1, gw, lc), lambda i, j: (i, j, 0)),
            pl.BlockSpec((1, lc, gw), lambda i, j: (i, 0, j)),
            grp(2, gw), grp(2, hpg * c),
        ],
        out_specs=tok,
        out_shape=jax.ShapeDtypeStruct((b, l, rw), BF16),
        scratch_shapes=[pltpu.VMEM((nchunk, npair, pw, pw), BF16), pltpu.VMEM((nchunk, npair, pw, pw), BF16),
                        pltpu.VMEM((npair, pw, pw), F32), pltpu.VMEM((npair, pw, pw), F32),
                        pltpu.VMEM((nchunk, c, hpg * c), BF16),
                        pltpu.VMEM((nchunk, c, gw), F32),
                        pltpu.VMEM((c, hpg * c), F32)] + [pltpu.VMEM((c, gw), F32)] * 4,
        compiler_params=_params("arbitrary", "arbitrary"),
        name="ret",
    )(q, kt, v, sg, kct, vc, lg_lane, lg_col)


def _out_kernel(x_ref, mod_ref, fm_ref, z_ref, gab_ref, w4_ref, wr_ref, wo_ref, w1_ref, w2_ref,
                gain2_ref, fgain_ref, o_ref, *, ff_chunk):
    mod = mod_ref[:, pl.ds(pl.program_id(0), 1), :]
    g1, sh2, sc2, g2 = mod[2], mod[3], mod[4], mod[5]
    y_four = _dot(fm_ref[0], w4_ref[...])
    y_ret = _dot(z_ref[0], wr_ref[...])
    d = x_ref.shape[2]
    y = (jax.nn.sigmoid(gab_ref[0, :, :d].astype(F32)) * y_four
         + jax.nn.sigmoid(gab_ref[0, :, d:].astype(F32)) * y_ret)
    x1 = x_ref[0] + g1 * _dot(y.astype(BF16), wo_ref[...])
    h2 = _norm_mod(x1, gain2_ref[...], sh2, sc2).astype(BF16)
    dff = w1_ref.shape[1]
    acc = None
    for lo in range(0, dff, ff_chunk):
        hid = jnp.maximum(_dot(h2, w1_ref[:, lo:lo + ff_chunk]), 0.0)
        part = _dot((hid * hid).astype(BF16), w2_ref[lo:lo + ff_chunk, :])
        acc = part if acc is None else acc + part
    x2 = x1 + g2 * acc
    ms = jnp.mean(x2 * x2, axis=-1, keepdims=True)
    o_ref[0] = x2 * lax.rsqrt(ms + EPS) * fgain_ref[...]


def _out(x, mods, fm, z, gab, w4, wr, wo, gain2, w1, w2, fgain):
    b, l, d = x.shape
    tm = min(TOKEN_TILE, l)
    tok = lambda width: pl.BlockSpec((1, tm, width), lambda i, j: (i, j, 0))
    weights = (w4, wr, wo, w1, w2)
    return pl.pallas_call(
        functools.partial(_out_kernel, ff_chunk=min(1024, w1.shape[1])),
        grid=(b, l // tm),
        in_specs=[
            tok(d),
            _const_spec(mods.shape),
            tok(fm.shape[2]), tok(z.shape[2]), tok(2 * d),
        ] + [_const_spec(w.shape) for w in weights] + [_const_spec((1, d)), _const_spec((1, d))],
        out_specs=tok(d),
        out_shape=jax.ShapeDtypeStruct((b, l, d), F32),
        compiler_params=_params("arbitrary", "arbitrary"),
        name="out",
    )(x, mods, fm, z, gab, *weights, gain2, fgain)


def _dft_tables(l, gd):
    la = FFT_LA
    lb = l // la

    def cs(n):
        idx = np.arange(n)
        ang = 2.0 * np.pi * ((idx[:, None] * idx[None, :]) % n) / n
        return np.cos(ang), np.sin(ang)

    cb, sb = cs(lb)
    eye = np.eye(FFT_LA_TILE)
    half = lb // 2 + 1
    kw = np.concatenate([np.kron(cb[:half], eye), -np.kron(sb[:half], eye)], axis=0)
    ca, sa = cs(la)
    w2 = np.block([[ca, sa], [-sa, ca]])
    cc, sc = cs(gd)
    chan = np.concatenate([cc, sc], axis=0)
    tw = 2.0 * np.pi * (np.arange(lb)[:, None] * np.arange(la)[None, :]) / l
    tw = tw.reshape(lb, la // FFT_LA_TILE, FFT_LA_TILE).transpose(1, 0, 2).reshape(la // FFT_LA_TILE, -1)
    twc = np.repeat(np.cos(tw)[:, :, None], LANES, axis=2)
    tws = np.repeat(np.sin(tw)[:, :, None], LANES, axis=2)
    as_bf = lambda a: jnp.asarray(a, dtype=F32).astype(BF16)
    return as_bf(kw), as_bf(w2), as_bf(chan), jnp.asarray(twc, F32), jnp.asarray(tws, F32), la, lb


def _rope_tables(l):
    f32 = np.float32
    nf = RET_HEAD_DIM // 4
    inv = np.power(f32(ROPE_BASE), -np.arange(nf, dtype=f32) / f32(nf)).astype(f32)
    rows = l // GRID_W
    r, cc = np.meshgrid(np.arange(rows, dtype=f32), np.arange(GRID_W, dtype=f32), indexing="ij")
    ang = np.concatenate([r.reshape(-1)[:, None] * inv, cc.reshape(-1)[:, None] * inv], axis=-1).astype(f32)
    cos, sin = np.cos(ang).astype(f32), np.sin(ang).astype(f32)
    cos_h = np.concatenate([cos, cos], axis=1)
    sin_h = np.concatenate([-sin, sin], axis=1)
    reps = LANES // RET_HEAD_DIM
    return jnp.asarray(np.tile(cos_h, (1, reps))), jnp.asarray(np.tile(sin_h, (1, reps)))


def kernel(x, c, ctx, c_ctx, w_ada, b_ada, norm1_gain, w_in, four_w_out, ret_decay_logit,
           ret_gn_gain, ret_w_out, w_out, norm2_gain, w_mlp1, w_mlp2, final_gain):
    assert w_ada.shape[0] == 1, "single-layer block"
    b, l, d = x.shape
    lc = ctx.shape[1]
    fw = four_w_out.shape[1]
    rw = ret_w_out.shape[1]
    gd = fw // FOUR_GROUPS
    assert l % FFT_LA == 0 and l % RET_CHUNK == 0 and rw % GROUP_W == 0

    mods, kct, vc = _ada(c, c_ctx, w_ada[0], b_ada, ctx, norm1_gain, w_in[0], fw, rw)

    kw, w2, chan, twc, tws, la, lb = _dft_tables(l, gd)
    cosf, sinf = _rope_tables(l)

    later = [four_w_out[0], ret_w_out[0], w_out[0], w_mlp1[0], w_mlp2[0]]
    (u, q, kt, v, sg, gab), later_b = _proj(x, mods, norm1_gain, w_in[0], cosf, sinf, ret_gn_gain,
                                             later, fw, rw)

    fm = _fft(u, kw, twc, tws, w2, chan, la, lb, gd, 1.0 / math.sqrt(l * gd))

    log_gamma = jax.nn.log_sigmoid(ret_decay_logit[0].astype(F32))
    z = _retention(q, kt, v, sg, kct, vc, log_gamma)

    w4, wr, wo, w1, w2 = later_b
    return _out(x, mods, fm, z, gab, w4, wr, wo, norm2_gain, w1, w2, final_gain[None, :])
```

```python
import functools
import math

import jax
import jax.numpy as jnp
import numpy as np
from jax import lax
from jax.experimental import pallas as pl
from jax.experimental.pallas import tpu as pltpu

F32 = jnp.float32
BF16 = jnp.bfloat16

GRID_W = 64
FOUR_GROUPS = 4
RET_HEAD_DIM = 64
N_MOD = 6
CTX_FIRST_STEP = 2
ROPE_BASE = 10000.0
EPS = 1e-6

LANES = 128
MXU_DIM = 256
F32_SUBLANES = 8
BF16_SUBLANES = 16
VMEM_LIMIT_BYTES = 56 * 1024 * 1024

RET_CHUNK = 128
RET_UNROLL = 32
HEADS_PER_GROUP = MXU_DIM // RET_HEAD_DIM
GROUP_W = HEADS_PER_GROUP * RET_HEAD_DIM
STATE_BLOCK_W = LANES
FFT_LA = 128
FFT_LA_TILE = BF16_SUBLANES
FFT_SCATTER_PAD = F32_SUBLANES
TOKEN_TILE = 512
PROJ_TOKEN_TILE = 1024
WEIGHT_STAGE_BYTES = 2 * 1024 * 1024
WEIGHT_STAGE_SLOTS = 4


def _dot(a, b):
    return jnp.dot(a, b, preferred_element_type=F32)


def _norm_mod(x, gain, shift, scale):
    ms = jnp.mean(x * x, axis=-1, keepdims=True)
    y = x * lax.rsqrt(ms + EPS) * gain
    return y * (1.0 + scale) + shift


def _tile_lanes(t, reps):
    return jnp.concatenate([t] * reps, axis=1) if reps > 1 else t


def _const_spec(shape):
    nd = len(shape)
    return pl.BlockSpec(shape, lambda *_: (0,) * nd, pipeline_mode=pl.Buffered(1))


def _params(*sem):
    return pltpu.CompilerParams(dimension_semantics=sem, vmem_limit_bytes=VMEM_LIMIT_BYTES)


def _load_weights_as_bf16(pairs, stage_ref, sem_ref):
    slots, sr, sc = stage_ref.shape
    chunks = [(src, dst, r0, c0)
              for src, dst in pairs
              for r0 in range(0, src.shape[0], sr)
              for c0 in range(0, src.shape[1], sc)]

    def copy(k):
        src, _, r0, c0 = chunks[k]
        slot = k % slots
        return pltpu.make_async_copy(src.at[pl.ds(r0, sr), pl.ds(c0, sc)], stage_ref.at[slot], sem_ref.at[slot])

    for k in range(min(slots - 1, len(chunks))):
        copy(k).start()
    for k, (_, dst, r0, c0) in enumerate(chunks):
        if k + slots - 1 < len(chunks):
            copy(k + slots - 1).start()
        copy(k).wait()
        dst[r0:r0 + sr, c0:c0 + sc] = stage_ref[k % slots].astype(BF16)


def _is_first_step(grid_rank):
    first = pl.program_id(0) == 0
    for axis in range(1, grid_rank):
        first = first & (pl.program_id(axis) == 0)
    return first


def _ada_kernel(c_ref, cctx_ref, w_ref, b_ref, ctx_ref, gain_ref, wkv_ref,
                o_ref, kt_ref, v_ref, cv_ref, modc_ref, wkv_bf_ref, *, rw, scale):
    j = pl.program_id(0)
    b = c_ref.shape[0]
    cv_ref[...] = jnp.zeros_like(cv_ref)
    cv_ref[0:b, :] = c_ref[...]
    cv_ref[b:b + 1, :] = cctx_ref[...]
    c = cv_ref[...]
    s = c * jax.nn.sigmoid(c)
    mod = _dot(s.astype(BF16), w_ref[...].astype(BF16)) + b_ref[...]
    o_ref[0] = mod

    @pl.when(j < CTX_FIRST_STEP)
    def _():
        modc_ref[pl.ds(j, 1), :] = mod[b:b + 1, :]

    @pl.when(j == CTX_FIRST_STEP)
    def _():
        wkv_bf_ref[...] = wkv_ref[...].astype(BF16)

    @pl.when(j >= CTX_FIRST_STEP)
    def _():
        h = _norm_mod(ctx_ref[0], gain_ref[...], modc_ref[0:1, :], modc_ref[1:2, :]).astype(BF16)
        kt_ref[0] = (_dot(h, wkv_bf_ref[:, :rw]) * scale).T.astype(BF16)
        v_ref[0] = _dot(h, wkv_bf_ref[:, rw:]).astype(BF16)


def _ada(c, c_ctx, w_ada, b_ada, ctx, gain, w_in, fw, rw):
    b, d = c.shape
    lc = ctx.shape[1]
    rows = -(-(b + 1) // F32_SUBLANES) * F32_SUBLANES
    assert w_ada.shape[1] == N_MOD * d and b <= N_MOD - CTX_FIRST_STEP
    kv_block = (fw + rw) // (2 * rw)
    assert kv_block * 2 * rw == fw + rw
    sample = lambda j: jnp.clip(j - CTX_FIRST_STEP, 0, b - 1)
    return pl.pallas_call(
        functools.partial(_ada_kernel, rw=rw, scale=RET_HEAD_DIM ** -0.5),
        grid=(N_MOD,),
        in_specs=[
            pl.BlockSpec((b, d), lambda j: (0, 0)),
            pl.BlockSpec((1, d), lambda j: (0, 0)),
            pl.BlockSpec((d, d), lambda j: (0, j)),
            pl.BlockSpec((1, d), lambda j: (0, j)),
            pl.BlockSpec((1, lc, d), lambda j: (sample(j), 0, 0)),
            pl.BlockSpec((1, d), lambda j: (0, 0)),
            pl.BlockSpec((d, 2 * rw), lambda j: (0, kv_block)),
        ],
        out_specs=[pl.BlockSpec((1, rows, d), lambda j: (j, 0, 0)),
                   pl.BlockSpec((1, rw, lc), lambda j: (sample(j), 0, 0)),
                   pl.BlockSpec((1, lc, rw), lambda j: (sample(j), 0, 0))],
        out_shape=[jax.ShapeDtypeStruct((N_MOD, rows, d), F32),
                   jax.ShapeDtypeStruct((b, rw, lc), BF16), jax.ShapeDtypeStruct((b, lc, rw), BF16)],
        scratch_shapes=[pltpu.VMEM((rows, d), F32), pltpu.VMEM((CTX_FIRST_STEP, d), F32),
                        pltpu.VMEM((d, 2 * rw), BF16)],
        compiler_params=_params("arbitrary"),
        name="ada",
    )(c, c_ctx[None, :], w_ada, b_ada, ctx, gain, w_in)


def _proj_kernel(*refs, fw, rw, d, scale, ncast):
    x_ref, mod_ref, cos_ref, sin_ref, w_hbm = refs[:5]
    cast_in = refs[5:5 + ncast]
    gain_ref, gng_ref = refs[5 + ncast:7 + ncast]
    u_ref, q_ref, kt_ref, v_ref, sg_ref, gab_ref = refs[7 + ncast:13 + ncast]
    cast_out = refs[13 + ncast:13 + 2 * ncast]
    w_ref, stage_ref, sem_ref = refs[13 + 2 * ncast:]

    @pl.when(_is_first_step(2))
    def _():
        _load_weights_as_bf16([(w_hbm, w_ref)], stage_ref, sem_ref)

    for src, dst in zip(cast_in, cast_out):
        dst[...] = src[...].astype(BF16)

    mod = mod_ref[:, pl.ds(pl.program_id(0), 1), :]
    h = _norm_mod(x_ref[0], gain_ref[...], mod[0], mod[1]).astype(BF16)
    tm = h.shape[0]

    def proj(lo, width):
        return _dot(h, w_ref[:, lo:lo + width])

    u_ref[0] = proj(0, fw).astype(BF16)

    reps = rw // cos_ref.shape[1]
    cosf = _tile_lanes(cos_ref[...], reps)
    sinf = _tile_lanes(sin_ref[...], reps)
    lane = lax.broadcasted_iota(jnp.int32, (tm, rw), 1)
    first_half = (lane & (RET_HEAD_DIM // 2)) == 0

    def rope(t):
        rot = jnp.where(first_half,
                        pltpu.roll(t, rw - RET_HEAD_DIM // 2, 1),
                        pltpu.roll(t, RET_HEAD_DIM // 2, 1))
        return t * cosf + rot * sinf

    q_ref[0] = rope(proj(fw, rw)).astype(BF16)
    kt_ref[0] = (rope(proj(fw + rw, rw)) * scale).T.astype(BF16)
    v_ref[0] = proj(fw + 2 * rw, rw).astype(BF16)
    g = proj(fw + 3 * rw, rw)
    sg_ref[0] = (g * jax.nn.sigmoid(g) * gng_ref[...]).astype(BF16)
    gab_ref[0, :, :d] = proj(fw + 4 * rw, d).astype(BF16)
    gab_ref[0, :, d:] = proj(fw + 4 * rw + d, d).astype(BF16)


def _proj(x, mods, gain, w_in, cosf, sinf, gn_gain, later_weights, fw, rw):
    b, l, d = x.shape
    tm = min(PROJ_TOKEN_TILE, l)
    nj = l // tm
    steps = b * nj
    tok = lambda width: pl.BlockSpec((1, tm, width), lambda i, j: (i, j, 0))
    out = lambda width: jax.ShapeDtypeStruct((b, l, width), BF16)
    sliced = [w.reshape(steps, w.shape[0] // steps, w.shape[1]) for w in later_weights]
    for w in sliced:
        assert w.shape[1] % BF16_SUBLANES == 0
    cast_spec = lambda w: pl.BlockSpec((1,) + w.shape[1:], lambda i, j: (i * nj + j, 0, 0))
    stage_rows = max(r for r in range(8, w_in.shape[0] + 1, 8)
                     if w_in.shape[0] % r == 0 and r * w_in.shape[1] * 4 <= WEIGHT_STAGE_BYTES)
    res = pl.pallas_call(
        functools.partial(_proj_kernel, fw=fw, rw=rw, d=d, scale=RET_HEAD_DIM ** -0.5, ncast=len(sliced)),
        grid=(b, nj),
        in_specs=[
            tok(d),
            _const_spec(mods.shape),
            pl.BlockSpec((tm, cosf.shape[1]), lambda i, j: (j, 0)),
            pl.BlockSpec((tm, sinf.shape[1]), lambda i, j: (j, 0)),
            pl.BlockSpec(memory_space=pl.ANY),
        ] + [cast_spec(w) for w in sliced] + [_const_spec((1, d)), _const_spec((1, rw))],
        out_specs=[tok(fw), tok(rw), pl.BlockSpec((1, rw, tm), lambda i, j: (i, 0, j)),
                   tok(rw), tok(rw), tok(2 * d)] + [cast_spec(w) for w in sliced],
        out_shape=[out(fw), out(rw), jax.ShapeDtypeStruct((b, rw, l), BF16),
                   out(rw), out(rw), out(2 * d)]
        + [jax.ShapeDtypeStruct(w.shape, BF16) for w in sliced],
        scratch_shapes=[pltpu.VMEM(w_in.shape, BF16),
                        pltpu.VMEM((WEIGHT_STAGE_SLOTS, stage_rows, w_in.shape[1]), F32),
                        pltpu.SemaphoreType.DMA((WEIGHT_STAGE_SLOTS,))],
        compiler_params=_params("arbitrary", "arbitrary"),
        name="proj",
    )(x, mods, cosf, sinf, w_in, *sliced, gain, gn_gain)
    casts = [c.reshape(w.shape) for c, w in zip(res[6:], later_weights)]
    return res[:6], casts


def _fft_kernel(u_ref, kw_ref, twc_ref, tws_ref, w2_ref, cs_ref, o_ref, tr_ref, ti_ref, scr_ref,
                *, la, lb, gd, npb, scale):
    r = FFT_LA_TILE
    rows = lb * r
    fw = u_ref.shape[2]
    ngroups = fw // gd
    reps = fw // twc_ref.shape[2]
    pitch = scr_ref.shape[1] // la

    @pl.when(pl.program_id(0) == 0)
    def _():
        scr_ref[...] = jnp.zeros_like(scr_ref)

    for j in range(la // r):
        u = jnp.concatenate([u_ref[0, k * la + j * r:k * la + (j + 1) * r, :] for k in range(lb)], axis=0)
        half = lb // 2 + 1
        t = _dot(kw_ref[...], u)
        mirror = [lb - k for k in range(half, lb)]
        a = jnp.concatenate([t[:half * r]] + [t[k * r:(k + 1) * r] for k in mirror], axis=0)
        b = jnp.concatenate([t[half * r:]] + [-t[(half + k) * r:(half + k + 1) * r] for k in mirror], axis=0)
        ct = _tile_lanes(twc_ref[j], reps)
        st = _tile_lanes(tws_ref[j], reps)
        tr = (a * ct + b * st).astype(BF16)
        ti = (b * ct - a * st).astype(BF16)
        for k in range(lb):
            tr_ref[k * la + j * r:k * la + (j + 1) * r, :] = tr[k * r:(k + 1) * r]
            ti_ref[k * la + j * r:k * la + (j + 1) * r, :] = ti[k * r:(k + 1) * r]

    for pb in range(lb // npb):
        xs = []
        for p in range(npb):
            lo = (pb * npb + p) * la
            t = jnp.concatenate([tr_ref[lo:lo + la, :], ti_ref[lo:lo + la, :]], axis=0)
            xs.append(_dot(w2_ref[...], t).astype(BF16))
        for gi in range(ngroups):
            cols = slice(gi * gd, (gi + 1) * gd)
            lhs = jnp.concatenate([jnp.concatenate([x[:la, cols], x[la:, cols]], axis=1) for x in xs], axis=0)
            fg = _dot(lhs, cs_ref[...]) * scale
            for p in range(npb):
                scr_ref[gi, pl.ds(p, la, stride=pitch), :] = fg[p * la:(p + 1) * la]
        full = jnp.concatenate([scr_ref[gi].reshape(la, pitch, gd)[:, :npb, :] for gi in range(ngroups)], axis=2)
        o_ref[0, :, pb * npb:(pb + 1) * npb, :] = full.astype(BF16)


def _fft(u, kw, twc, tws, w2, cs, la, lb, gd, scale):
    b, l, fw = u.shape
    npb = min(BF16_SUBLANES, lb)
    out = pl.pallas_call(
        functools.partial(_fft_kernel, la=la, lb=lb, gd=gd, npb=npb, scale=scale),
        grid=(b,),
        in_specs=[pl.BlockSpec((1, l, fw), lambda i: (i, 0, 0)),
                  _const_spec(kw.shape), _const_spec(twc.shape), _const_spec(tws.shape),
                  _const_spec(w2.shape), _const_spec(cs.shape)],
        out_specs=pl.BlockSpec((1, la, lb, fw), lambda i: (i, 0, 0, 0)),
        out_shape=jax.ShapeDtypeStruct((b, la, lb, fw), BF16),
        scratch_shapes=[pltpu.VMEM((l, fw), BF16), pltpu.VMEM((l, fw), BF16),
                        pltpu.VMEM((fw // gd, la * (npb + FFT_SCATTER_PAD), gd), F32)],
        compiler_params=_params("arbitrary"),
        name="fft",
    )(u, kw, twc, tws, w2, cs)
    return out.reshape(b, l, fw)


def _ret_kernel(q_ref, kt_ref, v_ref, sg_ref, kct_ref, vc_ref, lgl_ref, lgc_ref,
                z_ref, sf_ref, sb_ref, stf_ref, stb_ref, p_ref, o_ref,
                dall_ref, qdf_ref, qdb_ref, kdf_ref, kdb_ref, *, nchunk, unroll):
    c = RET_CHUNK
    gw = GROUP_W
    lc = vc_ref.shape[1]

    lgf, lgb = lgl_ref[0:1, :], lgl_ref[1:2, :]
    pos = lax.broadcasted_iota(jnp.int32, (c, gw), 0).astype(F32)
    qdf_ref[...] = jnp.exp(lgf * (pos + 1.0))
    qdb_ref[...] = jnp.exp(lgb * (c - pos))
    kdf_ref[...] = jnp.exp(lgf * (c - 1.0 - pos))
    kdb_ref[...] = jnp.exp(lgb * pos)
    cdf = jnp.exp(lgf * c)
    cdb = jnp.exp(lgb * c)
    cpos = lax.broadcasted_iota(jnp.int32, (lc, gw), 0).astype(F32)
    wcf = jnp.exp(lgf * (lc - 1.0 - cpos))
    wcb = jnp.exp(lgb * cpos)
    si = lax.broadcasted_iota(jnp.int32, (c, HEADS_PER_GROUP * c), 0)
    sj = lax.broadcasted_iota(jnp.int32, (c, HEADS_PER_GROUP * c), 1) & (c - 1)
    diff = (si - sj).astype(F32)
    dall_ref[...] = (jnp.where(diff >= 0, jnp.exp(lgc_ref[0:1, :] * jnp.maximum(diff, 0.0)), 0.0)
                     + jnp.where(diff <= 0, jnp.exp(lgc_ref[1:2, :] * jnp.maximum(-diff, 0.0)), 0.0))

    pw = STATE_BLOCK_W
    npair = gw // pw
    same_head_p = (lax.broadcasted_iota(jnp.int32, (pw, pw), 0) // RET_HEAD_DIM
                   == lax.broadcasted_iota(jnp.int32, (pw, pw), 1) // RET_HEAD_DIM)

    def diag_blocks(t):
        return jnp.where(same_head_p, t, 0.0)

    def weighted(t, w):
        return (t.astype(F32) * w).astype(BF16)

    def rows(ref, n):
        return ref[0, pl.ds(pl.multiple_of(n * c, c), c), :]

    def kt_chunk(n):
        return kt_ref[0, :, pl.ds(pl.multiple_of(n * c, c), c)]

    def state_update(st_ref, kt, vw, cd):
        for p in range(npair):
            cols = slice(p * pw, (p + 1) * pw)
            st_ref[p] = st_ref[p] * cd[:, cols] + diag_blocks(_dot(kt[cols, :], vw[:, cols]))

    def full_state(s_ref, n):
        zero = jnp.zeros((pw, pw), BF16)
        return jnp.concatenate(
            [jnp.concatenate([s_ref[n, p] if q == p else zero for q in range(npair)], axis=1)
             for p in range(npair)], axis=0)

    kct = kct_ref[0]
    vc = vc_ref[0]
    stf_ref[...] = jnp.zeros_like(stf_ref)
    stb_ref[...] = jnp.zeros_like(stb_ref)
    state_update(stf_ref, kct, weighted(vc, wcf), cdf)
    state_update(stb_ref, kct, weighted(vc, wcb), cdb)

    def scan_body(i, carry):
        nf = i
        nb = nchunk - 1 - i
        sf_ref[nf] = stf_ref[...].astype(BF16)
        state_update(stf_ref, kt_chunk(nf), weighted(rows(v_ref, nf), kdf_ref[...]), cdf)
        sb_ref[nb] = stb_ref[...].astype(BF16)
        state_update(stb_ref, kt_chunk(nb), weighted(rows(v_ref, nb), kdb_ref[...]), cdb)
        return carry

    lax.fori_loop(0, nchunk, scan_body, 0, unroll=unroll)

    lane = lax.broadcasted_iota(jnp.int32, (c, gw), 1)
    sub = lax.broadcasted_iota(jnp.int32, (gw, c), 0)
    lane_masks = [(lane >= h * RET_HEAD_DIM) & (lane < (h + 1) * RET_HEAD_DIM)
                  for h in range(HEADS_PER_GROUP)]
    sub_masks = [(sub >= h * RET_HEAD_DIM) & (sub < (h + 1) * RET_HEAD_DIM)
                 for h in range(HEADS_PER_GROUP)]
    same_head = (lax.broadcasted_iota(jnp.int32, (gw, gw), 0) // RET_HEAD_DIM
                 == lax.broadcasted_iota(jnp.int32, (gw, gw), 1) // RET_HEAD_DIM)
    bd_mean = jnp.where(same_head, 1.0 / RET_HEAD_DIM, 0.0).astype(BF16)

    def score_body(n, carry):
        ktn = kt_chunk(n)
        zk = jnp.zeros_like(ktn)
        kbd = jnp.concatenate([jnp.where(m, ktn, zk) for m in sub_masks], axis=1)
        p_ref[n] = (_dot(rows(q_ref, n), kbd) * dall_ref[...]).astype(BF16)
        return carry

    lax.fori_loop(0, nchunk, score_body, 0, unroll=unroll)

    def mix_body(n, carry):
        qn = rows(q_ref, n)
        vn = rows(v_ref, n)
        zv = jnp.zeros_like(vn)
        vbd = jnp.concatenate([jnp.where(m, vn, zv) for m in lane_masks], axis=0)
        o_ref[n] = (_dot(p_ref[n], vbd) + qdf_ref[...] * _dot(qn, full_state(sf_ref, n))
                    + qdb_ref[...] * _dot(qn, full_state(sb_ref, n)))
        return carry

    lax.fori_loop(0, nchunk, mix_body, 0, unroll=unroll)

    def norm_body(n, carry):
        o = o_ref[n]
        ms = _dot((o * o).astype(BF16), bd_mean)
        z = rows(sg_ref, n).astype(F32) * (o * lax.rsqrt(ms + EPS))
        z_ref[0, pl.ds(pl.multiple_of(n * c, c), c), :] = z.astype(BF16)
        return carry

    lax.fori_loop(0, nchunk, norm_body, 0, unroll=unroll)


def _retention(q, kt, v, sg, kct, vc, log_gamma):
    b, l, rw = q.shape
    lc = vc.shape[1]
    c = RET_CHUNK
    gw = GROUP_W
    hpg = HEADS_PER_GROUP
    ng = rw // gw
    nchunk = l // c
    pw = STATE_BLOCK_W
    npair = gw // pw
    assert c & (c - 1) == 0
    lg = log_gamma.reshape(2, ng, hpg).transpose(1, 0, 2)
    lg_lane = jnp.repeat(lg, RET_HEAD_DIM, axis=2)
    lg_col = jnp.repeat(lg, c, axis=2)
    tok = pl.BlockSpec((1, l, gw), lambda i, j: (i, 0, j))
    tokt = pl.BlockSpec((1, gw, l), lambda i, j: (i, j, 0))
    grp = lambda r, width: pl.BlockSpec((None, r, width), lambda i, j: (j, 0, 0))
    return pl.pallas_call(
        functools.partial(_ret_kernel, nchunk=nchunk, unroll=math.gcd(nchunk, RET_UNROLL)),
        grid=(b, ng),
        in_specs=[
            tok, tokt, tok, tok,
            pl.BlockSpec((1, gw, lc), lambda i, j: (i, j, 0)),
            pl.BlockSpec((1, lc, gw), lambda i, j: (i, 0, j)),
            grp(2, gw), grp(2, hpg * c),
        ],
        out_specs=tok,
        out_shape=jax.ShapeDtypeStruct((b, l, rw), BF16),
        scratch_shapes=[pltpu.VMEM((nchunk, npair, pw, pw), BF16), pltpu.VMEM((nchunk, npair, pw, pw), BF16),
                        pltpu.VMEM((npair, pw, pw), F32), pltpu.VMEM((npair, pw, pw), F32),
                        pltpu.VMEM((nchunk, c, hpg * c), BF16),
                        pltpu.VMEM((nchunk, c, gw), F32),
                        pltpu.VMEM((c, hpg * c), F32)] + [pltpu.VMEM((c, gw), F32)] * 4,
        compiler_params=_params("arbitrary", "arbitrary"),
        name="ret",
    )(q, kt, v, sg, kct, vc, lg_lane, lg_col)


def _out_kernel(w4_ref, wr_ref, wo_ref, w1_ref, w2_ref, x_ref, mod_ref, fm_ref, z_ref, gab_ref,
                gain2_ref, fgain_ref, o_ref, *, ff_chunk):
    mod = mod_ref[:, pl.ds(pl.program_id(0), 1), :]
    g1, sh2, sc2, g2 = mod[2], mod[3], mod[4], mod[5]
    y_four = _dot(fm_ref[0], w4_ref[...])
    y_ret = _dot(z_ref[0], wr_ref[...])
    d = x_ref.shape[2]
    y = (jax.nn.sigmoid(gab_ref[0, :, :d].astype(F32)) * y_four
         + jax.nn.sigmoid(gab_ref[0, :, d:].astype(F32)) * y_ret)
    x1 = x_ref[0] + g1 * _dot(y.astype(BF16), wo_ref[...])
    h2 = _norm_mod(x1, gain2_ref[...], sh2, sc2).astype(BF16)
    dff = w1_ref.shape[1]
    acc = None
    for lo in range(0, dff, ff_chunk):
        hid = jnp.maximum(_dot(h2, w1_ref[:, lo:lo + ff_chunk]), 0.0)
        part = _dot((hid * hid).astype(BF16), w2_ref[lo:lo + ff_chunk, :])
        acc = part if acc is None else acc + part
    x2 = x1 + g2 * acc
    ms = jnp.mean(x2 * x2, axis=-1, keepdims=True)
    o_ref[0] = x2 * lax.rsqrt(ms + EPS) * fgain_ref[...]


def _out(x, mods, fm, z, gab, w4, wr, wo, gain2, w1, w2, fgain):
    b, l, d = x.shape
    tm = min(TOKEN_TILE, l)
    tok = lambda width: pl.BlockSpec((1, tm, width), lambda i, j: (i, j, 0))
    weights = (w4, wr, wo, w1, w2)
    return pl.pallas_call(
        functools.partial(_out_kernel, ff_chunk=min(1024, w1.shape[1])),
        grid=(b, l // tm),
        in_specs=[_const_spec(w.shape) for w in weights] + [
            tok(d),
            _const_spec(mods.shape),
            tok(fm.shape[2]), tok(z.shape[2]), tok(2 * d),
            _const_spec((1, d)), _const_spec((1, d)),
        ],
        out_specs=tok(d),
        out_shape=jax.ShapeDtypeStruct((b, l, d), F32),
        compiler_params=_params("arbitrary", "arbitrary"),
        name="out",
    )(*weights, x, mods, fm, z, gab, gain2, fgain)


def _dft_tables(l, gd):
    la = FFT_LA
    lb = l // la

    def cs(n):
        idx = np.arange(n)
        ang = 2.0 * np.pi * ((idx[:, None] * idx[None, :]) % n) / n
        return np.cos(ang), np.sin(ang)

    cb, sb = cs(lb)
    eye = np.eye(FFT_LA_TILE)
    half = lb // 2 + 1
    kw = np.concatenate([np.kron(cb[:half], eye), -np.kron(sb[:half], eye)], axis=0)
    ca, sa = cs(la)
    w2 = np.block([[ca, sa], [-sa, ca]])
    cc, sc = cs(gd)
    chan = np.concatenate([cc, sc], axis=0)
    tw = 2.0 * np.pi * (np.arange(lb)[:, None] * np.arange(la)[None, :]) / l
    tw = tw.reshape(lb, la // FFT_LA_TILE, FFT_LA_TILE).transpose(1, 0, 2).reshape(la // FFT_LA_TILE, -1)
    twc = np.repeat(np.cos(tw)[:, :, None], LANES, axis=2)
    tws = np.repeat(np.sin(tw)[:, :, None], LANES, axis=2)
    as_bf = lambda a: jnp.asarray(a, dtype=F32).astype(BF16)
    return as_bf(kw), as_bf(w2), as_bf(chan), jnp.asarray(twc, F32), jnp.asarray(tws, F32), la, lb


def _rope_tables(l):
    f32 = np.float32
    nf = RET_HEAD_DIM // 4
    inv = np.power(f32(ROPE_BASE), -np.arange(nf, dtype=f32) / f32(nf)).astype(f32)
    rows = l // GRID_W
    r, cc = np.meshgrid(np.arange(rows, dtype=f32), np.arange(GRID_W, dtype=f32), indexing="ij")
    ang = np.concatenate([r.reshape(-1)[:, None] * inv, cc.reshape(-1)[:, None] * inv], axis=-1).astype(f32)
    cos, sin = np.cos(ang).astype(f32), np.sin(ang).astype(f32)
    cos_h = np.concatenate([cos, cos], axis=1)
    sin_h = np.concatenate([-sin, sin], axis=1)
    reps = LANES // RET_HEAD_DIM
    return jnp.asarray(np.tile(cos_h, (1, reps))), jnp.asarray(np.tile(sin_h, (1, reps)))


def kernel(x, c, ctx, c_ctx, w_ada, b_ada, norm1_gain, w_in, four_w_out, ret_decay_logit,
           ret_gn_gain, ret_w_out, w_out, norm2_gain, w_mlp1, w_mlp2, final_gain):
    assert w_ada.shape[0] == 1, "single-layer block"
    b, l, d = x.shape
    lc = ctx.shape[1]
    fw = four_w_out.shape[1]
    rw = ret_w_out.shape[1]
    gd = fw // FOUR_GROUPS
    assert l % FFT_LA == 0 and l % RET_CHUNK == 0 and rw % GROUP_W == 0

    mods, kct, vc = _ada(c, c_ctx, w_ada[0], b_ada, ctx, norm1_gain, w_in[0], fw, rw)

    kw, w2, chan, twc, tws, la, lb = _dft_tables(l, gd)
    cosf, sinf = _rope_tables(l)

    later = [four_w_out[0], ret_w_out[0], w_out[0], w_mlp1[0], w_mlp2[0]]
    (u, q, kt, v, sg, gab), later_b = _proj(x, mods, norm1_gain, w_in[0], cosf, sinf, ret_gn_gain,
                                             later, fw, rw)

    fm = _fft(u, kw, twc, tws, w2, chan, la, lb, gd, 1.0 / math.sqrt(l * gd))

    log_gamma = jax.nn.log_sigmoid(ret_decay_logit[0].astype(F32))
    z = _retention(q, kt, v, sg, kct, vc, log_gamma)

    w4, wr, wo, w1, w2 = later_b
    return _out(x, mods, fm, z, gab, w4, wr, wo, norm2_gain, w1, w2, final_gain[None, :])
```

```python
import functools
import math

import jax
import jax.numpy as jnp
import numpy as np
from jax import lax
from jax.experimental import pallas as pl
from jax.experimental.pallas import tpu as pltpu

F32 = jnp.float32
BF16 = jnp.bfloat16

GRID_W = 64
FOUR_GROUPS = 4
RET_HEAD_DIM = 64
N_MOD = 6
CTX_FIRST_STEP = 2
ROPE_BASE = 10000.0
EPS = 1e-6

LANES = 128
MXU_DIM = 256
F32_SUBLANES = 8
BF16_SUBLANES = 16
VMEM_LIMIT_BYTES = 56 * 1024 * 1024

RET_CHUNK = 128
RET_UNROLL = 32
HEADS_PER_GROUP = MXU_DIM // RET_HEAD_DIM
GROUP_W = HEADS_PER_GROUP * RET_HEAD_DIM
STATE_BLOCK_W = LANES
FFT_LA = 128
FFT_LA_TILE = BF16_SUBLANES
FFT_SCATTER_PAD = F32_SUBLANES
TOKEN_TILE = 512
PROJ_TOKEN_TILE = 1024
WEIGHT_STAGE_BYTES = 2 * 1024 * 1024
WEIGHT_STAGE_SLOTS = 4


def _dot(a, b):
    return jnp.dot(a, b, preferred_element_type=F32)


def _norm_mod(x, gain, shift, scale):
    ms = jnp.mean(x * x, axis=-1, keepdims=True)
    y = x * lax.rsqrt(ms + EPS) * gain
    return y * (1.0 + scale) + shift


def _tile_lanes(t, reps):
    return jnp.concatenate([t] * reps, axis=1) if reps > 1 else t


def _const_spec(shape):
    nd = len(shape)
    return pl.BlockSpec(shape, lambda *_: (0,) * nd, pipeline_mode=pl.Buffered(1))


def _params(*sem):
    return pltpu.CompilerParams(dimension_semantics=sem, vmem_limit_bytes=VMEM_LIMIT_BYTES)


def _load_weights_as_bf16(pairs, stage_ref, sem_ref):
    slots, sr, sc = stage_ref.shape
    chunks = [(src, dst, r0, c0)
              for src, dst in pairs
              for r0 in range(0, src.shape[0], sr)
              for c0 in range(0, src.shape[1], sc)]

    def copy(k):
        src, _, r0, c0 = chunks[k]
        slot = k % slots
        return pltpu.make_async_copy(src.at[pl.ds(r0, sr), pl.ds(c0, sc)], stage_ref.at[slot], sem_ref.at[slot])

    for k in range(min(slots - 1, len(chunks))):
        copy(k).start()
    for k, (_, dst, r0, c0) in enumerate(chunks):
        if k + slots - 1 < len(chunks):
            copy(k + slots - 1).start()
        copy(k).wait()
        dst[r0:r0 + sr, c0:c0 + sc] = stage_ref[k % slots].astype(BF16)


def _is_first_step(grid_rank):
    first = pl.program_id(0) == 0
    for axis in range(1, grid_rank):
        first = first & (pl.program_id(axis) == 0)
    return first


def _ada_kernel(c_ref, cctx_ref, w_ref, b_ref, ctx_ref, gain_ref, wkv_ref,
                o_ref, kt_ref, v_ref, cv_ref, modc_ref, wkv_bf_ref, *, rw, scale):
    j = pl.program_id(0)
    b = c_ref.shape[0]
    cv_ref[...] = jnp.zeros_like(cv_ref)
    cv_ref[0:b, :] = c_ref[...]
    cv_ref[b:b + 1, :] = cctx_ref[...]
    c = cv_ref[...]
    s = c * jax.nn.sigmoid(c)
    mod = _dot(s.astype(BF16), w_ref[...].astype(BF16)) + b_ref[...]
    o_ref[0] = mod

    @pl.when(j < CTX_FIRST_STEP)
    def _():
        modc_ref[pl.ds(j, 1), :] = mod[b:b + 1, :]

    @pl.when(j == CTX_FIRST_STEP)
    def _():
        wkv_bf_ref[...] = wkv_ref[...].astype(BF16)

    @pl.when(j >= CTX_FIRST_STEP)
    def _():
        h = _norm_mod(ctx_ref[0], gain_ref[...], modc_ref[0:1, :], modc_ref[1:2, :]).astype(BF16)
        kt_ref[0] = (_dot(h, wkv_bf_ref[:, :rw]) * scale).T.astype(BF16)
        v_ref[0] = _dot(h, wkv_bf_ref[:, rw:]).astype(BF16)


def _ada(c, c_ctx, w_ada, b_ada, ctx, gain, w_in, fw, rw):
    b, d = c.shape
    lc = ctx.shape[1]
    rows = -(-(b + 1) // F32_SUBLANES) * F32_SUBLANES
    assert w_ada.shape[1] == N_MOD * d and b <= N_MOD - CTX_FIRST_STEP
    kv_block = (fw + rw) // (2 * rw)
    assert kv_block * 2 * rw == fw + rw
    sample = lambda j: jnp.clip(j - CTX_FIRST_STEP, 0, b - 1)
    return pl.pallas_call(
        functools.partial(_ada_kernel, rw=rw, scale=RET_HEAD_DIM ** -0.5),
        grid=(N_MOD,),
        in_specs=[
            pl.BlockSpec((b, d), lambda j: (0, 0)),
            pl.BlockSpec((1, d), lambda j: (0, 0)),
            pl.BlockSpec((d, d), lambda j: (0, j)),
            pl.BlockSpec((1, d), lambda j: (0, j)),
            pl.BlockSpec((1, lc, d), lambda j: (sample(j), 0, 0)),
            pl.BlockSpec((1, d), lambda j: (0, 0)),
            pl.BlockSpec((d, 2 * rw), lambda j: (0, kv_block)),
        ],
        out_specs=[pl.BlockSpec((1, rows, d), lambda j: (j, 0, 0)),
                   pl.BlockSpec((1, rw, lc), lambda j: (sample(j), 0, 0)),
                   pl.BlockSpec((1, lc, rw), lambda j: (sample(j), 0, 0))],
        out_shape=[jax.ShapeDtypeStruct((N_MOD, rows, d), F32),
                   jax.ShapeDtypeStruct((b, rw, lc), BF16), jax.ShapeDtypeStruct((b, lc, rw), BF16)],
        scratch_shapes=[pltpu.VMEM((rows, d), F32), pltpu.VMEM((CTX_FIRST_STEP, d), F32),
                        pltpu.VMEM((d, 2 * rw), BF16)],
        compiler_params=_params("arbitrary"),
        name="ada",
    )(c, c_ctx[None, :], w_ada, b_ada, ctx, gain, w_in)


def _proj_kernel(*refs, fw, rw, d, scale, ncast):
    x_ref, mod_ref, cos_ref, sin_ref, w_hbm = refs[:5]
    cast_in = refs[5:5 + ncast]
    gain_ref, gng_ref = refs[5 + ncast:7 + ncast]
    u_ref, q_ref, kt_ref, v_ref, sg_ref, gab_ref = refs[7 + ncast:13 + ncast]
    cast_out = refs[13 + ncast:13 + 2 * ncast]
    w_ref, stage_ref, sem_ref = refs[13 + 2 * ncast:]

    @pl.when(_is_first_step(2))
    def _():
        _load_weights_as_bf16([(w_hbm, w_ref)], stage_ref, sem_ref)

    for src, dst in zip(cast_in, cast_out):
        dst[...] = src[...].astype(BF16)

    mod = mod_ref[:, pl.ds(pl.program_id(0), 1), :]
    h = _norm_mod(x_ref[0], gain_ref[...], mod[0], mod[1]).astype(BF16)
    tm = h.shape[0]

    def proj(lo, width):
        return _dot(h, w_ref[:, lo:lo + width])

    u_ref[0] = proj(0, fw).astype(BF16)

    reps = rw // cos_ref.shape[1]
    cosf = _tile_lanes(cos_ref[...], reps)
    sinf = _tile_lanes(sin_ref[...], reps)
    lane = lax.broadcasted_iota(jnp.int32, (tm, rw), 1)
    first_half = (lane & (RET_HEAD_DIM // 2)) == 0

    def rope(t):
        rot = jnp.where(first_half,
                        pltpu.roll(t, rw - RET_HEAD_DIM // 2, 1),
                        pltpu.roll(t, RET_HEAD_DIM // 2, 1))
        return t * cosf + rot * sinf

    q_ref[0] = rope(proj(fw, rw)).astype(BF16)
    kt_ref[0] = (rope(proj(fw + rw, rw)) * scale).T.astype(BF16)
    v_ref[0] = proj(fw + 2 * rw, rw).astype(BF16)
    g = proj(fw + 3 * rw, rw)
    sg_ref[0] = (g * jax.nn.sigmoid(g) * gng_ref[...]).astype(BF16)
    gab_ref[0, :, :d] = proj(fw + 4 * rw, d).astype(BF16)
    gab_ref[0, :, d:] = proj(fw + 4 * rw + d, d).astype(BF16)


def _proj(x, mods, gain, w_in, cosf, sinf, gn_gain, later_weights, fw, rw):
    b, l, d = x.shape
    tm = min(PROJ_TOKEN_TILE, l)
    nj = l // tm
    steps = b * nj
    tok = lambda width: pl.BlockSpec((1, tm, width), lambda i, j: (i, j, 0))
    out = lambda width: jax.ShapeDtypeStruct((b, l, width), BF16)
    sliced = [w.reshape(steps, w.shape[0] // steps, w.shape[1]) for w in later_weights]
    for w in sliced:
        assert w.shape[1] % BF16_SUBLANES == 0
    cast_spec = lambda w: pl.BlockSpec((1,) + w.shape[1:], lambda i, j: (i * nj + j, 0, 0))
    stage_rows = max(r for r in range(8, w_in.shape[0] + 1, 8)
                     if w_in.shape[0] % r == 0 and r * w_in.shape[1] * 4 <= WEIGHT_STAGE_BYTES)
    res = pl.pallas_call(
        functools.partial(_proj_kernel, fw=fw, rw=rw, d=d, scale=RET_HEAD_DIM ** -0.5, ncast=len(sliced)),
        grid=(b, nj),
        in_specs=[
            tok(d),
            _const_spec(mods.shape),
            pl.BlockSpec((tm, cosf.shape[1]), lambda i, j: (j, 0)),
            pl.BlockSpec((tm, sinf.shape[1]), lambda i, j: (j, 0)),
            pl.BlockSpec(memory_space=pl.ANY),
        ] + [cast_spec(w) for w in sliced] + [_const_spec((1, d)), _const_spec((1, rw))],
        out_specs=[tok(fw), tok(rw), pl.BlockSpec((1, rw, tm), lambda i, j: (i, 0, j)),
                   tok(rw), tok(rw), tok(2 * d)] + [cast_spec(w) for w in sliced],
        out_shape=[out(fw), out(rw), jax.ShapeDtypeStruct((b, rw, l), BF16),
                   out(rw), out(rw), out(2 * d)]
        + [jax.ShapeDtypeStruct(w.shape, BF16) for w in sliced],
        scratch_shapes=[pltpu.VMEM(w_in.shape, BF16),
                        pltpu.VMEM((WEIGHT_STAGE_SLOTS, stage_rows, w_in.shape[1]), F32),
                        pltpu.SemaphoreType.DMA((WEIGHT_STAGE_SLOTS,))],
        compiler_params=_params("arbitrary", "arbitrary"),
        name="proj",
    )(x, mods, cosf, sinf, w_in, *sliced, gain, gn_gain)
    casts = [c.reshape(w.shape) for c, w in zip(res[6:], later_weights)]
    return res[:6], casts


def _fft_kernel(u_ref, kw_ref, twc_ref, tws_ref, w2_ref, cs_ref, o_ref, tr_ref, ti_ref, scr_ref,
                *, la, lb, gd, npb, scale):
    r = FFT_LA_TILE
    rows = lb * r
    fw = u_ref.shape[2]
    ngroups = fw // gd
    reps = fw // twc_ref.shape[2]
    pitch = scr_ref.shape[1] // la

    @pl.when(pl.program_id(0) == 0)
    def _():
        scr_ref[...] = jnp.zeros_like(scr_ref)

    for j in range(la // r):
        u = jnp.concatenate([u_ref[0, k * la + j * r:k * la + (j + 1) * r, :] for k in range(lb)], axis=0)
        half = lb // 2 + 1
        t = _dot(kw_ref[...], u)
        mirror = [lb - k for k in range(half, lb)]
        a = jnp.concatenate([t[:half * r]] + [t[k * r:(k + 1) * r] for k in mirror], axis=0)
        b = jnp.concatenate([t[half * r:]] + [-t[(half + k) * r:(half + k + 1) * r] for k in mirror], axis=0)
        ct = _tile_lanes(twc_ref[j], reps)
        st = _tile_lanes(tws_ref[j], reps)
        tr = (a * ct + b * st).astype(BF16)
        ti = (b * ct - a * st).astype(BF16)
        for k in range(lb):
            tr_ref[k * la + j * r:k * la + (j + 1) * r, :] = tr[k * r:(k + 1) * r]
            ti_ref[k * la + j * r:k * la + (j + 1) * r, :] = ti[k * r:(k + 1) * r]

    for pb in range(lb // npb):
        xs = []
        for p in range(npb):
            lo = (pb * npb + p) * la
            t = jnp.concatenate([tr_ref[lo:lo + la, :], ti_ref[lo:lo + la, :]], axis=0)
            xs.append(_dot(w2_ref[...], t).astype(BF16))
        for gi in range(ngroups):
            cols = slice(gi * gd, (gi + 1) * gd)
            lhs = jnp.concatenate([jnp.concatenate([x[:la, cols], x[la:, cols]], axis=1) for x in xs], axis=0)
            fg = _dot(lhs, cs_ref[...]) * scale
            for p in range(npb):
                scr_ref[gi, pl.ds(p, la, stride=pitch), :] = fg[p * la:(p + 1) * la]
        full = jnp.concatenate([scr_ref[gi].reshape(la, pitch, gd)[:, :npb, :] for gi in range(ngroups)], axis=2)
        o_ref[0, :, pb * npb:(pb + 1) * npb, :] = full.astype(BF16)


def _fft(u, kw, twc, tws, w2, cs, la, lb, gd, scale):
    b, l, fw = u.shape
    npb = min(BF16_SUBLANES, lb)
    out = pl.pallas_call(
        functools.partial(_fft_kernel, la=la, lb=lb, gd=gd, npb=npb, scale=scale),
        grid=(b,),
        in_specs=[pl.BlockSpec((1, l, fw), lambda i: (i, 0, 0)),
                  _const_spec(kw.shape), _const_spec(twc.shape), _const_spec(tws.shape),
                  _const_spec(w2.shape), _const_spec(cs.shape)],
        out_specs=pl.BlockSpec((1, la, lb, fw), lambda i: (i, 0, 0, 0)),
        out_shape=jax.ShapeDtypeStruct((b, la, lb, fw), BF16),
        scratch_shapes=[pltpu.VMEM((l, fw), BF16), pltpu.VMEM((l, fw), BF16),
                        pltpu.VMEM((fw // gd, la * (npb + FFT_SCATTER_PAD), gd), F32)],
        compiler_params=_params("arbitrary"),
        name="fft",
    )(u, kw, twc, tws, w2, cs)
    return out.reshape(b, l, fw)


def _ret_kernel(q_ref, kt_ref, v_ref, sg_ref, kct_ref, vc_ref, lgl_ref, lgc_ref,
                z_ref, sf_ref, sb_ref, stf_ref, stb_ref, p_ref, o_ref,
                dall_ref, qdf_ref, qdb_ref, kdf_ref, kdb_ref, *, nchunk, unroll):
    c = RET_CHUNK
    gw = GROUP_W
    lc = vc_ref.shape[1]

    lgf, lgb = lgl_ref[0:1, :], lgl_ref[1:2, :]
    pos = lax.broadcasted_iota(jnp.int32, (c, gw), 0).astype(F32)
    qdf_ref[...] = jnp.exp(lgf * (pos + 1.0))
    qdb_ref[...] = jnp.exp(lgb * (c - pos))
    kdf_ref[...] = jnp.exp(lgf * (c - 1.0 - pos))
    kdb_ref[...] = jnp.exp(lgb * pos)
    cdf = jnp.exp(lgf * c)
    cdb = jnp.exp(lgb * c)
    cpos = lax.broadcasted_iota(jnp.int32, (lc, gw), 0).astype(F32)
    wcf = jnp.exp(lgf * (lc - 1.0 - cpos))
    wcb = jnp.exp(lgb * cpos)
    si = lax.broadcasted_iota(jnp.int32, (c, HEADS_PER_GROUP * c), 0)
    sj = lax.broadcasted_iota(jnp.int32, (c, HEADS_PER_GROUP * c), 1) & (c - 1)
    diff = (si - sj).astype(F32)
    dall_ref[...] = (jnp.where(diff >= 0, jnp.exp(lgc_ref[0:1, :] * jnp.maximum(diff, 0.0)), 0.0)
                     + jnp.where(diff <= 0, jnp.exp(lgc_ref[1:2, :] * jnp.maximum(-diff, 0.0)), 0.0))

    pw = STATE_BLOCK_W
    npair = gw // pw
    same_head_p = (lax.broadcasted_iota(jnp.int32, (pw, pw), 0) // RET_HEAD_DIM
                   == lax.broadcasted_iota(jnp.int32, (pw, pw), 1) // RET_HEAD_DIM)

    def diag_blocks(t):
        return jnp.where(same_head_p, t, 0.0)

    def weighted(t, w):
        return (t.astype(F32) * w).astype(BF16)

    def rows(ref, n):
        return ref[0, pl.ds(pl.multiple_of(n * c, c), c), :]

    def kt_chunk(n):
        return kt_ref[0, :, pl.ds(pl.multiple_of(n * c, c), c)]

    def state_update(st_ref, kt, vw, cd):
        for p in range(npair):
            cols = slice(p * pw, (p + 1) * pw)
            st_ref[p] = st_ref[p] * cd[:, cols] + diag_blocks(_dot(kt[cols, :], vw[:, cols]))

    def full_state(s_ref, n):
        zero = jnp.zeros((pw, pw), BF16)
        return jnp.concatenate(
            [jnp.concatenate([s_ref[n, p] if q == p else zero for q in range(npair)], axis=1)
             for p in range(npair)], axis=0)

    kct = kct_ref[0]
    vc = vc_ref[0]
    stf_ref[...] = jnp.zeros_like(stf_ref)
    stb_ref[...] = jnp.zeros_like(stb_ref)
    state_update(stf_ref, kct, weighted(vc, wcf), cdf)
    state_update(stb_ref, kct, weighted(vc, wcb), cdb)

    def scan_body(i, carry):
        nf = i
        nb = nchunk - 1 - i
        sf_ref[nf] = stf_ref[...].astype(BF16)
        state_update(stf_ref, kt_chunk(nf), weighted(rows(v_ref, nf), kdf_ref[...]), cdf)
        sb_ref[nb] = stb_ref[...].astype(BF16)
        state_update(stb_ref, kt_chunk(nb), weighted(rows(v_ref, nb), kdb_ref[...]), cdb)
        return carry

    lax.fori_loop(0, nchunk, scan_body, 0, unroll=unroll)

    lane = lax.broadcasted_iota(jnp.int32, (c, gw), 1)
    sub = lax.broadcasted_iota(jnp.int32, (gw, c), 0)
    lane_masks = [(lane >= h * RET_HEAD_DIM) & (lane < (h + 1) * RET_HEAD_DIM)
                  for h in range(HEADS_PER_GROUP)]
    sub_masks = [(sub >= h * RET_HEAD_DIM) & (sub < (h + 1) * RET_HEAD_DIM)
                 for h in range(HEADS_PER_GROUP)]
    same_head = (lax.broadcasted_iota(jnp.int32, (gw, gw), 0) // RET_HEAD_DIM
                 == lax.broadcasted_iota(jnp.int32, (gw, gw), 1) // RET_HEAD_DIM)
    bd_mean = jnp.where(same_head, 1.0 / RET_HEAD_DIM, 0.0).astype(BF16)

    def score_body(n, carry):
        ktn = kt_chunk(n)
        zk = jnp.zeros_like(ktn)
        kbd = jnp.concatenate([jnp.where(m, ktn, zk) for m in sub_masks], axis=1)
        p_ref[n] = (_dot(rows(q_ref, n), kbd) * dall_ref[...]).astype(BF16)
        return carry

    lax.fori_loop(0, nchunk, score_body, 0, unroll=unroll)

    def mix_body(n, carry):
        qn = rows(q_ref, n)
        vn = rows(v_ref, n)
        zv = jnp.zeros_like(vn)
        vbd = jnp.concatenate([jnp.where(m, vn, zv) for m in lane_masks], axis=0)
        o_ref[n] = (_dot(p_ref[n], vbd) + qdf_ref[...] * _dot(qn, full_state(sf_ref, n))
                    + qdb_ref[...] * _dot(qn, full_state(sb_ref, n)))
        return carry

    lax.fori_loop(0, nchunk, mix_body, 0, unroll=unroll)

    def norm_body(n, carry):
        o = o_ref[n]
        ms = _dot((o * o).astype(BF16), bd_mean)
        z = rows(sg_ref, n).astype(F32) * (o * lax.rsqrt(ms + EPS))
        z_ref[0, pl.ds(pl.multiple_of(n * c, c), c), :] = z.astype(BF16)
        return carry

    lax.fori_loop(0, nchunk, norm_body, 0, unroll=unroll)


def _retention(q, kt, v, sg, kct, vc, log_gamma):
    b, l, rw = q.shape
    lc = vc.shape[1]
    c = RET_CHUNK
    gw = GROUP_W
    hpg = HEADS_PER_GROUP
    ng = rw // gw
    nchunk = l // c
    pw = STATE_BLOCK_W
    npair = gw // pw
    assert c & (c - 1) == 0
    lg = log_gamma.reshape(2, ng, hpg).transpose(1, 0, 2)
    lg_lane = jnp.repeat(lg, RET_HEAD_DIM, axis=2)
    lg_col = jnp.repeat(lg, c, axis=2)
    tok = pl.BlockSpec((1, l, gw), lambda i, j: (i, 0, j))
    tokt = pl.BlockSpec((1, gw, l), lambda i, j: (i, j, 0))
    grp = lambda r, width: pl.BlockSpec((None, r, width), lambda i, j: (j, 0, 0))
    return pl.pallas_call(
        functools.partial(_ret_kernel, nchunk=nchunk, unroll=math.gcd(nchunk, RET_UNROLL)),
        grid=(b, ng),
        in_specs=[
            tok, tokt, tok, tok,
            pl.BlockSpec((1, gw, lc), lambda i, j: (i, j, 0)),
            pl.BlockSpec((1, lc, gw), lambda i, j: (i, 0, j)),
            grp(2, gw), grp(2, hpg * c),
        ],
        out_specs=tok,
        out_shape=jax.ShapeDtypeStruct((b, l, rw), BF16),
        scratch_shapes=[pltpu.VMEM((nchunk, npair, pw, pw), BF16), pltpu.VMEM((nchunk, npair, pw, pw), BF16),
                        pltpu.VMEM((npair, pw, pw), F32), pltpu.VMEM((npair, pw, pw), F32),
                        pltpu.VMEM((nchunk, c, hpg * c), BF16),
                        pltpu.VMEM((nchunk, c, gw), F32),
                        pltpu.VMEM((c, hpg * c), F32)] + [pltpu.VMEM((c, gw), F32)] * 4,
        compiler_params=_params("arbitrary", "arbitrary"),
        name="ret",
    )(q, kt, v, sg, kct, vc, lg_lane, lg_col)


def _out_kernel(w4_ref, wr_ref, wo_ref, w1_ref, w2_ref, x_ref, mod_ref, fm_ref, z_ref, gab_ref,
                gain2_ref, fgain_ref, o_ref, *, ff_chunk):
    mod = mod_ref[:, pl.ds(pl.program_id(0), 1), :]
    g1, sh2, sc2, g2 = mod[2], mod[3], mod[4], mod[5]
    y_four = _dot(fm_ref[0], w4_ref[...])
    y_ret = _dot(z_ref[0], wr_ref[...])
    d = x_ref.shape[2]
    y = (jax.nn.sigmoid(gab_ref[0, :, :d].astype(F32)) * y_four
         + jax.nn.sigmoid(gab_ref[0, :, d:].astype(F32)) * y_ret)
    x1 = x_ref[0] + g1 * _dot(y.astype(BF16), wo_ref[...])
    h2 = _norm_mod(x1, gain2_ref[...], sh2, sc2).astype(BF16)
    dff = w1_ref.shape[1]
    tm = x1.shape[0]
    chunks = list(range(0, dff, ff_chunk))
    acc = None
    for lo in chunks[:-1]:
        hid = jnp.maximum(_dot(h2, w1_ref[:, lo:lo + ff_chunk]), 0.0)
        part = _dot((hid * hid).astype(BF16), w2_ref[lo:lo + ff_chunk, :])
        acc = part if acc is None else acc + part
    lo = chunks[-1]
    hid = jnp.maximum(_dot(h2, w1_ref[:, lo:lo + ff_chunk]), 0.0)
    hid2 = (hid * hid).astype(BF16)
    for r0 in range(0, tm, tm // 2):
        rs = slice(r0, r0 + tm // 2)
        part = _dot(hid2[rs], w2_ref[lo:lo + ff_chunk, :])
        x2 = x1[rs] + g2 * (part if acc is None else acc[rs] + part)
        ms = jnp.mean(x2 * x2, axis=-1, keepdims=True)
        o_ref[0, rs, :] = x2 * lax.rsqrt(ms + EPS) * fgain_ref[...]


def _out(x, mods, fm, z, gab, w4, wr, wo, gain2, w1, w2, fgain):
    b, l, d = x.shape
    tm = min(TOKEN_TILE, l)
    tok = lambda width: pl.BlockSpec((1, tm, width), lambda i, j: (i, j, 0))
    weights = (w4, wr, wo, w1, w2)
    return pl.pallas_call(
        functools.partial(_out_kernel, ff_chunk=min(1024, w1.shape[1])),
        grid=(b, l // tm),
        in_specs=[_const_spec(w.shape) for w in weights] + [
            tok(d),
            _const_spec(mods.shape),
            tok(fm.shape[2]), tok(z.shape[2]), tok(2 * d),
            _const_spec((1, d)), _const_spec((1, d)),
        ],
        out_specs=tok(d),
        out_shape=jax.ShapeDtypeStruct((b, l, d), F32),
        compiler_params=_params("arbitrary", "arbitrary"),
        name="out",
    )(*weights, x, mods, fm, z, gab, gain2, fgain)


def _dft_tables(l, gd):
    la = FFT_LA
    lb = l // la

    def cs(n):
        idx = np.arange(n)
        ang = 2.0 * np.pi * ((idx[:, None] * idx[None, :]) % n) / n
        return np.cos(ang), np.sin(ang)

    cb, sb = cs(lb)
    eye = np.eye(FFT_LA_TILE)
    half = lb // 2 + 1
    kw = np.concatenate([np.kron(cb[:half], eye), -np.kron(sb[:half], eye)], axis=0)
    ca, sa = cs(la)
    w2 = np.block([[ca, sa], [-sa, ca]])
    cc, sc = cs(gd)
    chan = np.concatenate([cc, sc], axis=0)
    tw = 2.0 * np.pi * (np.arange(lb)[:, None] * np.arange(la)[None, :]) / l
    tw = tw.reshape(lb, la // FFT_LA_TILE, FFT_LA_TILE).transpose(1, 0, 2).reshape(la // FFT_LA_TILE, -1)
    twc = np.repeat(np.cos(tw)[:, :, None], LANES, axis=2)
    tws = np.repeat(np.sin(tw)[:, :, None], LANES, axis=2)
    as_bf = lambda a: jnp.asarray(a, dtype=F32).astype(BF16)
    return as_bf(kw), as_bf(w2), as_bf(chan), jnp.asarray(twc, F32), jnp.asarray(tws, F32), la, lb


def _rope_tables(l):
    f32 = np.float32
    nf = RET_HEAD_DIM // 4
    inv = np.power(f32(ROPE_BASE), -np.arange(nf, dtype=f32) / f32(nf)).astype(f32)
    rows = l // GRID_W
    r, cc = np.meshgrid(np.arange(rows, dtype=f32), np.arange(GRID_W, dtype=f32), indexing="ij")
    ang = np.concatenate([r.reshape(-1)[:, None] * inv, cc.reshape(-1)[:, None] * inv], axis=-1).astype(f32)
    cos, sin = np.cos(ang).astype(f32), np.sin(ang).astype(f32)
    cos_h = np.concatenate([cos, cos], axis=1)
    sin_h = np.concatenate([-sin, sin], axis=1)
    reps = LANES // RET_HEAD_DIM
    return jnp.asarray(np.tile(cos_h, (1, reps))), jnp.asarray(np.tile(sin_h, (1, reps)))


def kernel(x, c, ctx, c_ctx, w_ada, b_ada, norm1_gain, w_in, four_w_out, ret_decay_logit,
           ret_gn_gain, ret_w_out, w_out, norm2_gain, w_mlp1, w_mlp2, final_gain):
    assert w_ada.shape[0] == 1, "single-layer block"
    b, l, d = x.shape
    lc = ctx.shape[1]
    fw = four_w_out.shape[1]
    rw = ret_w_out.shape[1]
    gd = fw // FOUR_GROUPS
    assert l % FFT_LA == 0 and l % RET_CHUNK == 0 and rw % GROUP_W == 0

    mods, kct, vc = _ada(c, c_ctx, w_ada[0], b_ada, ctx, norm1_gain, w_in[0], fw, rw)

    kw, w2, chan, twc, tws, la, lb = _dft_tables(l, gd)
    cosf, sinf = _rope_tables(l)

    later = [four_w_out[0], ret_w_out[0], w_out[0], w_mlp1[0], w_mlp2[0]]
    (u, q, kt, v, sg, gab), later_b = _proj(x, mods, norm1_gain, w_in[0], cosf, sinf, ret_gn_gain,
                                             later, fw, rw)

    fm = _fft(u, kw, twc, tws, w2, chan, la, lb, gd, 1.0 / math.sqrt(l * gd))

    log_gamma = jax.nn.log_sigmoid(ret_decay_logit[0].astype(F32))
    z = _retention(q, kt, v, sg, kct, vc, log_gamma)

    w4, wr, wo, w1, w2 = later_b
    return _out(x, mods, fm, z, gab, w4, wr, wo, norm2_gain, w1, w2, final_gain[None, :])
```

```python
import functools
import math

import jax
import jax.numpy as jnp
import numpy as np
from jax import lax
from jax.experimental import pallas as pl
from jax.experimental.pallas import tpu as pltpu

F32 = jnp.float32
BF16 = jnp.bfloat16

GRID_W = 64
FOUR_GROUPS = 4
RET_HEAD_DIM = 64
N_MOD = 6
CTX_FIRST_STEP = 2
ROPE_BASE = 10000.0
EPS = 1e-6

LANES = 128
MXU_DIM = 256
F32_SUBLANES = 8
BF16_SUBLANES = 16
VMEM_LIMIT_BYTES = 56 * 1024 * 1024

RET_CHUNK = 128
RET_UNROLL = 32
HEADS_PER_GROUP = MXU_DIM // RET_HEAD_DIM
GROUP_W = HEADS_PER_GROUP * RET_HEAD_DIM
STATE_BLOCK_W = LANES
FFT_LA = 128
FFT_LA_TILE = BF16_SUBLANES
FFT_CHANNEL_ROWS = 512
FFT_SCATTER_PAD = F32_SUBLANES
TOKEN_TILE = 512
PROJ_TOKEN_TILE = 1024
WEIGHT_STAGE_BYTES = 2 * 1024 * 1024
WEIGHT_STAGE_SLOTS = 4


def _dot(a, b):
    return jnp.dot(a, b, preferred_element_type=F32)


def _norm_mod(x, gain, shift, scale):
    ms = jnp.mean(x * x, axis=-1, keepdims=True)
    y = x * lax.rsqrt(ms + EPS) * gain
    return y * (1.0 + scale) + shift


def _tile_lanes(t, reps):
    return jnp.concatenate([t] * reps, axis=1) if reps > 1 else t


def _const_spec(shape):
    nd = len(shape)
    return pl.BlockSpec(shape, lambda *_: (0,) * nd, pipeline_mode=pl.Buffered(1))


def _params(*sem):
    return pltpu.CompilerParams(dimension_semantics=sem, vmem_limit_bytes=VMEM_LIMIT_BYTES)


def _load_weights_as_bf16(pairs, stage_ref, sem_ref):
    slots, sr, sc = stage_ref.shape
    chunks = [(src, dst, r0, c0)
              for src, dst in pairs
              for r0 in range(0, src.shape[0], sr)
              for c0 in range(0, src.shape[1], sc)]

    def copy(k):
        src, _, r0, c0 = chunks[k]
        slot = k % slots
        return pltpu.make_async_copy(src.at[pl.ds(r0, sr), pl.ds(c0, sc)], stage_ref.at[slot], sem_ref.at[slot])

    for k in range(min(slots - 1, len(chunks))):
        copy(k).start()
    for k, (_, dst, r0, c0) in enumerate(chunks):
        if k + slots - 1 < len(chunks):
            copy(k + slots - 1).start()
        copy(k).wait()
        dst[r0:r0 + sr, c0:c0 + sc] = stage_ref[k % slots].astype(BF16)


def _is_first_step(grid_rank):
    first = pl.program_id(0) == 0
    for axis in range(1, grid_rank):
        first = first & (pl.program_id(axis) == 0)
    return first


def _ada_kernel(c_ref, cctx_ref, w_ref, b_ref, ctx_ref, gain_ref, wkv_ref,
                o_ref, kt_ref, v_ref, cv_ref, modc_ref, wkv_bf_ref, *, rw, scale):
    j = pl.program_id(0)
    b = c_ref.shape[0]
    cv_ref[...] = jnp.zeros_like(cv_ref)
    cv_ref[0:b, :] = c_ref[...]
    cv_ref[b:b + 1, :] = cctx_ref[...]
    c = cv_ref[...]
    s = c * jax.nn.sigmoid(c)
    mod = _dot(s.astype(BF16), w_ref[...].astype(BF16)) + b_ref[...]
    o_ref[0] = mod

    @pl.when(j < CTX_FIRST_STEP)
    def _():
        modc_ref[pl.ds(j, 1), :] = mod[b:b + 1, :]

    @pl.when(j == CTX_FIRST_STEP)
    def _():
        wkv_bf_ref[...] = wkv_ref[...].astype(BF16)

    @pl.when(j >= CTX_FIRST_STEP)
    def _():
        h = _norm_mod(ctx_ref[0], gain_ref[...], modc_ref[0:1, :], modc_ref[1:2, :]).astype(BF16)
        kt_ref[0] = (_dot(h, wkv_bf_ref[:, :rw]) * scale).T.astype(BF16)
        v_ref[0] = _dot(h, wkv_bf_ref[:, rw:]).astype(BF16)


def _ada(c, c_ctx, w_ada, b_ada, ctx, gain, w_in, fw, rw):
    b, d = c.shape
    lc = ctx.shape[1]
    rows = -(-(b + 1) // F32_SUBLANES) * F32_SUBLANES
    assert w_ada.shape[1] == N_MOD * d and b <= N_MOD - CTX_FIRST_STEP
    kv_block = (fw + rw) // (2 * rw)
    assert kv_block * 2 * rw == fw + rw
    sample = lambda j: jnp.clip(j - CTX_FIRST_STEP, 0, b - 1)
    return pl.pallas_call(
        functools.partial(_ada_kernel, rw=rw, scale=RET_HEAD_DIM ** -0.5),
        grid=(N_MOD,),
        in_specs=[
            pl.BlockSpec((b, d), lambda j: (0, 0)),
            pl.BlockSpec((1, d), lambda j: (0, 0)),
            pl.BlockSpec((d, d), lambda j: (0, j)),
            pl.BlockSpec((1, d), lambda j: (0, j)),
            pl.BlockSpec((1, lc, d), lambda j: (sample(j), 0, 0)),
            pl.BlockSpec((1, d), lambda j: (0, 0)),
            pl.BlockSpec((d, 2 * rw), lambda j: (0, kv_block)),
        ],
        out_specs=[pl.BlockSpec((1, rows, d), lambda j: (j, 0, 0)),
                   pl.BlockSpec((1, rw, lc), lambda j: (sample(j), 0, 0)),
                   pl.BlockSpec((1, lc, rw), lambda j: (sample(j), 0, 0))],
        out_shape=[jax.ShapeDtypeStruct((N_MOD, rows, d), F32),
                   jax.ShapeDtypeStruct((b, rw, lc), BF16), jax.ShapeDtypeStruct((b, lc, rw), BF16)],
        scratch_shapes=[pltpu.VMEM((rows, d), F32), pltpu.VMEM((CTX_FIRST_STEP, d), F32),
                        pltpu.VMEM((d, 2 * rw), BF16)],
        compiler_params=_params("arbitrary"),
        name="ada",
    )(c, c_ctx[None, :], w_ada, b_ada, ctx, gain, w_in)


def _proj_kernel(*refs, fw, rw, d, scale, ncast):
    x_ref, mod_ref, cos_ref, sin_ref, w_hbm = refs[:5]
    cast_in = refs[5:5 + ncast]
    gain_ref, gng_ref = refs[5 + ncast:7 + ncast]
    u_ref, q_ref, kt_ref, v_ref, sg_ref, gab_ref = refs[7 + ncast:13 + ncast]
    cast_out = refs[13 + ncast:13 + 2 * ncast]
    w_ref, stage_ref, sem_ref = refs[13 + 2 * ncast:]

    @pl.when(_is_first_step(2))
    def _():
        _load_weights_as_bf16([(w_hbm, w_ref)], stage_ref, sem_ref)

    for src, dst in zip(cast_in, cast_out):
        dst[...] = src[...].astype(BF16)

    mod = mod_ref[:, pl.ds(pl.program_id(0), 1), :]
    h = _norm_mod(x_ref[0], gain_ref[...], mod[0], mod[1]).astype(BF16)
    tm = h.shape[0]

    def proj(lo, width):
        return _dot(h, w_ref[:, lo:lo + width])

    u_ref[0] = proj(0, fw).astype(BF16)

    reps = rw // cos_ref.shape[1]
    cosf = _tile_lanes(cos_ref[...], reps)
    sinf = _tile_lanes(sin_ref[...], reps)
    lane = lax.broadcasted_iota(jnp.int32, (tm, rw), 1)
    first_half = (lane & (RET_HEAD_DIM // 2)) == 0

    def rope(t):
        rot = jnp.where(first_half,
                        pltpu.roll(t, rw - RET_HEAD_DIM // 2, 1),
                        pltpu.roll(t, RET_HEAD_DIM // 2, 1))
        return t * cosf + rot * sinf

    q_ref[0] = rope(proj(fw, rw)).astype(BF16)
    kt_ref[0] = (rope(proj(fw + rw, rw)) * scale).T.astype(BF16)
    v_ref[0] = proj(fw + 2 * rw, rw).astype(BF16)
    g = proj(fw + 3 * rw, rw)
    sg_ref[0] = (g * jax.nn.sigmoid(g) * gng_ref[...]).astype(BF16)
    gab_ref[0, :, :d] = proj(fw + 4 * rw, d).astype(BF16)
    gab_ref[0, :, d:] = proj(fw + 4 * rw + d, d).astype(BF16)


def _proj(x, mods, gain, w_in, cosf, sinf, gn_gain, later_weights, fw, rw):
    b, l, d = x.shape
    tm = min(PROJ_TOKEN_TILE, l)
    nj = l // tm
    steps = b * nj
    tok = lambda width: pl.BlockSpec((1, tm, width), lambda i, j: (i, j, 0))
    out = lambda width: jax.ShapeDtypeStruct((b, l, width), BF16)
    sliced = [w.reshape(steps, w.shape[0] // steps, w.shape[1]) for w in later_weights]
    for w in sliced:
        assert w.shape[1] % BF16_SUBLANES == 0
    cast_spec = lambda w: pl.BlockSpec((1,) + w.shape[1:], lambda i, j: (i * nj + j, 0, 0))
    stage_rows = max(r for r in range(8, w_in.shape[0] + 1, 8)
                     if w_in.shape[0] % r == 0 and r * w_in.shape[1] * 4 <= WEIGHT_STAGE_BYTES)
    res = pl.pallas_call(
        functools.partial(_proj_kernel, fw=fw, rw=rw, d=d, scale=RET_HEAD_DIM ** -0.5, ncast=len(sliced)),
        grid=(b, nj),
        in_specs=[
            tok(d),
            _const_spec(mods.shape),
            pl.BlockSpec((tm, cosf.shape[1]), lambda i, j: (j, 0)),
            pl.BlockSpec((tm, sinf.shape[1]), lambda i, j: (j, 0)),
            pl.BlockSpec(memory_space=pl.ANY),
        ] + [cast_spec(w) for w in sliced] + [_const_spec((1, d)), _const_spec((1, rw))],
        out_specs=[tok(fw), tok(rw), pl.BlockSpec((1, rw, tm), lambda i, j: (i, 0, j)),
                   tok(rw), tok(rw), tok(2 * d)] + [cast_spec(w) for w in sliced],
        out_shape=[out(fw), out(rw), jax.ShapeDtypeStruct((b, rw, l), BF16),
                   out(rw), out(rw), out(2 * d)]
        + [jax.ShapeDtypeStruct(w.shape, BF16) for w in sliced],
        scratch_shapes=[pltpu.VMEM(w_in.shape, BF16),
                        pltpu.VMEM((WEIGHT_STAGE_SLOTS, stage_rows, w_in.shape[1]), F32),
                        pltpu.SemaphoreType.DMA((WEIGHT_STAGE_SLOTS,))],
        compiler_params=_params("arbitrary", "arbitrary"),
        name="proj",
    )(x, mods, cosf, sinf, w_in, *sliced, gain, gn_gain)
    casts = [c.reshape(w.shape) for c, w in zip(res[6:], later_weights)]
    return res[:6], casts


def _fft_kernel(u_ref, kw_ref, twc_ref, tws_ref, w2_ref, cs_ref, o_ref, tr_ref, ti_ref, scr_ref,
                *, la, lb, gd, npb, scale):
    r = FFT_LA_TILE
    rows = lb * r
    fw = u_ref.shape[2]
    ngroups = fw // gd
    reps = fw // twc_ref.shape[2]
    pitch = scr_ref.shape[1] // la

    @pl.when(pl.program_id(0) == 0)
    def _():
        scr_ref[...] = jnp.zeros_like(scr_ref)

    for j in range(la // r):
        u = jnp.concatenate([u_ref[0, k * la + j * r:k * la + (j + 1) * r, :] for k in range(lb)], axis=0)
        half = lb // 2 + 1
        t = _dot(kw_ref[...], u)
        mirror = [lb - k for k in range(half, lb)]
        a = jnp.concatenate([t[:half * r]] + [t[k * r:(k + 1) * r] for k in mirror], axis=0)
        b = jnp.concatenate([t[half * r:]] + [-t[(half + k) * r:(half + k + 1) * r] for k in mirror], axis=0)
        ct = _tile_lanes(twc_ref[j], reps)
        st = _tile_lanes(tws_ref[j], reps)
        tr = (a * ct + b * st).astype(BF16)
        ti = (b * ct - a * st).astype(BF16)
        for k in range(lb):
            tr_ref[k * la + j * r:k * la + (j + 1) * r, :] = tr[k * r:(k + 1) * r]
            ti_ref[k * la + j * r:k * la + (j + 1) * r, :] = ti[k * r:(k + 1) * r]

    rb = min(FFT_CHANNEL_ROWS, u_ref.shape[1])
    for r0 in range(0, u_ref.shape[1], rb):
        for gi in range(ngroups):
            cols = slice(gi * gd, (gi + 1) * gd)
            z = _dot(jnp.concatenate([tr_ref[r0:r0 + rb, cols], ti_ref[r0:r0 + rb, cols]], axis=1),
                     cs_ref[...]).astype(BF16)
            tr_ref[r0:r0 + rb, cols] = z[:, :gd]
            ti_ref[r0:r0 + rb, cols] = z[:, gd:]

    for pb in range(lb // npb):
        for p in range(npb):
            lo = (pb * npb + p) * la
            t = jnp.concatenate([tr_ref[lo:lo + la, :], ti_ref[lo:lo + la, :]], axis=0)
            f = _dot(w2_ref[...], t) * scale
            for gi in range(ngroups):
                scr_ref[gi, pl.ds(p, la, stride=pitch), :] = f[:, gi * gd:(gi + 1) * gd]
        full = jnp.concatenate([scr_ref[gi].reshape(la, pitch, gd)[:, :npb, :] for gi in range(ngroups)], axis=2)
        o_ref[0, :, pb * npb:(pb + 1) * npb, :] = full.astype(BF16)


def _fft(u, kw, twc, tws, w2, cs, la, lb, gd, scale):
    b, l, fw = u.shape
    npb = min(BF16_SUBLANES, lb)
    out = pl.pallas_call(
        functools.partial(_fft_kernel, la=la, lb=lb, gd=gd, npb=npb, scale=scale),
        grid=(b,),
        in_specs=[pl.BlockSpec((1, l, fw), lambda i: (i, 0, 0)),
                  _const_spec(kw.shape), _const_spec(twc.shape), _const_spec(tws.shape),
                  _const_spec(w2.shape), _const_spec(cs.shape)],
        out_specs=pl.BlockSpec((1, la, lb, fw), lambda i: (i, 0, 0, 0)),
        out_shape=jax.ShapeDtypeStruct((b, la, lb, fw), BF16),
        scratch_shapes=[pltpu.VMEM((l, fw), BF16), pltpu.VMEM((l, fw), BF16),
                        pltpu.VMEM((fw // gd, la * (npb + FFT_SCATTER_PAD), gd), F32)],
        compiler_params=_params("arbitrary"),
        name="fft",
    )(u, kw, twc, tws, w2, cs)
    return out.reshape(b, l, fw)


def _ret_kernel(q_ref, kt_ref, v_ref, sg_ref, kct_ref, vc_ref, lgl_ref, lgc_ref,
                z_ref, sf_ref, sb_ref, stf_ref, stb_ref, p_ref, o_ref,
                dall_ref, qdf_ref, qdb_ref, kdf_ref, kdb_ref, *, nchunk, unroll):
    c = RET_CHUNK
    gw = GROUP_W
    lc = vc_ref.shape[1]

    lgf, lgb = lgl_ref[0:1, :], lgl_ref[1:2, :]
    pos = lax.broadcasted_iota(jnp.int32, (c, gw), 0).astype(F32)
    qdf_ref[...] = jnp.exp(lgf * (pos + 1.0))
    qdb_ref[...] = jnp.exp(lgb * (c - pos))
    kdf_ref[...] = jnp.exp(lgf * (c - 1.0 - pos))
    kdb_ref[...] = jnp.exp(lgb * pos)
    cdf = jnp.exp(lgf * c)
    cdb = jnp.exp(lgb * c)
    cpos = lax.broadcasted_iota(jnp.int32, (lc, gw), 0).astype(F32)
    wcf = jnp.exp(lgf * (lc - 1.0 - cpos))
    wcb = jnp.exp(lgb * cpos)
    si = lax.broadcasted_iota(jnp.int32, (c, HEADS_PER_GROUP * c), 0)
    sj = lax.broadcasted_iota(jnp.int32, (c, HEADS_PER_GROUP * c), 1) & (c - 1)
    diff = (si - sj).astype(F32)
    dall_ref[...] = (jnp.where(diff >= 0, jnp.exp(lgc_ref[0:1, :] * jnp.maximum(diff, 0.0)), 0.0)
                     + jnp.where(diff <= 0, jnp.exp(lgc_ref[1:2, :] * jnp.maximum(-diff, 0.0)), 0.0))

    pw = STATE_BLOCK_W
    npair = gw // pw
    same_head_p = (lax.broadcasted_iota(jnp.int32, (pw, pw), 0) // RET_HEAD_DIM
                   == lax.broadcasted_iota(jnp.int32, (pw, pw), 1) // RET_HEAD_DIM)

    def diag_blocks(t):
        return jnp.where(same_head_p, t, 0.0)

    def weighted(t, w):
        return (t.astype(F32) * w).astype(BF16)

    def rows(ref, n):
        return ref[0, pl.ds(pl.multiple_of(n * c, c), c), :]

    def kt_chunk(n):
        return kt_ref[0, :, pl.ds(pl.multiple_of(n * c, c), c)]

    def state_update(st_ref, kt, vw, cd):
        for p in range(npair):
            cols = slice(p * pw, (p + 1) * pw)
            st_ref[p] = st_ref[p] * cd[:, cols] + diag_blocks(_dot(kt[cols, :], vw[:, cols]))

    def full_state(s_ref, n):
        zero = jnp.zeros((pw, pw), BF16)
        return jnp.concatenate(
            [jnp.concatenate([s_ref[n, p] if q == p else zero for q in range(npair)], axis=1)
             for p in range(npair)], axis=0)

    kct = kct_ref[0]
    vc = vc_ref[0]
    stf_ref[...] = jnp.zeros_like(stf_ref)
    stb_ref[...] = jnp.zeros_like(stb_ref)
    state_update(stf_ref, kct, weighted(vc, wcf), cdf)
    state_update(stb_ref, kct, weighted(vc, wcb), cdb)

    def scan_body(i, carry):
        nf = i
        nb = nchunk - 1 - i
        sf_ref[nf] = stf_ref[...].astype(BF16)
        state_update(stf_ref, kt_chunk(nf), weighted(rows(v_ref, nf), kdf_ref[...]), cdf)
        sb_ref[nb] = stb_ref[...].astype(BF16)
        state_update(stb_ref, kt_chunk(nb), weighted(rows(v_ref, nb), kdb_ref[...]), cdb)
        return carry

    lax.fori_loop(0, nchunk, scan_body, 0, unroll=unroll)

    lane = lax.broadcasted_iota(jnp.int32, (c, gw), 1)
    sub = lax.broadcasted_iota(jnp.int32, (gw, c), 0)
    lane_masks = [(lane >= h * RET_HEAD_DIM) & (lane < (h + 1) * RET_HEAD_DIM)
                  for h in range(HEADS_PER_GROUP)]
    sub_masks = [(sub >= h * RET_HEAD_DIM) & (sub < (h + 1) * RET_HEAD_DIM)
                 for h in range(HEADS_PER_GROUP)]
    same_head = (lax.broadcasted_iota(jnp.int32, (gw, gw), 0) // RET_HEAD_DIM
                 == lax.broadcasted_iota(jnp.int32, (gw, gw), 1) // RET_HEAD_DIM)
    bd_mean = jnp.where(same_head, 1.0 / RET_HEAD_DIM, 0.0).astype(BF16)

    def score_body(n, carry):
        ktn = kt_chunk(n)
        zk = jnp.zeros_like(ktn)
        kbd = jnp.concatenate([jnp.where(m, ktn, zk) for m in sub_masks], axis=1)
        p_ref[n] = (_dot(rows(q_ref, n), kbd) * dall_ref[...]).astype(BF16)
        return carry

    lax.fori_loop(0, nchunk, score_body, 0, unroll=unroll)

    def mix_body(n, carry):
        qn = rows(q_ref, n)
        vn = rows(v_ref, n)
        zv = jnp.zeros_like(vn)
        vbd = jnp.concatenate([jnp.where(m, vn, zv) for m in lane_masks], axis=0)
        o_ref[n] = (_dot(p_ref[n], vbd) + qdf_ref[...] * _dot(qn, full_state(sf_ref, n))
                    + qdb_ref[...] * _dot(qn, full_state(sb_ref, n)))
        return carry

    lax.fori_loop(0, nchunk, mix_body, 0, unroll=unroll)

    def norm_body(n, carry):
        o = o_ref[n]
        ms = _dot((o * o).astype(BF16), bd_mean)
        z = rows(sg_ref, n).astype(F32) * (o * lax.rsqrt(ms + EPS))
        z_ref[0, pl.ds(pl.multiple_of(n * c, c), c), :] = z.astype(BF16)
        return carry

    lax.fori_loop(0, nchunk, norm_body, 0, unroll=unroll)


def _retention(q, kt, v, sg, kct, vc, log_gamma):
    b, l, rw = q.shape
    lc = vc.shape[1]
    c = RET_CHUNK
    gw = GROUP_W
    hpg = HEADS_PER_GROUP
    ng = rw // gw
    nchunk = l // c
    pw = STATE_BLOCK_W
    npair = gw // pw
    assert c & (c - 1) == 0
    lg = log_gamma.reshape(2, ng, hpg).transpose(1, 0, 2)
    lg_lane = jnp.repeat(lg, RET_HEAD_DIM, axis=2)
    lg_col = jnp.repeat(lg, c, axis=2)
    tok = pl.BlockSpec((1, l, gw), lambda i, j: (i, 0, j))
    tokt = pl.BlockSpec((1, gw, l), lambda i, j: (i, j, 0))
    grp = lambda r, width: pl.BlockSpec((None, r, width), lambda i, j: (j, 0, 0))
    return pl.pallas_call(
        functools.partial(_ret_kernel, nchunk=nchunk, unroll=math.gcd(nchunk, RET_UNROLL)),
        grid=(b, ng),
        in_specs=[
            tok, tokt, tok, tok,
            pl.BlockSpec((1, gw, lc), lambda i, j: (i, j, 0)),
            pl.BlockSpec((1, lc, gw), lambda i, j: (i, 0, j)),
            grp(2, gw), grp(2, hpg * c),
        ],
        out_specs=tok,
        out_shape=jax.ShapeDtypeStruct((b, l, rw), BF16),
        scratch_shapes=[pltpu.VMEM((nchunk, npair, pw, pw), BF16), pltpu.VMEM((nchunk, npair, pw, pw), BF16),
                        pltpu.VMEM((npair, pw, pw), F32), pltpu.VMEM((npair, pw, pw), F32),
                        pltpu.VMEM((nchunk, c, hpg * c), BF16),
                        pltpu.VMEM((nchunk, c, gw), F32),
                        pltpu.VMEM((c, hpg * c), F32)] + [pltpu.VMEM((c, gw), F32)] * 4,
        compiler_params=_params("arbitrary", "arbitrary"),
        name="ret",
    )(q, kt, v, sg, kct, vc, lg_lane, lg_col)


def _out_kernel(w4_ref, wr_ref, wo_ref, w1_ref, w2_ref, x_ref, mod_ref, fm_ref, z_ref, gab_ref,
                gain2_ref, fgain_ref, o_ref, *, ff_chunk):
    mod = mod_ref[:, pl.ds(pl.program_id(0), 1), :]
    g1, sh2, sc2, g2 = mod[2], mod[3], mod[4], mod[5]
    y_four = _dot(fm_ref[0], w4_ref[...])
    y_ret = _dot(z_ref[0], wr_ref[...])
    d = x_ref.shape[2]
    y = (jax.nn.sigmoid(gab_ref[0, :, :d].astype(F32)) * y_four
         + jax.nn.sigmoid(gab_ref[0, :, d:].astype(F32)) * y_ret)
    x1 = x_ref[0] + g1 * _dot(y.astype(BF16), wo_ref[...])
    h2 = _norm_mod(x1, gain2_ref[...], sh2, sc2).astype(BF16)
    dff = w1_ref.shape[1]
    acc = None
    for lo in range(0, dff, ff_chunk):
        hid = jnp.maximum(_dot(h2, w1_ref[:, lo:lo + ff_chunk]), 0.0)
        part = _dot((hid * hid).astype(BF16), w2_ref[lo:lo + ff_chunk, :])
        acc = part if acc is None else acc + part
    x2 = x1 + g2 * acc
    ms = jnp.mean(x2 * x2, axis=-1, keepdims=True)
    o_ref[0] = x2 * lax.rsqrt(ms + EPS) * fgain_ref[...]


def _out(x, mods, fm, z, gab, w4, wr, wo, gain2, w1, w2, fgain):
    b, l, d = x.shape
    tm = min(TOKEN_TILE, l)
    tok = lambda width: pl.BlockSpec((1, tm, width), lambda i, j: (i, j, 0))
    weights = (w4, wr, wo, w1, w2)
    return pl.pallas_call(
        functools.partial(_out_kernel, ff_chunk=min(1024, w1.shape[1])),
        grid=(b, l // tm),
        in_specs=[_const_spec(w.shape) for w in weights] + [
            tok(d),
            _const_spec(mods.shape),
            tok(fm.shape[2]), tok(z.shape[2]), tok(2 * d),
            _const_spec((1, d)), _const_spec((1, d)),
        ],
        out_specs=tok(d),
        out_shape=jax.ShapeDtypeStruct((b, l, d), F32),
        compiler_params=_params("arbitrary", "arbitrary"),
        name="out",
    )(*weights, x, mods, fm, z, gab, gain2, fgain)


def _dft_tables(l, gd):
    la = FFT_LA
    lb = l // la

    def cs(n):
        idx = np.arange(n)
        ang = 2.0 * np.pi * ((idx[:, None] * idx[None, :]) % n) / n
        return np.cos(ang), np.sin(ang)

    cb, sb = cs(lb)
    eye = np.eye(FFT_LA_TILE)
    half = lb // 2 + 1
    kw = np.concatenate([np.kron(cb[:half], eye), -np.kron(sb[:half], eye)], axis=0)
    ca, sa = cs(la)
    w2 = np.concatenate([ca, sa], axis=1)
    cc, sc = cs(gd)
    chan = np.block([[cc, -sc], [sc, cc]])
    tw = 2.0 * np.pi * (np.arange(lb)[:, None] * np.arange(la)[None, :]) / l
    tw = tw.reshape(lb, la // FFT_LA_TILE, FFT_LA_TILE).transpose(1, 0, 2).reshape(la // FFT_LA_TILE, -1)
    twc = np.repeat(np.cos(tw)[:, :, None], LANES, axis=2)
    tws = np.repeat(np.sin(tw)[:, :, None], LANES, axis=2)
    as_bf = lambda a: jnp.asarray(a, dtype=F32).astype(BF16)
    return as_bf(kw), as_bf(w2), as_bf(chan), jnp.asarray(twc, F32), jnp.asarray(tws, F32), la, lb


def _rope_tables(l):
    f32 = np.float32
    nf = RET_HEAD_DIM // 4
    inv = np.power(f32(ROPE_BASE), -np.arange(nf, dtype=f32) / f32(nf)).astype(f32)
    rows = l // GRID_W
    r, cc = np.meshgrid(np.arange(rows, dtype=f32), np.arange(GRID_W, dtype=f32), indexing="ij")
    ang = np.concatenate([r.reshape(-1)[:, None] * inv, cc.reshape(-1)[:, None] * inv], axis=-1).astype(f32)
    cos, sin = np.cos(ang).astype(f32), np.sin(ang).astype(f32)
    cos_h = np.concatenate([cos, cos], axis=1)
    sin_h = np.concatenate([-sin, sin], axis=1)
    reps = LANES // RET_HEAD_DIM
    return jnp.asarray(np.tile(cos_h, (1, reps))), jnp.asarray(np.tile(sin_h, (1, reps)))


def kernel(x, c, ctx, c_ctx, w_ada, b_ada, norm1_gain, w_in, four_w_out, ret_decay_logit,
           ret_gn_gain, ret_w_out, w_out, norm2_gain, w_mlp1, w_mlp2, final_gain):
    assert w_ada.shape[0] == 1, "single-layer block"
    b, l, d = x.shape
    lc = ctx.shape[1]
    fw = four_w_out.shape[1]
    rw = ret_w_out.shape[1]
    gd = fw // FOUR_GROUPS
    assert l % FFT_LA == 0 and l % RET_CHUNK == 0 and rw % GROUP_W == 0

    mods, kct, vc = _ada(c, c_ctx, w_ada[0], b_ada, ctx, norm1_gain, w_in[0], fw, rw)

    kw, w2, chan, twc, tws, la, lb = _dft_tables(l, gd)
    cosf, sinf = _rope_tables(l)

    later = [four_w_out[0], ret_w_out[0], w_out[0], w_mlp1[0], w_mlp2[0]]
    (u, q, kt, v, sg, gab), later_b = _proj(x, mods, norm1_gain, w_in[0], cosf, sinf, ret_gn_gain,
                                             later, fw, rw)

    fm = _fft(u, kw, twc, tws, w2, chan, la, lb, gd, 1.0 / math.sqrt(l * gd))

    log_gamma = jax.nn.log_sigmoid(ret_decay_logit[0].astype(F32))
    z = _retention(q, kt, v, sg, kct, vc, log_gamma)

    w4, wr, wo, w1, w2 = later_b
    return _out(x, mods, fm, z, gab, w4, wr, wo, norm2_gain, w1, w2, final_gain[None, :])
```

```python
import functools
import math

import jax
import jax.numpy as jnp
import numpy as np
from jax import lax
from jax.experimental import pallas as pl
from jax.experimental.pallas import tpu as pltpu

F32 = jnp.float32
BF16 = jnp.bfloat16

GRID_W = 64
FOUR_GROUPS = 4
RET_HEAD_DIM = 64
N_MOD = 6
ROPE_BASE = 10000.0
EPS = 1e-6

LANES = 128
MXU_DIM = 256
F32_SUBLANES = 8
BF16_SUBLANES = 16
VMEM_LIMIT_BYTES = 56 * 1024 * 1024

RET_CHUNK = 128
RET_UNROLL = 32
HEADS_PER_GROUP = MXU_DIM // RET_HEAD_DIM
GROUP_W = HEADS_PER_GROUP * RET_HEAD_DIM
STATE_BLOCK_W = LANES
FFT_LA = 128
FFT_LA_TILE = BF16_SUBLANES
FFT_SCATTER_PAD = F32_SUBLANES
TOKEN_TILE = 512
PROJ_TOKEN_TILE = 512
WEIGHT_STAGE_COLS = 512
WEIGHT_STAGE_SLOTS = 4


def _dot(a, b):
    return jnp.dot(a, b, preferred_element_type=F32)


def _norm_mod(x, gain, shift, scale):
    ms = jnp.mean(x * x, axis=-1, keepdims=True)
    y = x * lax.rsqrt(ms + EPS) * gain
    return y * (1.0 + scale) + shift


def _tile_lanes(t, reps):
    return jnp.concatenate([t] * reps, axis=1) if reps > 1 else t


def _const_spec(shape):
    nd = len(shape)
    return pl.BlockSpec(shape, lambda *_: (0,) * nd, pipeline_mode=pl.Buffered(1))


def _params(*sem):
    return pltpu.CompilerParams(dimension_semantics=sem, vmem_limit_bytes=VMEM_LIMIT_BYTES)


def _is_first_step(grid_rank):
    first = pl.program_id(0) == 0
    for axis in range(1, grid_rank):
        first = first & (pl.program_id(axis) == 0)
    return first


def _proj_prepare(c_ref, cctx_ref, wada_hbm, bada_ref, w_hbm, ctx_ref, gain_ref,
                  mods_ref, kct_ref, vc_ref, w_ref, mod_ref, cv_ref, stage_ref, sem_ref, *, fw, rw, scale):
    nb = c_ref.shape[0]
    d = w_ref.shape[0]
    cv_ref[...] = jnp.zeros_like(cv_ref)
    cv_ref[0:nb, :] = c_ref[...]
    cv_ref[nb:nb + 1, :] = cctx_ref[...]
    cv = cv_ref[...]
    s = (cv * jax.nn.sigmoid(cv)).astype(BF16)

    slots, _, sc = stage_ref.shape
    chunks = ([(wada_hbm, c0) for c0 in range(0, wada_hbm.shape[1], sc)]
              + [(w_hbm, c0) for c0 in range(0, w_hbm.shape[1], sc)])

    def copy(k):
        src, c0 = chunks[k]
        slot = k % slots
        return pltpu.make_async_copy(src.at[:, pl.ds(c0, sc)], stage_ref.at[slot], sem_ref.at[slot])

    for k in range(min(slots - 1, len(chunks))):
        copy(k).start()
    for k, (src, c0) in enumerate(chunks):
        if k + slots - 1 < len(chunks):
            copy(k + slots - 1).start()
        copy(k).wait()
        blk = stage_ref[k % slots].astype(BF16)
        if src is wada_hbm:
            j, off = divmod(c0, d)
            mod_ref[j, :, off:off + sc] = _dot(s, blk) + bada_ref[:, c0:c0 + sc]
        else:
            w_ref[:, c0:c0 + sc] = blk
    mods_ref[...] = mod_ref[...]

    for i in range(nb):
        h = _norm_mod(ctx_ref[i], gain_ref[...], mod_ref[0, nb:nb + 1, :], mod_ref[1, nb:nb + 1, :]).astype(BF16)
        kct_ref[i] = (_dot(h, w_ref[:, fw + rw:fw + 2 * rw]) * scale).T.astype(BF16)
        vc_ref[i] = _dot(h, w_ref[:, fw + 2 * rw:fw + 3 * rw]).astype(BF16)


def _proj_kernel(*refs, fw, rw, d, scale, ncast):
    x_ref, cos_ref, sin_ref, w_hbm, wada_hbm = refs[:5]
    cast_in = refs[5:5 + ncast]
    gain_ref, gng_ref, c_ref, cctx_ref, bada_ref, ctx_ref = refs[5 + ncast:11 + ncast]
    outs = refs[11 + ncast:]
    u_ref, q_ref, kt_ref, v_ref, sg_ref, gab_ref = outs[:6]
    cast_out = outs[6:6 + ncast]
    mods_ref, kct_ref, vc_ref = outs[6 + ncast:9 + ncast]
    w_ref, mod_ref, cv_ref, stage_ref, sem_ref = outs[9 + ncast:]

    @pl.when(_is_first_step(2))
    def _():
        _proj_prepare(c_ref, cctx_ref, wada_hbm, bada_ref, w_hbm, ctx_ref, gain_ref,
                      mods_ref, kct_ref, vc_ref, w_ref, mod_ref, cv_ref, stage_ref, sem_ref,
                      fw=fw, rw=rw, scale=scale)

    for src, dst in zip(cast_in, cast_out):
        dst[...] = src[...].astype(BF16)

    mod = mod_ref[:, pl.ds(pl.program_id(0), 1), :]
    h = _norm_mod(x_ref[0], gain_ref[...], mod[0], mod[1]).astype(BF16)
    tm = h.shape[0]

    def proj(lo, width):
        return _dot(h, w_ref[:, lo:lo + width])

    u_ref[0] = proj(0, fw).astype(BF16)

    reps = rw // cos_ref.shape[1]
    cosf = _tile_lanes(cos_ref[...], reps)
    sinf = _tile_lanes(sin_ref[...], reps)
    lane = lax.broadcasted_iota(jnp.int32, (tm, rw), 1)
    first_half = (lane & (RET_HEAD_DIM // 2)) == 0

    def rope(t):
        rot = jnp.where(first_half,
                        pltpu.roll(t, rw - RET_HEAD_DIM // 2, 1),
                        pltpu.roll(t, RET_HEAD_DIM // 2, 1))
        return t * cosf + rot * sinf

    q_ref[0] = rope(proj(fw, rw)).astype(BF16)
    kt_ref[0] = (rope(proj(fw + rw, rw)) * scale).T.astype(BF16)
    v_ref[0] = proj(fw + 2 * rw, rw).astype(BF16)
    g = proj(fw + 3 * rw, rw)
    sg_ref[0] = (g * jax.nn.sigmoid(g) * gng_ref[...]).astype(BF16)
    gab_ref[0, :, :d] = proj(fw + 4 * rw, d).astype(BF16)
    gab_ref[0, :, d:] = proj(fw + 4 * rw + d, d).astype(BF16)


def _proj(x, c, c_ctx, w_ada, b_ada, ctx, gain, w_in, cosf, sinf, gn_gain, later_weights, fw, rw):
    b, l, d = x.shape
    lc = ctx.shape[1]
    tm = min(PROJ_TOKEN_TILE, l)
    nj = l // tm
    steps = b * nj
    rows = -(-(b + 1) // F32_SUBLANES) * F32_SUBLANES
    sc = WEIGHT_STAGE_COLS
    assert w_ada.shape == (d, N_MOD * d) and d % sc == 0 and w_in.shape[1] % sc == 0
    tok = lambda width: pl.BlockSpec((1, tm, width), lambda i, j: (i, j, 0))
    out = lambda width: jax.ShapeDtypeStruct((b, l, width), BF16)
    sliced = [w.reshape(steps, w.shape[0] // steps, w.shape[1]) for w in later_weights]
    for w in sliced:
        assert w.shape[1] % BF16_SUBLANES == 0
    cast_spec = lambda w: pl.BlockSpec((1,) + w.shape[1:], lambda i, j: (i * nj + j, 0, 0))
    whole = lambda shape: pl.BlockSpec(shape, lambda i, j: (0,) * len(shape))
    res = pl.pallas_call(
        functools.partial(_proj_kernel, fw=fw, rw=rw, d=d, scale=RET_HEAD_DIM ** -0.5, ncast=len(sliced)),
        grid=(b, nj),
        in_specs=[
            tok(d),
            pl.BlockSpec((tm, cosf.shape[1]), lambda i, j: (j, 0)),
            pl.BlockSpec((tm, sinf.shape[1]), lambda i, j: (j, 0)),
            pl.BlockSpec(memory_space=pl.ANY),
            pl.BlockSpec(memory_space=pl.ANY),
        ] + [cast_spec(w) for w in sliced] + [
            _const_spec((1, d)), _const_spec((1, rw)), _const_spec((b, d)), _const_spec((1, d)),
            _const_spec((1, N_MOD * d)), _const_spec((b, lc, d)),
        ],
        out_specs=[tok(fw), tok(rw), pl.BlockSpec((1, rw, tm), lambda i, j: (i, 0, j)),
                   tok(rw), tok(rw), tok(2 * d)] + [cast_spec(w) for w in sliced]
        + [whole((N_MOD, rows, d)), whole((b, rw, lc)), whole((b, lc, rw))],
        out_shape=[out(fw), out(rw), jax.ShapeDtypeStruct((b, rw, l), BF16),
                   out(rw), out(rw), out(2 * d)]
        + [jax.ShapeDtypeStruct(w.shape, BF16) for w in sliced]
        + [jax.ShapeDtypeStruct((N_MOD, rows, d), F32),
           jax.ShapeDtypeStruct((b, rw, lc), BF16), jax.ShapeDtypeStruct((b, lc, rw), BF16)],
        scratch_shapes=[pltpu.VMEM(w_in.shape, BF16),
                        pltpu.VMEM((N_MOD, rows, d), F32),
                        pltpu.VMEM((rows, d), F32),
                        pltpu.VMEM((WEIGHT_STAGE_SLOTS, d, sc), F32),
                        pltpu.SemaphoreType.DMA((WEIGHT_STAGE_SLOTS,))],
        compiler_params=_params("arbitrary", "arbitrary"),
        name="proj",
    )(x, cosf, sinf, w_in, w_ada, *sliced, gain, gn_gain, c, c_ctx[None, :], b_ada, ctx)
    ncast = len(sliced)
    casts = [cc.reshape(w.shape) for cc, w in zip(res[6:6 + ncast], later_weights)]
    mods, kct, vc = res[6 + ncast:]
    return res[:6], casts, mods, kct, vc


def _fft_kernel(u_ref, kw_ref, twc_ref, tws_ref, w2_ref, cs_ref, o_ref, tr_ref, ti_ref, scr_ref,
                *, la, lb, gd, npb, scale):
    r = FFT_LA_TILE
    rows = lb * r
    fw = u_ref.shape[2]
    ngroups = fw // gd
    reps = fw // twc_ref.shape[2]
    pitch = scr_ref.shape[1] // la

    @pl.when(pl.program_id(0) == 0)
    def _():
        scr_ref[...] = jnp.zeros_like(scr_ref)

    for j in range(la // r):
        u = jnp.concatenate([u_ref[0, k * la + j * r:k * la + (j + 1) * r, :] for k in range(lb)], axis=0)
        half = lb // 2 + 1
        t = _dot(kw_ref[...], u)
        mirror = [lb - k for k in range(half, lb)]
        a = jnp.concatenate([t[:half * r]] + [t[k * r:(k + 1) * r] for k in mirror], axis=0)
        b = jnp.concatenate([t[half * r:]] + [-t[(half + k) * r:(half + k + 1) * r] for k in mirror], axis=0)
        ct = _tile_lanes(twc_ref[j], reps)
        st = _tile_lanes(tws_ref[j], reps)
        tr = (a * ct + b * st).astype(BF16)
        ti = (b * ct - a * st).astype(BF16)
        for k in range(lb):
            tr_ref[k * la + j * r:k * la + (j + 1) * r, :] = tr[k * r:(k + 1) * r]
            ti_ref[k * la + j * r:k * la + (j + 1) * r, :] = ti[k * r:(k + 1) * r]

    for pb in range(lb // npb):
        xs = []
        for p in range(npb):
            lo = (pb * npb + p) * la
            t = jnp.concatenate([tr_ref[lo:lo + la, :], ti_ref[lo:lo + la, :]], axis=0)
            xs.append(_dot(w2_ref[...], t).astype(BF16))
        for gi in range(ngroups):
            cols = slice(gi * gd, (gi + 1) * gd)
            lhs = jnp.concatenate([jnp.concatenate([x[:la, cols], x[la:, cols]], axis=1) for x in xs], axis=0)
            fg = _dot(lhs, cs_ref[...]) * scale
            for p in range(npb):
                scr_ref[gi, pl.ds(p, la, stride=pitch), :] = fg[p * la:(p + 1) * la]
        full = jnp.concatenate([scr_ref[gi].reshape(la, pitch, gd)[:, :npb, :] for gi in range(ngroups)], axis=2)
        o_ref[0, :, pb * npb:(pb + 1) * npb, :] = full.astype(BF16)


def _fft(u, kw, twc, tws, w2, cs, la, lb, gd, scale):
    b, l, fw = u.shape
    npb = min(BF16_SUBLANES, lb)
    out = pl.pallas_call(
        functools.partial(_fft_kernel, la=la, lb=lb, gd=gd, npb=npb, scale=scale),
        grid=(b,),
        in_specs=[pl.BlockSpec((1, l, fw), lambda i: (i, 0, 0)),
                  _const_spec(kw.shape), _const_spec(twc.shape), _const_spec(tws.shape),
                  _const_spec(w2.shape), _const_spec(cs.shape)],
        out_specs=pl.BlockSpec((1, la, lb, fw), lambda i: (i, 0, 0, 0)),
        out_shape=jax.ShapeDtypeStruct((b, la, lb, fw), BF16),
        scratch_shapes=[pltpu.VMEM((l, fw), BF16), pltpu.VMEM((l, fw), BF16),
                        pltpu.VMEM((fw // gd, la * (npb + FFT_SCATTER_PAD), gd), F32)],
        compiler_params=_params("arbitrary"),
        name="fft",
    )(u, kw, twc, tws, w2, cs)
    return out.reshape(b, l, fw)


def _ret_kernel(q_ref, kt_ref, v_ref, sg_ref, kct_ref, vc_ref, lgl_ref, lgc_ref,
                z_ref, sf_ref, sb_ref, stf_ref, stb_ref, p_ref, o_ref,
                dall_ref, qdf_ref, qdb_ref, kdf_ref, kdb_ref, *, nchunk, unroll):
    c = RET_CHUNK
    gw = GROUP_W
    lc = vc_ref.shape[1]

    lgf, lgb = lgl_ref[0:1, :], lgl_ref[1:2, :]
    pos = lax.broadcasted_iota(jnp.int32, (c, gw), 0).astype(F32)
    qdf_ref[...] = jnp.exp(lgf * (pos + 1.0))
    qdb_ref[...] = jnp.exp(lgb * (c - pos))
    kdf_ref[...] = jnp.exp(lgf * (c - 1.0 - pos))
    kdb_ref[...] = jnp.exp(lgb * pos)
    cdf = jnp.exp(lgf * c)
    cdb = jnp.exp(lgb * c)
    cpos = lax.broadcasted_iota(jnp.int32, (lc, gw), 0).astype(F32)
    wcf = jnp.exp(lgf * (lc - 1.0 - cpos))
    wcb = jnp.exp(lgb * cpos)
    si = lax.broadcasted_iota(jnp.int32, (c, HEADS_PER_GROUP * c), 0)
    sj = lax.broadcasted_iota(jnp.int32, (c, HEADS_PER_GROUP * c), 1) & (c - 1)
    diff = (si - sj).astype(F32)
    dall_ref[...] = (jnp.where(diff >= 0, jnp.exp(lgc_ref[0:1, :] * jnp.maximum(diff, 0.0)), 0.0)
                     + jnp.where(diff <= 0, jnp.exp(lgc_ref[1:2, :] * jnp.maximum(-diff, 0.0)), 0.0))

    pw = STATE_BLOCK_W
    npair = gw // pw
    same_head_p = (lax.broadcasted_iota(jnp.int32, (pw, pw), 0) // RET_HEAD_DIM
                   == lax.broadcasted_iota(jnp.int32, (pw, pw), 1) // RET_HEAD_DIM)

    def diag_blocks(t):
        return jnp.where(same_head_p, t, 0.0)

    def weighted(t, w):
        return (t.astype(F32) * w).astype(BF16)

    def rows(ref, n):
        return ref[0, pl.ds(pl.multiple_of(n * c, c), c), :]

    def kt_chunk(n):
        return kt_ref[0, :, pl.ds(pl.multiple_of(n * c, c), c)]

    def state_update(st_ref, kt, vw, cd):
        for p in range(npair):
            cols = slice(p * pw, (p + 1) * pw)
            st_ref[p] = st_ref[p] * cd[:, cols] + diag_blocks(_dot(kt[cols, :], vw[:, cols]))

    def full_state(s_ref, n):
        zero = jnp.zeros((pw, pw), BF16)
        return jnp.concatenate(
            [jnp.concatenate([s_ref[n, p] if q == p else zero for q in range(npair)], axis=1)
             for p in range(npair)], axis=0)

    kct = kct_ref[0]
    vc = vc_ref[0]
    stf_ref[...] = jnp.zeros_like(stf_ref)
    stb_ref[...] = jnp.zeros_like(stb_ref)
    state_update(stf_ref, kct, weighted(vc, wcf), cdf)
    state_update(stb_ref, kct, weighted(vc, wcb), cdb)

    def scan_body(i, carry):
        nf = i
        nb = nchunk - 1 - i
        sf_ref[nf] = stf_ref[...].astype(BF16)
        state_update(stf_ref, kt_chunk(nf), weighted(rows(v_ref, nf), kdf_ref[...]), cdf)
        sb_ref[nb] = stb_ref[...].astype(BF16)
        state_update(stb_ref, kt_chunk(nb), weighted(rows(v_ref, nb), kdb_ref[...]), cdb)
        return carry

    lax.fori_loop(0, nchunk, scan_body, 0, unroll=unroll)

    lane = lax.broadcasted_iota(jnp.int32, (c, gw), 1)
    sub = lax.broadcasted_iota(jnp.int32, (gw, c), 0)
    lane_masks = [(lane >= h * RET_HEAD_DIM) & (lane < (h + 1) * RET_HEAD_DIM)
                  for h in range(HEADS_PER_GROUP)]
    sub_masks = [(sub >= h * RET_HEAD_DIM) & (sub < (h + 1) * RET_HEAD_DIM)
                 for h in range(HEADS_PER_GROUP)]
    same_head = (lax.broadcasted_iota(jnp.int32, (gw, gw), 0) // RET_HEAD_DIM
                 == lax.broadcasted_iota(jnp.int32, (gw, gw), 1) // RET_HEAD_DIM)
    bd_mean = jnp.where(same_head, 1.0 / RET_HEAD_DIM, 0.0).astype(BF16)

    def score_body(n, carry):
        ktn = kt_chunk(n)
        zk = jnp.zeros_like(ktn)
        kbd = jnp.concatenate([jnp.where(m, ktn, zk) for m in sub_masks], axis=1)
        p_ref[n] = (_dot(rows(q_ref, n), kbd) * dall_ref[...]).astype(BF16)
        return carry

    lax.fori_loop(0, nchunk, score_body, 0, unroll=unroll)

    def mix_body(n, carry):
        qn = rows(q_ref, n)
        vn = rows(v_ref, n)
        zv = jnp.zeros_like(vn)
        vbd = jnp.concatenate([jnp.where(m, vn, zv) for m in lane_masks], axis=0)
        o_ref[n] = (_dot(p_ref[n], vbd) + qdf_ref[...] * _dot(qn, full_state(sf_ref, n))
                    + qdb_ref[...] * _dot(qn, full_state(sb_ref, n)))
        return carry

    lax.fori_loop(0, nchunk, mix_body, 0, unroll=unroll)

    def norm_body(n, carry):
        o = o_ref[n]
        ms = _dot((o * o).astype(BF16), bd_mean)
        z = rows(sg_ref, n).astype(F32) * (o * lax.rsqrt(ms + EPS))
        z_ref[0, pl.ds(pl.multiple_of(n * c, c), c), :] = z.astype(BF16)
        return carry

    lax.fori_loop(0, nchunk, norm_body, 0, unroll=unroll)


def _retention(q, kt, v, sg, kct, vc, log_gamma):
    b, l, rw = q.shape
    lc = vc.shape[1]
    c = RET_CHUNK
    gw = GROUP_W
    hpg = HEADS_PER_GROUP
    ng = rw // gw
    nchunk = l // c
    pw = STATE_BLOCK_W
    npair = gw // pw
    assert c & (c - 1) == 0
    lg = log_gamma.reshape(2, ng, hpg).transpose(1, 0, 2)
    lg_lane = jnp.repeat(lg, RET_HEAD_DIM, axis=2)
    lg_col = jnp.repeat(lg, c, axis=2)
    tok = pl.BlockSpec((1, l, gw), lambda i, j: (i, 0, j))
    tokt = pl.BlockSpec((1, gw, l), lambda i, j: (i, j, 0))
    grp = lambda r, width: pl.BlockSpec((None, r, width), lambda i, j: (j, 0, 0))
    return pl.pallas_call(
        functools.partial(_ret_kernel, nchunk=nchunk, unroll=math.gcd(nchunk, RET_UNROLL)),
        grid=(b, ng),
        in_specs=[
            tok, tokt, tok, tok,
            pl.BlockSpec((1, gw, lc), lambda i, j: (i, j, 0)),
            pl.BlockSpec((1, lc, gw), lambda i, j: (i, 0, j)),
            grp(2, gw), grp(2, hpg * c),
        ],
        out_specs=tok,
        out_shape=jax.ShapeDtypeStruct((b, l, rw), BF16),
        scratch_shapes=[pltpu.VMEM((nchunk, npair, pw, pw), BF16), pltpu.VMEM((nchunk, npair, pw, pw), BF16),
                        pltpu.VMEM((npair, pw, pw), F32), pltpu.VMEM((npair, pw, pw), F32),
                        pltpu.VMEM((nchunk, c, hpg * c), BF16),
                        pltpu.VMEM((nchunk, c, gw), F32),
                        pltpu.VMEM((c, hpg * c), F32)] + [pltpu.VMEM((c, gw), F32)] * 4,
        compiler_params=_params("arbitrary", "arbitrary"),
        name="ret",
    )(q, kt, v, sg, kct, vc, lg_lane, lg_col)


def _out_kernel(w4_ref, wr_ref, wo_ref, w1_ref, w2_ref, x_ref, mod_ref, fm_ref, z_ref, gab_ref,
                gain2_ref, fgain_ref, o_ref, *, ff_chunk):
    mod = mod_ref[:, pl.ds(pl.program_id(0), 1), :]
    g1, sh2, sc2, g2 = mod[2], mod[3], mod[4], mod[5]
    y_four = _dot(fm_ref[0], w4_ref[...])
    y_ret = _dot(z_ref[0], wr_ref[...])
    d = x_ref.shape[2]
    y = (jax.nn.sigmoid(gab_ref[0, :, :d].astype(F32)) * y_four
         + jax.nn.sigmoid(gab_ref[0, :, d:].astype(F32)) * y_ret)
    x1 = x_ref[0] + g1 * _dot(y.astype(BF16), wo_ref[...])
    h2 = _norm_mod(x1, gain2_ref[...], sh2, sc2).astype(BF16)
    dff = w1_ref.shape[1]
    acc = None
    for lo in range(0, dff, ff_chunk):
        hid = jnp.maximum(_dot(h2, w1_ref[:, lo:lo + ff_chunk]), 0.0)
        part = _dot((hid * hid).astype(BF16), w2_ref[lo:lo + ff_chunk, :])
        acc = part if acc is None else acc + part
    x2 = x1 + g2 * acc
    ms = jnp.mean(x2 * x2, axis=-1, keepdims=True)
    o_ref[0] = x2 * lax.rsqrt(ms + EPS) * fgain_ref[...]


def _out(x, mods, fm, z, gab, w4, wr, wo, gain2, w1, w2, fgain):
    b, l, d = x.shape
    tm = min(TOKEN_TILE, l)
    tok = lambda width: pl.BlockSpec((1, tm, width), lambda i, j: (i, j, 0))
    weights = (w4, wr, wo, w1, w2)
    return pl.pallas_call(
        functools.partial(_out_kernel, ff_chunk=min(1024, w1.shape[1])),
        grid=(b, l // tm),
        in_specs=[_const_spec(w.shape) for w in weights] + [
            tok(d),
            _const_spec(mods.shape),
            tok(fm.shape[2]), tok(z.shape[2]), tok(2 * d),
            _const_spec((1, d)), _const_spec((1, d)),
        ],
        out_specs=tok(d),
        out_shape=jax.ShapeDtypeStruct((b, l, d), F32),
        compiler_params=_params("arbitrary", "arbitrary"),
        name="out",
    )(*weights, x, mods, fm, z, gab, gain2, fgain)


def _dft_tables(l, gd):
    la = FFT_LA
    lb = l // la

    def cs(n):
        idx = np.arange(n)
        ang = 2.0 * np.pi * ((idx[:, None] * idx[None, :]) % n) / n
        return np.cos(ang), np.sin(ang)

    cb, sb = cs(lb)
    eye = np.eye(FFT_LA_TILE)
    half = lb // 2 + 1
    kw = np.concatenate([np.kron(cb[:half], eye), -np.kron(sb[:half], eye)], axis=0)
    ca, sa = cs(la)
    w2 = np.block([[ca, sa], [-sa, ca]])
    cc, sc = cs(gd)
    chan = np.concatenate([cc, sc], axis=0)
    tw = 2.0 * np.pi * (np.arange(lb)[:, None] * np.arange(la)[None, :]) / l
    tw = tw.reshape(lb, la // FFT_LA_TILE, FFT_LA_TILE).transpose(1, 0, 2).reshape(la // FFT_LA_TILE, -1)
    twc = np.repeat(np.cos(tw)[:, :, None], LANES, axis=2)
    tws = np.repeat(np.sin(tw)[:, :, None], LANES, axis=2)
    as_bf = lambda a: jnp.asarray(a, dtype=F32).astype(BF16)
    return as_bf(kw), as_bf(w2), as_bf(chan), jnp.asarray(twc, F32), jnp.asarray(tws, F32), la, lb


def _rope_tables(l):
    f32 = np.float32
    nf = RET_HEAD_DIM // 4
    inv = np.power(f32(ROPE_BASE), -np.arange(nf, dtype=f32) / f32(nf)).astype(f32)
    rows = l // GRID_W
    r, cc = np.meshgrid(np.arange(rows, dtype=f32), np.arange(GRID_W, dtype=f32), indexing="ij")
    ang = np.concatenate([r.reshape(-1)[:, None] * inv, cc.reshape(-1)[:, None] * inv], axis=-1).astype(f32)
    cos, sin = np.cos(ang).astype(f32), np.sin(ang).astype(f32)
    cos_h = np.concatenate([cos, cos], axis=1)
    sin_h = np.concatenate([-sin, sin], axis=1)
    reps = LANES // RET_HEAD_DIM
    return jnp.asarray(np.tile(cos_h, (1, reps))), jnp.asarray(np.tile(sin_h, (1, reps)))


def kernel(x, c, ctx, c_ctx, w_ada, b_ada, norm1_gain, w_in, four_w_out, ret_decay_logit,
           ret_gn_gain, ret_w_out, w_out, norm2_gain, w_mlp1, w_mlp2, final_gain):
    assert w_ada.shape[0] == 1, "single-layer block"
    b, l, d = x.shape
    lc = ctx.shape[1]
    fw = four_w_out.shape[1]
    rw = ret_w_out.shape[1]
    gd = fw // FOUR_GROUPS
    assert l % FFT_LA == 0 and l % RET_CHUNK == 0 and rw % GROUP_W == 0

    kw, w2, chan, twc, tws, la, lb = _dft_tables(l, gd)
    cosf, sinf = _rope_tables(l)

    later = [four_w_out[0], ret_w_out[0], w_out[0], w_mlp1[0], w_mlp2[0]]
    (u, q, kt, v, sg, gab), later_b, mods, kct, vc = _proj(
        x, c, c_ctx, w_ada[0], b_ada, ctx, norm1_gain, w_in[0], cosf, sinf, ret_gn_gain, later, fw, rw)

    fm = _fft(u, kw, twc, tws, w2, chan, la, lb, gd, 1.0 / math.sqrt(l * gd))

    log_gamma = jax.nn.log_sigmoid(ret_decay_logit[0].astype(F32))
    z = _retention(q, kt, v, sg, kct, vc, log_gamma)

    w4, wr, wo, w1, w2 = later_b
    return _out(x, mods, fm, z, gab, w4, wr, wo, norm2_gain, w1, w2, final_gain[None, :])
```

```python
import functools
import math

import jax
import jax.numpy as jnp
import numpy as np
from jax import lax
from jax.experimental import pallas as pl
from jax.experimental.pallas import tpu as pltpu

F32 = jnp.float32
BF16 = jnp.bfloat16

GRID_W = 64
FOUR_GROUPS = 4
RET_HEAD_DIM = 64
N_MOD = 6
ROPE_BASE = 10000.0
EPS = 1e-6

LANES = 128
MXU_DIM = 256
F32_SUBLANES = 8
BF16_SUBLANES = 16
VMEM_LIMIT_BYTES = 56 * 1024 * 1024

RET_CHUNK = 128
RET_UNROLL = 32
HEADS_PER_GROUP = MXU_DIM // RET_HEAD_DIM
GROUP_W = HEADS_PER_GROUP * RET_HEAD_DIM
STATE_BLOCK_W = LANES
FFT_LA = 128
FFT_LA_TILE = BF16_SUBLANES
FFT_SCATTER_PAD = F32_SUBLANES
TOKEN_TILE = 512
PROJ_TOKEN_TILE = 512
WEIGHT_STAGE_COLS = 512
WEIGHT_STAGE_SLOTS = 6


def _dot(a, b):
    return jnp.dot(a, b, preferred_element_type=F32)


def _norm_mod(x, gain, shift, scale):
    ms = jnp.mean(x * x, axis=-1, keepdims=True)
    y = x * lax.rsqrt(ms + EPS) * gain
    return y * (1.0 + scale) + shift


def _tile_lanes(t, reps):
    return jnp.concatenate([t] * reps, axis=1) if reps > 1 else t


def _const_spec(shape):
    nd = len(shape)
    return pl.BlockSpec(shape, lambda *_: (0,) * nd, pipeline_mode=pl.Buffered(1))


def _params(*sem):
    return pltpu.CompilerParams(dimension_semantics=sem, vmem_limit_bytes=VMEM_LIMIT_BYTES)


def _is_first_step(grid_rank):
    first = pl.program_id(0) == 0
    for axis in range(1, grid_rank):
        first = first & (pl.program_id(axis) == 0)
    return first


def _proj_prepare(c_ref, cctx_ref, wada_hbm, bada_ref, w_hbm, ctx_ref, gain_ref,
                  mods_ref, kct_ref, vc_ref, w_ref, mod_ref, cv_ref, stage_ref, sem_ref, *, fw, rw, scale):
    nb = c_ref.shape[0]
    d = w_ref.shape[0]
    cv_ref[...] = jnp.zeros_like(cv_ref)
    cv_ref[0:nb, :] = c_ref[...]
    cv_ref[nb:nb + 1, :] = cctx_ref[...]
    cv = cv_ref[...]
    s = (cv * jax.nn.sigmoid(cv)).astype(BF16)

    slots, _, sc = stage_ref.shape
    ada = [(wada_hbm, c0) for c0 in range(0, wada_hbm.shape[1], sc)]
    win = [(w_hbm, c0) for c0 in range(0, w_hbm.shape[1], sc)]
    ctx_mods = 2 * d // sc
    chunks = ada[:ctx_mods] + win + ada[ctx_mods:]
    ctx_after = ctx_mods + len(win) - 1

    def copy(k):
        src, c0 = chunks[k]
        slot = k % slots
        return pltpu.make_async_copy(src.at[:, pl.ds(c0, sc)], stage_ref.at[slot], sem_ref.at[slot])

    for k in range(min(slots - 1, len(chunks))):
        copy(k).start()
    for k, (src, c0) in enumerate(chunks):
        if k + slots - 1 < len(chunks):
            copy(k + slots - 1).start()
        copy(k).wait()
        blk = stage_ref[k % slots].astype(BF16)
        if src is wada_hbm:
            j, off = divmod(c0, d)
            hw = sc // 2
            mod_ref[j, :, off:off + hw] = _dot(s, blk[:, :hw]) + bada_ref[:, c0:c0 + hw]
            mod_ref[j, :, off + hw:off + sc] = _dot(s, blk[:, hw:]) + bada_ref[:, c0 + hw:c0 + sc]
        else:
            w_ref[:, c0:c0 + sc] = blk
        if k == ctx_after:
            for i in range(nb):
                h = _norm_mod(ctx_ref[i], gain_ref[...],
                              mod_ref[0, nb:nb + 1, :], mod_ref[1, nb:nb + 1, :]).astype(BF16)
                kct_ref[i] = (_dot(h, w_ref[:, fw + rw:fw + 2 * rw]) * scale).T.astype(BF16)
                vc_ref[i] = _dot(h, w_ref[:, fw + 2 * rw:fw + 3 * rw]).astype(BF16)
    mods_ref[...] = mod_ref[...]


def _proj_kernel(*refs, fw, rw, d, scale, ncast):
    x_ref, cos_ref, sin_ref, w_hbm, wada_hbm = refs[:5]
    cast_in = refs[5:5 + ncast]
    gain_ref, gng_ref, c_ref, cctx_ref, bada_ref, ctx_ref = refs[5 + ncast:11 + ncast]
    outs = refs[11 + ncast:]
    u_ref, q_ref, kt_ref, v_ref, sg_ref, gab_ref = outs[:6]
    cast_out = outs[6:6 + ncast]
    mods_ref, kct_ref, vc_ref = outs[6 + ncast:9 + ncast]
    w_ref, mod_ref, cv_ref, stage_ref, sem_ref = outs[9 + ncast:]

    @pl.when(_is_first_step(2))
    def _():
        _proj_prepare(c_ref, cctx_ref, wada_hbm, bada_ref, w_hbm, ctx_ref, gain_ref,
                      mods_ref, kct_ref, vc_ref, w_ref, mod_ref, cv_ref, stage_ref, sem_ref,
                      fw=fw, rw=rw, scale=scale)

    for src, dst in zip(cast_in, cast_out):
        dst[...] = src[...].astype(BF16)

    mod = mod_ref[:, pl.ds(pl.program_id(0), 1), :]
    h = _norm_mod(x_ref[0], gain_ref[...], mod[0], mod[1]).astype(BF16)
    tm = h.shape[0]

    def proj(lo, width):
        return _dot(h, w_ref[:, lo:lo + width])

    u_ref[0] = proj(0, fw).astype(BF16)

    reps = rw // cos_ref.shape[1]
    cosf = _tile_lanes(cos_ref[...], reps)
    sinf = _tile_lanes(sin_ref[...], reps)
    lane = lax.broadcasted_iota(jnp.int32, (tm, rw), 1)
    first_half = (lane & (RET_HEAD_DIM // 2)) == 0

    def rope(t):
        rot = jnp.where(first_half,
                        pltpu.roll(t, rw - RET_HEAD_DIM // 2, 1),
                        pltpu.roll(t, RET_HEAD_DIM // 2, 1))
        return t * cosf + rot * sinf

    q_ref[0] = rope(proj(fw, rw)).astype(BF16)
    kt_ref[0] = (rope(proj(fw + rw, rw)) * scale).T.astype(BF16)
    v_ref[0] = proj(fw + 2 * rw, rw).astype(BF16)
    g = proj(fw + 3 * rw, rw)
    sg_ref[0] = (g * jax.nn.sigmoid(g) * gng_ref[...]).astype(BF16)
    gab_ref[0, :, :d] = proj(fw + 4 * rw, d).astype(BF16)
    gab_ref[0, :, d:] = proj(fw + 4 * rw + d, d).astype(BF16)


def _proj(x, c, c_ctx, w_ada, b_ada, ctx, gain, w_in, cosf, sinf, gn_gain, later_weights, fw, rw):
    b, l, d = x.shape
    lc = ctx.shape[1]
    tm = min(PROJ_TOKEN_TILE, l)
    nj = l // tm
    steps = b * nj
    rows = -(-(b + 1) // F32_SUBLANES) * F32_SUBLANES
    sc = WEIGHT_STAGE_COLS
    assert w_ada.shape == (d, N_MOD * d) and d % sc == 0 and w_in.shape[1] % sc == 0
    tok = lambda width: pl.BlockSpec((1, tm, width), lambda i, j: (i, j, 0))
    out = lambda width: jax.ShapeDtypeStruct((b, l, width), BF16)
    sliced = [w.reshape(steps, w.shape[0] // steps, w.shape[1]) for w in later_weights]
    for w in sliced:
        assert w.shape[1] % BF16_SUBLANES == 0
    cast_spec = lambda w: pl.BlockSpec((1,) + w.shape[1:], lambda i, j: (i * nj + j, 0, 0))
    whole = lambda shape: pl.BlockSpec(shape, lambda i, j: (0,) * len(shape))
    res = pl.pallas_call(
        functools.partial(_proj_kernel, fw=fw, rw=rw, d=d, scale=RET_HEAD_DIM ** -0.5, ncast=len(sliced)),
        grid=(b, nj),
        in_specs=[
            tok(d),
            pl.BlockSpec((tm, cosf.shape[1]), lambda i, j: (j, 0)),
            pl.BlockSpec((tm, sinf.shape[1]), lambda i, j: (j, 0)),
            pl.BlockSpec(memory_space=pl.ANY),
            pl.BlockSpec(memory_space=pl.ANY),
        ] + [cast_spec(w) for w in sliced] + [
            _const_spec((1, d)), _const_spec((1, rw)), _const_spec((b, d)), _const_spec((1, d)),
            _const_spec((1, N_MOD * d)), _const_spec((b, lc, d)),
        ],
        out_specs=[tok(fw), tok(rw), pl.BlockSpec((1, rw, tm), lambda i, j: (i, 0, j)),
                   tok(rw), tok(rw), tok(2 * d)] + [cast_spec(w) for w in sliced]
        + [whole((N_MOD, rows, d)), whole((b, rw, lc)), whole((b, lc, rw))],
        out_shape=[out(fw), out(rw), jax.ShapeDtypeStruct((b, rw, l), BF16),
                   out(rw), out(rw), out(2 * d)]
        + [jax.ShapeDtypeStruct(w.shape, BF16) for w in sliced]
        + [jax.ShapeDtypeStruct((N_MOD, rows, d), F32),
           jax.ShapeDtypeStruct((b, rw, lc), BF16), jax.ShapeDtypeStruct((b, lc, rw), BF16)],
        scratch_shapes=[pltpu.VMEM(w_in.shape, BF16),
                        pltpu.VMEM((N_MOD, rows, d), F32),
                        pltpu.VMEM((rows, d), F32),
                        pltpu.VMEM((WEIGHT_STAGE_SLOTS, d, sc), F32),
                        pltpu.SemaphoreType.DMA((WEIGHT_STAGE_SLOTS,))],
        compiler_params=_params("arbitrary", "arbitrary"),
        name="proj",
    )(x, cosf, sinf, w_in, w_ada, *sliced, gain, gn_gain, c, c_ctx[None, :], b_ada, ctx)
    ncast = len(sliced)
    casts = [cc.reshape(w.shape) for cc, w in zip(res[6:6 + ncast], later_weights)]
    mods, kct, vc = res[6 + ncast:]
    return res[:6], casts, mods, kct, vc


def _fft_kernel(u_ref, kw_ref, twc_ref, tws_ref, w2_ref, cs_ref, o_ref, tr_ref, ti_ref, scr_ref,
                *, la, lb, gd, npb, scale):
    r = FFT_LA_TILE
    rows = lb * r
    fw = u_ref.shape[2]
    ngroups = fw // gd
    reps = fw // twc_ref.shape[2]
    pitch = scr_ref.shape[1] // la

    @pl.when(pl.program_id(0) == 0)
    def _():
        scr_ref[...] = jnp.zeros_like(scr_ref)

    for j in range(la // r):
        u = jnp.concatenate([u_ref[0, k * la + j * r:k * la + (j + 1) * r, :] for k in range(lb)], axis=0)
        half = lb // 2 + 1
        t = _dot(kw_ref[...], u)
        for k in range(lb):
            src = k if k < half else lb - k
            a = t[src * r:(src + 1) * r]
            b = t[(half + src) * r:(half + src + 1) * r]
            ct = _tile_lanes(twc_ref[j, k * r:(k + 1) * r, :], reps)
            st = _tile_lanes(tws_ref[j, k * r:(k + 1) * r, :], reps)
            if k < half:
                tr, ti = a * ct + b * st, b * ct - a * st
            else:
                tr, ti = a * ct - b * st, -(b * ct) - a * st
            tr_ref[k * la + j * r:k * la + (j + 1) * r, :] = tr.astype(BF16)
            ti_ref[k * la + j * r:k * la + (j + 1) * r, :] = ti.astype(BF16)

    for pb in range(lb // npb):
        xs = []
        for p in range(npb):
            lo = (pb * npb + p) * la
            t = jnp.concatenate([tr_ref[lo:lo + la, :], ti_ref[lo:lo + la, :]], axis=0)
            xs.append(_dot(w2_ref[...], t).astype(BF16))
        for gi in range(ngroups):
            cols = slice(gi * gd, (gi + 1) * gd)
            lhs = jnp.concatenate([jnp.concatenate([x[:la, cols], x[la:, cols]], axis=1) for x in xs], axis=0)
            fg = _dot(lhs, cs_ref[...]) * scale
            for p in range(npb):
                scr_ref[gi, pl.ds(p, la, stride=pitch), :] = fg[p * la:(p + 1) * la]
        full = jnp.concatenate([scr_ref[gi].reshape(la, pitch, gd)[:, :npb, :] for gi in range(ngroups)], axis=2)
        o_ref[0, :, pb * npb:(pb + 1) * npb, :] = full.astype(BF16)


def _fft(u, kw, twc, tws, w2, cs, la, lb, gd, scale):
    b, l, fw = u.shape
    npb = min(BF16_SUBLANES, lb)
    out = pl.pallas_call(
        functools.partial(_fft_kernel, la=la, lb=lb, gd=gd, npb=npb, scale=scale),
        grid=(b,),
        in_specs=[pl.BlockSpec((1, l, fw), lambda i: (i, 0, 0)),
                  _const_spec(kw.shape), _const_spec(twc.shape), _const_spec(tws.shape),
                  _const_spec(w2.shape), _const_spec(cs.shape)],
        out_specs=pl.BlockSpec((1, la, lb, fw), lambda i: (i, 0, 0, 0)),
        out_shape=jax.ShapeDtypeStruct((b, la, lb, fw), BF16),
        scratch_shapes=[pltpu.VMEM((l, fw), BF16), pltpu.VMEM((l, fw), BF16),
                        pltpu.VMEM((fw // gd, la * (npb + FFT_SCATTER_PAD), gd), F32)],
        compiler_params=_params("arbitrary"),
        name="fft",
    )(u, kw, twc, tws, w2, cs)
    return out.reshape(b, l, fw)


def _ret_kernel(q_ref, kt_ref, v_ref, sg_ref, kct_ref, vc_ref, lgl_ref, lgc_ref,
                z_ref, sf_ref, sb_ref, stf_ref, stb_ref, p_ref, o_ref,
                dall_ref, qdf_ref, qdb_ref, kdf_ref, kdb_ref, *, nchunk, unroll):
    c = RET_CHUNK
    gw = GROUP_W
    lc = vc_ref.shape[1]

    lgf, lgb = lgl_ref[0:1, :], lgl_ref[1:2, :]
    pos = lax.broadcasted_iota(jnp.int32, (c, gw), 0).astype(F32)
    qdf_ref[...] = jnp.exp(lgf * (pos + 1.0))
    qdb_ref[...] = jnp.exp(lgb * (c - pos))
    kdf_ref[...] = jnp.exp(lgf * (c - 1.0 - pos))
    kdb_ref[...] = jnp.exp(lgb * pos)
    cdf = jnp.exp(lgf * c)
    cdb = jnp.exp(lgb * c)
    cpos = lax.broadcasted_iota(jnp.int32, (lc, gw), 0).astype(F32)
    wcf = jnp.exp(lgf * (lc - 1.0 - cpos))
    wcb = jnp.exp(lgb * cpos)
    si = lax.broadcasted_iota(jnp.int32, (c, HEADS_PER_GROUP * c), 0)
    sj = lax.broadcasted_iota(jnp.int32, (c, HEADS_PER_GROUP * c), 1) & (c - 1)
    diff = (si - sj).astype(F32)
    dall_ref[...] = (jnp.where(diff >= 0, jnp.exp(lgc_ref[0:1, :] * jnp.maximum(diff, 0.0)), 0.0)
                     + jnp.where(diff <= 0, jnp.exp(lgc_ref[1:2, :] * jnp.maximum(-diff, 0.0)), 0.0))

    pw = STATE_BLOCK_W
    npair = gw // pw
    same_head_p = (lax.broadcasted_iota(jnp.int32, (pw, pw), 0) // RET_HEAD_DIM
                   == lax.broadcasted_iota(jnp.int32, (pw, pw), 1) // RET_HEAD_DIM)

    def diag_blocks(t):
        return jnp.where(same_head_p, t, 0.0)

    def weighted(t, w):
        return (t.astype(F32) * w).astype(BF16)

    def rows(ref, n):
        return ref[0, pl.ds(pl.multiple_of(n * c, c), c), :]

    def kt_chunk(n):
        return kt_ref[0, :, pl.ds(pl.multiple_of(n * c, c), c)]

    def state_update(st_ref, kt, vw, cd):
        for p in range(npair):
            cols = slice(p * pw, (p + 1) * pw)
            st_ref[p] = st_ref[p] * cd[:, cols] + diag_blocks(_dot(kt[cols, :], vw[:, cols]))

    def full_state(s_ref, n):
        zero = jnp.zeros((pw, pw), BF16)
        return jnp.concatenate(
            [jnp.concatenate([s_ref[n, p] if q == p else zero for q in range(npair)], axis=1)
             for p in range(npair)], axis=0)

    kct = kct_ref[0]
    vc = vc_ref[0]
    stf_ref[...] = jnp.zeros_like(stf_ref)
    stb_ref[...] = jnp.zeros_like(stb_ref)
    state_update(stf_ref, kct, weighted(vc, wcf), cdf)
    state_update(stb_ref, kct, weighted(vc, wcb), cdb)

    def scan_body(i, carry):
        nf = i
        nb = nchunk - 1 - i
        sf_ref[nf] = stf_ref[...].astype(BF16)
        state_update(stf_ref, kt_chunk(nf), weighted(rows(v_ref, nf), kdf_ref[...]), cdf)
        sb_ref[nb] = stb_ref[...].astype(BF16)
        state_update(stb_ref, kt_chunk(nb), weighted(rows(v_ref, nb), kdb_ref[...]), cdb)
        return carry

    lax.fori_loop(0, nchunk, scan_body, 0, unroll=unroll)

    lane = lax.broadcasted_iota(jnp.int32, (c, gw), 1)
    sub = lax.broadcasted_iota(jnp.int32, (gw, c), 0)
    lane_masks = [(lane >= h * RET_HEAD_DIM) & (lane < (h + 1) * RET_HEAD_DIM)
                  for h in range(HEADS_PER_GROUP)]
    sub_masks = [(sub >= h * RET_HEAD_DIM) & (sub < (h + 1) * RET_HEAD_DIM)
                 for h in range(HEADS_PER_GROUP)]
    same_head = (lax.broadcasted_iota(jnp.int32, (gw, gw), 0) // RET_HEAD_DIM
                 == lax.broadcasted_iota(jnp.int32, (gw, gw), 1) // RET_HEAD_DIM)
    bd_mean = jnp.where(same_head, 1.0 / RET_HEAD_DIM, 0.0).astype(BF16)

    def score_body(n, carry):
        ktn = kt_chunk(n)
        zk = jnp.zeros_like(ktn)
        kbd = jnp.concatenate([jnp.where(m, ktn, zk) for m in sub_masks], axis=1)
        p_ref[n] = (_dot(rows(q_ref, n), kbd) * dall_ref[...]).astype(BF16)
        return carry

    lax.fori_loop(0, nchunk, score_body, 0, unroll=unroll)

    def mix_body(n, carry):
        qn = rows(q_ref, n)
        vn = rows(v_ref, n)
        zv = jnp.zeros_like(vn)
        vbd = jnp.concatenate([jnp.where(m, vn, zv) for m in lane_masks], axis=0)
        o_ref[n] = (_dot(p_ref[n], vbd) + qdf_ref[...] * _dot(qn, full_state(sf_ref, n))
                    + qdb_ref[...] * _dot(qn, full_state(sb_ref, n)))
        return carry

    lax.fori_loop(0, nchunk, mix_body, 0, unroll=unroll)

    def norm_body(n, carry):
        o = o_ref[n]
        ms = _dot((o * o).astype(BF16), bd_mean)
        z = rows(sg_ref, n).astype(F32) * (o * lax.rsqrt(ms + EPS))
        z_ref[0, pl.ds(pl.multiple_of(n * c, c), c), :] = z.astype(BF16)
        return carry

    lax.fori_loop(0, nchunk, norm_body, 0, unroll=unroll)


def _retention(q, kt, v, sg, kct, vc, log_gamma):
    b, l, rw = q.shape
    lc = vc.shape[1]
    c = RET_CHUNK
    gw = GROUP_W
    hpg = HEADS_PER_GROUP
    ng = rw // gw
    nchunk = l // c
    pw = STATE_BLOCK_W
    npair = gw // pw
    assert c & (c - 1) == 0
    lg = log_gamma.reshape(2, ng, hpg).transpose(1, 0, 2)
    lg_lane = jnp.repeat(lg, RET_HEAD_DIM, axis=2)
    lg_col = jnp.repeat(lg, c, axis=2)
    tok = pl.BlockSpec((1, l, gw), lambda i, j: (i, 0, j))
    tokt = pl.BlockSpec((1, gw, l), lambda i, j: (i, j, 0))
    grp = lambda r, width: pl.BlockSpec((None, r, width), lambda i, j: (j, 0, 0))
    return pl.pallas_call(
        functools.partial(_ret_kernel, nchunk=nchunk, unroll=math.gcd(nchunk, RET_UNROLL)),
        grid=(b, ng),
        in_specs=[
            tok, tokt, tok, tok,
            pl.BlockSpec((1, gw, lc), lambda i, j: (i, j, 0)),
            pl.BlockSpec((1, lc, gw), lambda i, j: (i, 0, j)),
            grp(2, gw), grp(2, hpg * c),
        ],
        out_specs=tok,
        out_shape=jax.ShapeDtypeStruct((b, l, rw), BF16),
        scratch_shapes=[pltpu.VMEM((nchunk, npair, pw, pw), BF16), pltpu.VMEM((nchunk, npair, pw, pw), BF16),
                        pltpu.VMEM((npair, pw, pw), F32), pltpu.VMEM((npair, pw, pw), F32),
                        pltpu.VMEM((nchunk, c, hpg * c), BF16),
                        pltpu.VMEM((nchunk, c, gw), F32),
                        pltpu.VMEM((c, hpg * c), F32)] + [pltpu.VMEM((c, gw), F32)] * 4,
        compiler_params=_params("arbitrary", "arbitrary"),
        name="ret",
    )(q, kt, v, sg, kct, vc, lg_lane, lg_col)


def _out_kernel(w4_ref, wr_ref, wo_ref, w1_ref, w2_ref, x_ref, mod_ref, fm_ref, z_ref, gab_ref,
                gain2_ref, fgain_ref, o_ref, *, ff_chunk):
    mod = mod_ref[:, pl.ds(pl.program_id(0), 1), :]
    g1, sh2, sc2, g2 = mod[2], mod[3], mod[4], mod[5]
    y_four = _dot(fm_ref[0], w4_ref[...])
    y_ret = _dot(z_ref[0], wr_ref[...])
    d = x_ref.shape[2]
    y = (jax.nn.sigmoid(gab_ref[0, :, :d].astype(F32)) * y_four
         + jax.nn.sigmoid(gab_ref[0, :, d:].astype(F32)) * y_ret)
    x1 = x_ref[0] + g1 * _dot(y.astype(BF16), wo_ref[...])
    h2 = _norm_mod(x1, gain2_ref[...], sh2, sc2).astype(BF16)
    dff = w1_ref.shape[1]
    acc = None
    for lo in range(0, dff, ff_chunk):
        hid = jnp.maximum(_dot(h2, w1_ref[:, lo:lo + ff_chunk]), 0.0)
        part = _dot((hid * hid).astype(BF16), w2_ref[lo:lo + ff_chunk, :])
        acc = part if acc is None else acc + part
    x2 = x1 + g2 * acc
    ms = jnp.mean(x2 * x2, axis=-1, keepdims=True)
    o_ref[0] = x2 * lax.rsqrt(ms + EPS) * fgain_ref[...]


def _out(x, mods, fm, z, gab, w4, wr, wo, gain2, w1, w2, fgain):
    b, l, d = x.shape
    tm = min(TOKEN_TILE, l)
    tok = lambda width: pl.BlockSpec((1, tm, width), lambda i, j: (i, j, 0))
    weights = (w4, wr, wo, w1, w2)
    return pl.pallas_call(
        functools.partial(_out_kernel, ff_chunk=min(1024, w1.shape[1])),
        grid=(b, l // tm),
        in_specs=[_const_spec(w.shape) for w in weights] + [
            tok(d),
            _const_spec(mods.shape),
            tok(fm.shape[2]), tok(z.shape[2]), tok(2 * d),
            _const_spec((1, d)), _const_spec((1, d)),
        ],
        out_specs=tok(d),
        out_shape=jax.ShapeDtypeStruct((b, l, d), F32),
        compiler_params=_params("arbitrary", "arbitrary"),
        name="out",
    )(*weights, x, mods, fm, z, gab, gain2, fgain)


def _dft_tables(l, gd):
    la = FFT_LA
    lb = l // la

    def cs(n):
        idx = np.arange(n)
        ang = 2.0 * np.pi * ((idx[:, None] * idx[None, :]) % n) / n
        return np.cos(ang), np.sin(ang)

    cb, sb = cs(lb)
    eye = np.eye(FFT_LA_TILE)
    half = lb // 2 + 1
    kw = np.concatenate([np.kron(cb[:half], eye), -np.kron(sb[:half], eye)], axis=0)
    ca, sa = cs(la)
    w2 = np.block([[ca, sa], [-sa, ca]])
    cc, sc = cs(gd)
    chan = np.concatenate([cc, sc], axis=0)
    tw = 2.0 * np.pi * (np.arange(lb)[:, None] * np.arange(la)[None, :]) / l
    tw = tw.reshape(lb, la // FFT_LA_TILE, FFT_LA_TILE).transpose(1, 0, 2).reshape(la // FFT_LA_TILE, -1)
    twc = np.repeat(np.cos(tw)[:, :, None], LANES, axis=2)
    tws = np.repeat(np.sin(tw)[:, :, None], LANES, axis=2)
    as_bf = lambda a: jnp.asarray(a, dtype=F32).astype(BF16)
    return as_bf(kw), as_bf(w2), as_bf(chan), jnp.asarray(twc, F32), jnp.asarray(tws, F32), la, lb


def _rope_tables(l):
    f32 = np.float32
    nf = RET_HEAD_DIM // 4
    inv = np.power(f32(ROPE_BASE), -np.arange(nf, dtype=f32) / f32(nf)).astype(f32)
    rows = l // GRID_W
    r, cc = np.meshgrid(np.arange(rows, dtype=f32), np.arange(GRID_W, dtype=f32), indexing="ij")
    ang = np.concatenate([r.reshape(-1)[:, None] * inv, cc.reshape(-1)[:, None] * inv], axis=-1).astype(f32)
    cos, sin = np.cos(ang).astype(f32), np.sin(ang).astype(f32)
    cos_h = np.concatenate([cos, cos], axis=1)
    sin_h = np.concatenate([-sin, sin], axis=1)
    reps = LANES // RET_HEAD_DIM
    return jnp.asarray(np.tile(cos_h, (1, reps))), jnp.asarray(np.tile(sin_h, (1, reps)))


def kernel(x, c, ctx, c_ctx, w_ada, b_ada, norm1_gain, w_in, four_w_out, ret_decay_logit,
           ret_gn_gain, ret_w_out, w_out, norm2_gain, w_mlp1, w_mlp2, final_gain):
    assert w_ada.shape[0] == 1, "single-layer block"
    b, l, d = x.shape
    lc = ctx.shape[1]
    fw = four_w_out.shape[1]
    rw = ret_w_out.shape[1]
    gd = fw // FOUR_GROUPS
    assert l % FFT_LA == 0 and l % RET_CHUNK == 0 and rw % GROUP_W == 0

    kw, w2, chan, twc, tws, la, lb = _dft_tables(l, gd)
    cosf, sinf = _rope_tables(l)

    later = [four_w_out[0], ret_w_out[0], w_out[0], w_mlp1[0], w_mlp2[0]]
    (u, q, kt, v, sg, gab), later_b, mods, kct, vc = _proj(
        x, c, c_ctx, w_ada[0], b_ada, ctx, norm1_gain, w_in[0], cosf, sinf, ret_gn_gain, later, fw, rw)

    fm = _fft(u, kw, twc, tws, w2, chan, la, lb, gd, 1.0 / math.sqrt(l * gd))

    log_gamma = jax.nn.log_sigmoid(ret_decay_logit[0].astype(F32))
    z = _retention(q, kt, v, sg, kct, vc, log_gamma)

    w4, wr, wo, w1, w2 = later_b
    return _out(x, mods, fm, z, gab, w4, wr, wo, norm2_gain, w1, w2, final_gain[None, :])
```

```python
import functools
import math

import jax
import jax.numpy as jnp
import numpy as np
from jax import lax
from jax.experimental import pallas as pl
from jax.experimental.pallas import tpu as pltpu

F32 = jnp.float32
BF16 = jnp.bfloat16

GRID_W = 64
FOUR_GROUPS = 4
RET_HEAD_DIM = 64
N_MOD = 6
ROPE_BASE = 10000.0
EPS = 1e-6

LANES = 128
MXU_DIM = 256
F32_SUBLANES = 8
BF16_SUBLANES = 16
VMEM_LIMIT_BYTES = 56 * 1024 * 1024

RET_CHUNK = 128
RET_UNROLL = 32
HEADS_PER_GROUP = MXU_DIM // RET_HEAD_DIM
GROUP_W = HEADS_PER_GROUP * RET_HEAD_DIM
STATE_BLOCK_W = LANES
FFT_LA = 128
FFT_LA_TILE = BF16_SUBLANES
FFT_SCATTER_PAD = F32_SUBLANES
TOKEN_TILE = 512
PROJ_TOKEN_TILE = 512
WEIGHT_STAGE_COLS = 512
WEIGHT_STAGE_SLOTS = 6


def _dot(a, b):
    return jnp.dot(a, b, preferred_element_type=F32)


def _norm_mod(x, gain, shift, scale):
    ms = jnp.mean(x * x, axis=-1, keepdims=True)
    y = x * lax.rsqrt(ms + EPS) * gain
    return y * (1.0 + scale) + shift


def _tile_lanes(t, reps):
    return jnp.concatenate([t] * reps, axis=1) if reps > 1 else t


def _const_spec(shape):
    nd = len(shape)
    return pl.BlockSpec(shape, lambda *_: (0,) * nd, pipeline_mode=pl.Buffered(1))


def _params(*sem):
    return pltpu.CompilerParams(dimension_semantics=sem, vmem_limit_bytes=VMEM_LIMIT_BYTES)


def _is_first_step(grid_rank):
    first = pl.program_id(0) == 0
    for axis in range(1, grid_rank):
        first = first & (pl.program_id(axis) == 0)
    return first


def _proj_prepare(before_body, c_ref, cctx_ref, wada_hbm, bada_ref, w_hbm, ctx_ref, gain_ref,
                  mods_ref, kct_ref, vc_ref, w_ref, mod_ref, cv_ref, stage_ref, sem_ref, *, fw, rw, scale):
    nb = c_ref.shape[0]
    d = w_ref.shape[0]
    if before_body:
        cv_ref[...] = jnp.zeros_like(cv_ref)
        cv_ref[0:nb, :] = c_ref[...]
        cv_ref[nb:nb + 1, :] = cctx_ref[...]
    cv = cv_ref[...]
    s = (cv * jax.nn.sigmoid(cv)).astype(BF16)

    slots, _, sc = stage_ref.shape
    ada = [(wada_hbm, c0) for c0 in range(0, wada_hbm.shape[1], sc)]
    win = [(w_hbm, c0) for c0 in range(0, w_hbm.shape[1], sc)]
    body_mods = 2 * d // sc
    chunks = ada[:body_mods] + win + ada[body_mods:]
    split = body_mods + len(win)
    ctx_after = body_mods + (fw + 3 * rw) // sc - 1

    def copy(k):
        src, c0 = chunks[k]
        slot = k % slots
        return pltpu.make_async_copy(src.at[:, pl.ds(c0, sc)], stage_ref.at[slot], sem_ref.at[slot])

    if before_body:
        for k in range(min(slots - 1, len(chunks))):
            copy(k).start()
    for k in range(0, split) if before_body else range(split, len(chunks)):
        src, c0 = chunks[k]
        if k + slots - 1 < len(chunks):
            copy(k + slots - 1).start()
        copy(k).wait()
        blk = stage_ref[k % slots].astype(BF16)
        if src is wada_hbm:
            j, off = divmod(c0, d)
            hw = sc // 2
            mod_ref[j, :, off:off + hw] = _dot(s, blk[:, :hw]) + bada_ref[:, c0:c0 + hw]
            mod_ref[j, :, off + hw:off + sc] = _dot(s, blk[:, hw:]) + bada_ref[:, c0 + hw:c0 + sc]
        else:
            w_ref[:, c0:c0 + sc] = blk
        if k == ctx_after:
            for i in range(nb):
                h = _norm_mod(ctx_ref[i], gain_ref[...],
                              mod_ref[0, nb:nb + 1, :], mod_ref[1, nb:nb + 1, :]).astype(BF16)
                kct_ref[i] = (_dot(h, w_ref[:, fw + rw:fw + 2 * rw]) * scale).T.astype(BF16)
                vc_ref[i] = _dot(h, w_ref[:, fw + 2 * rw:fw + 3 * rw]).astype(BF16)
    if not before_body:
        mods_ref[...] = mod_ref[...]


def _proj_kernel(*refs, fw, rw, d, scale, ncast):
    x_ref, cos_ref, sin_ref, w_hbm, wada_hbm = refs[:5]
    cast_in = refs[5:5 + ncast]
    gain_ref, gng_ref, c_ref, cctx_ref, bada_ref, ctx_ref = refs[5 + ncast:11 + ncast]
    outs = refs[11 + ncast:]
    u_ref, q_ref, kt_ref, v_ref, sg_ref, gab_ref = outs[:6]
    cast_out = outs[6:6 + ncast]
    mods_ref, kct_ref, vc_ref = outs[6 + ncast:9 + ncast]
    w_ref, mod_ref, cv_ref, stage_ref, sem_ref = outs[9 + ncast:]

    prepare = functools.partial(
        _proj_prepare, c_ref=c_ref, cctx_ref=cctx_ref, wada_hbm=wada_hbm, bada_ref=bada_ref, w_hbm=w_hbm,
        ctx_ref=ctx_ref, gain_ref=gain_ref, mods_ref=mods_ref, kct_ref=kct_ref, vc_ref=vc_ref, w_ref=w_ref,
        mod_ref=mod_ref, cv_ref=cv_ref, stage_ref=stage_ref, sem_ref=sem_ref, fw=fw, rw=rw, scale=scale)
    pl.when(_is_first_step(2))(functools.partial(prepare, True))

    for src, dst in zip(cast_in, cast_out):
        dst[...] = src[...].astype(BF16)

    mod = mod_ref[0:2, pl.ds(pl.program_id(0), 1), :]
    h = _norm_mod(x_ref[0], gain_ref[...], mod[0], mod[1]).astype(BF16)
    tm = h.shape[0]

    def proj(lo, width):
        return _dot(h, w_ref[:, lo:lo + width])

    u_ref[0] = proj(0, fw).astype(BF16)

    reps = rw // cos_ref.shape[1]
    cosf = _tile_lanes(cos_ref[...], reps)
    sinf = _tile_lanes(sin_ref[...], reps)
    lane = lax.broadcasted_iota(jnp.int32, (tm, rw), 1)
    first_half = (lane & (RET_HEAD_DIM // 2)) == 0

    def rope(t):
        rot = jnp.where(first_half,
                        pltpu.roll(t, rw - RET_HEAD_DIM // 2, 1),
                        pltpu.roll(t, RET_HEAD_DIM // 2, 1))
        return t * cosf + rot * sinf

    q_ref[0] = rope(proj(fw, rw)).astype(BF16)
    kt_ref[0] = (rope(proj(fw + rw, rw)) * scale).T.astype(BF16)
    v_ref[0] = proj(fw + 2 * rw, rw).astype(BF16)
    g = proj(fw + 3 * rw, rw)
    sg_ref[0] = (g * jax.nn.sigmoid(g) * gng_ref[...]).astype(BF16)
    gab_ref[0, :, :d] = proj(fw + 4 * rw, d).astype(BF16)
    gab_ref[0, :, d:] = proj(fw + 4 * rw + d, d).astype(BF16)

    pl.when(_is_first_step(2))(functools.partial(prepare, False))


def _proj(x, c, c_ctx, w_ada, b_ada, ctx, gain, w_in, cosf, sinf, gn_gain, later_weights, fw, rw):
    b, l, d = x.shape
    lc = ctx.shape[1]
    tm = min(PROJ_TOKEN_TILE, l)
    nj = l // tm
    steps = b * nj
    rows = -(-(b + 1) // F32_SUBLANES) * F32_SUBLANES
    sc = WEIGHT_STAGE_COLS
    assert w_ada.shape == (d, N_MOD * d) and d % sc == 0 and w_in.shape[1] % sc == 0
    tok = lambda width: pl.BlockSpec((1, tm, width), lambda i, j: (i, j, 0))
    out = lambda width: jax.ShapeDtypeStruct((b, l, width), BF16)
    sliced = [w.reshape(steps, w.shape[0] // steps, w.shape[1]) for w in later_weights]
    for w in sliced:
        assert w.shape[1] % BF16_SUBLANES == 0
    cast_spec = lambda w: pl.BlockSpec((1,) + w.shape[1:], lambda i, j: (i * nj + j, 0, 0))
    whole = lambda shape: pl.BlockSpec(shape, lambda i, j: (0,) * len(shape))
    res = pl.pallas_call(
        functools.partial(_proj_kernel, fw=fw, rw=rw, d=d, scale=RET_HEAD_DIM ** -0.5, ncast=len(sliced)),
        grid=(b, nj),
        in_specs=[
            tok(d),
            pl.BlockSpec((tm, cosf.shape[1]), lambda i, j: (j, 0)),
            pl.BlockSpec((tm, sinf.shape[1]), lambda i, j: (j, 0)),
            pl.BlockSpec(memory_space=pl.ANY),
            pl.BlockSpec(memory_space=pl.ANY),
        ] + [cast_spec(w) for w in sliced] + [
            _const_spec((1, d)), _const_spec((1, rw)), _const_spec((b, d)), _const_spec((1, d)),
            _const_spec((1, N_MOD * d)), _const_spec((b, lc, d)),
        ],
        out_specs=[tok(fw), tok(rw), pl.BlockSpec((1, rw, tm), lambda i, j: (i, 0, j)),
                   tok(rw), tok(rw), tok(2 * d)] + [cast_spec(w) for w in sliced]
        + [whole((N_MOD, rows, d)), whole((b, rw, lc)), whole((b, lc, rw))],
        out_shape=[out(fw), out(rw), jax.ShapeDtypeStruct((b, rw, l), BF16),
                   out(rw), out(rw), out(2 * d)]
        + [jax.ShapeDtypeStruct(w.shape, BF16) for w in sliced]
        + [jax.ShapeDtypeStruct((N_MOD, rows, d), F32),
           jax.ShapeDtypeStruct((b, rw, lc), BF16), jax.ShapeDtypeStruct((b, lc, rw), BF16)],
        scratch_shapes=[pltpu.VMEM(w_in.shape, BF16),
                        pltpu.VMEM((N_MOD, rows, d), F32),
                        pltpu.VMEM((rows, d), F32),
                        pltpu.VMEM((WEIGHT_STAGE_SLOTS, d, sc), F32),
                        pltpu.SemaphoreType.DMA((WEIGHT_STAGE_SLOTS,))],
        compiler_params=_params("arbitrary", "arbitrary"),
        name="proj",
    )(x, cosf, sinf, w_in, w_ada, *sliced, gain, gn_gain, c, c_ctx[None, :], b_ada, ctx)
    ncast = len(sliced)
    casts = [cc.reshape(w.shape) for cc, w in zip(res[6:6 + ncast], later_weights)]
    mods, kct, vc = res[6 + ncast:]
    return res[:6], casts, mods, kct, vc


def _fft_kernel(u_ref, kw_ref, twc_ref, tws_ref, w2_ref, cs_ref, o_ref, tr_ref, ti_ref, scr_ref,
                *, la, lb, gd, npb, scale):
    r = FFT_LA_TILE
    rows = lb * r
    fw = u_ref.shape[2]
    ngroups = fw // gd
    reps = fw // twc_ref.shape[2]
    pitch = scr_ref.shape[1] // la

    @pl.when(pl.program_id(0) == 0)
    def _():
        scr_ref[...] = jnp.zeros_like(scr_ref)

    for j in range(la // r):
        u = jnp.concatenate([u_ref[0, k * la + j * r:k * la + (j + 1) * r, :] for k in range(lb)], axis=0)
        half = lb // 2 + 1
        t = _dot(kw_ref[...], u)
        for k in range(lb):
            src = k if k < half else lb - k
            a = t[src * r:(src + 1) * r]
            b = t[(half + src) * r:(half + src + 1) * r]
            ct = _tile_lanes(twc_ref[j, k * r:(k + 1) * r, :], reps)
            st = _tile_lanes(tws_ref[j, k * r:(k + 1) * r, :], reps)
            if k < half:
                tr, ti = a * ct + b * st, b * ct - a * st
            else:
                tr, ti = a * ct - b * st, -(b * ct) - a * st
            tr_ref[k * la + j * r:k * la + (j + 1) * r, :] = tr.astype(BF16)
            ti_ref[k * la + j * r:k * la + (j + 1) * r, :] = ti.astype(BF16)

    for pb in range(lb // npb):
        xs = []
        for p in range(npb):
            lo = (pb * npb + p) * la
            t = jnp.concatenate([tr_ref[lo:lo + la, :], ti_ref[lo:lo + la, :]], axis=0)
            xs.append(_dot(w2_ref[...], t).astype(BF16))
        for gi in range(ngroups):
            cols = slice(gi * gd, (gi + 1) * gd)
            lhs = jnp.concatenate([jnp.concatenate([x[:la, cols], x[la:, cols]], axis=1) for x in xs], axis=0)
            fg = _dot(lhs, cs_ref[...]) * scale
            for p in range(npb):
                scr_ref[gi, pl.ds(p, la, stride=pitch), :] = fg[p * la:(p + 1) * la]
        full = jnp.concatenate([scr_ref[gi].reshape(la, pitch, gd)[:, :npb, :] for gi in range(ngroups)], axis=2)
        o_ref[0, :, pb * npb:(pb + 1) * npb, :] = full.astype(BF16)


def _fft(u, kw, twc, tws, w2, cs, la, lb, gd, scale):
    b, l, fw = u.shape
    npb = min(BF16_SUBLANES, lb)
    out = pl.pallas_call(
        functools.partial(_fft_kernel, la=la, lb=lb, gd=gd, npb=npb, scale=scale),
        grid=(b,),
        in_specs=[pl.BlockSpec((1, l, fw), lambda i: (i, 0, 0)),
                  _const_spec(kw.shape), _const_spec(twc.shape), _const_spec(tws.shape),
                  _const_spec(w2.shape), _const_spec(cs.shape)],
        out_specs=pl.BlockSpec((1, la, lb, fw), lambda i: (i, 0, 0, 0)),
        out_shape=jax.ShapeDtypeStruct((b, la, lb, fw), BF16),
        scratch_shapes=[pltpu.VMEM((l, fw), BF16), pltpu.VMEM((l, fw), BF16),
                        pltpu.VMEM((fw // gd, la * (npb + FFT_SCATTER_PAD), gd), F32)],
        compiler_params=_params("arbitrary"),
        name="fft",
    )(u, kw, twc, tws, w2, cs)
    return out.reshape(b, l, fw)


def _ret_kernel(q_ref, kt_ref, v_ref, sg_ref, kct_ref, vc_ref, lgl_ref, lgc_ref,
                z_ref, sf_ref, sb_ref, stf_ref, stb_ref, p_ref, o_ref,
                dall_ref, qdf_ref, qdb_ref, kdf_ref, kdb_ref, *, nchunk, unroll):
    c = RET_CHUNK
    gw = GROUP_W
    lc = vc_ref.shape[1]

    lgf, lgb = lgl_ref[0:1, :], lgl_ref[1:2, :]
    pos = lax.broadcasted_iota(jnp.int32, (c, gw), 0).astype(F32)
    qdf_ref[...] = jnp.exp(lgf * (pos + 1.0))
    qdb_ref[...] = jnp.exp(lgb * (c - pos))
    kdf_ref[...] = jnp.exp(lgf * (c - 1.0 - pos))
    kdb_ref[...] = jnp.exp(lgb * pos)
    cdf = jnp.exp(lgf * c)
    cdb = jnp.exp(lgb * c)
    cpos = lax.broadcasted_iota(jnp.int32, (lc, gw), 0).astype(F32)
    wcf = jnp.exp(lgf * (lc - 1.0 - cpos))
    wcb = jnp.exp(lgb * cpos)
    si = lax.broadcasted_iota(jnp.int32, (c, HEADS_PER_GROUP * c), 0)
    sj = lax.broadcasted_iota(jnp.int32, (c, HEADS_PER_GROUP * c), 1) & (c - 1)
    diff = (si - sj).astype(F32)
    dall_ref[...] = (jnp.where(diff >= 0, jnp.exp(lgc_ref[0:1, :] * jnp.maximum(diff, 0.0)), 0.0)
                     + jnp.where(diff <= 0, jnp.exp(lgc_ref[1:2, :] * jnp.maximum(-diff, 0.0)), 0.0))

    pw = STATE_BLOCK_W
    npair = gw // pw
    same_head_p = (lax.broadcasted_iota(jnp.int32, (pw, pw), 0) // RET_HEAD_DIM
                   == lax.broadcasted_iota(jnp.int32, (pw, pw), 1) // RET_HEAD_DIM)

    def diag_blocks(t):
        return jnp.where(same_head_p, t, 0.0)

    def weighted(t, w):
        return (t.astype(F32) * w).astype(BF16)

    def rows(ref, n):
        return ref[0, pl.ds(pl.multiple_of(n * c, c), c), :]

    def kt_chunk(n):
        return kt_ref[0, :, pl.ds(pl.multiple_of(n * c, c), c)]

    def state_update(st_ref, kt, vw, cd):
        for p in range(npair):
            cols = slice(p * pw, (p + 1) * pw)
            st_ref[p] = st_ref[p] * cd[:, cols] + diag_blocks(_dot(kt[cols, :], vw[:, cols]))

    def full_state(s_ref, n):
        zero = jnp.zeros((pw, pw), BF16)
        return jnp.concatenate(
            [jnp.concatenate([s_ref[n, p] if q == p else zero for q in range(npair)], axis=1)
             for p in range(npair)], axis=0)

    kct = kct_ref[0]
    vc = vc_ref[0]
    stf_ref[...] = jnp.zeros_like(stf_ref)
    stb_ref[...] = jnp.zeros_like(stb_ref)
    state_update(stf_ref, kct, weighted(vc, wcf), cdf)
    state_update(stb_ref, kct, weighted(vc, wcb), cdb)

    def scan_body(i, carry):
        nf = i
        nb = nchunk - 1 - i
        sf_ref[nf] = stf_ref[...].astype(BF16)
        state_update(stf_ref, kt_chunk(nf), weighted(rows(v_ref, nf), kdf_ref[...]), cdf)
        sb_ref[nb] = stb_ref[...].astype(BF16)
        state_update(stb_ref, kt_chunk(nb), weighted(rows(v_ref, nb), kdb_ref[...]), cdb)
        return carry

    lax.fori_loop(0, nchunk, scan_body, 0, unroll=unroll)

    lane = lax.broadcasted_iota(jnp.int32, (c, gw), 1)
    sub = lax.broadcasted_iota(jnp.int32, (gw, c), 0)
    lane_masks = [(lane >= h * RET_HEAD_DIM) & (lane < (h + 1) * RET_HEAD_DIM)
                  for h in range(HEADS_PER_GROUP)]
    sub_masks = [(sub >= h * RET_HEAD_DIM) & (sub < (h + 1) * RET_HEAD_DIM)
                 for h in range(HEADS_PER_GROUP)]
    same_head = (lax.broadcasted_iota(jnp.int32, (gw, gw), 0) // RET_HEAD_DIM
                 == lax.broadcasted_iota(jnp.int32, (gw, gw), 1) // RET_HEAD_DIM)
    bd_mean = jnp.where(same_head, 1.0 / RET_HEAD_DIM, 0.0).astype(BF16)

    def score_body(n, carry):
        ktn = kt_chunk(n)
        zk = jnp.zeros_like(ktn)
        kbd = jnp.concatenate([jnp.where(m, ktn, zk) for m in sub_masks], axis=1)
        p_ref[n] = (_dot(rows(q_ref, n), kbd) * dall_ref[...]).astype(BF16)
        return carry

    lax.fori_loop(0, nchunk, score_body, 0, unroll=unroll)

    def mix_body(n, carry):
        qn = rows(q_ref, n)
        vn = rows(v_ref, n)
        zv = jnp.zeros_like(vn)
        vbd = jnp.concatenate([jnp.where(m, vn, zv) for m in lane_masks], axis=0)
        o_ref[n] = (_dot(p_ref[n], vbd) + qdf_ref[...] * _dot(qn, full_state(sf_ref, n))
                    + qdb_ref[...] * _dot(qn, full_state(sb_ref, n)))
        return carry

    lax.fori_loop(0, nchunk, mix_body, 0, unroll=unroll)

    def norm_body(n, carry):
        o = o_ref[n]
        ms = _dot((o * o).astype(BF16), bd_mean)
        z = rows(sg_ref, n).astype(F32) * (o * lax.rsqrt(ms + EPS))
        z_ref[0, pl.ds(pl.multiple_of(n * c, c), c), :] = z.astype(BF16)
        return carry

    lax.fori_loop(0, nchunk, norm_body, 0, unroll=unroll)


def _retention(q, kt, v, sg, kct, vc, log_gamma):
    b, l, rw = q.shape
    lc = vc.shape[1]
    c = RET_CHUNK
    gw = GROUP_W
    hpg = HEADS_PER_GROUP
    ng = rw // gw
    nchunk = l // c
    pw = STATE_BLOCK_W
    npair = gw // pw
    assert c & (c - 1) == 0
    lg = log_gamma.reshape(2, ng, hpg).transpose(1, 0, 2)
    lg_lane = jnp.repeat(lg, RET_HEAD_DIM, axis=2)
    lg_col = jnp.repeat(lg, c, axis=2)
    tok = pl.BlockSpec((1, l, gw), lambda i, j: (i, 0, j))
    tokt = pl.BlockSpec((1, gw, l), lambda i, j: (i, j, 0))
    grp = lambda r, width: pl.BlockSpec((None, r, width), lambda i, j: (j, 0, 0))
    return pl.pallas_call(
        functools.partial(_ret_kernel, nchunk=nchunk, unroll=math.gcd(nchunk, RET_UNROLL)),
        grid=(b, ng),
        in_specs=[
            tok, tokt, tok, tok,
            pl.BlockSpec((1, gw, lc), lambda i, j: (i, j, 0)),
            pl.BlockSpec((1, lc, gw), lambda i, j: (i, 0, j)),
            grp(2, gw), grp(2, hpg * c),
        ],
        out_specs=tok,
        out_shape=jax.ShapeDtypeStruct((b, l, rw), BF16),
        scratch_shapes=[pltpu.VMEM((nchunk, npair, pw, pw), BF16), pltpu.VMEM((nchunk, npair, pw, pw), BF16),
                        pltpu.VMEM((npair, pw, pw), F32), pltpu.VMEM((npair, pw, pw), F32),
                        pltpu.VMEM((nchunk, c, hpg * c), BF16),
                        pltpu.VMEM((nchunk, c, gw), F32),
                        pltpu.VMEM((c, hpg * c), F32)] + [pltpu.VMEM((c, gw), F32)] * 4,
        compiler_params=_params("arbitrary", "arbitrary"),
        name="ret",
    )(q, kt, v, sg, kct, vc, lg_lane, lg_col)


def _out_kernel(w4_ref, wr_ref, wo_ref, w1_ref, w2_ref, x_ref, mod_ref, fm_ref, z_ref, gab_ref,
                gain2_ref, fgain_ref, o_ref, *, ff_chunk):
    mod = mod_ref[:, pl.ds(pl.program_id(0), 1), :]
    g1, sh2, sc2, g2 = mod[2], mod[3], mod[4], mod[5]
    y_four = _dot(fm_ref[0], w4_ref[...])
    y_ret = _dot(z_ref[0], wr_ref[...])
    d = x_ref.shape[2]
    y = (jax.nn.sigmoid(gab_ref[0, :, :d].astype(F32)) * y_four
         + jax.nn.sigmoid(gab_ref[0, :, d:].astype(F32)) * y_ret)
    x1 = x_ref[0] + g1 * _dot(y.astype(BF16), wo_ref[...])
    h2 = _norm_mod(x1, gain2_ref[...], sh2, sc2).astype(BF16)
    dff = w1_ref.shape[1]
    acc = None
    for lo in range(0, dff, ff_chunk):
        hid = jnp.maximum(_dot(h2, w1_ref[:, lo:lo + ff_chunk]), 0.0)
        part = _dot((hid * hid).astype(BF16), w2_ref[lo:lo + ff_chunk, :])
        acc = part if acc is None else acc + part
    x2 = x1 + g2 * acc
    ms = jnp.mean(x2 * x2, axis=-1, keepdims=True)
    o_ref[0] = x2 * lax.rsqrt(ms + EPS) * fgain_ref[...]


def _out(x, mods, fm, z, gab, w4, wr, wo, gain2, w1, w2, fgain):
    b, l, d = x.shape
    tm = min(TOKEN_TILE, l)
    tok = lambda width: pl.BlockSpec((1, tm, width), lambda i, j: (i, j, 0))
    weights = (w4, wr, wo, w1, w2)
    return pl.pallas_call(
        functools.partial(_out_kernel, ff_chunk=min(1024, w1.shape[1])),
        grid=(b, l // tm),
        in_specs=[_const_spec(w.shape) for w in weights] + [
            tok(d),
            _const_spec(mods.shape),
            tok(fm.shape[2]), tok(z.shape[2]), tok(2 * d),
            _const_spec((1, d)), _const_spec((1, d)),
        ],
        out_specs=tok(d),
        out_shape=jax.ShapeDtypeStruct((b, l, d), F32),
        compiler_params=_params("arbitrary", "arbitrary"),
        name="out",
    )(*weights, x, mods, fm, z, gab, gain2, fgain)


def _dft_tables(l, gd):
    la = FFT_LA
    lb = l // la

    def cs(n):
        idx = np.arange(n)
        ang = 2.0 * np.pi * ((idx[:, None] * idx[None, :]) % n) / n
        return np.cos(ang), np.sin(ang)

    cb, sb = cs(lb)
    eye = np.eye(FFT_LA_TILE)
    half = lb // 2 + 1
    kw = np.concatenate([np.kron(cb[:half], eye), -np.kron(sb[:half], eye)], axis=0)
    ca, sa = cs(la)
    w2 = np.block([[ca, sa], [-sa, ca]])
    cc, sc = cs(gd)
    chan = np.concatenate([cc, sc], axis=0)
    tw = 2.0 * np.pi * (np.arange(lb)[:, None] * np.arange(la)[None, :]) / l
    tw = tw.reshape(lb, la // FFT_LA_TILE, FFT_LA_TILE).transpose(1, 0, 2).reshape(la // FFT_LA_TILE, -1)
    twc = np.repeat(np.cos(tw)[:, :, None], LANES, axis=2)
    tws = np.repeat(np.sin(tw)[:, :, None], LANES, axis=2)
    as_bf = lambda a: jnp.asarray(a, dtype=F32).astype(BF16)
    return as_bf(kw), as_bf(w2), as_bf(chan), jnp.asarray(twc, F32), jnp.asarray(tws, F32), la, lb


def _rope_tables(l):
    f32 = np.float32
    nf = RET_HEAD_DIM // 4
    inv = np.power(f32(ROPE_BASE), -np.arange(nf, dtype=f32) / f32(nf)).astype(f32)
    rows = l // GRID_W
    r, cc = np.meshgrid(np.arange(rows, dtype=f32), np.arange(GRID_W, dtype=f32), indexing="ij")
    ang = np.concatenate([r.reshape(-1)[:, None] * inv, cc.reshape(-1)[:, None] * inv], axis=-1).astype(f32)
    cos, sin = np.cos(ang).astype(f32), np.sin(ang).astype(f32)
    cos_h = np.concatenate([cos, cos], axis=1)
    sin_h = np.concatenate([-sin, sin], axis=1)
    reps = LANES // RET_HEAD_DIM
    return jnp.asarray(np.tile(cos_h, (1, reps))), jnp.asarray(np.tile(sin_h, (1, reps)))


def kernel(x, c, ctx, c_ctx, w_ada, b_ada, norm1_gain, w_in, four_w_out, ret_decay_logit,
           ret_gn_gain, ret_w_out, w_out, norm2_gain, w_mlp1, w_mlp2, final_gain):
    assert w_ada.shape[0] == 1, "single-layer block"
    b, l, d = x.shape
    lc = ctx.shape[1]
    fw = four_w_out.shape[1]
    rw = ret_w_out.shape[1]
    gd = fw // FOUR_GROUPS
    assert l % FFT_LA == 0 and l % RET_CHUNK == 0 and rw % GROUP_W == 0

    kw, w2, chan, twc, tws, la, lb = _dft_tables(l, gd)
    cosf, sinf = _rope_tables(l)

    later = [four_w_out[0], ret_w_out[0], w_out[0], w_mlp1[0], w_mlp2[0]]
    (u, q, kt, v, sg, gab), later_b, mods, kct, vc = _proj(
        x, c, c_ctx, w_ada[0], b_ada, ctx, norm1_gain, w_in[0], cosf, sinf, ret_gn_gain, later, fw, rw)

    fm = _fft(u, kw, twc, tws, w2, chan, la, lb, gd, 1.0 / math.sqrt(l * gd))

    log_gamma = jax.nn.log_sigmoid(ret_decay_logit[0].astype(F32))
    z = _retention(q, kt, v, sg, kct, vc, log_gamma)

    w4, wr, wo, w1, w2 = later_b
    return _out(x, mods, fm, z, gab, w4, wr, wo, norm2_gain, w1, w2, final_gain[None, :])
```

```python
import functools
import math

import jax
import jax.numpy as jnp
import numpy as np
from jax import lax
from jax.experimental import pallas as pl
from jax.experimental.pallas import tpu as pltpu

F32 = jnp.float32
BF16 = jnp.bfloat16

GRID_W = 64
FOUR_GROUPS = 4
RET_HEAD_DIM = 64
N_MOD = 6
ROPE_BASE = 10000.0
EPS = 1e-6

LANES = 128
MXU_DIM = 256
F32_SUBLANES = 8
BF16_SUBLANES = 16
VMEM_LIMIT_BYTES = 56 * 1024 * 1024

RET_CHUNK = 128
RET_UNROLL = 32
HEADS_PER_GROUP = MXU_DIM // RET_HEAD_DIM
GROUP_W = HEADS_PER_GROUP * RET_HEAD_DIM
STATE_BLOCK_W = LANES
FFT_LA = 128
FFT_LA_TILE = BF16_SUBLANES
FFT_SCATTER_PAD = F32_SUBLANES
TOKEN_TILE = 512
PROJ_TOKEN_TILE = 512
WEIGHT_STAGE_COLS = 512
WEIGHT_STAGE_SLOTS = 6


def _dot(a, b):
    return jnp.dot(a, b, preferred_element_type=F32)


def _norm_mod(x, gain, shift, scale):
    ms = jnp.mean(x * x, axis=-1, keepdims=True)
    y = x * lax.rsqrt(ms + EPS) * gain
    return y * (1.0 + scale) + shift


def _tile_lanes(t, reps):
    return jnp.concatenate([t] * reps, axis=1) if reps > 1 else t


def _const_spec(shape):
    nd = len(shape)
    return pl.BlockSpec(shape, lambda *_: (0,) * nd, pipeline_mode=pl.Buffered(1))


def _params(*sem):
    return pltpu.CompilerParams(dimension_semantics=sem, vmem_limit_bytes=VMEM_LIMIT_BYTES)


def _is_first_step(grid_rank):
    first = pl.program_id(0) == 0
    for axis in range(1, grid_rank):
        first = first & (pl.program_id(axis) == 0)
    return first


def _proj_prepare(after_body, c_ref, cctx_ref, wada_hbm, bada_ref, w_hbm, ctx_ref, gain_ref,
                  mods_ref, kct_ref, vc_ref, w_ref, mod_ref, cv_ref, stage_ref, sem_ref, *,
                  fw, rw, scale, on_columns=None):
    nb = c_ref.shape[0]
    d = w_ref.shape[0]
    if not after_body:
        cv_ref[...] = jnp.zeros_like(cv_ref)
        cv_ref[0:nb, :] = c_ref[...]
        cv_ref[nb:nb + 1, :] = cctx_ref[...]
    cv = cv_ref[...]
    s = (cv * jax.nn.sigmoid(cv)).astype(BF16)

    slots, _, sc = stage_ref.shape
    ada = [(wada_hbm, c0) for c0 in range(0, wada_hbm.shape[1], sc)]
    win = [(w_hbm, c0) for c0 in range(0, w_hbm.shape[1], sc)]
    body_mods = 2 * d // sc
    chunks = ada[:body_mods] + win + ada[body_mods:]
    split = body_mods + len(win)
    ctx_after = max(split, len(chunks) - slots)

    def copy(k):
        src, c0 = chunks[k]
        slot = k % slots
        return pltpu.make_async_copy(src.at[:, pl.ds(c0, sc)], stage_ref.at[slot], sem_ref.at[slot])

    if not after_body:
        for k in range(min(slots - 1, len(chunks))):
            copy(k).start()
    for k in range(split, len(chunks)) if after_body else range(0, split):
        src, c0 = chunks[k]
        if k + slots - 1 < len(chunks):
            copy(k + slots - 1).start()
        copy(k).wait()
        blk = stage_ref[k % slots].astype(BF16)
        if src is wada_hbm:
            j, off = divmod(c0, d)
            hw = sc // 2
            mod_ref[j, :, off:off + hw] = _dot(s, blk[:, :hw]) + bada_ref[:, c0:c0 + hw]
            mod_ref[j, :, off + hw:off + sc] = _dot(s, blk[:, hw:]) + bada_ref[:, c0 + hw:c0 + sc]
        else:
            w_ref[:, c0:c0 + sc] = blk
            on_columns(c0 + sc)
        if k == ctx_after:
            for i in range(nb):
                h = _norm_mod(ctx_ref[i], gain_ref[...],
                              mod_ref[0, nb:nb + 1, :], mod_ref[1, nb:nb + 1, :]).astype(BF16)
                kct_ref[i] = (_dot(h, w_ref[:, fw + rw:fw + 2 * rw]) * scale).T.astype(BF16)
                vc_ref[i] = _dot(h, w_ref[:, fw + 2 * rw:fw + 3 * rw]).astype(BF16)
    if after_body:
        mods_ref[...] = mod_ref[...]


def _proj_outputs(h, w_ref, cos_ref, sin_ref, gng_ref, u_ref, q_ref, kt_ref, v_ref, sg_ref, gab_ref, *,
                  fw, rw, d, scale):
    tm = h.shape[0]

    def proj(lo, width):
        return _dot(h, w_ref[:, lo:lo + width])

    reps = rw // cos_ref.shape[1]
    lane = lax.broadcasted_iota(jnp.int32, (tm, rw), 1)
    first_half = (lane & (RET_HEAD_DIM // 2)) == 0

    def rope(t):
        rot = jnp.where(first_half,
                        pltpu.roll(t, rw - RET_HEAD_DIM // 2, 1),
                        pltpu.roll(t, RET_HEAD_DIM // 2, 1))
        return t * _tile_lanes(cos_ref[...], reps) + rot * _tile_lanes(sin_ref[...], reps)

    def u():
        u_ref[0] = proj(0, fw).astype(BF16)

    def q():
        q_ref[0] = rope(proj(fw, rw)).astype(BF16)

    def kt():
        kt_ref[0] = (rope(proj(fw + rw, rw)) * scale).T.astype(BF16)

    def v():
        v_ref[0] = proj(fw + 2 * rw, rw).astype(BF16)

    def sg():
        g = proj(fw + 3 * rw, rw)
        sg_ref[0] = (g * jax.nn.sigmoid(g) * gng_ref[...]).astype(BF16)

    def gab(off):
        gab_ref[0, :, off:off + rw] = proj(fw + 4 * rw + off, rw).astype(BF16)

    stores = [u, q, kt, v, sg] + [functools.partial(gab, off) for off in range(0, 2 * d, rw)]
    return [(fw + i * rw, store) for i, store in enumerate(stores)]


def _proj_kernel(*refs, fw, rw, d, scale, ncast):
    x_ref, cos_ref, sin_ref, w_hbm, wada_hbm = refs[:5]
    cast_in = refs[5:5 + ncast]
    gain_ref, gng_ref, c_ref, cctx_ref, bada_ref, ctx_ref = refs[5 + ncast:11 + ncast]
    outs = refs[11 + ncast:]
    u_ref, q_ref, kt_ref, v_ref, sg_ref, gab_ref = outs[:6]
    cast_out = outs[6:6 + ncast]
    mods_ref, kct_ref, vc_ref = outs[6 + ncast:9 + ncast]
    w_ref, mod_ref, cv_ref, stage_ref, sem_ref = outs[9 + ncast:]
    first = _is_first_step(2)

    def outputs():
        mod = mod_ref[0:2, pl.ds(pl.program_id(0), 1), :]
        h = _norm_mod(x_ref[0], gain_ref[...], mod[0], mod[1]).astype(BF16)
        return _proj_outputs(h, w_ref, cos_ref, sin_ref, gng_ref, u_ref, q_ref, kt_ref, v_ref, sg_ref, gab_ref,
                             fw=fw, rw=rw, d=d, scale=scale)

    prepare = functools.partial(
        _proj_prepare, c_ref=c_ref, cctx_ref=cctx_ref, wada_hbm=wada_hbm, bada_ref=bada_ref, w_hbm=w_hbm,
        ctx_ref=ctx_ref, gain_ref=gain_ref, mods_ref=mods_ref, kct_ref=kct_ref, vc_ref=vc_ref, w_ref=w_ref,
        mod_ref=mod_ref, cv_ref=cv_ref, stage_ref=stage_ref, sem_ref=sem_ref, fw=fw, rw=rw, scale=scale)

    @pl.when(first)
    def _():
        pending = []

        def on_columns(resident):
            if resident == WEIGHT_STAGE_COLS:
                pending.extend(outputs())
            while pending and pending[0][0] <= resident:
                pending.pop(0)[1]()

        prepare(False, on_columns=on_columns)
        assert not pending

    @pl.when(jnp.logical_not(first))
    def _():
        for _, store in outputs():
            store()

    for src, dst in zip(cast_in, cast_out):
        dst[...] = src[...].astype(BF16)

    pl.when(first)(functools.partial(prepare, True))


def _proj(x, c, c_ctx, w_ada, b_ada, ctx, gain, w_in, cosf, sinf, gn_gain, later_weights, fw, rw):
    b, l, d = x.shape
    lc = ctx.shape[1]
    tm = min(PROJ_TOKEN_TILE, l)
    nj = l // tm
    steps = b * nj
    rows = -(-(b + 1) // F32_SUBLANES) * F32_SUBLANES
    sc = WEIGHT_STAGE_COLS
    assert w_ada.shape == (d, N_MOD * d) and d % sc == 0 and w_in.shape[1] % sc == 0
    tok = lambda width: pl.BlockSpec((1, tm, width), lambda i, j: (i, j, 0))
    out = lambda width: jax.ShapeDtypeStruct((b, l, width), BF16)
    sliced = [w.reshape(steps, w.shape[0] // steps, w.shape[1]) for w in later_weights]
    for w in sliced:
        assert w.shape[1] % BF16_SUBLANES == 0
    cast_spec = lambda w: pl.BlockSpec((1,) + w.shape[1:], lambda i, j: (i * nj + j, 0, 0))
    whole = lambda shape: pl.BlockSpec(shape, lambda i, j: (0,) * len(shape))
    res = pl.pallas_call(
        functools.partial(_proj_kernel, fw=fw, rw=rw, d=d, scale=RET_HEAD_DIM ** -0.5, ncast=len(sliced)),
        grid=(b, nj),
        in_specs=[
            tok(d),
            pl.BlockSpec((tm, cosf.shape[1]), lambda i, j: (j, 0)),
            pl.BlockSpec((tm, sinf.shape[1]), lambda i, j: (j, 0)),
            pl.BlockSpec(memory_space=pl.ANY),
            pl.BlockSpec(memory_space=pl.ANY),
        ] + [cast_spec(w) for w in sliced] + [
            _const_spec((1, d)), _const_spec((1, rw)), _const_spec((b, d)), _const_spec((1, d)),
            _const_spec((1, N_MOD * d)), _const_spec((b, lc, d)),
        ],
        out_specs=[tok(fw), tok(rw), pl.BlockSpec((1, rw, tm), lambda i, j: (i, 0, j)),
                   tok(rw), tok(rw), tok(2 * d)] + [cast_spec(w) for w in sliced]
        + [whole((N_MOD, rows, d)), whole((b, rw, lc)), whole((b, lc, rw))],
        out_shape=[out(fw), out(rw), jax.ShapeDtypeStruct((b, rw, l), BF16),
                   out(rw), out(rw), out(2 * d)]
        + [jax.ShapeDtypeStruct(w.shape, BF16) for w in sliced]
        + [jax.ShapeDtypeStruct((N_MOD, rows, d), F32),
           jax.ShapeDtypeStruct((b, rw, lc), BF16), jax.ShapeDtypeStruct((b, lc, rw), BF16)],
        scratch_shapes=[pltpu.VMEM(w_in.shape, BF16),
                        pltpu.VMEM((N_MOD, rows, d), F32),
                        pltpu.VMEM((rows, d), F32),
                        pltpu.VMEM((WEIGHT_STAGE_SLOTS, d, sc), F32),
                        pltpu.SemaphoreType.DMA((WEIGHT_STAGE_SLOTS,))],
        compiler_params=_params("arbitrary", "arbitrary"),
        name="proj",
    )(x, cosf, sinf, w_in, w_ada, *sliced, gain, gn_gain, c, c_ctx[None, :], b_ada, ctx)
    ncast = len(sliced)
    casts = [cc.reshape(w.shape) for cc, w in zip(res[6:6 + ncast], later_weights)]
    mods, kct, vc = res[6 + ncast:]
    return res[:6], casts, mods, kct, vc


def _fft_kernel(u_ref, kw_ref, twc_ref, tws_ref, w2_ref, cs_ref, o_ref, tr_ref, ti_ref, scr_ref,
                *, la, lb, gd, npb, scale):
    r = FFT_LA_TILE
    rows = lb * r
    fw = u_ref.shape[2]
    ngroups = fw // gd
    reps = fw // twc_ref.shape[2]
    pitch = scr_ref.shape[1] // la

    @pl.when(pl.program_id(0) == 0)
    def _():
        scr_ref[...] = jnp.zeros_like(scr_ref)

    for j in range(la // r):
        u = jnp.concatenate([u_ref[0, k * la + j * r:k * la + (j + 1) * r, :] for k in range(lb)], axis=0)
        half = lb // 2 + 1
        t = _dot(kw_ref[...], u)
        for k in range(lb):
            src = k if k < half else lb - k
            a = t[src * r:(src + 1) * r]
            b = t[(half + src) * r:(half + src + 1) * r]
            ct = _tile_lanes(twc_ref[j, k * r:(k + 1) * r, :], reps)
            st = _tile_lanes(tws_ref[j, k * r:(k + 1) * r, :], reps)
            if k < half:
                tr, ti = a * ct + b * st, b * ct - a * st
            else:
                tr, ti = a * ct - b * st, -(b * ct) - a * st
            tr_ref[k * la + j * r:k * la + (j + 1) * r, :] = tr.astype(BF16)
            ti_ref[k * la + j * r:k * la + (j + 1) * r, :] = ti.astype(BF16)

    for pb in range(lb // npb):
        xs = []
        for p in range(npb):
            lo = (pb * npb + p) * la
            t = jnp.concatenate([tr_ref[lo:lo + la, :], ti_ref[lo:lo + la, :]], axis=0)
            xs.append(_dot(w2_ref[...], t).astype(BF16))
        for gi in range(ngroups):
            cols = slice(gi * gd, (gi + 1) * gd)
            lhs = jnp.concatenate([jnp.concatenate([x[:la, cols], x[la:, cols]], axis=1) for x in xs], axis=0)
            fg = _dot(lhs, cs_ref[...]) * scale
            for p in range(npb):
                scr_ref[gi, pl.ds(p, la, stride=pitch), :] = fg[p * la:(p + 1) * la]
        full = jnp.concatenate([scr_ref[gi].reshape(la, pitch, gd)[:, :npb, :] for gi in range(ngroups)], axis=2)
        o_ref[0, :, pb * npb:(pb + 1) * npb, :] = full.astype(BF16)


def _fft(u, kw, twc, tws, w2, cs, la, lb, gd, scale):
    b, l, fw = u.shape
    npb = min(BF16_SUBLANES, lb)
    out = pl.pallas_call(
        functools.partial(_fft_kernel, la=la, lb=lb, gd=gd, npb=npb, scale=scale),
        grid=(b,),
        in_specs=[pl.BlockSpec((1, l, fw), lambda i: (i, 0, 0)),
                  _const_spec(kw.shape), _const_spec(twc.shape), _const_spec(tws.shape),
                  _const_spec(w2.shape), _const_spec(cs.shape)],
        out_specs=pl.BlockSpec((1, la, lb, fw), lambda i: (i, 0, 0, 0)),
        out_shape=jax.ShapeDtypeStruct((b, la, lb, fw), BF16),
        scratch_shapes=[pltpu.VMEM((l, fw), BF16), pltpu.VMEM((l, fw), BF16),
                        pltpu.VMEM((fw // gd, la * (npb + FFT_SCATTER_PAD), gd), F32)],
        compiler_params=_params("arbitrary"),
        name="fft",
    )(u, kw, twc, tws, w2, cs)
    return out.reshape(b, l, fw)


def _ret_kernel(q_ref, kt_ref, v_ref, sg_ref, kct_ref, vc_ref, lgl_ref, lgc_ref,
                z_ref, sf_ref, sb_ref, stf_ref, stb_ref, p_ref, o_ref,
                dall_ref, qdf_ref, qdb_ref, kdf_ref, kdb_ref, *, nchunk, unroll):
    c = RET_CHUNK
    gw = GROUP_W
    lc = vc_ref.shape[1]

    lgf, lgb = lgl_ref[0:1, :], lgl_ref[1:2, :]
    pos = lax.broadcasted_iota(jnp.int32, (c, gw), 0).astype(F32)
    qdf_ref[...] = jnp.exp(lgf * (pos + 1.0))
    qdb_ref[...] = jnp.exp(lgb * (c - pos))
    kdf_ref[...] = jnp.exp(lgf * (c - 1.0 - pos))
    kdb_ref[...] = jnp.exp(lgb * pos)
    cdf = jnp.exp(lgf * c)
    cdb = jnp.exp(lgb * c)
    cpos = lax.broadcasted_iota(jnp.int32, (lc, gw), 0).astype(F32)
    wcf = jnp.exp(lgf * (lc - 1.0 - cpos))
    wcb = jnp.exp(lgb * cpos)
    si = lax.broadcasted_iota(jnp.int32, (c, HEADS_PER_GROUP * c), 0)
    sj = lax.broadcasted_iota(jnp.int32, (c, HEADS_PER_GROUP * c), 1) & (c - 1)
    diff = (si - sj).astype(F32)
    dall_ref[...] = (jnp.where(diff >= 0, jnp.exp(lgc_ref[0:1, :] * jnp.maximum(diff, 0.0)), 0.0)
                     + jnp.where(diff <= 0, jnp.exp(lgc_ref[1:2, :] * jnp.maximum(-diff, 0.0)), 0.0))

    pw = STATE_BLOCK_W
    npair = gw // pw
    same_head_p = (lax.broadcasted_iota(jnp.int32, (pw, pw), 0) // RET_HEAD_DIM
                   == lax.broadcasted_iota(jnp.int32, (pw, pw), 1) // RET_HEAD_DIM)

    def diag_blocks(t):
        return jnp.where(same_head_p, t, 0.0)

    def weighted(t, w):
        return (t.astype(F32) * w).astype(BF16)

    def rows(ref, n):
        return ref[0, pl.ds(pl.multiple_of(n * c, c), c), :]

    def kt_chunk(n):
        return kt_ref[0, :, pl.ds(pl.multiple_of(n * c, c), c)]

    def state_update(st_ref, kt, vw, cd):
        for p in range(npair):
            cols = slice(p * pw, (p + 1) * pw)
            st_ref[p] = st_ref[p] * cd[:, cols] + diag_blocks(_dot(kt[cols, :], vw[:, cols]))

    def full_state(s_ref, n):
        zero = jnp.zeros((pw, pw), BF16)
        return jnp.concatenate(
            [jnp.concatenate([s_ref[n, p] if q == p else zero for q in range(npair)], axis=1)
             for p in range(npair)], axis=0)

    kct = kct_ref[0]
    vc = vc_ref[0]
    stf_ref[...] = jnp.zeros_like(stf_ref)
    stb_ref[...] = jnp.zeros_like(stb_ref)
    state_update(stf_ref, kct, weighted(vc, wcf), cdf)
    state_update(stb_ref, kct, weighted(vc, wcb), cdb)

    def scan_body(i, carry):
        nf = i
        nb = nchunk - 1 - i
        sf_ref[nf] = stf_ref[...].astype(BF16)
        state_update(stf_ref, kt_chunk(nf), weighted(rows(v_ref, nf), kdf_ref[...]), cdf)
        sb_ref[nb] = stb_ref[...].astype(BF16)
        state_update(stb_ref, kt_chunk(nb), weighted(rows(v_ref, nb), kdb_ref[...]), cdb)
        return carry

    lax.fori_loop(0, nchunk, scan_body, 0, unroll=unroll)

    lane = lax.broadcasted_iota(jnp.int32, (c, gw), 1)
    sub = lax.broadcasted_iota(jnp.int32, (gw, c), 0)
    lane_masks = [(lane >= h * RET_HEAD_DIM) & (lane < (h + 1) * RET_HEAD_DIM)
                  for h in range(HEADS_PER_GROUP)]
    sub_masks = [(sub >= h * RET_HEAD_DIM) & (sub < (h + 1) * RET_HEAD_DIM)
                 for h in range(HEADS_PER_GROUP)]
    same_head = (lax.broadcasted_iota(jnp.int32, (gw, gw), 0) // RET_HEAD_DIM
                 == lax.broadcasted_iota(jnp.int32, (gw, gw), 1) // RET_HEAD_DIM)
    bd_mean = jnp.where(same_head, 1.0 / RET_HEAD_DIM, 0.0).astype(BF16)

    def score_body(n, carry):
        ktn = kt_chunk(n)
        zk = jnp.zeros_like(ktn)
        kbd = jnp.concatenate([jnp.where(m, ktn, zk) for m in sub_masks], axis=1)
        p_ref[n] = (_dot(rows(q_ref, n), kbd) * dall_ref[...]).astype(BF16)
        return carry

    lax.fori_loop(0, nchunk, score_body, 0, unroll=unroll)

    def mix_body(n, carry):
        qn = rows(q_ref, n)
        vn = rows(v_ref, n)
        zv = jnp.zeros_like(vn)
        vbd = jnp.concatenate([jnp.where(m, vn, zv) for m in lane_masks], axis=0)
        o_ref[n] = (_dot(p_ref[n], vbd) + qdf_ref[...] * _dot(qn, full_state(sf_ref, n))
                    + qdb_ref[...] * _dot(qn, full_state(sb_ref, n)))
        return carry

    lax.fori_loop(0, nchunk, mix_body, 0, unroll=unroll)

    def norm_body(n, carry):
        o = o_ref[n]
        ms = _dot((o * o).astype(BF16), bd_mean)
        z = rows(sg_ref, n).astype(F32) * (o * lax.rsqrt(ms + EPS))
        z_ref[0, pl.ds(pl.multiple_of(n * c, c), c), :] = z.astype(BF16)
        return carry

    lax.fori_loop(0, nchunk, norm_body, 0, unroll=unroll)


def _retention(q, kt, v, sg, kct, vc, log_gamma):
    b, l, rw = q.shape
    lc = vc.shape[1]
    c = RET_CHUNK
    gw = GROUP_W
    hpg = HEADS_PER_GROUP
    ng = rw // gw
    nchunk = l // c
    pw = STATE_BLOCK_W
    npair = gw // pw
    assert c & (c - 1) == 0
    lg = log_gamma.reshape(2, ng, hpg).transpose(1, 0, 2)
    lg_lane = jnp.repeat(lg, RET_HEAD_DIM, axis=2)
    lg_col = jnp.repeat(lg, c, axis=2)
    tok = pl.BlockSpec((1, l, gw), lambda i, j: (i, 0, j))
    tokt = pl.BlockSpec((1, gw, l), lambda i, j: (i, j, 0))
    grp = lambda r, width: pl.BlockSpec((None, r, width), lambda i, j: (j, 0, 0))
    return pl.pallas_call(
        functools.partial(_ret_kernel, nchunk=nchunk, unroll=math.gcd(nchunk, RET_UNROLL)),
        grid=(b, ng),
        in_specs=[
            tok, tokt, tok, tok,
            pl.BlockSpec((1, gw, lc), lambda i, j: (i, j, 0)),
            pl.BlockSpec((1, lc, gw), lambda i, j: (i, 0, j)),
            grp(2, gw), grp(2, hpg * c),
        ],
        out_specs=tok,
        out_shape=jax.ShapeDtypeStruct((b, l, rw), BF16),
        scratch_shapes=[pltpu.VMEM((nchunk, npair, pw, pw), BF16), pltpu.VMEM((nchunk, npair, pw, pw), BF16),
                        pltpu.VMEM((npair, pw, pw), F32), pltpu.VMEM((npair, pw, pw), F32),
                        pltpu.VMEM((nchunk, c, hpg * c), BF16),
                        pltpu.VMEM((nchunk, c, gw), F32),
                        pltpu.VMEM((c, hpg * c), F32)] + [pltpu.VMEM((c, gw), F32)] * 4,
        compiler_params=_params("arbitrary", "arbitrary"),
        name="ret",
    )(q, kt, v, sg, kct, vc, lg_lane, lg_col)


def _out_kernel(w4_ref, wr_ref, wo_ref, w1_ref, w2_ref, x_ref, mod_ref, fm_ref, z_ref, gab_ref,
                gain2_ref, fgain_ref, o_ref, *, ff_chunk):
    mod = mod_ref[:, pl.ds(pl.program_id(0), 1), :]
    g1, sh2, sc2, g2 = mod[2], mod[3], mod[4], mod[5]
    y_four = _dot(fm_ref[0], w4_ref[...])
    y_ret = _dot(z_ref[0], wr_ref[...])
    d = x_ref.shape[2]
    y = (jax.nn.sigmoid(gab_ref[0, :, :d].astype(F32)) * y_four
         + jax.nn.sigmoid(gab_ref[0, :, d:].astype(F32)) * y_ret)
    x1 = x_ref[0] + g1 * _dot(y.astype(BF16), wo_ref[...])
    h2 = _norm_mod(x1, gain2_ref[...], sh2, sc2).astype(BF16)
    dff = w1_ref.shape[1]
    acc = None
    for lo in range(0, dff, ff_chunk):
        hid = jnp.maximum(_dot(h2, w1_ref[:, lo:lo + ff_chunk]), 0.0)
        part = _dot((hid * hid).astype(BF16), w2_ref[lo:lo + ff_chunk, :])
        acc = part if acc is None else acc + part
    x2 = x1 + g2 * acc
    ms = jnp.mean(x2 * x2, axis=-1, keepdims=True)
    o_ref[0] = x2 * lax.rsqrt(ms + EPS) * fgain_ref[...]


def _out(x, mods, fm, z, gab, w4, wr, wo, gain2, w1, w2, fgain):
    b, l, d = x.shape
    tm = min(TOKEN_TILE, l)
    tok = lambda width: pl.BlockSpec((1, tm, width), lambda i, j: (i, j, 0))
    weights = (w4, wr, wo, w1, w2)
    return pl.pallas_call(
        functools.partial(_out_kernel, ff_chunk=min(1024, w1.shape[1])),
        grid=(b, l // tm),
        in_specs=[_const_spec(w.shape) for w in weights] + [
            tok(d),
            _const_spec(mods.shape),
            tok(fm.shape[2]), tok(z.shape[2]), tok(2 * d),
            _const_spec((1, d)), _const_spec((1, d)),
        ],
        out_specs=tok(d),
        out_shape=jax.ShapeDtypeStruct((b, l, d), F32),
        compiler_params=_params("arbitrary", "arbitrary"),
        name="out",
    )(*weights, x, mods, fm, z, gab, gain2, fgain)


def _dft_tables(l, gd):
    la = FFT_LA
    lb = l // la

    def cs(n):
        idx = np.arange(n)
        ang = 2.0 * np.pi * ((idx[:, None] * idx[None, :]) % n) / n
        return np.cos(ang), np.sin(ang)

    cb, sb = cs(lb)
    eye = np.eye(FFT_LA_TILE)
    half = lb // 2 + 1
    kw = np.concatenate([np.kron(cb[:half], eye), -np.kron(sb[:half], eye)], axis=0)
    ca, sa = cs(la)
    w2 = np.block([[ca, sa], [-sa, ca]])
    cc, sc = cs(gd)
    chan = np.concatenate([cc, sc], axis=0)
    tw = 2.0 * np.pi * (np.arange(lb)[:, None] * np.arange(la)[None, :]) / l
    tw = tw.reshape(lb, la // FFT_LA_TILE, FFT_LA_TILE).transpose(1, 0, 2).reshape(la // FFT_LA_TILE, -1)
    twc = np.repeat(np.cos(tw)[:, :, None], LANES, axis=2)
    tws = np.repeat(np.sin(tw)[:, :, None], LANES, axis=2)
    as_bf = lambda a: jnp.asarray(a, dtype=F32).astype(BF16)
    return as_bf(kw), as_bf(w2), as_bf(chan), jnp.asarray(twc, F32), jnp.asarray(tws, F32), la, lb


def _rope_tables(l):
    f32 = np.float32
    nf = RET_HEAD_DIM // 4
    inv = np.power(f32(ROPE_BASE), -np.arange(nf, dtype=f32) / f32(nf)).astype(f32)
    rows = l // GRID_W
    r, cc = np.meshgrid(np.arange(rows, dtype=f32), np.arange(GRID_W, dtype=f32), indexing="ij")
    ang = np.concatenate([r.reshape(-1)[:, None] * inv, cc.reshape(-1)[:, None] * inv], axis=-1).astype(f32)
    cos, sin = np.cos(ang).astype(f32), np.sin(ang).astype(f32)
    cos_h = np.concatenate([cos, cos], axis=1)
    sin_h = np.concatenate([-sin, sin], axis=1)
    reps = LANES // RET_HEAD_DIM
    return jnp.asarray(np.tile(cos_h, (1, reps))), jnp.asarray(np.tile(sin_h, (1, reps)))


def kernel(x, c, ctx, c_ctx, w_ada, b_ada, norm1_gain, w_in, four_w_out, ret_decay_logit,
           ret_gn_gain, ret_w_out, w_out, norm2_gain, w_mlp1, w_mlp2, final_gain):
    assert w_ada.shape[0] == 1, "single-layer block"
    b, l, d = x.shape
    lc = ctx.shape[1]
    fw = four_w_out.shape[1]
    rw = ret_w_out.shape[1]
    gd = fw // FOUR_GROUPS
    assert l % FFT_LA == 0 and l % RET_CHUNK == 0 and rw % GROUP_W == 0

    kw, w2, chan, twc, tws, la, lb = _dft_tables(l, gd)
    cosf, sinf = _rope_tables(l)

    later = [four_w_out[0], ret_w_out[0], w_out[0], w_mlp1[0], w_mlp2[0]]
    (u, q, kt, v, sg, gab), later_b, mods, kct, vc = _proj(
        x, c, c_ctx, w_ada[0], b_ada, ctx, norm1_gain, w_in[0], cosf, sinf, ret_gn_gain, later, fw, rw)

    fm = _fft(u, kw, twc, tws, w2, chan, la, lb, gd, 1.0 / math.sqrt(l * gd))

    log_gamma = jax.nn.log_sigmoid(ret_decay_logit[0].astype(F32))
    z = _retention(q, kt, v, sg, kct, vc, log_gamma)

    w4, wr, wo, w1, w2 = later_b
    return _out(x, mods, fm, z, gab, w4, wr, wo, norm2_gain, w1, w2, final_gain[None, :])
```

```python
import functools
import math

import jax
import jax.numpy as jnp
import numpy as np
from jax import lax
from jax.experimental import pallas as pl
from jax.experimental.pallas import tpu as pltpu

F32 = jnp.float32
BF16 = jnp.bfloat16

GRID_W = 64
FOUR_GROUPS = 4
RET_HEAD_DIM = 64
N_MOD = 6
ROPE_BASE = 10000.0
EPS = 1e-6

LANES = 128
MXU_DIM = 256
F32_SUBLANES = 8
BF16_SUBLANES = 16
VMEM_LIMIT_BYTES = 56 * 1024 * 1024

RET_CHUNK = 128
RET_UNROLL = 32
HEADS_PER_GROUP = MXU_DIM // RET_HEAD_DIM
GROUP_W = HEADS_PER_GROUP * RET_HEAD_DIM
STATE_BLOCK_W = LANES
FFT_LA = 128
FFT_LA_TILE = BF16_SUBLANES
FFT_SCATTER_PAD = F32_SUBLANES
TOKEN_TILE = 512
PROJ_TOKEN_TILE = 512
WEIGHT_STAGE_COLS = 512
WEIGHT_STAGE_SLOTS = 6


def _dot(a, b):
    return jnp.dot(a, b, preferred_element_type=F32)


def _norm_mod(x, gain, shift, scale):
    ms = jnp.mean(x * x, axis=-1, keepdims=True)
    y = x * lax.rsqrt(ms + EPS) * gain
    return y * (1.0 + scale) + shift


def _tile_lanes(t, reps):
    return jnp.concatenate([t] * reps, axis=1) if reps > 1 else t


def _const_spec(shape):
    nd = len(shape)
    return pl.BlockSpec(shape, lambda *_: (0,) * nd, pipeline_mode=pl.Buffered(1))


def _params(*sem):
    return pltpu.CompilerParams(dimension_semantics=sem, vmem_limit_bytes=VMEM_LIMIT_BYTES)


def _is_first_step(grid_rank):
    first = pl.program_id(0) == 0
    for axis in range(1, grid_rank):
        first = first & (pl.program_id(axis) == 0)
    return first


def _proj_prepare(before_body, c_ref, cctx_ref, wada_hbm, bada_ref, w_hbm, ctx_ref, gain_ref,
                  mods_ref, kct_ref, vc_ref, w_ref, mod_ref, cv_ref, stage_ref, sem_ref, *, fw, rw, scale):
    nb = c_ref.shape[0]
    d = w_ref.shape[0]
    if before_body:
        cv_ref[...] = jnp.zeros_like(cv_ref)
        cv_ref[0:nb, :] = c_ref[...]
        cv_ref[nb:nb + 1, :] = cctx_ref[...]
    cv = cv_ref[...]
    s = (cv * jax.nn.sigmoid(cv)).astype(BF16)

    slots, _, sc = stage_ref.shape
    ada = [(wada_hbm, c0) for c0 in range(0, wada_hbm.shape[1], sc)]
    win = [(w_hbm, c0) for c0 in range(0, w_hbm.shape[1], sc)]
    body_mods = 2 * d // sc
    chunks = ada[:body_mods] + win + ada[body_mods:]
    split = body_mods + len(win)
    ctx_after = body_mods + (fw + 3 * rw) // sc - 1

    def copy(k):
        src, c0 = chunks[k]
        slot = k % slots
        return pltpu.make_async_copy(src.at[:, pl.ds(c0, sc)], stage_ref.at[slot], sem_ref.at[slot])

    if before_body:
        for k in range(min(slots - 1, len(chunks))):
            copy(k).start()
    for k in range(0, split) if before_body else range(split, len(chunks)):
        src, c0 = chunks[k]
        if k + slots - 1 < len(chunks):
            copy(k + slots - 1).start()
        copy(k).wait()
        blk = stage_ref[k % slots].astype(BF16)
        if src is wada_hbm:
            j, off = divmod(c0, d)
            hw = sc // 2
            mod_ref[j, :, off:off + hw] = _dot(s, blk[:, :hw]) + bada_ref[:, c0:c0 + hw]
            mod_ref[j, :, off + hw:off + sc] = _dot(s, blk[:, hw:]) + bada_ref[:, c0 + hw:c0 + sc]
        else:
            w_ref[:, c0:c0 + sc] = blk
        if k == ctx_after:
            for i in range(nb):
                h = _norm_mod(ctx_ref[i], gain_ref[...],
                              mod_ref[0, nb:nb + 1, :], mod_ref[1, nb:nb + 1, :]).astype(BF16)
                kct_ref[i] = (_dot(h, w_ref[:, fw + rw:fw + 2 * rw]) * scale).T.astype(BF16)
                vc_ref[i] = _dot(h, w_ref[:, fw + 2 * rw:fw + 3 * rw]).astype(BF16)
    if not before_body:
        mods_ref[...] = mod_ref[...]


def _proj_kernel(*refs, fw, rw, d, scale, ncast):
    x_ref, cos_ref, sin_ref, w_hbm, wada_hbm = refs[:5]
    cast_in = refs[5:5 + ncast]
    gain_ref, gng_ref, c_ref, cctx_ref, bada_ref, ctx_ref = refs[5 + ncast:11 + ncast]
    outs = refs[11 + ncast:]
    u_ref, q_ref, kt_ref, v_ref, sg_ref, gab_ref = outs[:6]
    cast_out = outs[6:6 + ncast]
    mods_ref, kct_ref, vc_ref = outs[6 + ncast:9 + ncast]
    w_ref, mod_ref, cv_ref, stage_ref, sem_ref = outs[9 + ncast:]

    prepare = functools.partial(
        _proj_prepare, c_ref=c_ref, cctx_ref=cctx_ref, wada_hbm=wada_hbm, bada_ref=bada_ref, w_hbm=w_hbm,
        ctx_ref=ctx_ref, gain_ref=gain_ref, mods_ref=mods_ref, kct_ref=kct_ref, vc_ref=vc_ref, w_ref=w_ref,
        mod_ref=mod_ref, cv_ref=cv_ref, stage_ref=stage_ref, sem_ref=sem_ref, fw=fw, rw=rw, scale=scale)
    pl.when(_is_first_step(2))(functools.partial(prepare, True))

    for src, dst in zip(cast_in, cast_out):
        dst[...] = src[...].astype(BF16)

    mod = mod_ref[0:2, pl.ds(pl.program_id(0), 1), :]
    h = _norm_mod(x_ref[0], gain_ref[...], mod[0], mod[1]).astype(BF16)
    tm = h.shape[0]

    def proj(lo, width):
        return _dot(h, w_ref[:, lo:lo + width])

    u_ref[0] = proj(0, fw).astype(BF16)

    reps = rw // cos_ref.shape[1]
    cosf = _tile_lanes(cos_ref[...], reps)
    sinf = _tile_lanes(sin_ref[...], reps)
    lane = lax.broadcasted_iota(jnp.int32, (tm, rw), 1)
    first_half = (lane & (RET_HEAD_DIM // 2)) == 0

    def rope(t):
        rot = jnp.where(first_half,
                        pltpu.roll(t, rw - RET_HEAD_DIM // 2, 1),
                        pltpu.roll(t, RET_HEAD_DIM // 2, 1))
        return t * cosf + rot * sinf

    q_ref[0] = rope(proj(fw, rw)).astype(BF16)
    kt_ref[0] = (rope(proj(fw + rw, rw)) * scale).T.astype(BF16)
    v_ref[0] = proj(fw + 2 * rw, rw).astype(BF16)
    g = proj(fw + 3 * rw, rw)
    sg_ref[0] = (g * jax.nn.sigmoid(g) * gng_ref[...]).astype(BF16)
    gab_ref[0, :, :d] = proj(fw + 4 * rw, d).astype(BF16)
    gab_ref[0, :, d:] = proj(fw + 4 * rw + d, d).astype(BF16)

    pl.when(_is_first_step(2))(functools.partial(prepare, False))


def _proj(x, c, c_ctx, w_ada, b_ada, ctx, gain, w_in, cosf, sinf, gn_gain, later_weights, fw, rw):
    b, l, d = x.shape
    lc = ctx.shape[1]
    tm = min(PROJ_TOKEN_TILE, l)
    nj = l // tm
    steps = b * nj
    rows = -(-(b + 1) // F32_SUBLANES) * F32_SUBLANES
    sc = WEIGHT_STAGE_COLS
    assert w_ada.shape == (d, N_MOD * d) and d % sc == 0 and w_in.shape[1] % sc == 0
    tok = lambda width: pl.BlockSpec((1, tm, width), lambda i, j: (i, j, 0))
    out = lambda width: jax.ShapeDtypeStruct((b, l, width), BF16)
    sliced = [w.reshape(steps, w.shape[0] // steps, w.shape[1]) for w in later_weights]
    for w in sliced:
        assert w.shape[1] % BF16_SUBLANES == 0
    cast_spec = lambda w: pl.BlockSpec((1,) + w.shape[1:], lambda i, j: (i * nj + j, 0, 0))
    whole = lambda shape: pl.BlockSpec(shape, lambda i, j: (0,) * len(shape))
    res = pl.pallas_call(
        functools.partial(_proj_kernel, fw=fw, rw=rw, d=d, scale=RET_HEAD_DIM ** -0.5, ncast=len(sliced)),
        grid=(b, nj),
        in_specs=[
            tok(d),
            pl.BlockSpec((tm, cosf.shape[1]), lambda i, j: (j, 0)),
            pl.BlockSpec((tm, sinf.shape[1]), lambda i, j: (j, 0)),
            pl.BlockSpec(memory_space=pl.ANY),
            pl.BlockSpec(memory_space=pl.ANY),
        ] + [cast_spec(w) for w in sliced] + [
            _const_spec((1, d)), _const_spec((1, rw)), _const_spec((b, d)), _const_spec((1, d)),
            _const_spec((1, N_MOD * d)), _const_spec((b, lc, d)),
        ],
        out_specs=[tok(fw), tok(rw), pl.BlockSpec((1, rw, tm), lambda i, j: (i, 0, j)),
                   tok(rw), tok(rw), tok(2 * d)] + [cast_spec(w) for w in sliced]
        + [whole((N_MOD, rows, d)), whole((b, rw, lc)), whole((b, lc, rw))],
        out_shape=[out(fw), out(rw), jax.ShapeDtypeStruct((b, rw, l), BF16),
                   out(rw), out(rw), out(2 * d)]
        + [jax.ShapeDtypeStruct(w.shape, BF16) for w in sliced]
        + [jax.ShapeDtypeStruct((N_MOD, rows, d), F32),
           jax.ShapeDtypeStruct((b, rw, lc), BF16), jax.ShapeDtypeStruct((b, lc, rw), BF16)],
        scratch_shapes=[pltpu.VMEM(w_in.shape, BF16),
                        pltpu.VMEM((N_MOD, rows, d), F32),
                        pltpu.VMEM((rows, d), F32),
                        pltpu.VMEM((WEIGHT_STAGE_SLOTS, d, sc), F32),
                        pltpu.SemaphoreType.DMA((WEIGHT_STAGE_SLOTS,))],
        compiler_params=_params("arbitrary", "arbitrary"),
        name="proj",
    )(x, cosf, sinf, w_in, w_ada, *sliced, gain, gn_gain, c, c_ctx[None, :], b_ada, ctx)
    ncast = len(sliced)
    casts = [cc.reshape(w.shape) for cc, w in zip(res[6:6 + ncast], later_weights)]
    mods, kct, vc = res[6 + ncast:]
    return res[:6], casts, mods, kct, vc


def _fft_kernel(u_ref, kw_ref, twc_ref, tws_ref, w2_ref, cs_ref, o_ref, tr_ref, ti_ref, scr_ref,
                *, la, lb, gd, npb, scale):
    r = FFT_LA_TILE
    rows = lb * r
    fw = u_ref.shape[2]
    ngroups = fw // gd
    reps = fw // twc_ref.shape[2]
    pitch = scr_ref.shape[1] // la

    @pl.when(pl.program_id(0) == 0)
    def _():
        scr_ref[...] = jnp.zeros_like(scr_ref)

    for j in range(la // r):
        u = jnp.concatenate([u_ref[0, k * la + j * r:k * la + (j + 1) * r, :] for k in range(lb)], axis=0)
        half = lb // 2 + 1
        t = _dot(kw_ref[...], u)
        for k in range(lb):
            src = k if k < half else lb - k
            a = t[src * r:(src + 1) * r]
            b = t[(half + src) * r:(half + src + 1) * r]
            ct = _tile_lanes(twc_ref[j, k * r:(k + 1) * r, :], reps)
            st = _tile_lanes(tws_ref[j, k * r:(k + 1) * r, :], reps)
            if k < half:
                tr, ti = a * ct + b * st, b * ct - a * st
            else:
                tr, ti = a * ct - b * st, -(b * ct) - a * st
            tr_ref[k * la + j * r:k * la + (j + 1) * r, :] = tr.astype(BF16)
            ti_ref[k * la + j * r:k * la + (j + 1) * r, :] = ti.astype(BF16)

    for pb in range(lb // npb):
        xs = []
        for p in range(npb):
            lo = (pb * npb + p) * la
            t = jnp.concatenate([tr_ref[lo:lo + la, :], ti_ref[lo:lo + la, :]], axis=0)
            xs.append(_dot(w2_ref[...], t).astype(BF16))
        for gi in range(ngroups):
            cols = slice(gi * gd, (gi + 1) * gd)
            lhs = jnp.concatenate([jnp.concatenate([x[:la, cols], x[la:, cols]], axis=1) for x in xs], axis=0)
            fg = _dot(lhs, cs_ref[...]) * scale
            for p in range(npb):
                scr_ref[gi, pl.ds(p, la, stride=pitch), :] = fg[p * la:(p + 1) * la]
        full = jnp.concatenate([scr_ref[gi].reshape(la, pitch, gd)[:, :npb, :] for gi in range(ngroups)], axis=2)
        o_ref[0, :, pb * npb:(pb + 1) * npb, :] = full.astype(BF16)


def _fft(u, kw, twc, tws, w2, cs, la, lb, gd, scale):
    b, l, fw = u.shape
    npb = min(BF16_SUBLANES, lb)
    out = pl.pallas_call(
        functools.partial(_fft_kernel, la=la, lb=lb, gd=gd, npb=npb, scale=scale),
        grid=(b,),
        in_specs=[pl.BlockSpec((1, l, fw), lambda i: (i, 0, 0)),
                  _const_spec(kw.shape), _const_spec(twc.shape), _const_spec(tws.shape),
                  _const_spec(w2.shape), _const_spec(cs.shape)],
        out_specs=pl.BlockSpec((1, la, lb, fw), lambda i: (i, 0, 0, 0)),
        out_shape=jax.ShapeDtypeStruct((b, la, lb, fw), BF16),
        scratch_shapes=[pltpu.VMEM((l, fw), BF16), pltpu.VMEM((l, fw), BF16),
                        pltpu.VMEM((fw // gd, la * (npb + FFT_SCATTER_PAD), gd), F32)],
        compiler_params=_params("arbitrary"),
        name="fft",
    )(u, kw, twc, tws, w2, cs)
    return out.reshape(b, l, fw)


def _ret_kernel(q_ref, kt_ref, v_ref, sg_ref, kct_ref, vc_ref, lgl_ref, lgc_ref,
                z_ref, sf_ref, sb_ref, stf_ref, stb_ref, p_ref, o_ref,
                dall_ref, qdf_ref, qdb_ref, kdf_ref, kdb_ref, inc_ref, *, nchunk, unroll):
    c = RET_CHUNK
    gw = GROUP_W
    lc = vc_ref.shape[1]

    lgf, lgb = lgl_ref[0:1, :], lgl_ref[1:2, :]
    pos = lax.broadcasted_iota(jnp.int32, (c, gw), 0).astype(F32)
    qdf_ref[...] = jnp.exp(lgf * (pos + 1.0))
    qdb_ref[...] = jnp.exp(lgb * (c - pos))
    kdf_ref[...] = jnp.exp(lgf * (c - 1.0 - pos))
    kdb_ref[...] = jnp.exp(lgb * pos)
    cdf = jnp.exp(lgf * c)
    cdb = jnp.exp(lgb * c)
    cpos = lax.broadcasted_iota(jnp.int32, (lc, gw), 0).astype(F32)
    wcf = jnp.exp(lgf * (lc - 1.0 - cpos))
    wcb = jnp.exp(lgb * cpos)
    si = lax.broadcasted_iota(jnp.int32, (c, HEADS_PER_GROUP * c), 0)
    sj = lax.broadcasted_iota(jnp.int32, (c, HEADS_PER_GROUP * c), 1) & (c - 1)
    diff = (si - sj).astype(F32)
    dall_ref[...] = (jnp.where(diff >= 0, jnp.exp(lgc_ref[0:1, :] * jnp.maximum(diff, 0.0)), 0.0)
                     + jnp.where(diff <= 0, jnp.exp(lgc_ref[1:2, :] * jnp.maximum(-diff, 0.0)), 0.0))

    pw = STATE_BLOCK_W
    npair = gw // pw
    same_head_p = (lax.broadcasted_iota(jnp.int32, (pw, pw), 0) // RET_HEAD_DIM
                   == lax.broadcasted_iota(jnp.int32, (pw, pw), 1) // RET_HEAD_DIM)

    def diag_blocks(t):
        return jnp.where(same_head_p, t, 0.0)

    def weighted(t, w):
        return (t.astype(F32) * w).astype(BF16)

    def rows(ref, n):
        return ref[0, pl.ds(pl.multiple_of(n * c, c), c), :]

    def kt_chunk(n):
        return kt_ref[0, :, pl.ds(pl.multiple_of(n * c, c), c)]

    def state_update(st_ref, kt, vw, cd):
        for p in range(npair):
            cols = slice(p * pw, (p + 1) * pw)
            st_ref[p] = st_ref[p] * cd[:, cols] + diag_blocks(_dot(kt[cols, :], vw[:, cols]))

    def full_state(s_ref, n):
        zero = jnp.zeros((pw, pw), BF16)
        return jnp.concatenate(
            [jnp.concatenate([s_ref[n, p] if q == p else zero for q in range(npair)], axis=1)
             for p in range(npair)], axis=0)

    kct = kct_ref[0]
    vc = vc_ref[0]
    stf_ref[...] = jnp.zeros_like(stf_ref)
    stb_ref[...] = jnp.zeros_like(stb_ref)
    state_update(stf_ref, kct, weighted(vc, wcf), cdf)
    state_update(stb_ref, kct, weighted(vc, wcb), cdb)

    def increment_body(n, carry):
        ktn = kt_chunk(n)
        vn = rows(v_ref, n)
        vwf = weighted(vn, kdf_ref[...])
        vwb = weighted(vn, kdb_ref[...])
        for p in range(npair):
            cols = slice(p * pw, (p + 1) * pw)
            inc_ref[n, p] = _dot(ktn[cols, :], jnp.concatenate([vwf[:, cols], vwb[:, cols]], axis=1))
        return carry

    lax.fori_loop(0, nchunk, increment_body, 0, unroll=unroll)

    def scan_body(i, carry):
        nf = i
        nb = nchunk - 1 - i
        sf_ref[nf] = stf_ref[...].astype(BF16)
        sb_ref[nb] = stb_ref[...].astype(BF16)
        for p in range(npair):
            cols = slice(p * pw, (p + 1) * pw)
            stf_ref[p] = stf_ref[p] * cdf[:, cols] + diag_blocks(inc_ref[nf, p, :, :pw])
            stb_ref[p] = stb_ref[p] * cdb[:, cols] + diag_blocks(inc_ref[nb, p, :, pw:])
        return carry

    lax.fori_loop(0, nchunk, scan_body, 0, unroll=unroll)

    lane = lax.broadcasted_iota(jnp.int32, (c, gw), 1)
    sub = lax.broadcasted_iota(jnp.int32, (gw, c), 0)
    lane_masks = [(lane >= h * RET_HEAD_DIM) & (lane < (h + 1) * RET_HEAD_DIM)
                  for h in range(HEADS_PER_GROUP)]
    sub_masks = [(sub >= h * RET_HEAD_DIM) & (sub < (h + 1) * RET_HEAD_DIM)
                 for h in range(HEADS_PER_GROUP)]
    same_head = (lax.broadcasted_iota(jnp.int32, (gw, gw), 0) // RET_HEAD_DIM
                 == lax.broadcasted_iota(jnp.int32, (gw, gw), 1) // RET_HEAD_DIM)
    bd_mean = jnp.where(same_head, 1.0 / RET_HEAD_DIM, 0.0).astype(BF16)

    def score_body(n, carry):
        ktn = kt_chunk(n)
        zk = jnp.zeros_like(ktn)
        kbd = jnp.concatenate([jnp.where(m, ktn, zk) for m in sub_masks], axis=1)
        p_ref[n] = (_dot(rows(q_ref, n), kbd) * dall_ref[...]).astype(BF16)
        return carry

    lax.fori_loop(0, nchunk, score_body, 0, unroll=unroll)

    def mix_body(n, carry):
        qn = rows(q_ref, n)
        vn = rows(v_ref, n)
        zv = jnp.zeros_like(vn)
        vbd = jnp.concatenate([jnp.where(m, vn, zv) for m in lane_masks], axis=0)
        o_ref[n] = (_dot(p_ref[n], vbd) + qdf_ref[...] * _dot(qn, full_state(sf_ref, n))
                    + qdb_ref[...] * _dot(qn, full_state(sb_ref, n)))
        return carry

    lax.fori_loop(0, nchunk, mix_body, 0, unroll=unroll)

    def norm_body(n, carry):
        o = o_ref[n]
        ms = _dot((o * o).astype(BF16), bd_mean)
        z = rows(sg_ref, n).astype(F32) * (o * lax.rsqrt(ms + EPS))
        z_ref[0, pl.ds(pl.multiple_of(n * c, c), c), :] = z.astype(BF16)
        return carry

    lax.fori_loop(0, nchunk, norm_body, 0, unroll=unroll)


def _retention(q, kt, v, sg, kct, vc, log_gamma):
    b, l, rw = q.shape
    lc = vc.shape[1]
    c = RET_CHUNK
    gw = GROUP_W
    hpg = HEADS_PER_GROUP
    ng = rw // gw
    nchunk = l // c
    pw = STATE_BLOCK_W
    npair = gw // pw
    assert c & (c - 1) == 0
    lg = log_gamma.reshape(2, ng, hpg).transpose(1, 0, 2)
    lg_lane = jnp.repeat(lg, RET_HEAD_DIM, axis=2)
    lg_col = jnp.repeat(lg, c, axis=2)
    tok = pl.BlockSpec((1, l, gw), lambda i, j: (i, 0, j))
    tokt = pl.BlockSpec((1, gw, l), lambda i, j: (i, j, 0))
    grp = lambda r, width: pl.BlockSpec((None, r, width), lambda i, j: (j, 0, 0))
    return pl.pallas_call(
        functools.partial(_ret_kernel, nchunk=nchunk, unroll=math.gcd(nchunk, RET_UNROLL)),
        grid=(b, ng),
        in_specs=[
            tok, tokt, tok, tok,
            pl.BlockSpec((1, gw, lc), lambda i, j: (i, j, 0)),
            pl.BlockSpec((1, lc, gw), lambda i, j: (i, 0, j)),
            grp(2, gw), grp(2, hpg * c),
        ],
        out_specs=tok,
        out_shape=jax.ShapeDtypeStruct((b, l, rw), BF16),
        scratch_shapes=[pltpu.VMEM((nchunk, npair, pw, pw), BF16), pltpu.VMEM((nchunk, npair, pw, pw), BF16),
                        pltpu.VMEM((npair, pw, pw), F32), pltpu.VMEM((npair, pw, pw), F32),
                        pltpu.VMEM((nchunk, c, hpg * c), BF16),
                        pltpu.VMEM((nchunk, c, gw), F32),
                        pltpu.VMEM((c, hpg * c), F32)] + [pltpu.VMEM((c, gw), F32)] * 4
        + [pltpu.VMEM((nchunk, npair, pw, 2 * pw), F32)],
        compiler_params=_params("arbitrary", "arbitrary"),
        name="ret",
    )(q, kt, v, sg, kct, vc, lg_lane, lg_col)


def _out_kernel(w4_ref, wr_ref, wo_ref, w1_ref, w2_ref, x_ref, mod_ref, fm_ref, z_ref, gab_ref,
                gain2_ref, fgain_ref, o_ref, *, ff_chunk):
    mod = mod_ref[:, pl.ds(pl.program_id(0), 1), :]
    g1, sh2, sc2, g2 = mod[2], mod[3], mod[4], mod[5]
    y_four = _dot(fm_ref[0], w4_ref[...])
    y_ret = _dot(z_ref[0], wr_ref[...])
    d = x_ref.shape[2]
    y = (jax.nn.sigmoid(gab_ref[0, :, :d].astype(F32)) * y_four
         + jax.nn.sigmoid(gab_ref[0, :, d:].astype(F32)) * y_ret)
    x1 = x_ref[0] + g1 * _dot(y.astype(BF16), wo_ref[...])
    h2 = _norm_mod(x1, gain2_ref[...], sh2, sc2).astype(BF16)
    dff = w1_ref.shape[1]
    acc = None
    for lo in range(0, dff, ff_chunk):
        hid = jnp.maximum(_dot(h2, w1_ref[:, lo:lo + ff_chunk]), 0.0)
        part = _dot((hid * hid).astype(BF16), w2_ref[lo:lo + ff_chunk, :])
        acc = part if acc is None else acc + part
    x2 = x1 + g2 * acc
    ms = jnp.mean(x2 * x2, axis=-1, keepdims=True)
    o_ref[0] = x2 * lax.rsqrt(ms + EPS) * fgain_ref[...]


def _out(x, mods, fm, z, gab, w4, wr, wo, gain2, w1, w2, fgain):
    b, l, d = x.shape
    tm = min(TOKEN_TILE, l)
    tok = lambda width: pl.BlockSpec((1, tm, width), lambda i, j: (i, j, 0))
    weights = (w4, wr, wo, w1, w2)
    return pl.pallas_call(
        functools.partial(_out_kernel, ff_chunk=min(1024, w1.shape[1])),
        grid=(b, l // tm),
        in_specs=[_const_spec(w.shape) for w in weights] + [
            tok(d),
            _const_spec(mods.shape),
            tok(fm.shape[2]), tok(z.shape[2]), tok(2 * d),
            _const_spec((1, d)), _const_spec((1, d)),
        ],
        out_specs=tok(d),
        out_shape=jax.ShapeDtypeStruct((b, l, d), F32),
        compiler_params=_params("arbitrary", "arbitrary"),
        name="out",
    )(*weights, x, mods, fm, z, gab, gain2, fgain)


def _dft_tables(l, gd):
    la = FFT_LA
    lb = l // la

    def cs(n):
        idx = np.arange(n)
        ang = 2.0 * np.pi * ((idx[:, None] * idx[None, :]) % n) / n
        return np.cos(ang), np.sin(ang)

    cb, sb = cs(lb)
    eye = np.eye(FFT_LA_TILE)
    half = lb // 2 + 1
    kw = np.concatenate([np.kron(cb[:half], eye), -np.kron(sb[:half], eye)], axis=0)
    ca, sa = cs(la)
    w2 = np.block([[ca, sa], [-sa, ca]])
    cc, sc = cs(gd)
    chan = np.concatenate([cc, sc], axis=0)
    tw = 2.0 * np.pi * (np.arange(lb)[:, None] * np.arange(la)[None, :]) / l
    tw = tw.reshape(lb, la // FFT_LA_TILE, FFT_LA_TILE).transpose(1, 0, 2).reshape(la // FFT_LA_TILE, -1)
    twc = np.repeat(np.cos(tw)[:, :, None], LANES, axis=2)
    tws = np.repeat(np.sin(tw)[:, :, None], LANES, axis=2)
    as_bf = lambda a: jnp.asarray(a, dtype=F32).astype(BF16)
    return as_bf(kw), as_bf(w2), as_bf(chan), jnp.asarray(twc, F32), jnp.asarray(tws, F32), la, lb


def _rope_tables(l):
    f32 = np.float32
    nf = RET_HEAD_DIM // 4
    inv = np.power(f32(ROPE_BASE), -np.arange(nf, dtype=f32) / f32(nf)).astype(f32)
    rows = l // GRID_W
    r, cc = np.meshgrid(np.arange(rows, dtype=f32), np.arange(GRID_W, dtype=f32), indexing="ij")
    ang = np.concatenate([r.reshape(-1)[:, None] * inv, cc.reshape(-1)[:, None] * inv], axis=-1).astype(f32)
    cos, sin = np.cos(ang).astype(f32), np.sin(ang).astype(f32)
    cos_h = np.concatenate([cos, cos], axis=1)
    sin_h = np.concatenate([-sin, sin], axis=1)
    reps = LANES // RET_HEAD_DIM
    return jnp.asarray(np.tile(cos_h, (1, reps))), jnp.asarray(np.tile(sin_h, (1, reps)))


def kernel(x, c, ctx, c_ctx, w_ada, b_ada, norm1_gain, w_in, four_w_out, ret_decay_logit,
           ret_gn_gain, ret_w_out, w_out, norm2_gain, w_mlp1, w_mlp2, final_gain):
    assert w_ada.shape[0] == 1, "single-layer block"
    b, l, d = x.shape
    lc = ctx.shape[1]
    fw = four_w_out.shape[1]
    rw = ret_w_out.shape[1]
    gd = fw // FOUR_GROUPS
    assert l % FFT_LA == 0 and l % RET_CHUNK == 0 and rw % GROUP_W == 0

    kw, w2, chan, twc, tws, la, lb = _dft_tables(l, gd)
    cosf, sinf = _rope_tables(l)

    later = [four_w_out[0], ret_w_out[0], w_out[0], w_mlp1[0], w_mlp2[0]]
    (u, q, kt, v, sg, gab), later_b, mods, kct, vc = _proj(
        x, c, c_ctx, w_ada[0], b_ada, ctx, norm1_gain, w_in[0], cosf, sinf, ret_gn_gain, later, fw, rw)

    fm = _fft(u, kw, twc, tws, w2, chan, la, lb, gd, 1.0 / math.sqrt(l * gd))

    log_gamma = jax.nn.log_sigmoid(ret_decay_logit[0].astype(F32))
    z = _retention(q, kt, v, sg, kct, vc, log_gamma)

    w4, wr, wo, w1, w2 = later_b
    return _out(x, mods, fm, z, gab, w4, wr, wo, norm2_gain, w1, w2, final_gain[None, :])
```

```python
import functools
import math

import jax
import jax.numpy as jnp
import numpy as np
from jax import lax
from jax.experimental import pallas as pl
from jax.experimental.pallas import tpu as pltpu

F32 = jnp.float32
BF16 = jnp.bfloat16

GRID_W = 64
FOUR_GROUPS = 4
RET_HEAD_DIM = 64
N_MOD = 6
ROPE_BASE = 10000.0
EPS = 1e-6

LANES = 128
MXU_DIM = 256
F32_SUBLANES = 8
BF16_SUBLANES = 16
VMEM_LIMIT_BYTES = 56 * 1024 * 1024

RET_CHUNK = 128
RET_UNROLL = 32
HEADS_PER_GROUP = MXU_DIM // RET_HEAD_DIM
GROUP_W = HEADS_PER_GROUP * RET_HEAD_DIM
STATE_BLOCK_W = LANES
FFT_LA = 128
FFT_LA_TILE = BF16_SUBLANES
FFT_SCATTER_PAD = F32_SUBLANES
TOKEN_TILE = 512
PROJ_TOKEN_TILE = 512
WEIGHT_STAGE_COLS = 512
WEIGHT_STAGE_SLOTS = 6


def _dot(a, b):
    return jnp.dot(a, b, preferred_element_type=F32)


def _norm_mod(x, gain, shift, scale):
    ms = jnp.mean(x * x, axis=-1, keepdims=True)
    y = x * lax.rsqrt(ms + EPS) * gain
    return y * (1.0 + scale) + shift


def _tile_lanes(t, reps):
    return jnp.concatenate([t] * reps, axis=1) if reps > 1 else t


def _const_spec(shape):
    nd = len(shape)
    return pl.BlockSpec(shape, lambda *_: (0,) * nd, pipeline_mode=pl.Buffered(1))


def _params(*sem):
    return pltpu.CompilerParams(dimension_semantics=sem, vmem_limit_bytes=VMEM_LIMIT_BYTES)


def _is_first_step(grid_rank):
    first = pl.program_id(0) == 0
    for axis in range(1, grid_rank):
        first = first & (pl.program_id(axis) == 0)
    return first


def _proj_prepare(before_body, c_ref, cctx_ref, wada_hbm, bada_ref, w_hbm, ctx_ref, gain_ref,
                  mods_ref, kct_ref, vc_ref, w_ref, mod_ref, cv_ref, stage_ref, sem_ref, *, fw, rw, scale):
    nb = c_ref.shape[0]
    d = w_ref.shape[0]
    if before_body:
        cv_ref[...] = jnp.zeros_like(cv_ref)
        cv_ref[0:nb, :] = c_ref[...]
        cv_ref[nb:nb + 1, :] = cctx_ref[...]
    cv = cv_ref[...]
    s = (cv * jax.nn.sigmoid(cv)).astype(BF16)

    slots, _, sc = stage_ref.shape
    ada = [(wada_hbm, c0) for c0 in range(0, wada_hbm.shape[1], sc)]
    win = [(w_hbm, c0) for c0 in range(0, w_hbm.shape[1], sc)]
    body_mods = 2 * d // sc
    chunks = ada[:body_mods] + win + ada[body_mods:]
    split = body_mods + len(win)
    ctx_after = body_mods + (fw + 3 * rw) // sc - 1

    def copy(k):
        src, c0 = chunks[k]
        slot = k % slots
        return pltpu.make_async_copy(src.at[:, pl.ds(c0, sc)], stage_ref.at[slot], sem_ref.at[slot])

    if before_body:
        for k in range(min(slots - 1, len(chunks))):
            copy(k).start()
    for k in range(0, split) if before_body else range(split, len(chunks)):
        src, c0 = chunks[k]
        if k + slots - 1 < len(chunks):
            copy(k + slots - 1).start()
        copy(k).wait()
        blk = stage_ref[k % slots].astype(BF16)
        if src is wada_hbm:
            j, off = divmod(c0, d)
            hw = sc // 2
            mod_ref[j, :, off:off + hw] = _dot(s, blk[:, :hw]) + bada_ref[:, c0:c0 + hw]
            mod_ref[j, :, off + hw:off + sc] = _dot(s, blk[:, hw:]) + bada_ref[:, c0 + hw:c0 + sc]
        else:
            w_ref[:, c0:c0 + sc] = blk
        if k == ctx_after:
            for i in range(nb):
                h = _norm_mod(ctx_ref[i], gain_ref[...],
                              mod_ref[0, nb:nb + 1, :], mod_ref[1, nb:nb + 1, :]).astype(BF16)
                kct_ref[i] = (_dot(h, w_ref[:, fw + rw:fw + 2 * rw]) * scale).T.astype(BF16)
                vc_ref[i] = _dot(h, w_ref[:, fw + 2 * rw:fw + 3 * rw]).astype(BF16)
    if not before_body:
        mods_ref[...] = mod_ref[...]


def _proj_kernel(*refs, fw, rw, d, scale, ncast):
    x_ref, cos_ref, sin_ref, w_hbm, wada_hbm = refs[:5]
    cast_in = refs[5:5 + ncast]
    gain_ref, gng_ref, c_ref, cctx_ref, bada_ref, ctx_ref = refs[5 + ncast:11 + ncast]
    outs = refs[11 + ncast:]
    u_ref, q_ref, kt_ref, v_ref, sg_ref, gab_ref = outs[:6]
    cast_out = outs[6:6 + ncast]
    mods_ref, kct_ref, vc_ref = outs[6 + ncast:9 + ncast]
    w_ref, mod_ref, cv_ref, stage_ref, sem_ref = outs[9 + ncast:]

    prepare = functools.partial(
        _proj_prepare, c_ref=c_ref, cctx_ref=cctx_ref, wada_hbm=wada_hbm, bada_ref=bada_ref, w_hbm=w_hbm,
        ctx_ref=ctx_ref, gain_ref=gain_ref, mods_ref=mods_ref, kct_ref=kct_ref, vc_ref=vc_ref, w_ref=w_ref,
        mod_ref=mod_ref, cv_ref=cv_ref, stage_ref=stage_ref, sem_ref=sem_ref, fw=fw, rw=rw, scale=scale)
    pl.when(_is_first_step(2))(functools.partial(prepare, True))

    for src, dst in zip(cast_in, cast_out):
        dst[...] = src[...].astype(BF16)

    mod = mod_ref[0:2, pl.ds(pl.program_id(0), 1), :]
    h = _norm_mod(x_ref[0], gain_ref[...], mod[0], mod[1]).astype(BF16)
    tm = h.shape[0]

    def proj(lo, width):
        return _dot(h, w_ref[:, lo:lo + width])

    u_ref[0] = proj(0, fw).astype(BF16)

    reps = rw // cos_ref.shape[1]
    cosf = _tile_lanes(cos_ref[...], reps)
    sinf = _tile_lanes(sin_ref[...], reps)
    lane = lax.broadcasted_iota(jnp.int32, (tm, rw), 1)
    first_half = (lane & (RET_HEAD_DIM // 2)) == 0

    def rope(t):
        rot = jnp.where(first_half,
                        pltpu.roll(t, rw - RET_HEAD_DIM // 2, 1),
                        pltpu.roll(t, RET_HEAD_DIM // 2, 1))
        return t * cosf + rot * sinf

    q_ref[0] = rope(proj(fw, rw)).astype(BF16)
    kt_ref[0] = (rope(proj(fw + rw, rw)) * scale).T.astype(BF16)
    v_ref[0] = proj(fw + 2 * rw, rw).astype(BF16)
    g = proj(fw + 3 * rw, rw)
    sg_ref[0] = (g * jax.nn.sigmoid(g) * gng_ref[...]).astype(BF16)
    gab_ref[0, :, :d] = proj(fw + 4 * rw, d).astype(BF16)
    gab_ref[0, :, d:] = proj(fw + 4 * rw + d, d).astype(BF16)

    pl.when(_is_first_step(2))(functools.partial(prepare, False))


def _proj(x, c, c_ctx, w_ada, b_ada, ctx, gain, w_in, cosf, sinf, gn_gain, later_weights, fw, rw):
    b, l, d = x.shape
    lc = ctx.shape[1]
    tm = min(PROJ_TOKEN_TILE, l)
    nj = l // tm
    steps = b * nj
    rows = -(-(b + 1) // F32_SUBLANES) * F32_SUBLANES
    sc = WEIGHT_STAGE_COLS
    assert w_ada.shape == (d, N_MOD * d) and d % sc == 0 and w_in.shape[1] % sc == 0
    tok = lambda width: pl.BlockSpec((1, tm, width), lambda i, j: (i, j, 0))
    out = lambda width: jax.ShapeDtypeStruct((b, l, width), BF16)
    sliced = [w.reshape(steps, w.shape[0] // steps, w.shape[1]) for w in later_weights]
    for w in sliced:
        assert w.shape[1] % BF16_SUBLANES == 0
    cast_spec = lambda w: pl.BlockSpec((1,) + w.shape[1:], lambda i, j: (i * nj + j, 0, 0))
    whole = lambda shape: pl.BlockSpec(shape, lambda i, j: (0,) * len(shape))
    res = pl.pallas_call(
        functools.partial(_proj_kernel, fw=fw, rw=rw, d=d, scale=RET_HEAD_DIM ** -0.5, ncast=len(sliced)),
        grid=(b, nj),
        in_specs=[
            tok(d),
            pl.BlockSpec((tm, cosf.shape[1]), lambda i, j: (j, 0)),
            pl.BlockSpec((tm, sinf.shape[1]), lambda i, j: (j, 0)),
            pl.BlockSpec(memory_space=pl.ANY),
            pl.BlockSpec(memory_space=pl.ANY),
        ] + [cast_spec(w) for w in sliced] + [
            _const_spec((1, d)), _const_spec((1, rw)), _const_spec((b, d)), _const_spec((1, d)),
            _const_spec((1, N_MOD * d)), _const_spec((b, lc, d)),
        ],
        out_specs=[tok(fw), tok(rw), pl.BlockSpec((1, rw, tm), lambda i, j: (i, 0, j)),
                   tok(rw), tok(rw), tok(2 * d)] + [cast_spec(w) for w in sliced]
        + [whole((N_MOD, rows, d)), whole((b, rw, lc)), whole((b, lc, rw))],
        out_shape=[out(fw), out(rw), jax.ShapeDtypeStruct((b, rw, l), BF16),
                   out(rw), out(rw), out(2 * d)]
        + [jax.ShapeDtypeStruct(w.shape, BF16) for w in sliced]
        + [jax.ShapeDtypeStruct((N_MOD, rows, d), F32),
           jax.ShapeDtypeStruct((b, rw, lc), BF16), jax.ShapeDtypeStruct((b, lc, rw), BF16)],
        scratch_shapes=[pltpu.VMEM(w_in.shape, BF16),
                        pltpu.VMEM((N_MOD, rows, d), F32),
                        pltpu.VMEM((rows, d), F32),
                        pltpu.VMEM((WEIGHT_STAGE_SLOTS, d, sc), F32),
                        pltpu.SemaphoreType.DMA((WEIGHT_STAGE_SLOTS,))],
        compiler_params=_params("arbitrary", "arbitrary"),
        name="proj",
    )(x, cosf, sinf, w_in, w_ada, *sliced, gain, gn_gain, c, c_ctx[None, :], b_ada, ctx)
    ncast = len(sliced)
    casts = [cc.reshape(w.shape) for cc, w in zip(res[6:6 + ncast], later_weights)]
    mods, kct, vc = res[6 + ncast:]
    return res[:6], casts, mods, kct, vc


def _fft_kernel(u_ref, kw_ref, twc_ref, tws_ref, w2_ref, cs_ref, o_ref, tr_ref, ti_ref, scr_ref,
                *, la, lb, gd, npb, scale):
    r = FFT_LA_TILE
    rows = lb * r
    fw = u_ref.shape[2]
    ngroups = fw // gd
    reps = fw // twc_ref.shape[2]
    pitch = scr_ref.shape[1] // la

    @pl.when(pl.program_id(0) == 0)
    def _():
        scr_ref[...] = jnp.zeros_like(scr_ref)

    for j in range(la // r):
        u = jnp.concatenate([u_ref[0, k * la + j * r:k * la + (j + 1) * r, :] for k in range(lb)], axis=0)
        half = lb // 2 + 1
        t = _dot(kw_ref[...], u)
        for k in range(lb):
            src = k if k < half else lb - k
            a = t[src * r:(src + 1) * r]
            b = t[(half + src) * r:(half + src + 1) * r]
            ct = _tile_lanes(twc_ref[j, k * r:(k + 1) * r, :], reps)
            st = _tile_lanes(tws_ref[j, k * r:(k + 1) * r, :], reps)
            if k < half:
                tr, ti = a * ct + b * st, b * ct - a * st
            else:
                tr, ti = a * ct - b * st, -(b * ct) - a * st
            tr_ref[k * la + j * r:k * la + (j + 1) * r, :] = tr.astype(BF16)
            ti_ref[k * la + j * r:k * la + (j + 1) * r, :] = ti.astype(BF16)

    for pb in range(lb // npb):
        xs = []
        for p in range(npb):
            lo = (pb * npb + p) * la
            t = jnp.concatenate([tr_ref[lo:lo + la, :], ti_ref[lo:lo + la, :]], axis=0)
            xs.append(_dot(w2_ref[...], t).astype(BF16))
        for gi in range(ngroups):
            cols = slice(gi * gd, (gi + 1) * gd)
            lhs = jnp.concatenate([jnp.concatenate([x[:la, cols], x[la:, cols]], axis=1) for x in xs], axis=0)
            fg = _dot(lhs, cs_ref[...]) * scale
            for p in range(npb):
                scr_ref[gi, pl.ds(p, la, stride=pitch), :] = fg[p * la:(p + 1) * la]
        full = jnp.concatenate([scr_ref[gi].reshape(la, pitch, gd)[:, :npb, :] for gi in range(ngroups)], axis=2)
        o_ref[0, :, pb * npb:(pb + 1) * npb, :] = full.astype(BF16)


def _fft(u, kw, twc, tws, w2, cs, la, lb, gd, scale):
    b, l, fw = u.shape
    npb = min(BF16_SUBLANES, lb)
    out = pl.pallas_call(
        functools.partial(_fft_kernel, la=la, lb=lb, gd=gd, npb=npb, scale=scale),
        grid=(b,),
        in_specs=[pl.BlockSpec((1, l, fw), lambda i: (i, 0, 0)),
                  _const_spec(kw.shape), _const_spec(twc.shape), _const_spec(tws.shape),
                  _const_spec(w2.shape), _const_spec(cs.shape)],
        out_specs=pl.BlockSpec((1, la, lb, fw), lambda i: (i, 0, 0, 0)),
        out_shape=jax.ShapeDtypeStruct((b, la, lb, fw), BF16),
        scratch_shapes=[pltpu.VMEM((l, fw), BF16), pltpu.VMEM((l, fw), BF16),
                        pltpu.VMEM((fw // gd, la * (npb + FFT_SCATTER_PAD), gd), F32)],
        compiler_params=_params("arbitrary"),
        name="fft",
    )(u, kw, twc, tws, w2, cs)
    return out.reshape(b, l, fw)


def _ret_kernel(q_ref, kt_ref, v_ref, sg_ref, kct_ref, vc_ref, lgl_ref, lgc_ref,
                z_ref, sf_ref, sb_ref, stf_ref, stb_ref, p_ref, o_ref,
                dall_ref, qdf_ref, qdb_ref, kdf_ref, kdb_ref, inc_ref, *, nchunk, unroll):
    c = RET_CHUNK
    gw = GROUP_W
    lc = vc_ref.shape[1]

    lgf, lgb = lgl_ref[0:1, :], lgl_ref[1:2, :]
    pos = lax.broadcasted_iota(jnp.int32, (c, gw), 0).astype(F32)
    qdf_ref[...] = jnp.exp(lgf * (pos + 1.0))
    qdb_ref[...] = jnp.exp(lgb * (c - pos))
    kdf_ref[...] = jnp.exp(lgf * (c - 1.0 - pos))
    kdb_ref[...] = jnp.exp(lgb * pos)
    cdf = jnp.exp(lgf * c)
    cdb = jnp.exp(lgb * c)
    cpos = lax.broadcasted_iota(jnp.int32, (lc, gw), 0).astype(F32)
    wcf = jnp.exp(lgf * (lc - 1.0 - cpos))
    wcb = jnp.exp(lgb * cpos)
    si = lax.broadcasted_iota(jnp.int32, (c, HEADS_PER_GROUP * c), 0)
    sj = lax.broadcasted_iota(jnp.int32, (c, HEADS_PER_GROUP * c), 1) & (c - 1)
    diff = (si - sj).astype(F32)
    dall_ref[...] = (jnp.where(diff >= 0, jnp.exp(lgc_ref[0:1, :] * jnp.maximum(diff, 0.0)), 0.0)
                     + jnp.where(diff <= 0, jnp.exp(lgc_ref[1:2, :] * jnp.maximum(-diff, 0.0)), 0.0))

    pw = STATE_BLOCK_W
    npair = gw // pw
    same_head_p = (lax.broadcasted_iota(jnp.int32, (pw, pw), 0) // RET_HEAD_DIM
                   == lax.broadcasted_iota(jnp.int32, (pw, pw), 1) // RET_HEAD_DIM)

    def diag_blocks(t):
        return jnp.where(same_head_p, t, 0.0)

    def weighted(t, w):
        return (t.astype(F32) * w).astype(BF16)

    def rows(ref, n):
        return ref[0, pl.ds(pl.multiple_of(n * c, c), c), :]

    def kt_chunk(n):
        return kt_ref[0, :, pl.ds(pl.multiple_of(n * c, c), c)]

    def state_update(st_ref, kt, vw, cd):
        for p in range(npair):
            cols = slice(p * pw, (p + 1) * pw)
            st_ref[p] = st_ref[p] * cd[:, cols] + diag_blocks(_dot(kt[cols, :], vw[:, cols]))

    def full_state(s_ref, n):
        zero = jnp.zeros((pw, pw), BF16)
        return jnp.concatenate(
            [jnp.concatenate([s_ref[n, p] if q == p else zero for q in range(npair)], axis=1)
             for p in range(npair)], axis=0)

    kct = kct_ref[0]
    vc = vc_ref[0]
    stf_ref[...] = jnp.zeros_like(stf_ref)
    stb_ref[...] = jnp.zeros_like(stb_ref)
    state_update(stf_ref, kct, weighted(vc, wcf), cdf)
    state_update(stb_ref, kct, weighted(vc, wcb), cdb)

    def increment_body(n, carry):
        ktn = kt_chunk(n)
        vn = rows(v_ref, n)
        vwf = weighted(vn, kdf_ref[...])
        vwb = weighted(vn, kdb_ref[...])
        for p in range(npair):
            cols = slice(p * pw, (p + 1) * pw)
            inc_ref[n, p] = _dot(ktn[cols, :], jnp.concatenate([vwf[:, cols], vwb[:, cols]], axis=1))
        return carry

    lax.fori_loop(0, nchunk, increment_body, 0, unroll=unroll)

    def scan_body(i, carry):
        nf = i
        nb = nchunk - 1 - i
        sf_ref[nf] = stf_ref[...].astype(BF16)
        sb_ref[nb] = stb_ref[...].astype(BF16)
        for p in range(npair):
            cols = slice(p * pw, (p + 1) * pw)
            stf_ref[p] = stf_ref[p] * cdf[:, cols] + diag_blocks(inc_ref[nf, p, :, :pw])
            stb_ref[p] = stb_ref[p] * cdb[:, cols] + diag_blocks(inc_ref[nb, p, :, pw:])
        return carry

    lax.fori_loop(0, nchunk, scan_body, 0, unroll=unroll)

    lane = lax.broadcasted_iota(jnp.int32, (c, gw), 1)
    sub = lax.broadcasted_iota(jnp.int32, (gw, c), 0)
    lane_masks = [(lane >= h * RET_HEAD_DIM) & (lane < (h + 1) * RET_HEAD_DIM)
                  for h in range(HEADS_PER_GROUP)]
    sub_masks = [(sub >= h * RET_HEAD_DIM) & (sub < (h + 1) * RET_HEAD_DIM)
                 for h in range(HEADS_PER_GROUP)]
    same_head = (lax.broadcasted_iota(jnp.int32, (gw, gw), 0) // RET_HEAD_DIM
                 == lax.broadcasted_iota(jnp.int32, (gw, gw), 1) // RET_HEAD_DIM)
    bd_mean = jnp.where(same_head, 1.0 / RET_HEAD_DIM, 0.0).astype(BF16)

    def score_body(n, carry):
        ktn = kt_chunk(n)
        zk = jnp.zeros_like(ktn)
        kbd = jnp.concatenate([jnp.where(m, ktn, zk) for m in sub_masks], axis=1)
        p_ref[n] = (_dot(rows(q_ref, n), kbd) * dall_ref[...]).astype(BF16)
        return carry

    lax.fori_loop(0, nchunk, score_body, 0, unroll=unroll)

    def intra_body(n, carry):
        vn = rows(v_ref, n)
        zv = jnp.zeros_like(vn)
        vbd = jnp.concatenate([jnp.where(m, vn, zv) for m in lane_masks], axis=0)
        o_ref[n] = _dot(p_ref[n], vbd)
        return carry

    lax.fori_loop(0, nchunk, intra_body, 0, unroll=unroll)

    def inter_body(n, carry):
        qn = rows(q_ref, n)
        o_ref[n] = (o_ref[n] + qdf_ref[...] * _dot(qn, full_state(sf_ref, n))
                    + qdb_ref[...] * _dot(qn, full_state(sb_ref, n)))
        return carry

    lax.fori_loop(0, nchunk, inter_body, 0, unroll=unroll)

    def norm_body(n, carry):
        o = o_ref[n]
        ms = _dot((o * o).astype(BF16), bd_mean)
        z = rows(sg_ref, n).astype(F32) * (o * lax.rsqrt(ms + EPS))
        z_ref[0, pl.ds(pl.multiple_of(n * c, c), c), :] = z.astype(BF16)
        return carry

    lax.fori_loop(0, nchunk, norm_body, 0, unroll=unroll)


def _retention(q, kt, v, sg, kct, vc, log_gamma):
    b, l, rw = q.shape
    lc = vc.shape[1]
    c = RET_CHUNK
    gw = GROUP_W
    hpg = HEADS_PER_GROUP
    ng = rw // gw
    nchunk = l // c
    pw = STATE_BLOCK_W
    npair = gw // pw
    assert c & (c - 1) == 0
    lg = log_gamma.reshape(2, ng, hpg).transpose(1, 0, 2)
    lg_lane = jnp.repeat(lg, RET_HEAD_DIM, axis=2)
    lg_col = jnp.repeat(lg, c, axis=2)
    tok = pl.BlockSpec((1, l, gw), lambda i, j: (i, 0, j))
    tokt = pl.BlockSpec((1, gw, l), lambda i, j: (i, j, 0))
    grp = lambda r, width: pl.BlockSpec((None, r, width), lambda i, j: (j, 0, 0))
    return pl.pallas_call(
        functools.partial(_ret_kernel, nchunk=nchunk, unroll=math.gcd(nchunk, RET_UNROLL)),
        grid=(b, ng),
        in_specs=[
            tok, tokt, tok, tok,
            pl.BlockSpec((1, gw, lc), lambda i, j: (i, j, 0)),
            pl.BlockSpec((1, lc, gw), lambda i, j: (i, 0, j)),
            grp(2, gw), grp(2, hpg * c),
        ],
        out_specs=tok,
        out_shape=jax.ShapeDtypeStruct((b, l, rw), BF16),
        scratch_shapes=[pltpu.VMEM((nchunk, npair, pw, pw), BF16), pltpu.VMEM((nchunk, npair, pw, pw), BF16),
                        pltpu.VMEM((npair, pw, pw), F32), pltpu.VMEM((npair, pw, pw), F32),
                        pltpu.VMEM((nchunk, c, hpg * c), BF16),
                        pltpu.VMEM((nchunk, c, gw), F32),
                        pltpu.VMEM((c, hpg * c), F32)] + [pltpu.VMEM((c, gw), F32)] * 4
        + [pltpu.VMEM((nchunk, npair, pw, 2 * pw), F32)],
        compiler_params=_params("arbitrary", "arbitrary"),
        name="ret",
    )(q, kt, v, sg, kct, vc, lg_lane, lg_col)


def _out_kernel(w4_ref, wr_ref, wo_ref, w1_ref, w2_ref, x_ref, mod_ref, fm_ref, z_ref, gab_ref,
                gain2_ref, fgain_ref, o_ref, *, ff_chunk):
    mod = mod_ref[:, pl.ds(pl.program_id(0), 1), :]
    g1, sh2, sc2, g2 = mod[2], mod[3], mod[4], mod[5]
    y_four = _dot(fm_ref[0], w4_ref[...])
    y_ret = _dot(z_ref[0], wr_ref[...])
    d = x_ref.shape[2]
    y = (jax.nn.sigmoid(gab_ref[0, :, :d].astype(F32)) * y_four
         + jax.nn.sigmoid(gab_ref[0, :, d:].astype(F32)) * y_ret)
    x1 = x_ref[0] + g1 * _dot(y.astype(BF16), wo_ref[...])
    h2 = _norm_mod(x1, gain2_ref[...], sh2, sc2).astype(BF16)
    dff = w1_ref.shape[1]
    acc = None
    for lo in range(0, dff, ff_chunk):
        hid = jnp.maximum(_dot(h2, w1_ref[:, lo:lo + ff_chunk]), 0.0)
        part = _dot((hid * hid).astype(BF16), w2_ref[lo:lo + ff_chunk, :])
        acc = part if acc is None else acc + part
    x2 = x1 + g2 * acc
    ms = jnp.mean(x2 * x2, axis=-1, keepdims=True)
    o_ref[0] = x2 * lax.rsqrt(ms + EPS) * fgain_ref[...]


def _out(x, mods, fm, z, gab, w4, wr, wo, gain2, w1, w2, fgain):
    b, l, d = x.shape
    tm = min(TOKEN_TILE, l)
    tok = lambda width: pl.BlockSpec((1, tm, width), lambda i, j: (i, j, 0))
    weights = (w4, wr, wo, w1, w2)
    return pl.pallas_call(
        functools.partial(_out_kernel, ff_chunk=min(1024, w1.shape[1])),
        grid=(b, l // tm),
        in_specs=[_const_spec(w.shape) for w in weights] + [
            tok(d),
            _const_spec(mods.shape),
            tok(fm.shape[2]), tok(z.shape[2]), tok(2 * d),
            _const_spec((1, d)), _const_spec((1, d)),
        ],
        out_specs=tok(d),
        out_shape=jax.ShapeDtypeStruct((b, l, d), F32),
        compiler_params=_params("arbitrary", "arbitrary"),
        name="out",
    )(*weights, x, mods, fm, z, gab, gain2, fgain)


def _dft_tables(l, gd):
    la = FFT_LA
    lb = l // la

    def cs(n):
        idx = np.arange(n)
        ang = 2.0 * np.pi * ((idx[:, None] * idx[None, :]) % n) / n
        return np.cos(ang), np.sin(ang)

    cb, sb = cs(lb)
    eye = np.eye(FFT_LA_TILE)
    half = lb // 2 + 1
    kw = np.concatenate([np.kron(cb[:half], eye), -np.kron(sb[:half], eye)], axis=0)
    ca, sa = cs(la)
    w2 = np.block([[ca, sa], [-sa, ca]])
    cc, sc = cs(gd)
    chan = np.concatenate([cc, sc], axis=0)
    tw = 2.0 * np.pi * (np.arange(lb)[:, None] * np.arange(la)[None, :]) / l
    tw = tw.reshape(lb, la // FFT_LA_TILE, FFT_LA_TILE).transpose(1, 0, 2).reshape(la // FFT_LA_TILE, -1)
    twc = np.repeat(np.cos(tw)[:, :, None], LANES, axis=2)
    tws = np.repeat(np.sin(tw)[:, :, None], LANES, axis=2)
    as_bf = lambda a: jnp.asarray(a, dtype=F32).astype(BF16)
    return as_bf(kw), as_bf(w2), as_bf(chan), jnp.asarray(twc, F32), jnp.asarray(tws, F32), la, lb


def _rope_tables(l):
    f32 = np.float32
    nf = RET_HEAD_DIM // 4
    inv = np.power(f32(ROPE_BASE), -np.arange(nf, dtype=f32) / f32(nf)).astype(f32)
    rows = l // GRID_W
    r, cc = np.meshgrid(np.arange(rows, dtype=f32), np.arange(GRID_W, dtype=f32), indexing="ij")
    ang = np.concatenate([r.reshape(-1)[:, None] * inv, cc.reshape(-1)[:, None] * inv], axis=-1).astype(f32)
    cos, sin = np.cos(ang).astype(f32), np.sin(ang).astype(f32)
    cos_h = np.concatenate([cos, cos], axis=1)
    sin_h = np.concatenate([-sin, sin], axis=1)
    reps = LANES // RET_HEAD_DIM
    return jnp.asarray(np.tile(cos_h, (1, reps))), jnp.asarray(np.tile(sin_h, (1, reps)))


def kernel(x, c, ctx, c_ctx, w_ada, b_ada, norm1_gain, w_in, four_w_out, ret_decay_logit,
           ret_gn_gain, ret_w_out, w_out, norm2_gain, w_mlp1, w_mlp2, final_gain):
    assert w_ada.shape[0] == 1, "single-layer block"
    b, l, d = x.shape
    lc = ctx.shape[1]
    fw = four_w_out.shape[1]
    rw = ret_w_out.shape[1]
    gd = fw // FOUR_GROUPS
    assert l % FFT_LA == 0 and l % RET_CHUNK == 0 and rw % GROUP_W == 0

    kw, w2, chan, twc, tws, la, lb = _dft_tables(l, gd)
    cosf, sinf = _rope_tables(l)

    later = [four_w_out[0], ret_w_out[0], w_out[0], w_mlp1[0], w_mlp2[0]]
    (u, q, kt, v, sg, gab), later_b, mods, kct, vc = _proj(
        x, c, c_ctx, w_ada[0], b_ada, ctx, norm1_gain, w_in[0], cosf, sinf, ret_gn_gain, later, fw, rw)

    fm = _fft(u, kw, twc, tws, w2, chan, la, lb, gd, 1.0 / math.sqrt(l * gd))

    log_gamma = jax.nn.log_sigmoid(ret_decay_logit[0].astype(F32))
    z = _retention(q, kt, v, sg, kct, vc, log_gamma)

    w4, wr, wo, w1, w2 = later_b
    return _out(x, mods, fm, z, gab, w4, wr, wo, norm2_gain, w1, w2, final_gain[None, :])
```

```python
import functools
import math

import jax
import jax.numpy as jnp
import numpy as np
from jax import lax
from jax.experimental import pallas as pl
from jax.experimental.pallas import tpu as pltpu

F32 = jnp.float32
BF16 = jnp.bfloat16

GRID_W = 64
FOUR_GROUPS = 4
RET_HEAD_DIM = 64
N_MOD = 6
ROPE_BASE = 10000.0
EPS = 1e-6

LANES = 128
MXU_DIM = 256
F32_SUBLANES = 8
BF16_SUBLANES = 16
VMEM_LIMIT_BYTES = 56 * 1024 * 1024

RET_CHUNK = 128
RET_UNROLL = 32
HEADS_PER_GROUP = MXU_DIM // RET_HEAD_DIM
GROUP_W = HEADS_PER_GROUP * RET_HEAD_DIM
STATE_BLOCK_W = LANES
FFT_LA = 128
FFT_LA_TILE = BF16_SUBLANES
FFT_SCATTER_PAD = F32_SUBLANES
TOKEN_TILE = 512
PROJ_TOKEN_TILE = 512
WEIGHT_STAGE_COLS = 512
WEIGHT_STAGE_SLOTS = 6


def _dot(a, b):
    return jnp.dot(a, b, preferred_element_type=F32)


def _norm_mod(x, gain, shift, scale):
    ms = jnp.mean(x * x, axis=-1, keepdims=True)
    y = x * lax.rsqrt(ms + EPS) * gain
    return y * (1.0 + scale) + shift


def _tile_lanes(t, reps):
    return jnp.concatenate([t] * reps, axis=1) if reps > 1 else t


def _const_spec(shape):
    nd = len(shape)
    return pl.BlockSpec(shape, lambda *_: (0,) * nd, pipeline_mode=pl.Buffered(1))


def _params(*sem):
    return pltpu.CompilerParams(dimension_semantics=sem, vmem_limit_bytes=VMEM_LIMIT_BYTES)


def _is_first_step(grid_rank):
    first = pl.program_id(0) == 0
    for axis in range(1, grid_rank):
        first = first & (pl.program_id(axis) == 0)
    return first


def _proj_prepare(before_body, c_ref, cctx_ref, wada_hbm, bada_ref, w_hbm, ctx_ref, gain_ref,
                  mods_ref, kct_ref, vc_ref, w_ref, mod_ref, cv_ref, stage_ref, sem_ref, *, fw, rw, scale):
    nb = c_ref.shape[0]
    d = w_ref.shape[0]
    if before_body:
        cv_ref[...] = jnp.zeros_like(cv_ref)
        cv_ref[0:nb, :] = c_ref[...]
        cv_ref[nb:nb + 1, :] = cctx_ref[...]
    cv = cv_ref[...]
    s = (cv * jax.nn.sigmoid(cv)).astype(BF16)

    slots, _, sc = stage_ref.shape
    ada = [(wada_hbm, c0) for c0 in range(0, wada_hbm.shape[1], sc)]
    win = [(w_hbm, c0) for c0 in range(0, w_hbm.shape[1], sc)]
    body_mods = 2 * d // sc
    chunks = ada[:body_mods] + win + ada[body_mods:]
    split = body_mods + len(win)
    ctx_after = body_mods + (fw + 3 * rw) // sc - 1

    def copy(k):
        src, c0 = chunks[k]
        slot = k % slots
        return pltpu.make_async_copy(src.at[:, pl.ds(c0, sc)], stage_ref.at[slot], sem_ref.at[slot])

    if before_body:
        for k in range(min(slots - 1, len(chunks))):
            copy(k).start()
    for k in range(0, split) if before_body else range(split, len(chunks)):
        src, c0 = chunks[k]
        if k + slots - 1 < len(chunks):
            copy(k + slots - 1).start()
        copy(k).wait()
        blk = stage_ref[k % slots].astype(BF16)
        if src is wada_hbm:
            j, off = divmod(c0, d)
            hw = sc // 2
            mod_ref[j, :, off:off + hw] = _dot(s, blk[:, :hw]) + bada_ref[:, c0:c0 + hw]
            mod_ref[j, :, off + hw:off + sc] = _dot(s, blk[:, hw:]) + bada_ref[:, c0 + hw:c0 + sc]
        else:
            w_ref[:, c0:c0 + sc] = blk
        if k == ctx_after:
            for i in range(nb):
                h = _norm_mod(ctx_ref[i], gain_ref[...],
                              mod_ref[0, nb:nb + 1, :], mod_ref[1, nb:nb + 1, :]).astype(BF16)
                kct_ref[i] = (_dot(h, w_ref[:, fw + rw:fw + 2 * rw]) * scale).T.astype(BF16)
                vc_ref[i] = _dot(h, w_ref[:, fw + 2 * rw:fw + 3 * rw]).astype(BF16)
    if not before_body:
        mods_ref[...] = mod_ref[...]


def _proj_kernel(*refs, fw, rw, d, scale, ncast):
    x_ref, cos_ref, sin_ref, w_hbm, wada_hbm = refs[:5]
    cast_in = refs[5:5 + ncast]
    gain_ref, gng_ref, c_ref, cctx_ref, bada_ref, ctx_ref = refs[5 + ncast:11 + ncast]
    outs = refs[11 + ncast:]
    u_ref, q_ref, kt_ref, v_ref, sg_ref, gab_ref = outs[:6]
    cast_out = outs[6:6 + ncast]
    mods_ref, kct_ref, vc_ref = outs[6 + ncast:9 + ncast]
    w_ref, mod_ref, cv_ref, stage_ref, sem_ref = outs[9 + ncast:]

    prepare = functools.partial(
        _proj_prepare, c_ref=c_ref, cctx_ref=cctx_ref, wada_hbm=wada_hbm, bada_ref=bada_ref, w_hbm=w_hbm,
        ctx_ref=ctx_ref, gain_ref=gain_ref, mods_ref=mods_ref, kct_ref=kct_ref, vc_ref=vc_ref, w_ref=w_ref,
        mod_ref=mod_ref, cv_ref=cv_ref, stage_ref=stage_ref, sem_ref=sem_ref, fw=fw, rw=rw, scale=scale)
    pl.when(_is_first_step(2))(functools.partial(prepare, True))

    for src, dst in zip(cast_in, cast_out):
        dst[...] = src[...].astype(BF16)

    mod = mod_ref[0:2, pl.ds(pl.program_id(0), 1), :]
    h = _norm_mod(x_ref[0], gain_ref[...], mod[0], mod[1]).astype(BF16)
    tm = h.shape[0]

    def proj(lo, width):
        return _dot(h, w_ref[:, lo:lo + width])

    u_ref[0] = proj(0, fw).astype(BF16)

    reps = rw // cos_ref.shape[1]
    cosf = _tile_lanes(cos_ref[...], reps)
    sinf = _tile_lanes(sin_ref[...], reps)
    lane = lax.broadcasted_iota(jnp.int32, (tm, rw), 1)
    first_half = (lane & (RET_HEAD_DIM // 2)) == 0

    def rope(t):
        rot = jnp.where(first_half,
                        pltpu.roll(t, rw - RET_HEAD_DIM // 2, 1),
                        pltpu.roll(t, RET_HEAD_DIM // 2, 1))
        return t * cosf + rot * sinf

    q_ref[0] = rope(proj(fw, rw)).astype(BF16)
    kt_ref[0] = (rope(proj(fw + rw, rw)) * scale).T.astype(BF16)
    v_ref[0] = proj(fw + 2 * rw, rw).astype(BF16)
    g = proj(fw + 3 * rw, rw)
    sg_ref[0] = (g * jax.nn.sigmoid(g) * gng_ref[...]).astype(BF16)
    gab_ref[0, :, :d] = proj(fw + 4 * rw, d).astype(BF16)
    gab_ref[0, :, d:] = proj(fw + 4 * rw + d, d).astype(BF16)

    pl.when(_is_first_step(2))(functools.partial(prepare, False))


def _proj(x, c, c_ctx, w_ada, b_ada, ctx, gain, w_in, cosf, sinf, gn_gain, later_weights, fw, rw):
    b, l, d = x.shape
    lc = ctx.shape[1]
    tm = min(PROJ_TOKEN_TILE, l)
    nj = l // tm
    steps = b * nj
    rows = -(-(b + 1) // F32_SUBLANES) * F32_SUBLANES
    sc = WEIGHT_STAGE_COLS
    assert w_ada.shape == (d, N_MOD * d) and d % sc == 0 and w_in.shape[1] % sc == 0
    tok = lambda width: pl.BlockSpec((1, tm, width), lambda i, j: (i, j, 0))
    out = lambda width: jax.ShapeDtypeStruct((b, l, width), BF16)
    sliced = [w.reshape(steps, w.shape[0] // steps, w.shape[1]) for w in later_weights]
    for w in sliced:
        assert w.shape[1] % BF16_SUBLANES == 0
    cast_spec = lambda w: pl.BlockSpec((1,) + w.shape[1:], lambda i, j: (i * nj + j, 0, 0))
    whole = lambda shape: pl.BlockSpec(shape, lambda i, j: (0,) * len(shape))
    res = pl.pallas_call(
        functools.partial(_proj_kernel, fw=fw, rw=rw, d=d, scale=RET_HEAD_DIM ** -0.5, ncast=len(sliced)),
        grid=(b, nj),
        in_specs=[
            tok(d),
            pl.BlockSpec((tm, cosf.shape[1]), lambda i, j: (j, 0)),
            pl.BlockSpec((tm, sinf.shape[1]), lambda i, j: (j, 0)),
            pl.BlockSpec(memory_space=pl.ANY),
            pl.BlockSpec(memory_space=pl.ANY),
        ] + [cast_spec(w) for w in sliced] + [
            _const_spec((1, d)), _const_spec((1, rw)), _const_spec((b, d)), _const_spec((1, d)),
            _const_spec((1, N_MOD * d)), _const_spec((b, lc, d)),
        ],
        out_specs=[tok(fw), tok(rw), pl.BlockSpec((1, rw, tm), lambda i, j: (i, 0, j)),
                   tok(rw), tok(rw), tok(2 * d)] + [cast_spec(w) for w in sliced]
        + [whole((N_MOD, rows, d)), whole((b, rw, lc)), whole((b, lc, rw))],
        out_shape=[out(fw), out(rw), jax.ShapeDtypeStruct((b, rw, l), BF16),
                   out(rw), out(rw), out(2 * d)]
        + [jax.ShapeDtypeStruct(w.shape, BF16) for w in sliced]
        + [jax.ShapeDtypeStruct((N_MOD, rows, d), F32),
           jax.ShapeDtypeStruct((b, rw, lc), BF16), jax.ShapeDtypeStruct((b, lc, rw), BF16)],
        scratch_shapes=[pltpu.VMEM(w_in.shape, BF16),
                        pltpu.VMEM((N_MOD, rows, d), F32),
                        pltpu.VMEM((rows, d), F32),
                        pltpu.VMEM((WEIGHT_STAGE_SLOTS, d, sc), F32),
                        pltpu.SemaphoreType.DMA((WEIGHT_STAGE_SLOTS,))],
        compiler_params=_params("arbitrary", "arbitrary"),
        name="proj",
    )(x, cosf, sinf, w_in, w_ada, *sliced, gain, gn_gain, c, c_ctx[None, :], b_ada, ctx)
    ncast = len(sliced)
    casts = [cc.reshape(w.shape) for cc, w in zip(res[6:6 + ncast], later_weights)]
    mods, kct, vc = res[6 + ncast:]
    return res[:6], casts, mods, kct, vc


def _fft_kernel(u_ref, kw_ref, twc_ref, tws_ref, w2_ref, cs_ref, o_ref, tr_ref, ti_ref, scr_ref,
                *, la, lb, gd, npb, scale):
    r = FFT_LA_TILE
    rows = lb * r
    fw = u_ref.shape[2]
    ngroups = fw // gd
    reps = fw // twc_ref.shape[2]
    pitch = scr_ref.shape[1] // la

    @pl.when(pl.program_id(0) == 0)
    def _():
        scr_ref[...] = jnp.zeros_like(scr_ref)

    for j in range(la // r):
        u = jnp.concatenate([u_ref[0, k * la + j * r:k * la + (j + 1) * r, :] for k in range(lb)], axis=0)
        half = lb // 2 + 1
        t = _dot(kw_ref[...], u)
        for k in range(lb):
            src = k if k < half else lb - k
            a = t[src * r:(src + 1) * r]
            b = t[(half + src) * r:(half + src + 1) * r]
            ct = _tile_lanes(twc_ref[j, k * r:(k + 1) * r, :], reps)
            st = _tile_lanes(tws_ref[j, k * r:(k + 1) * r, :], reps)
            if k < half:
                tr, ti = a * ct + b * st, b * ct - a * st
            else:
                tr, ti = a * ct - b * st, -(b * ct) - a * st
            tr_ref[k * la + j * r:k * la + (j + 1) * r, :] = tr.astype(BF16)
            ti_ref[k * la + j * r:k * la + (j + 1) * r, :] = ti.astype(BF16)

    for pb in range(lb // npb):
        xs = []
        for p in range(npb):
            lo = (pb * npb + p) * la
            t = jnp.concatenate([tr_ref[lo:lo + la, :], ti_ref[lo:lo + la, :]], axis=0)
            xs.append(_dot(w2_ref[...], t).astype(BF16))
        for gi in range(ngroups):
            cols = slice(gi * gd, (gi + 1) * gd)
            lhs = jnp.concatenate([jnp.concatenate([x[:la, cols], x[la:, cols]], axis=1) for x in xs], axis=0)
            fg = _dot(lhs, cs_ref[...]) * scale
            for p in range(npb):
                scr_ref[gi, pl.ds(p, la, stride=pitch), :] = fg[p * la:(p + 1) * la]
        full = jnp.concatenate([scr_ref[gi].reshape(la, pitch, gd)[:, :npb, :] for gi in range(ngroups)], axis=2)
        o_ref[0, :, pb * npb:(pb + 1) * npb, :] = full.astype(BF16)


def _fft(u, kw, twc, tws, w2, cs, la, lb, gd, scale):
    b, l, fw = u.shape
    npb = min(BF16_SUBLANES, lb)
    out = pl.pallas_call(
        functools.partial(_fft_kernel, la=la, lb=lb, gd=gd, npb=npb, scale=scale),
        grid=(b,),
        in_specs=[pl.BlockSpec((1, l, fw), lambda i: (i, 0, 0)),
                  _const_spec(kw.shape), _const_spec(twc.shape), _const_spec(tws.shape),
                  _const_spec(w2.shape), _const_spec(cs.shape)],
        out_specs=pl.BlockSpec((1, la, lb, fw), lambda i: (i, 0, 0, 0)),
        out_shape=jax.ShapeDtypeStruct((b, la, lb, fw), BF16),
        scratch_shapes=[pltpu.VMEM((l, fw), BF16), pltpu.VMEM((l, fw), BF16),
                        pltpu.VMEM((fw // gd, la * (npb + FFT_SCATTER_PAD), gd), F32)],
        compiler_params=_params("arbitrary"),
        name="fft",
    )(u, kw, twc, tws, w2, cs)
    return out.reshape(b, l, fw)


def _ret_kernel(q_ref, kt_ref, v_ref, sg_ref, kct_ref, vc_ref, lgl_ref, lgc_ref,
                z_ref, sf_ref, sb_ref, stf_ref, stb_ref, p_ref, o_ref,
                dall_ref, qdf_ref, qdb_ref, kdf_ref, kdb_ref, inc_ref, *, nchunk, unroll):
    c = RET_CHUNK
    gw = GROUP_W
    lc = vc_ref.shape[1]

    lgf, lgb = lgl_ref[0:1, :], lgl_ref[1:2, :]
    pos = lax.broadcasted_iota(jnp.int32, (c, gw), 0).astype(F32)
    qdf_ref[...] = jnp.exp(lgf * (pos + 1.0))
    qdb_ref[...] = jnp.exp(lgb * (c - pos))
    kdf_ref[...] = jnp.exp(lgf * (c - 1.0 - pos))
    kdb_ref[...] = jnp.exp(lgb * pos)
    cdf = jnp.exp(lgf * c)
    cdb = jnp.exp(lgb * c)
    cpos = lax.broadcasted_iota(jnp.int32, (lc, gw), 0).astype(F32)
    wcf = jnp.exp(lgf * (lc - 1.0 - cpos))
    wcb = jnp.exp(lgb * cpos)
    si = lax.broadcasted_iota(jnp.int32, (c, HEADS_PER_GROUP * c), 0)
    sj = lax.broadcasted_iota(jnp.int32, (c, HEADS_PER_GROUP * c), 1) & (c - 1)
    diff = (si - sj).astype(F32)
    dall_ref[...] = (jnp.where(diff >= 0, jnp.exp(lgc_ref[0:1, :] * jnp.maximum(diff, 0.0)), 0.0)
                     + jnp.where(diff <= 0, jnp.exp(lgc_ref[1:2, :] * jnp.maximum(-diff, 0.0)), 0.0))

    pw = STATE_BLOCK_W
    npair = gw // pw
    same_head_p = (lax.broadcasted_iota(jnp.int32, (pw, pw), 0) // RET_HEAD_DIM
                   == lax.broadcasted_iota(jnp.int32, (pw, pw), 1) // RET_HEAD_DIM)

    def diag_blocks(t):
        return jnp.where(same_head_p, t, 0.0)

    def weighted(t, w):
        return (t.astype(F32) * w).astype(BF16)

    def rows(ref, n):
        return ref[0, pl.ds(pl.multiple_of(n * c, c), c), :]

    def kt_chunk(n):
        return kt_ref[0, :, pl.ds(pl.multiple_of(n * c, c), c)]

    def state_update(st_ref, kt, vw, cd):
        for p in range(npair):
            cols = slice(p * pw, (p + 1) * pw)
            st_ref[p] = st_ref[p] * cd[:, cols] + diag_blocks(_dot(kt[cols, :], vw[:, cols]))

    def full_state(s_ref, n):
        zero = jnp.zeros((pw, pw), BF16)
        return jnp.concatenate(
            [jnp.concatenate([s_ref[n, p] if q == p else zero for q in range(npair)], axis=1)
             for p in range(npair)], axis=0)

    kct = kct_ref[0]
    vc = vc_ref[0]
    stf_ref[...] = jnp.zeros_like(stf_ref)
    stb_ref[...] = jnp.zeros_like(stb_ref)
    state_update(stf_ref, kct, weighted(vc, wcf), cdf)
    state_update(stb_ref, kct, weighted(vc, wcb), cdb)

    def increment_body(n, carry):
        ktn = kt_chunk(n)
        vn = rows(v_ref, n)
        vwf = weighted(vn, kdf_ref[...])
        vwb = weighted(vn, kdb_ref[...])
        for p in range(npair):
            cols = slice(p * pw, (p + 1) * pw)
            inc_ref[n, p] = _dot(ktn[cols, :], jnp.concatenate([vwf[:, cols], vwb[:, cols]], axis=1))
        return carry

    lax.fori_loop(0, nchunk, increment_body, 0, unroll=unroll)

    def scan_body(i, carry):
        nf = i
        nb = nchunk - 1 - i
        sf_ref[nf] = diag_blocks(stf_ref[...].astype(BF16))
        sb_ref[nb] = diag_blocks(stb_ref[...].astype(BF16))
        for p in range(npair):
            cols = slice(p * pw, (p + 1) * pw)
            stf_ref[p] = stf_ref[p] * cdf[:, cols] + inc_ref[nf, p, :, :pw]
            stb_ref[p] = stb_ref[p] * cdb[:, cols] + inc_ref[nb, p, :, pw:]
        return carry

    lax.fori_loop(0, nchunk, scan_body, 0, unroll=unroll)

    lane = lax.broadcasted_iota(jnp.int32, (c, gw), 1)
    sub = lax.broadcasted_iota(jnp.int32, (gw, c), 0)
    lane_masks = [(lane >= h * RET_HEAD_DIM) & (lane < (h + 1) * RET_HEAD_DIM)
                  for h in range(HEADS_PER_GROUP)]
    sub_masks = [(sub >= h * RET_HEAD_DIM) & (sub < (h + 1) * RET_HEAD_DIM)
                 for h in range(HEADS_PER_GROUP)]
    same_head = (lax.broadcasted_iota(jnp.int32, (gw, gw), 0) // RET_HEAD_DIM
                 == lax.broadcasted_iota(jnp.int32, (gw, gw), 1) // RET_HEAD_DIM)
    bd_mean = jnp.where(same_head, 1.0 / RET_HEAD_DIM, 0.0).astype(BF16)

    def score_body(n, carry):
        ktn = kt_chunk(n)
        zk = jnp.zeros_like(ktn)
        kbd = jnp.concatenate([jnp.where(m, ktn, zk) for m in sub_masks], axis=1)
        p_ref[n] = (_dot(rows(q_ref, n), kbd) * dall_ref[...]).astype(BF16)
        return carry

    lax.fori_loop(0, nchunk, score_body, 0, unroll=unroll)

    def mix_body(n, carry):
        qn = rows(q_ref, n)
        vn = rows(v_ref, n)
        zv = jnp.zeros_like(vn)
        vbd = jnp.concatenate([jnp.where(m, vn, zv) for m in lane_masks], axis=0)
        o_ref[n] = (_dot(p_ref[n], vbd) + qdf_ref[...] * _dot(qn, full_state(sf_ref, n))
                    + qdb_ref[...] * _dot(qn, full_state(sb_ref, n)))
        return carry

    lax.fori_loop(0, nchunk, mix_body, 0, unroll=unroll)

    def norm_body(n, carry):
        o = o_ref[n]
        ms = _dot((o * o).astype(BF16), bd_mean)
        z = rows(sg_ref, n).astype(F32) * (o * lax.rsqrt(ms + EPS))
        z_ref[0, pl.ds(pl.multiple_of(n * c, c), c), :] = z.astype(BF16)
        return carry

    lax.fori_loop(0, nchunk, norm_body, 0, unroll=unroll)


def _retention(q, kt, v, sg, kct, vc, log_gamma):
    b, l, rw = q.shape
    lc = vc.shape[1]
    c = RET_CHUNK
    gw = GROUP_W
    hpg = HEADS_PER_GROUP
    ng = rw // gw
    nchunk = l // c
    pw = STATE_BLOCK_W
    npair = gw // pw
    assert c & (c - 1) == 0
    lg = log_gamma.reshape(2, ng, hpg).transpose(1, 0, 2)
    lg_lane = jnp.repeat(lg, RET_HEAD_DIM, axis=2)
    lg_col = jnp.repeat(lg, c, axis=2)
    tok = pl.BlockSpec((1, l, gw), lambda i, j: (i, 0, j))
    tokt = pl.BlockSpec((1, gw, l), lambda i, j: (i, j, 0))
    grp = lambda r, width: pl.BlockSpec((None, r, width), lambda i, j: (j, 0, 0))
    return pl.pallas_call(
        functools.partial(_ret_kernel, nchunk=nchunk, unroll=math.gcd(nchunk, RET_UNROLL)),
        grid=(b, ng),
        in_specs=[
            tok, tokt, tok, tok,
            pl.BlockSpec((1, gw, lc), lambda i, j: (i, j, 0)),
            pl.BlockSpec((1, lc, gw), lambda i, j: (i, 0, j)),
            grp(2, gw), grp(2, hpg * c),
        ],
        out_specs=tok,
        out_shape=jax.ShapeDtypeStruct((b, l, rw), BF16),
        scratch_shapes=[pltpu.VMEM((nchunk, npair, pw, pw), BF16), pltpu.VMEM((nchunk, npair, pw, pw), BF16),
                        pltpu.VMEM((npair, pw, pw), F32), pltpu.VMEM((npair, pw, pw), F32),
                        pltpu.VMEM((nchunk, c, hpg * c), BF16),
                        pltpu.VMEM((nchunk, c, gw), F32),
                        pltpu.VMEM((c, hpg * c), F32)] + [pltpu.VMEM((c, gw), F32)] * 4
        + [pltpu.VMEM((nchunk, npair, pw, 2 * pw), F32)],
        compiler_params=_params("arbitrary", "arbitrary"),
        name="ret",
    )(q, kt, v, sg, kct, vc, lg_lane, lg_col)


def _out_kernel(w4_ref, wr_ref, wo_ref, w1_ref, w2_ref, x_ref, mod_ref, fm_ref, z_ref, gab_ref,
                gain2_ref, fgain_ref, o_ref, *, ff_chunk):
    mod = mod_ref[:, pl.ds(pl.program_id(0), 1), :]
    g1, sh2, sc2, g2 = mod[2], mod[3], mod[4], mod[5]
    y_four = _dot(fm_ref[0], w4_ref[...])
    y_ret = _dot(z_ref[0], wr_ref[...])
    d = x_ref.shape[2]
    y = (jax.nn.sigmoid(gab_ref[0, :, :d].astype(F32)) * y_four
         + jax.nn.sigmoid(gab_ref[0, :, d:].astype(F32)) * y_ret)
    x1 = x_ref[0] + g1 * _dot(y.astype(BF16), wo_ref[...])
    h2 = _norm_mod(x1, gain2_ref[...], sh2, sc2).astype(BF16)
    dff = w1_ref.shape[1]
    acc = None
    for lo in range(0, dff, ff_chunk):
        hid = jnp.maximum(_dot(h2, w1_ref[:, lo:lo + ff_chunk]), 0.0)
        part = _dot((hid * hid).astype(BF16), w2_ref[lo:lo + ff_chunk, :])
        acc = part if acc is None else acc + part
    x2 = x1 + g2 * acc
    ms = jnp.mean(x2 * x2, axis=-1, keepdims=True)
    o_ref[0] = x2 * lax.rsqrt(ms + EPS) * fgain_ref[...]


def _out(x, mods, fm, z, gab, w4, wr, wo, gain2, w1, w2, fgain):
    b, l, d = x.shape
    tm = min(TOKEN_TILE, l)
    tok = lambda width: pl.BlockSpec((1, tm, width), lambda i, j: (i, j, 0))
    weights = (w4, wr, wo, w1, w2)
    return pl.pallas_call(
        functools.partial(_out_kernel, ff_chunk=min(1024, w1.shape[1])),
        grid=(b, l // tm),
        in_specs=[_const_spec(w.shape) for w in weights] + [
            tok(d),
            _const_spec(mods.shape),
            tok(fm.shape[2]), tok(z.shape[2]), tok(2 * d),
            _const_spec((1, d)), _const_spec((1, d)),
        ],
        out_specs=tok(d),
        out_shape=jax.ShapeDtypeStruct((b, l, d), F32),
        compiler_params=_params("arbitrary", "arbitrary"),
        name="out",
    )(*weights, x, mods, fm, z, gab, gain2, fgain)


def _dft_tables(l, gd):
    la = FFT_LA
    lb = l // la

    def cs(n):
        idx = np.arange(n)
        ang = 2.0 * np.pi * ((idx[:, None] * idx[None, :]) % n) / n
        return np.cos(ang), np.sin(ang)

    cb, sb = cs(lb)
    eye = np.eye(FFT_LA_TILE)
    half = lb // 2 + 1
    kw = np.concatenate([np.kron(cb[:half], eye), -np.kron(sb[:half], eye)], axis=0)
    ca, sa = cs(la)
    w2 = np.block([[ca, sa], [-sa, ca]])
    cc, sc = cs(gd)
    chan = np.concatenate([cc, sc], axis=0)
    tw = 2.0 * np.pi * (np.arange(lb)[:, None] * np.arange(la)[None, :]) / l
    tw = tw.reshape(lb, la // FFT_LA_TILE, FFT_LA_TILE).transpose(1, 0, 2).reshape(la // FFT_LA_TILE, -1)
    twc = np.repeat(np.cos(tw)[:, :, None], LANES, axis=2)
    tws = np.repeat(np.sin(tw)[:, :, None], LANES, axis=2)
    as_bf = lambda a: jnp.asarray(a, dtype=F32).astype(BF16)
    return as_bf(kw), as_bf(w2), as_bf(chan), jnp.asarray(twc, F32), jnp.asarray(tws, F32), la, lb


def _rope_tables(l):
    f32 = np.float32
    nf = RET_HEAD_DIM // 4
    inv = np.power(f32(ROPE_BASE), -np.arange(nf, dtype=f32) / f32(nf)).astype(f32)
    rows = l // GRID_W
    r, cc = np.meshgrid(np.arange(rows, dtype=f32), np.arange(GRID_W, dtype=f32), indexing="ij")
    ang = np.concatenate([r.reshape(-1)[:, None] * inv, cc.reshape(-1)[:, None] * inv], axis=-1).astype(f32)
    cos, sin = np.cos(ang).astype(f32), np.sin(ang).astype(f32)
    cos_h = np.concatenate([cos, cos], axis=1)
    sin_h = np.concatenate([-sin, sin], axis=1)
    reps = LANES // RET_HEAD_DIM
    return jnp.asarray(np.tile(cos_h, (1, reps))), jnp.asarray(np.tile(sin_h, (1, reps)))


def kernel(x, c, ctx, c_ctx, w_ada, b_ada, norm1_gain, w_in, four_w_out, ret_decay_logit,
           ret_gn_gain, ret_w_out, w_out, norm2_gain, w_mlp1, w_mlp2, final_gain):
    assert w_ada.shape[0] == 1, "single-layer block"
    b, l, d = x.shape
    lc = ctx.shape[1]
    fw = four_w_out.shape[1]
    rw = ret_w_out.shape[1]
    gd = fw // FOUR_GROUPS
    assert l % FFT_LA == 0 and l % RET_CHUNK == 0 and rw % GROUP_W == 0

    kw, w2, chan, twc, tws, la, lb = _dft_tables(l, gd)
    cosf, sinf = _rope_tables(l)

    later = [four_w_out[0], ret_w_out[0], w_out[0], w_mlp1[0], w_mlp2[0]]
    (u, q, kt, v, sg, gab), later_b, mods, kct, vc = _proj(
        x, c, c_ctx, w_ada[0], b_ada, ctx, norm1_gain, w_in[0], cosf, sinf, ret_gn_gain, later, fw, rw)

    fm = _fft(u, kw, twc, tws, w2, chan, la, lb, gd, 1.0 / math.sqrt(l * gd))

    log_gamma = jax.nn.log_sigmoid(ret_decay_logit[0].astype(F32))
    z = _retention(q, kt, v, sg, kct, vc, log_gamma)

    w4, wr, wo, w1, w2 = later_b
    return _out(x, mods, fm, z, gab, w4, wr, wo, norm2_gain, w1, w2, final_gain[None, :])
```

```python
import functools
import math

import jax
import jax.numpy as jnp
import numpy as np
from jax import lax
from jax.experimental import pallas as pl
from jax.experimental.pallas import tpu as pltpu

F32 = jnp.float32
BF16 = jnp.bfloat16

GRID_W = 64
FOUR_GROUPS = 4
RET_HEAD_DIM = 64
N_MOD = 6
ROPE_BASE = 10000.0
EPS = 1e-6

LANES = 128
MXU_DIM = 256
F32_SUBLANES = 8
BF16_SUBLANES = 16
VMEM_LIMIT_BYTES = 56 * 1024 * 1024

RET_CHUNK = 128
RET_UNROLL = 32
HEADS_PER_GROUP = MXU_DIM // RET_HEAD_DIM
GROUP_W = HEADS_PER_GROUP * RET_HEAD_DIM
STATE_BLOCK_W = LANES
FFT_LA = 128
FFT_LA_TILE = BF16_SUBLANES
FFT_SCATTER_PAD = F32_SUBLANES
TOKEN_TILE = 512
PROJ_TOKEN_TILE = 512
WEIGHT_STAGE_COLS = 512
WEIGHT_STAGE_SLOTS = 8


def _dot(a, b):
    return jnp.dot(a, b, preferred_element_type=F32)


def _norm_mod(x, gain, shift, scale):
    ms = jnp.mean(x * x, axis=-1, keepdims=True)
    y = x * lax.rsqrt(ms + EPS) * gain
    return y * (1.0 + scale) + shift


def _tile_lanes(t, reps):
    return jnp.concatenate([t] * reps, axis=1) if reps > 1 else t


def _const_spec(shape):
    nd = len(shape)
    return pl.BlockSpec(shape, lambda *_: (0,) * nd, pipeline_mode=pl.Buffered(1))


def _params(*sem):
    return pltpu.CompilerParams(dimension_semantics=sem, vmem_limit_bytes=VMEM_LIMIT_BYTES)


def _is_first_step(grid_rank):
    first = pl.program_id(0) == 0
    for axis in range(1, grid_rank):
        first = first & (pl.program_id(axis) == 0)
    return first


def _proj_prepare(before_body, c_ref, cctx_ref, wada_hbm, bada_ref, w_hbm, ctx_ref, gain_ref,
                  mods_ref, kct_ref, vc_ref, w_ref, mod_ref, cv_ref, stage_ref, sem_ref, *, fw, rw, scale):
    nb = c_ref.shape[0]
    d = w_ref.shape[0]
    if before_body:
        cv_ref[...] = jnp.zeros_like(cv_ref)
        cv_ref[0:nb, :] = c_ref[...]
        cv_ref[nb:nb + 1, :] = cctx_ref[...]
    cv = cv_ref[...]
    s = (cv * jax.nn.sigmoid(cv)).astype(BF16)

    slots, _, sc = stage_ref.shape
    ada = [(wada_hbm, c0) for c0 in range(0, wada_hbm.shape[1], sc)]
    win = [(w_hbm, c0) for c0 in range(0, w_hbm.shape[1], sc)]
    body_mods = 2 * d // sc
    chunks = ada[:body_mods] + win + ada[body_mods:]
    split = body_mods + len(win)
    ctx_after = body_mods + (fw + 3 * rw) // sc - 1

    def copy(k):
        src, c0 = chunks[k]
        slot = k % slots
        return pltpu.make_async_copy(src.at[:, pl.ds(c0, sc)], stage_ref.at[slot], sem_ref.at[slot])

    if before_body:
        for k in range(min(slots - 1, len(chunks))):
            copy(k).start()
    for k in range(0, split) if before_body else range(split, len(chunks)):
        src, c0 = chunks[k]
        if k + slots - 1 < len(chunks):
            copy(k + slots - 1).start()
        copy(k).wait()
        blk = stage_ref[k % slots].astype(BF16)
        if src is wada_hbm:
            j, off = divmod(c0, d)
            hw = sc // 2
            mod_ref[j, :, off:off + hw] = _dot(s, blk[:, :hw]) + bada_ref[:, c0:c0 + hw]
            mod_ref[j, :, off + hw:off + sc] = _dot(s, blk[:, hw:]) + bada_ref[:, c0 + hw:c0 + sc]
        else:
            w_ref[:, c0:c0 + sc] = blk
        if k == ctx_after:
            for i in range(nb):
                h = _norm_mod(ctx_ref[i], gain_ref[...],
                              mod_ref[0, nb:nb + 1, :], mod_ref[1, nb:nb + 1, :]).astype(BF16)
                kct_ref[i] = (_dot(h, w_ref[:, fw + rw:fw + 2 * rw]) * scale).T.astype(BF16)
                vc_ref[i] = _dot(h, w_ref[:, fw + 2 * rw:fw + 3 * rw]).astype(BF16)
    if not before_body:
        mods_ref[...] = mod_ref[...]


def _proj_kernel(*refs, fw, rw, d, scale, ncast):
    x_ref, cos_ref, sin_ref, w_hbm, wada_hbm = refs[:5]
    cast_in = refs[5:5 + ncast]
    gain_ref, gng_ref, c_ref, cctx_ref, bada_ref, ctx_ref = refs[5 + ncast:11 + ncast]
    outs = refs[11 + ncast:]
    u_ref, q_ref, kt_ref, v_ref, sg_ref, gab_ref = outs[:6]
    cast_out = outs[6:6 + ncast]
    mods_ref, kct_ref, vc_ref = outs[6 + ncast:9 + ncast]
    w_ref, mod_ref, cv_ref, stage_ref, sem_ref = outs[9 + ncast:]

    prepare = functools.partial(
        _proj_prepare, c_ref=c_ref, cctx_ref=cctx_ref, wada_hbm=wada_hbm, bada_ref=bada_ref, w_hbm=w_hbm,
        ctx_ref=ctx_ref, gain_ref=gain_ref, mods_ref=mods_ref, kct_ref=kct_ref, vc_ref=vc_ref, w_ref=w_ref,
        mod_ref=mod_ref, cv_ref=cv_ref, stage_ref=stage_ref, sem_ref=sem_ref, fw=fw, rw=rw, scale=scale)
    pl.when(_is_first_step(2))(functools.partial(prepare, True))

    for src, dst in zip(cast_in, cast_out):
        dst[...] = src[...].astype(BF16)

    mod = mod_ref[0:2, pl.ds(pl.program_id(0), 1), :]
    h = _norm_mod(x_ref[0], gain_ref[...], mod[0], mod[1]).astype(BF16)
    tm = h.shape[0]

    def proj(lo, width):
        return _dot(h, w_ref[:, lo:lo + width])

    u_ref[0] = proj(0, fw).astype(BF16)

    reps = rw // cos_ref.shape[1]
    cosf = _tile_lanes(cos_ref[...], reps)
    sinf = _tile_lanes(sin_ref[...], reps)
    lane = lax.broadcasted_iota(jnp.int32, (tm, rw), 1)
    first_half = (lane & (RET_HEAD_DIM // 2)) == 0

    def rope(t):
        rot = jnp.where(first_half,
                        pltpu.roll(t, rw - RET_HEAD_DIM // 2, 1),
                        pltpu.roll(t, RET_HEAD_DIM // 2, 1))
        return t * cosf + rot * sinf

    q_ref[0] = rope(proj(fw, rw)).astype(BF16)
    kt_ref[0] = (rope(proj(fw + rw, rw)) * scale).T.astype(BF16)
    v_ref[0] = proj(fw + 2 * rw, rw).astype(BF16)
    g = proj(fw + 3 * rw, rw)
    sg_ref[0] = (g * jax.nn.sigmoid(g) * gng_ref[...]).astype(BF16)
    gab_ref[0, :, :d] = proj(fw + 4 * rw, d).astype(BF16)
    gab_ref[0, :, d:] = proj(fw + 4 * rw + d, d).astype(BF16)

    pl.when(_is_first_step(2))(functools.partial(prepare, False))


def _proj(x, c, c_ctx, w_ada, b_ada, ctx, gain, w_in, cosf, sinf, gn_gain, later_weights, fw, rw):
    b, l, d = x.shape
    lc = ctx.shape[1]
    tm = min(PROJ_TOKEN_TILE, l)
    nj = l // tm
    steps = b * nj
    rows = -(-(b + 1) // F32_SUBLANES) * F32_SUBLANES
    sc = WEIGHT_STAGE_COLS
    assert w_ada.shape == (d, N_MOD * d) and d % sc == 0 and w_in.shape[1] % sc == 0
    tok = lambda width: pl.BlockSpec((1, tm, width), lambda i, j: (i, j, 0))
    out = lambda width: jax.ShapeDtypeStruct((b, l, width), BF16)
    sliced = [w.reshape(steps, w.shape[0] // steps, w.shape[1]) for w in later_weights]
    for w in sliced:
        assert w.shape[1] % BF16_SUBLANES == 0
    cast_spec = lambda w: pl.BlockSpec((1,) + w.shape[1:], lambda i, j: (i * nj + j, 0, 0))
    whole = lambda shape: pl.BlockSpec(shape, lambda i, j: (0,) * len(shape))
    res = pl.pallas_call(
        functools.partial(_proj_kernel, fw=fw, rw=rw, d=d, scale=RET_HEAD_DIM ** -0.5, ncast=len(sliced)),
        grid=(b, nj),
        in_specs=[
            tok(d),
            pl.BlockSpec((tm, cosf.shape[1]), lambda i, j: (j, 0)),
            pl.BlockSpec((tm, sinf.shape[1]), lambda i, j: (j, 0)),
            pl.BlockSpec(memory_space=pl.ANY),
            pl.BlockSpec(memory_space=pl.ANY),
        ] + [cast_spec(w) for w in sliced] + [
            _const_spec((1, d)), _const_spec((1, rw)), _const_spec((b, d)), _const_spec((1, d)),
            _const_spec((1, N_MOD * d)), _const_spec((b, lc, d)),
        ],
        out_specs=[tok(fw), tok(rw), pl.BlockSpec((1, rw, tm), lambda i, j: (i, 0, j)),
                   tok(rw), tok(rw), tok(2 * d)] + [cast_spec(w) for w in sliced]
        + [whole((N_MOD, rows, d)), whole((b, rw, lc)), whole((b, lc, rw))],
        out_shape=[out(fw), out(rw), jax.ShapeDtypeStruct((b, rw, l), BF16),
                   out(rw), out(rw), out(2 * d)]
        + [jax.ShapeDtypeStruct(w.shape, BF16) for w in sliced]
        + [jax.ShapeDtypeStruct((N_MOD, rows, d), F32),
           jax.ShapeDtypeStruct((b, rw, lc), BF16), jax.ShapeDtypeStruct((b, lc, rw), BF16)],
        scratch_shapes=[pltpu.VMEM(w_in.shape, BF16),
                        pltpu.VMEM((N_MOD, rows, d), F32),
                        pltpu.VMEM((rows, d), F32),
                        pltpu.VMEM((WEIGHT_STAGE_SLOTS, d, sc), F32),
                        pltpu.SemaphoreType.DMA((WEIGHT_STAGE_SLOTS,))],
        compiler_params=_params("arbitrary", "arbitrary"),
        name="proj",
    )(x, cosf, sinf, w_in, w_ada, *sliced, gain, gn_gain, c, c_ctx[None, :], b_ada, ctx)
    ncast = len(sliced)
    casts = [cc.reshape(w.shape) for cc, w in zip(res[6:6 + ncast], later_weights)]
    mods, kct, vc = res[6 + ncast:]
    return res[:6], casts, mods, kct, vc


def _fft_kernel(u_ref, kw_ref, twc_ref, tws_ref, w2_ref, cs_ref, o_ref, tr_ref, ti_ref, scr_ref,
                *, la, lb, gd, npb, scale):
    r = FFT_LA_TILE
    rows = lb * r
    fw = u_ref.shape[2]
    ngroups = fw // gd
    reps = fw // twc_ref.shape[2]
    pitch = scr_ref.shape[1] // la

    @pl.when(pl.program_id(0) == 0)
    def _():
        scr_ref[...] = jnp.zeros_like(scr_ref)

    for j in range(la // r):
        u = jnp.concatenate([u_ref[0, k * la + j * r:k * la + (j + 1) * r, :] for k in range(lb)], axis=0)
        half = lb // 2 + 1
        t = _dot(kw_ref[...], u)
        for k in range(lb):
            src = k if k < half else lb - k
            a = t[src * r:(src + 1) * r]
            b = t[(half + src) * r:(half + src + 1) * r]
            ct = _tile_lanes(twc_ref[j, k * r:(k + 1) * r, :], reps)
            st = _tile_lanes(tws_ref[j, k * r:(k + 1) * r, :], reps)
            if k < half:
                tr, ti = a * ct + b * st, b * ct - a * st
            else:
                tr, ti = a * ct - b * st, -(b * ct) - a * st
            tr_ref[k * la + j * r:k * la + (j + 1) * r, :] = tr.astype(BF16)
            ti_ref[k * la + j * r:k * la + (j + 1) * r, :] = ti.astype(BF16)

    for pb in range(lb // npb):
        xs = []
        for p in range(npb):
            lo = (pb * npb + p) * la
            t = jnp.concatenate([tr_ref[lo:lo + la, :], ti_ref[lo:lo + la, :]], axis=0)
            xs.append(_dot(w2_ref[...], t).astype(BF16))
        for gi in range(ngroups):
            cols = slice(gi * gd, (gi + 1) * gd)
            lhs = jnp.concatenate([jnp.concatenate([x[:la, cols], x[la:, cols]], axis=1) for x in xs], axis=0)
            fg = _dot(lhs, cs_ref[...]) * scale
            for p in range(npb):
                scr_ref[gi, pl.ds(p, la, stride=pitch), :] = fg[p * la:(p + 1) * la]
        full = jnp.concatenate([scr_ref[gi].reshape(la, pitch, gd)[:, :npb, :] for gi in range(ngroups)], axis=2)
        o_ref[0, :, pb * npb:(pb + 1) * npb, :] = full.astype(BF16)


def _fft(u, kw, twc, tws, w2, cs, la, lb, gd, scale):
    b, l, fw = u.shape
    npb = min(BF16_SUBLANES, lb)
    out = pl.pallas_call(
        functools.partial(_fft_kernel, la=la, lb=lb, gd=gd, npb=npb, scale=scale),
        grid=(b,),
        in_specs=[pl.BlockSpec((1, l, fw), lambda i: (i, 0, 0)),
                  _const_spec(kw.shape), _const_spec(twc.shape), _const_spec(tws.shape),
                  _const_spec(w2.shape), _const_spec(cs.shape)],
        out_specs=pl.BlockSpec((1, la, lb, fw), lambda i: (i, 0, 0, 0)),
        out_shape=jax.ShapeDtypeStruct((b, la, lb, fw), BF16),
        scratch_shapes=[pltpu.VMEM((l, fw), BF16), pltpu.VMEM((l, fw), BF16),
                        pltpu.VMEM((fw // gd, la * (npb + FFT_SCATTER_PAD), gd), F32)],
        compiler_params=_params("arbitrary"),
        name="fft",
    )(u, kw, twc, tws, w2, cs)
    return out.reshape(b, l, fw)


def _ret_kernel(q_ref, kt_ref, v_ref, sg_ref, kct_ref, vc_ref, lgl_ref, lgc_ref,
                z_ref, sf_ref, sb_ref, stf_ref, stb_ref, p_ref, o_ref,
                dall_ref, qdf_ref, qdb_ref, kdf_ref, kdb_ref, inc_ref, *, nchunk, unroll):
    c = RET_CHUNK
    gw = GROUP_W
    lc = vc_ref.shape[1]

    lgf, lgb = lgl_ref[0:1, :], lgl_ref[1:2, :]
    pos = lax.broadcasted_iota(jnp.int32, (c, gw), 0).astype(F32)
    qdf_ref[...] = jnp.exp(lgf * (pos + 1.0))
    qdb_ref[...] = jnp.exp(lgb * (c - pos))
    kdf_ref[...] = jnp.exp(lgf * (c - 1.0 - pos))
    kdb_ref[...] = jnp.exp(lgb * pos)
    cdf = jnp.exp(lgf * c)
    cdb = jnp.exp(lgb * c)
    cpos = lax.broadcasted_iota(jnp.int32, (lc, gw), 0).astype(F32)
    wcf = jnp.exp(lgf * (lc - 1.0 - cpos))
    wcb = jnp.exp(lgb * cpos)
    si = lax.broadcasted_iota(jnp.int32, (c, HEADS_PER_GROUP * c), 0)
    sj = lax.broadcasted_iota(jnp.int32, (c, HEADS_PER_GROUP * c), 1) & (c - 1)
    diff = (si - sj).astype(F32)
    dall_ref[...] = (jnp.where(diff >= 0, jnp.exp(lgc_ref[0:1, :] * jnp.maximum(diff, 0.0)), 0.0)
                     + jnp.where(diff <= 0, jnp.exp(lgc_ref[1:2, :] * jnp.maximum(-diff, 0.0)), 0.0))

    pw = STATE_BLOCK_W
    npair = gw // pw
    same_head_p = (lax.broadcasted_iota(jnp.int32, (pw, pw), 0) // RET_HEAD_DIM
                   == lax.broadcasted_iota(jnp.int32, (pw, pw), 1) // RET_HEAD_DIM)

    def diag_blocks(t):
        return jnp.where(same_head_p, t, 0.0)

    def weighted(t, w):
        return (t.astype(F32) * w).astype(BF16)

    def rows(ref, n):
        return ref[0, pl.ds(pl.multiple_of(n * c, c), c), :]

    def kt_chunk(n):
        return kt_ref[0, :, pl.ds(pl.multiple_of(n * c, c), c)]

    def state_update(st_ref, kt, vw, cd):
        for p in range(npair):
            cols = slice(p * pw, (p + 1) * pw)
            st_ref[p] = st_ref[p] * cd[:, cols] + diag_blocks(_dot(kt[cols, :], vw[:, cols]))

    def full_state(s_ref, n):
        zero = jnp.zeros((pw, pw), BF16)
        return jnp.concatenate(
            [jnp.concatenate([s_ref[n, p] if q == p else zero for q in range(npair)], axis=1)
             for p in range(npair)], axis=0)

    kct = kct_ref[0]
    vc = vc_ref[0]
    stf_ref[...] = jnp.zeros_like(stf_ref)
    stb_ref[...] = jnp.zeros_like(stb_ref)
    state_update(stf_ref, kct, weighted(vc, wcf), cdf)
    state_update(stb_ref, kct, weighted(vc, wcb), cdb)

    def increment_body(n, carry):
        ktn = kt_chunk(n)
        vn = rows(v_ref, n)
        vwf = weighted(vn, kdf_ref[...])
        vwb = weighted(vn, kdb_ref[...])
        for p in range(npair):
            cols = slice(p * pw, (p + 1) * pw)
            inc_ref[n, p] = _dot(ktn[cols, :], jnp.concatenate([vwf[:, cols], vwb[:, cols]], axis=1))
        return carry

    lax.fori_loop(0, nchunk, increment_body, 0, unroll=unroll)

    def scan_body(i, carry):
        nf = i
        nb = nchunk - 1 - i
        sf_ref[nf] = stf_ref[...].astype(BF16)
        sb_ref[nb] = stb_ref[...].astype(BF16)
        for p in range(npair):
            cols = slice(p * pw, (p + 1) * pw)
            stf_ref[p] = stf_ref[p] * cdf[:, cols] + diag_blocks(inc_ref[nf, p, :, :pw])
            stb_ref[p] = stb_ref[p] * cdb[:, cols] + diag_blocks(inc_ref[nb, p, :, pw:])
        return carry

    lax.fori_loop(0, nchunk, scan_body, 0, unroll=unroll)

    lane = lax.broadcasted_iota(jnp.int32, (c, gw), 1)
    sub = lax.broadcasted_iota(jnp.int32, (gw, c), 0)
    lane_masks = [(lane >= h * RET_HEAD_DIM) & (lane < (h + 1) * RET_HEAD_DIM)
                  for h in range(HEADS_PER_GROUP)]
    sub_masks = [(sub >= h * RET_HEAD_DIM) & (sub < (h + 1) * RET_HEAD_DIM)
                 for h in range(HEADS_PER_GROUP)]
    same_head = (lax.broadcasted_iota(jnp.int32, (gw, gw), 0) // RET_HEAD_DIM
                 == lax.broadcasted_iota(jnp.int32, (gw, gw), 1) // RET_HEAD_DIM)
    bd_mean = jnp.where(same_head, 1.0 / RET_HEAD_DIM, 0.0).astype(BF16)

    def score_body(n, carry):
        ktn = kt_chunk(n)
        zk = jnp.zeros_like(ktn)
        kbd = jnp.concatenate([jnp.where(m, ktn, zk) for m in sub_masks], axis=1)
        p_ref[n] = (_dot(rows(q_ref, n), kbd) * dall_ref[...]).astype(BF16)
        return carry

    lax.fori_loop(0, nchunk, score_body, 0, unroll=unroll)

    def mix_body(n, carry):
        qn = rows(q_ref, n)
        vn = rows(v_ref, n)
        zv = jnp.zeros_like(vn)
        vbd = jnp.concatenate([jnp.where(m, vn, zv) for m in lane_masks], axis=0)
        o_ref[n] = (_dot(p_ref[n], vbd) + qdf_ref[...] * _dot(qn, full_state(sf_ref, n))
                    + qdb_ref[...] * _dot(qn, full_state(sb_ref, n)))
        return carry

    lax.fori_loop(0, nchunk, mix_body, 0, unroll=unroll)

    def norm_body(n, carry):
        o = o_ref[n]
        ms = _dot((o * o).astype(BF16), bd_mean)
        z = rows(sg_ref, n).astype(F32) * (o * lax.rsqrt(ms + EPS))
        z_ref[0, pl.ds(pl.multiple_of(n * c, c), c), :] = z.astype(BF16)
        return carry

    lax.fori_loop(0, nchunk, norm_body, 0, unroll=unroll)


def _retention(q, kt, v, sg, kct, vc, log_gamma):
    b, l, rw = q.shape
    lc = vc.shape[1]
    c = RET_CHUNK
    gw = GROUP_W
    hpg = HEADS_PER_GROUP
    ng = rw // gw
    nchunk = l // c
    pw = STATE_BLOCK_W
    npair = gw // pw
    assert c & (c - 1) == 0
    lg = log_gamma.reshape(2, ng, hpg).transpose(1, 0, 2)
    lg_lane = jnp.repeat(lg, RET_HEAD_DIM, axis=2)
    lg_col = jnp.repeat(lg, c, axis=2)
    tok = pl.BlockSpec((1, l, gw), lambda i, j: (i, 0, j))
    tokt = pl.BlockSpec((1, gw, l), lambda i, j: (i, j, 0))
    grp = lambda r, width: pl.BlockSpec((None, r, width), lambda i, j: (j, 0, 0))
    return pl.pallas_call(
        functools.partial(_ret_kernel, nchunk=nchunk, unroll=math.gcd(nchunk, RET_UNROLL)),
        grid=(b, ng),
        in_specs=[
            tok, tokt, tok, tok,
            pl.BlockSpec((1, gw, lc), lambda i, j: (i, j, 0)),
            pl.BlockSpec((1, lc, gw), lambda i, j: (i, 0, j)),
            grp(2, gw), grp(2, hpg * c),
        ],
        out_specs=tok,
        out_shape=jax.ShapeDtypeStruct((b, l, rw), BF16),
        scratch_shapes=[pltpu.VMEM((nchunk, npair, pw, pw), BF16), pltpu.VMEM((nchunk, npair, pw, pw), BF16),
                        pltpu.VMEM((npair, pw, pw), F32), pltpu.VMEM((npair, pw, pw), F32),
                        pltpu.VMEM((nchunk, c, hpg * c), BF16),
                        pltpu.VMEM((nchunk, c, gw), F32),
                        pltpu.VMEM((c, hpg * c), F32)] + [pltpu.VMEM((c, gw), F32)] * 4
        + [pltpu.VMEM((nchunk, npair, pw, 2 * pw), F32)],
        compiler_params=_params("arbitrary", "arbitrary"),
        name="ret",
    )(q, kt, v, sg, kct, vc, lg_lane, lg_col)


def _out_kernel(w4_ref, wr_ref, wo_ref, w1_ref, w2_ref, x_ref, mod_ref, fm_ref, z_ref, gab_ref,
                gain2_ref, fgain_ref, o_ref, *, ff_chunk):
    mod = mod_ref[:, pl.ds(pl.program_id(0), 1), :]
    g1, sh2, sc2, g2 = mod[2], mod[3], mod[4], mod[5]
    y_four = _dot(fm_ref[0], w4_ref[...])
    y_ret = _dot(z_ref[0], wr_ref[...])
    d = x_ref.shape[2]
    y = (jax.nn.sigmoid(gab_ref[0, :, :d].astype(F32)) * y_four
         + jax.nn.sigmoid(gab_ref[0, :, d:].astype(F32)) * y_ret)
    x1 = x_ref[0] + g1 * _dot(y.astype(BF16), wo_ref[...])
    h2 = _norm_mod(x1, gain2_ref[...], sh2, sc2).astype(BF16)
    dff = w1_ref.shape[1]
    acc = None
    for lo in range(0, dff, ff_chunk):
        hid = jnp.maximum(_dot(h2, w1_ref[:, lo:lo + ff_chunk]), 0.0)
        part = _dot((hid * hid).astype(BF16), w2_ref[lo:lo + ff_chunk, :])
        acc = part if acc is None else acc + part
    x2 = x1 + g2 * acc
    ms = jnp.mean(x2 * x2, axis=-1, keepdims=True)
    o_ref[0] = x2 * lax.rsqrt(ms + EPS) * fgain_ref[...]


def _out(x, mods, fm, z, gab, w4, wr, wo, gain2, w1, w2, fgain):
    b, l, d = x.shape
    tm = min(TOKEN_TILE, l)
    tok = lambda width: pl.BlockSpec((1, tm, width), lambda i, j: (i, j, 0))
    weights = (w4, wr, wo, w1, w2)
    return pl.pallas_call(
        functools.partial(_out_kernel, ff_chunk=min(1024, w1.shape[1])),
        grid=(b, l // tm),
        in_specs=[_const_spec(w.shape) for w in weights] + [
            tok(d),
            _const_spec(mods.shape),
            tok(fm.shape[2]), tok(z.shape[2]), tok(2 * d),
            _const_spec((1, d)), _const_spec((1, d)),
        ],
        out_specs=tok(d),
        out_shape=jax.ShapeDtypeStruct((b, l, d), F32),
        compiler_params=_params("arbitrary", "arbitrary"),
        name="out",
    )(*weights, x, mods, fm, z, gab, gain2, fgain)


def _dft_tables(l, gd):
    la = FFT_LA
    lb = l // la

    def cs(n):
        idx = np.arange(n)
        ang = 2.0 * np.pi * ((idx[:, None] * idx[None, :]) % n) / n
        return np.cos(ang), np.sin(ang)

    cb, sb = cs(lb)
    eye = np.eye(FFT_LA_TILE)
    half = lb // 2 + 1
    kw = np.concatenate([np.kron(cb[:half], eye), -np.kron(sb[:half], eye)], axis=0)
    ca, sa = cs(la)
    w2 = np.block([[ca, sa], [-sa, ca]])
    cc, sc = cs(gd)
    chan = np.concatenate([cc, sc], axis=0)
    tw = 2.0 * np.pi * (np.arange(lb)[:, None] * np.arange(la)[None, :]) / l
    tw = tw.reshape(lb, la // FFT_LA_TILE, FFT_LA_TILE).transpose(1, 0, 2).reshape(la // FFT_LA_TILE, -1)
    twc = np.repeat(np.cos(tw)[:, :, None], LANES, axis=2)
    tws = np.repeat(np.sin(tw)[:, :, None], LANES, axis=2)
    as_bf = lambda a: jnp.asarray(a, dtype=F32).astype(BF16)
    return as_bf(kw), as_bf(w2), as_bf(chan), jnp.asarray(twc, F32), jnp.asarray(tws, F32), la, lb


def _rope_tables(l):
    f32 = np.float32
    nf = RET_HEAD_DIM // 4
    inv = np.power(f32(ROPE_BASE), -np.arange(nf, dtype=f32) / f32(nf)).astype(f32)
    rows = l // GRID_W
    r, cc = np.meshgrid(np.arange(rows, dtype=f32), np.arange(GRID_W, dtype=f32), indexing="ij")
    ang = np.concatenate([r.reshape(-1)[:, None] * inv, cc.reshape(-1)[:, None] * inv], axis=-1).astype(f32)
    cos, sin = np.cos(ang).astype(f32), np.sin(ang).astype(f32)
    cos_h = np.concatenate([cos, cos], axis=1)
    sin_h = np.concatenate([-sin, sin], axis=1)
    reps = LANES // RET_HEAD_DIM
    return jnp.asarray(np.tile(cos_h, (1, reps))), jnp.asarray(np.tile(sin_h, (1, reps)))


def kernel(x, c, ctx, c_ctx, w_ada, b_ada, norm1_gain, w_in, four_w_out, ret_decay_logit,
           ret_gn_gain, ret_w_out, w_out, norm2_gain, w_mlp1, w_mlp2, final_gain):
    assert w_ada.shape[0] == 1, "single-layer block"
    b, l, d = x.shape
    lc = ctx.shape[1]
    fw = four_w_out.shape[1]
    rw = ret_w_out.shape[1]
    gd = fw // FOUR_GROUPS
    assert l % FFT_LA == 0 and l % RET_CHUNK == 0 and rw % GROUP_W == 0

    kw, w2, chan, twc, tws, la, lb = _dft_tables(l, gd)
    cosf, sinf = _rope_tables(l)

    later = [four_w_out[0], ret_w_out[0], w_out[0], w_mlp1[0], w_mlp2[0]]
    (u, q, kt, v, sg, gab), later_b, mods, kct, vc = _proj(
        x, c, c_ctx, w_ada[0], b_ada, ctx, norm1_gain, w_in[0], cosf, sinf, ret_gn_gain, later, fw, rw)

    fm = _fft(u, kw, twc, tws, w2, chan, la, lb, gd, 1.0 / math.sqrt(l * gd))

    log_gamma = jax.nn.log_sigmoid(ret_decay_logit[0].astype(F32))
    z = _retention(q, kt, v, sg, kct, vc, log_gamma)

    w4, wr, wo, w1, w2 = later_b
    return _out(x, mods, fm, z, gab, w4, wr, wo, norm2_gain, w1, w2, final_gain[None, :])
```

```python
import functools
import math

import jax
import jax.numpy as jnp
import numpy as np
from jax import lax
from jax.experimental import pallas as pl
from jax.experimental.pallas import tpu as pltpu

F32 = jnp.float32
BF16 = jnp.bfloat16

GRID_W = 64
FOUR_GROUPS = 4
RET_HEAD_DIM = 64
N_MOD = 6
ROPE_BASE = 10000.0
EPS = 1e-6

LANES = 128
MXU_DIM = 256
F32_SUBLANES = 8
BF16_SUBLANES = 16
VMEM_LIMIT_BYTES = 56 * 1024 * 1024

RET_CHUNK = 128
RET_UNROLL = 32
RET_NORM_CHUNKS = 4
HEADS_PER_GROUP = MXU_DIM // RET_HEAD_DIM
GROUP_W = HEADS_PER_GROUP * RET_HEAD_DIM
STATE_BLOCK_W = LANES
FFT_LA = 128
FFT_LA_TILE = BF16_SUBLANES
FFT_SCATTER_PAD = F32_SUBLANES
TOKEN_TILE = 512
PROJ_TOKEN_TILE = 512
WEIGHT_STAGE_COLS = 512
WEIGHT_STAGE_SLOTS = 6


def _dot(a, b):
    return jnp.dot(a, b, preferred_element_type=F32)


def _norm_mod(x, gain, shift, scale):
    ms = jnp.mean(x * x, axis=-1, keepdims=True)
    y = x * lax.rsqrt(ms + EPS) * gain
    return y * (1.0 + scale) + shift


def _tile_lanes(t, reps):
    return jnp.concatenate([t] * reps, axis=1) if reps > 1 else t


def _const_spec(shape):
    nd = len(shape)
    return pl.BlockSpec(shape, lambda *_: (0,) * nd, pipeline_mode=pl.Buffered(1))


def _params(*sem):
    return pltpu.CompilerParams(dimension_semantics=sem, vmem_limit_bytes=VMEM_LIMIT_BYTES)


def _is_first_step(grid_rank):
    first = pl.program_id(0) == 0
    for axis in range(1, grid_rank):
        first = first & (pl.program_id(axis) == 0)
    return first


def _proj_prepare(before_body, c_ref, cctx_ref, wada_hbm, bada_ref, w_hbm, ctx_ref, gain_ref,
                  mods_ref, kct_ref, vc_ref, w_ref, mod_ref, cv_ref, stage_ref, sem_ref, *, fw, rw, scale):
    nb = c_ref.shape[0]
    d = w_ref.shape[0]
    if before_body:
        cv_ref[...] = jnp.zeros_like(cv_ref)
        cv_ref[0:nb, :] = c_ref[...]
        cv_ref[nb:nb + 1, :] = cctx_ref[...]
    cv = cv_ref[...]
    s = (cv * jax.nn.sigmoid(cv)).astype(BF16)

    slots, _, sc = stage_ref.shape
    ada = [(wada_hbm, c0) for c0 in range(0, wada_hbm.shape[1], sc)]
    win = [(w_hbm, c0) for c0 in range(0, w_hbm.shape[1], sc)]
    body_mods = 2 * d // sc
    chunks = ada[:body_mods] + win + ada[body_mods:]
    split = body_mods + len(win)
    ctx_after = body_mods + (fw + 3 * rw) // sc - 1

    def copy(k):
        src, c0 = chunks[k]
        slot = k % slots
        return pltpu.make_async_copy(src.at[:, pl.ds(c0, sc)], stage_ref.at[slot], sem_ref.at[slot])

    if before_body:
        for k in range(min(slots - 1, len(chunks))):
            copy(k).start()
    for k in range(0, split) if before_body else range(split, len(chunks)):
        src, c0 = chunks[k]
        if k + slots - 1 < len(chunks):
            copy(k + slots - 1).start()
        copy(k).wait()
        blk = stage_ref[k % slots].astype(BF16)
        if src is wada_hbm:
            j, off = divmod(c0, d)
            hw = sc // 2
            mod_ref[j, :, off:off + hw] = _dot(s, blk[:, :hw]) + bada_ref[:, c0:c0 + hw]
            mod_ref[j, :, off + hw:off + sc] = _dot(s, blk[:, hw:]) + bada_ref[:, c0 + hw:c0 + sc]
        else:
            w_ref[:, c0:c0 + sc] = blk
        if k == ctx_after:
            for i in range(nb):
                h = _norm_mod(ctx_ref[i], gain_ref[...],
                              mod_ref[0, nb:nb + 1, :], mod_ref[1, nb:nb + 1, :]).astype(BF16)
                kct_ref[i] = (_dot(h, w_ref[:, fw + rw:fw + 2 * rw]) * scale).T.astype(BF16)
                vc_ref[i] = _dot(h, w_ref[:, fw + 2 * rw:fw + 3 * rw]).astype(BF16)
    if not before_body:
        mods_ref[...] = mod_ref[...]


def _proj_kernel(*refs, fw, rw, d, scale, ncast):
    x_ref, cos_ref, sin_ref, w_hbm, wada_hbm = refs[:5]
    cast_in = refs[5:5 + ncast]
    gain_ref, gng_ref, c_ref, cctx_ref, bada_ref, ctx_ref = refs[5 + ncast:11 + ncast]
    outs = refs[11 + ncast:]
    u_ref, q_ref, kt_ref, v_ref, sg_ref, gab_ref = outs[:6]
    cast_out = outs[6:6 + ncast]
    mods_ref, kct_ref, vc_ref = outs[6 + ncast:9 + ncast]
    w_ref, mod_ref, cv_ref, stage_ref, sem_ref = outs[9 + ncast:]

    prepare = functools.partial(
        _proj_prepare, c_ref=c_ref, cctx_ref=cctx_ref, wada_hbm=wada_hbm, bada_ref=bada_ref, w_hbm=w_hbm,
        ctx_ref=ctx_ref, gain_ref=gain_ref, mods_ref=mods_ref, kct_ref=kct_ref, vc_ref=vc_ref, w_ref=w_ref,
        mod_ref=mod_ref, cv_ref=cv_ref, stage_ref=stage_ref, sem_ref=sem_ref, fw=fw, rw=rw, scale=scale)
    pl.when(_is_first_step(2))(functools.partial(prepare, True))

    for src, dst in zip(cast_in, cast_out):
        dst[...] = src[...].astype(BF16)

    mod = mod_ref[0:2, pl.ds(pl.program_id(0), 1), :]
    h = _norm_mod(x_ref[0], gain_ref[...], mod[0], mod[1]).astype(BF16)
    tm = h.shape[0]

    def proj(lo, width):
        return _dot(h, w_ref[:, lo:lo + width])

    u_ref[0] = proj(0, fw).astype(BF16)

    reps = rw // cos_ref.shape[1]
    cosf = _tile_lanes(cos_ref[...], reps)
    sinf = _tile_lanes(sin_ref[...], reps)
    lane = lax.broadcasted_iota(jnp.int32, (tm, rw), 1)
    first_half = (lane & (RET_HEAD_DIM // 2)) == 0

    def rope(t):
        rot = jnp.where(first_half,
                        pltpu.roll(t, rw - RET_HEAD_DIM // 2, 1),
                        pltpu.roll(t, RET_HEAD_DIM // 2, 1))
        return t * cosf + rot * sinf

    q_ref[0] = rope(proj(fw, rw)).astype(BF16)
    kt_ref[0] = (rope(proj(fw + rw, rw)) * scale).T.astype(BF16)
    v_ref[0] = proj(fw + 2 * rw, rw).astype(BF16)
    g = proj(fw + 3 * rw, rw)
    sg_ref[0] = (g * jax.nn.sigmoid(g) * gng_ref[...]).astype(BF16)
    gab_ref[0, :, :d] = proj(fw + 4 * rw, d).astype(BF16)
    gab_ref[0, :, d:] = proj(fw + 4 * rw + d, d).astype(BF16)

    pl.when(_is_first_step(2))(functools.partial(prepare, False))


def _proj(x, c, c_ctx, w_ada, b_ada, ctx, gain, w_in, cosf, sinf, gn_gain, later_weights, fw, rw):
    b, l, d = x.shape
    lc = ctx.shape[1]
    tm = min(PROJ_TOKEN_TILE, l)
    nj = l // tm
    steps = b * nj
    rows = -(-(b + 1) // F32_SUBLANES) * F32_SUBLANES
    sc = WEIGHT_STAGE_COLS
    assert w_ada.shape == (d, N_MOD * d) and d % sc == 0 and w_in.shape[1] % sc == 0
    tok = lambda width: pl.BlockSpec((1, tm, width), lambda i, j: (i, j, 0))
    out = lambda width: jax.ShapeDtypeStruct((b, l, width), BF16)
    sliced = [w.reshape(steps, w.shape[0] // steps, w.shape[1]) for w in later_weights]
    for w in sliced:
        assert w.shape[1] % BF16_SUBLANES == 0
    cast_spec = lambda w: pl.BlockSpec((1,) + w.shape[1:], lambda i, j: (i * nj + j, 0, 0))
    whole = lambda shape: pl.BlockSpec(shape, lambda i, j: (0,) * len(shape))
    res = pl.pallas_call(
        functools.partial(_proj_kernel, fw=fw, rw=rw, d=d, scale=RET_HEAD_DIM ** -0.5, ncast=len(sliced)),
        grid=(b, nj),
        in_specs=[
            tok(d),
            pl.BlockSpec((tm, cosf.shape[1]), lambda i, j: (j, 0)),
            pl.BlockSpec((tm, sinf.shape[1]), lambda i, j: (j, 0)),
            pl.BlockSpec(memory_space=pl.ANY),
            pl.BlockSpec(memory_space=pl.ANY),
        ] + [cast_spec(w) for w in sliced] + [
            _const_spec((1, d)), _const_spec((1, rw)), _const_spec((b, d)), _const_spec((1, d)),
            _const_spec((1, N_MOD * d)), _const_spec((b, lc, d)),
        ],
        out_specs=[tok(fw), tok(rw), pl.BlockSpec((1, rw, tm), lambda i, j: (i, 0, j)),
                   tok(rw), tok(rw), tok(2 * d)] + [cast_spec(w) for w in sliced]
        + [whole((N_MOD, rows, d)), whole((b, rw, lc)), whole((b, lc, rw))],
        out_shape=[out(fw), out(rw), jax.ShapeDtypeStruct((b, rw, l), BF16),
                   out(rw), out(rw), out(2 * d)]
        + [jax.ShapeDtypeStruct(w.shape, BF16) for w in sliced]
        + [jax.ShapeDtypeStruct((N_MOD, rows, d), F32),
           jax.ShapeDtypeStruct((b, rw, lc), BF16), jax.ShapeDtypeStruct((b, lc, rw), BF16)],
        scratch_shapes=[pltpu.VMEM(w_in.shape, BF16),
                        pltpu.VMEM((N_MOD, rows, d), F32),
                        pltpu.VMEM((rows, d), F32),
                        pltpu.VMEM((WEIGHT_STAGE_SLOTS, d, sc), F32),
                        pltpu.SemaphoreType.DMA((WEIGHT_STAGE_SLOTS,))],
        compiler_params=_params("arbitrary", "arbitrary"),
        name="proj",
    )(x, cosf, sinf, w_in, w_ada, *sliced, gain, gn_gain, c, c_ctx[None, :], b_ada, ctx)
    ncast = len(sliced)
    casts = [cc.reshape(w.shape) for cc, w in zip(res[6:6 + ncast], later_weights)]
    mods, kct, vc = res[6 + ncast:]
    return res[:6], casts, mods, kct, vc


def _fft_kernel(u_ref, kw_ref, twc_ref, tws_ref, w2_ref, cs_ref, o_ref, tr_ref, ti_ref, scr_ref,
                *, la, lb, gd, npb, scale):
    r = FFT_LA_TILE
    rows = lb * r
    fw = u_ref.shape[2]
    ngroups = fw // gd
    reps = fw // twc_ref.shape[2]
    pitch = scr_ref.shape[1] // la

    @pl.when(pl.program_id(0) == 0)
    def _():
        scr_ref[...] = jnp.zeros_like(scr_ref)

    for j in range(la // r):
        u = jnp.concatenate([u_ref[0, k * la + j * r:k * la + (j + 1) * r, :] for k in range(lb)], axis=0)
        half = lb // 2 + 1
        t = _dot(kw_ref[...], u)
        for k in range(lb):
            src = k if k < half else lb - k
            a = t[src * r:(src + 1) * r]
            b = t[(half + src) * r:(half + src + 1) * r]
            ct = _tile_lanes(twc_ref[j, k * r:(k + 1) * r, :], reps)
            st = _tile_lanes(tws_ref[j, k * r:(k + 1) * r, :], reps)
            if k < half:
                tr, ti = a * ct + b * st, b * ct - a * st
            else:
                tr, ti = a * ct - b * st, -(b * ct) - a * st
            tr_ref[k * la + j * r:k * la + (j + 1) * r, :] = tr.astype(BF16)
            ti_ref[k * la + j * r:k * la + (j + 1) * r, :] = ti.astype(BF16)

    for pb in range(lb // npb):
        xs = []
        for p in range(npb):
            lo = (pb * npb + p) * la
            t = jnp.concatenate([tr_ref[lo:lo + la, :], ti_ref[lo:lo + la, :]], axis=0)
            xs.append(_dot(w2_ref[...], t).astype(BF16))
        for gi in range(ngroups):
            cols = slice(gi * gd, (gi + 1) * gd)
            lhs = jnp.concatenate([jnp.concatenate([x[:la, cols], x[la:, cols]], axis=1) for x in xs], axis=0)
            fg = _dot(lhs, cs_ref[...]) * scale
            for p in range(npb):
                scr_ref[gi, pl.ds(p, la, stride=pitch), :] = fg[p * la:(p + 1) * la]
        full = jnp.concatenate([scr_ref[gi].reshape(la, pitch, gd)[:, :npb, :] for gi in range(ngroups)], axis=2)
        o_ref[0, :, pb * npb:(pb + 1) * npb, :] = full.astype(BF16)


def _fft(u, kw, twc, tws, w2, cs, la, lb, gd, scale):
    b, l, fw = u.shape
    npb = min(BF16_SUBLANES, lb)
    out = pl.pallas_call(
        functools.partial(_fft_kernel, la=la, lb=lb, gd=gd, npb=npb, scale=scale),
        grid=(b,),
        in_specs=[pl.BlockSpec((1, l, fw), lambda i: (i, 0, 0)),
                  _const_spec(kw.shape), _const_spec(twc.shape), _const_spec(tws.shape),
                  _const_spec(w2.shape), _const_spec(cs.shape)],
        out_specs=pl.BlockSpec((1, la, lb, fw), lambda i: (i, 0, 0, 0)),
        out_shape=jax.ShapeDtypeStruct((b, la, lb, fw), BF16),
        scratch_shapes=[pltpu.VMEM((l, fw), BF16), pltpu.VMEM((l, fw), BF16),
                        pltpu.VMEM((fw // gd, la * (npb + FFT_SCATTER_PAD), gd), F32)],
        compiler_params=_params("arbitrary"),
        name="fft",
    )(u, kw, twc, tws, w2, cs)
    return out.reshape(b, l, fw)


def _ret_kernel(q_ref, kt_ref, v_ref, sg_ref, kct_ref, vc_ref, lgl_ref, lgc_ref,
                z_ref, sf_ref, sb_ref, stf_ref, stb_ref, p_ref, o_ref,
                dall_ref, qdf_ref, qdb_ref, kdf_ref, kdb_ref, inc_ref, *, nchunk, unroll):
    c = RET_CHUNK
    gw = GROUP_W
    lc = vc_ref.shape[1]

    lgf, lgb = lgl_ref[0:1, :], lgl_ref[1:2, :]
    pos = lax.broadcasted_iota(jnp.int32, (c, gw), 0).astype(F32)
    qdf_ref[...] = jnp.exp(lgf * (pos + 1.0))
    qdb_ref[...] = jnp.exp(lgb * (c - pos))
    kdf_ref[...] = jnp.exp(lgf * (c - 1.0 - pos))
    kdb_ref[...] = jnp.exp(lgb * pos)
    cdf = jnp.exp(lgf * c)
    cdb = jnp.exp(lgb * c)
    cpos = lax.broadcasted_iota(jnp.int32, (lc, gw), 0).astype(F32)
    wcf = jnp.exp(lgf * (lc - 1.0 - cpos))
    wcb = jnp.exp(lgb * cpos)
    si = lax.broadcasted_iota(jnp.int32, (c, HEADS_PER_GROUP * c), 0)
    sj = lax.broadcasted_iota(jnp.int32, (c, HEADS_PER_GROUP * c), 1) & (c - 1)
    diff = (si - sj).astype(F32)
    dall_ref[...] = (jnp.where(diff >= 0, jnp.exp(lgc_ref[0:1, :] * jnp.maximum(diff, 0.0)), 0.0)
                     + jnp.where(diff <= 0, jnp.exp(lgc_ref[1:2, :] * jnp.maximum(-diff, 0.0)), 0.0))

    pw = STATE_BLOCK_W
    npair = gw // pw
    same_head_p = (lax.broadcasted_iota(jnp.int32, (pw, pw), 0) // RET_HEAD_DIM
                   == lax.broadcasted_iota(jnp.int32, (pw, pw), 1) // RET_HEAD_DIM)

    def diag_blocks(t):
        return jnp.where(same_head_p, t, 0.0)

    def weighted(t, w):
        return (t.astype(F32) * w).astype(BF16)

    def rows(ref, n):
        return ref[0, pl.ds(pl.multiple_of(n * c, c), c), :]

    def kt_chunk(n):
        return kt_ref[0, :, pl.ds(pl.multiple_of(n * c, c), c)]

    def state_update(st_ref, kt, vw, cd):
        for p in range(npair):
            cols = slice(p * pw, (p + 1) * pw)
            st_ref[p] = st_ref[p] * cd[:, cols] + diag_blocks(_dot(kt[cols, :], vw[:, cols]))

    def full_state(s_ref, n):
        zero = jnp.zeros((pw, pw), BF16)
        return jnp.concatenate(
            [jnp.concatenate([s_ref[n, p] if q == p else zero for q in range(npair)], axis=1)
             for p in range(npair)], axis=0)

    kct = kct_ref[0]
    vc = vc_ref[0]
    stf_ref[...] = jnp.zeros_like(stf_ref)
    stb_ref[...] = jnp.zeros_like(stb_ref)
    state_update(stf_ref, kct, weighted(vc, wcf), cdf)
    state_update(stb_ref, kct, weighted(vc, wcb), cdb)

    def increment_body(n, carry):
        ktn = kt_chunk(n)
        vn = rows(v_ref, n)
        vwf = weighted(vn, kdf_ref[...])
        vwb = weighted(vn, kdb_ref[...])
        for p in range(npair):
            cols = slice(p * pw, (p + 1) * pw)
            inc_ref[n, p] = _dot(ktn[cols, :], jnp.concatenate([vwf[:, cols], vwb[:, cols]], axis=1))
        return carry

    lax.fori_loop(0, nchunk, increment_body, 0, unroll=unroll)

    def scan_body(i, carry):
        nf = i
        nb = nchunk - 1 - i
        sf_ref[nf] = stf_ref[...].astype(BF16)
        sb_ref[nb] = stb_ref[...].astype(BF16)
        for p in range(npair):
            cols = slice(p * pw, (p + 1) * pw)
            stf_ref[p] = stf_ref[p] * cdf[:, cols] + diag_blocks(inc_ref[nf, p, :, :pw])
            stb_ref[p] = stb_ref[p] * cdb[:, cols] + diag_blocks(inc_ref[nb, p, :, pw:])
        return carry

    lax.fori_loop(0, nchunk, scan_body, 0, unroll=unroll)

    lane = lax.broadcasted_iota(jnp.int32, (c, gw), 1)
    sub = lax.broadcasted_iota(jnp.int32, (gw, c), 0)
    lane_masks = [(lane >= h * RET_HEAD_DIM) & (lane < (h + 1) * RET_HEAD_DIM)
                  for h in range(HEADS_PER_GROUP)]
    sub_masks = [(sub >= h * RET_HEAD_DIM) & (sub < (h + 1) * RET_HEAD_DIM)
                 for h in range(HEADS_PER_GROUP)]
    same_head = (lax.broadcasted_iota(jnp.int32, (gw, gw), 0) // RET_HEAD_DIM
                 == lax.broadcasted_iota(jnp.int32, (gw, gw), 1) // RET_HEAD_DIM)
    bd_mean = jnp.where(same_head, 1.0 / RET_HEAD_DIM, 0.0).astype(BF16)

    def score_body(n, carry):
        ktn = kt_chunk(n)
        zk = jnp.zeros_like(ktn)
        kbd = jnp.concatenate([jnp.where(m, ktn, zk) for m in sub_masks], axis=1)
        p_ref[n] = (_dot(rows(q_ref, n), kbd) * dall_ref[...]).astype(BF16)
        return carry

    lax.fori_loop(0, nchunk, score_body, 0, unroll=unroll)

    def mix_body(n, carry):
        qn = rows(q_ref, n)
        vn = rows(v_ref, n)
        zv = jnp.zeros_like(vn)
        vbd = jnp.concatenate([jnp.where(m, vn, zv) for m in lane_masks], axis=0)
        o_ref[n] = (_dot(p_ref[n], vbd) + qdf_ref[...] * _dot(qn, full_state(sf_ref, n))
                    + qdb_ref[...] * _dot(qn, full_state(sb_ref, n)))
        return carry

    lax.fori_loop(0, nchunk, mix_body, 0, unroll=unroll)

    nc = math.gcd(nchunk, RET_NORM_CHUNKS)

    def norm_body(m, carry):
        o = o_ref[pl.ds(m * nc, nc)].reshape(nc * c, gw)
        tokens = pl.ds(pl.multiple_of(m * (nc * c), nc * c), nc * c)
        ms = _dot((o * o).astype(BF16), bd_mean)
        z = sg_ref[0, tokens, :].astype(F32) * (o * lax.rsqrt(ms + EPS))
        z_ref[0, tokens, :] = z.astype(BF16)
        return carry

    lax.fori_loop(0, nchunk // nc, norm_body, 0, unroll=max(unroll // nc, 1))


def _retention(q, kt, v, sg, kct, vc, log_gamma):
    b, l, rw = q.shape
    lc = vc.shape[1]
    c = RET_CHUNK
    gw = GROUP_W
    hpg = HEADS_PER_GROUP
    ng = rw // gw
    nchunk = l // c
    pw = STATE_BLOCK_W
    npair = gw // pw
    assert c & (c - 1) == 0
    lg = log_gamma.reshape(2, ng, hpg).transpose(1, 0, 2)
    lg_lane = jnp.repeat(lg, RET_HEAD_DIM, axis=2)
    lg_col = jnp.repeat(lg, c, axis=2)
    tok = pl.BlockSpec((1, l, gw), lambda i, j: (i, 0, j))
    tokt = pl.BlockSpec((1, gw, l), lambda i, j: (i, j, 0))
    grp = lambda r, width: pl.BlockSpec((None, r, width), lambda i, j: (j, 0, 0))
    return pl.pallas_call(
        functools.partial(_ret_kernel, nchunk=nchunk, unroll=math.gcd(nchunk, RET_UNROLL)),
        grid=(b, ng),
        in_specs=[
            tok, tokt, tok, tok,
            pl.BlockSpec((1, gw, lc), lambda i, j: (i, j, 0)),
            pl.BlockSpec((1, lc, gw), lambda i, j: (i, 0, j)),
            grp(2, gw), grp(2, hpg * c),
        ],
        out_specs=tok,
        out_shape=jax.ShapeDtypeStruct((b, l, rw), BF16),
        scratch_shapes=[pltpu.VMEM((nchunk, npair, pw, pw), BF16), pltpu.VMEM((nchunk, npair, pw, pw), BF16),
                        pltpu.VMEM((npair, pw, pw), F32), pltpu.VMEM((npair, pw, pw), F32),
                        pltpu.VMEM((nchunk, c, hpg * c), BF16),
                        pltpu.VMEM((nchunk, c, gw), F32),
                        pltpu.VMEM((c, hpg * c), F32)] + [pltpu.VMEM((c, gw), F32)] * 4
        + [pltpu.VMEM((nchunk, npair, pw, 2 * pw), F32)],
        compiler_params=_params("arbitrary", "arbitrary"),
        name="ret",
    )(q, kt, v, sg, kct, vc, lg_lane, lg_col)


def _out_kernel(w4_ref, wr_ref, wo_ref, w1_ref, w2_ref, x_ref, mod_ref, fm_ref, z_ref, gab_ref,
                gain2_ref, fgain_ref, o_ref, *, ff_chunk):
    mod = mod_ref[:, pl.ds(pl.program_id(0), 1), :]
    g1, sh2, sc2, g2 = mod[2], mod[3], mod[4], mod[5]
    y_four = _dot(fm_ref[0], w4_ref[...])
    y_ret = _dot(z_ref[0], wr_ref[...])
    d = x_ref.shape[2]
    y = (jax.nn.sigmoid(gab_ref[0, :, :d].astype(F32)) * y_four
         + jax.nn.sigmoid(gab_ref[0, :, d:].astype(F32)) * y_ret)
    x1 = x_ref[0] + g1 * _dot(y.astype(BF16), wo_ref[...])
    h2 = _norm_mod(x1, gain2_ref[...], sh2, sc2).astype(BF16)
    dff = w1_ref.shape[1]
    acc = None
    for lo in range(0, dff, ff_chunk):
        hid = jnp.maximum(_dot(h2, w1_ref[:, lo:lo + ff_chunk]), 0.0)
        part = _dot((hid * hid).astype(BF16), w2_ref[lo:lo + ff_chunk, :])
        acc = part if acc is None else acc + part
    x2 = x1 + g2 * acc
    ms = jnp.mean(x2 * x2, axis=-1, keepdims=True)
    o_ref[0] = x2 * lax.rsqrt(ms + EPS) * fgain_ref[...]


def _out(x, mods, fm, z, gab, w4, wr, wo, gain2, w1, w2, fgain):
    b, l, d = x.shape
    tm = min(TOKEN_TILE, l)
    tok = lambda width: pl.BlockSpec((1, tm, width), lambda i, j: (i, j, 0))
    weights = (w4, wr, wo, w1, w2)
    return pl.pallas_call(
        functools.partial(_out_kernel, ff_chunk=min(1024, w1.shape[1])),
        grid=(b, l // tm),
        in_specs=[_const_spec(w.shape) for w in weights] + [
            tok(d),
            _const_spec(mods.shape),
            tok(fm.shape[2]), tok(z.shape[2]), tok(2 * d),
            _const_spec((1, d)), _const_spec((1, d)),
        ],
        out_specs=tok(d),
        out_shape=jax.ShapeDtypeStruct((b, l, d), F32),
        compiler_params=_params("arbitrary", "arbitrary"),
        name="out",
    )(*weights, x, mods, fm, z, gab, gain2, fgain)


def _dft_tables(l, gd):
    la = FFT_LA
    lb = l // la

    def cs(n):
        idx = np.arange(n)
        ang = 2.0 * np.pi * ((idx[:, None] * idx[None, :]) % n) / n
        return np.cos(ang), np.sin(ang)

    cb, sb = cs(lb)
    eye = np.eye(FFT_LA_TILE)
    half = lb // 2 + 1
    kw = np.concatenate([np.kron(cb[:half], eye), -np.kron(sb[:half], eye)], axis=0)
    ca, sa = cs(la)
    w2 = np.block([[ca, sa], [-sa, ca]])
    cc, sc = cs(gd)
    chan = np.concatenate([cc, sc], axis=0)
    tw = 2.0 * np.pi * (np.arange(lb)[:, None] * np.arange(la)[None, :]) / l
    tw = tw.reshape(lb, la // FFT_LA_TILE, FFT_LA_TILE).transpose(1, 0, 2).reshape(la // FFT_LA_TILE, -1)
    twc = np.repeat(np.cos(tw)[:, :, None], LANES, axis=2)
    tws = np.repeat(np.sin(tw)[:, :, None], LANES, axis=2)
    as_bf = lambda a: jnp.asarray(a, dtype=F32).astype(BF16)
    return as_bf(kw), as_bf(w2), as_bf(chan), jnp.asarray(twc, F32), jnp.asarray(tws, F32), la, lb


def _rope_tables(l):
    f32 = np.float32
    nf = RET_HEAD_DIM // 4
    inv = np.power(f32(ROPE_BASE), -np.arange(nf, dtype=f32) / f32(nf)).astype(f32)
    rows = l // GRID_W
    r, cc = np.meshgrid(np.arange(rows, dtype=f32), np.arange(GRID_W, dtype=f32), indexing="ij")
    ang = np.concatenate([r.reshape(-1)[:, None] * inv, cc.reshape(-1)[:, None] * inv], axis=-1).astype(f32)
    cos, sin = np.cos(ang).astype(f32), np.sin(ang).astype(f32)
    cos_h = np.concatenate([cos, cos], axis=1)
    sin_h = np.concatenate([-sin, sin], axis=1)
    reps = LANES // RET_HEAD_DIM
    return jnp.asarray(np.tile(cos_h, (1, reps))), jnp.asarray(np.tile(sin_h, (1, reps)))


def kernel(x, c, ctx, c_ctx, w_ada, b_ada, norm1_gain, w_in, four_w_out, ret_decay_logit,
           ret_gn_gain, ret_w_out, w_out, norm2_gain, w_mlp1, w_mlp2, final_gain):
    assert w_ada.shape[0] == 1, "single-layer block"
    b, l, d = x.shape
    lc = ctx.shape[1]
    fw = four_w_out.shape[1]
    rw = ret_w_out.shape[1]
    gd = fw // FOUR_GROUPS
    assert l % FFT_LA == 0 and l % RET_CHUNK == 0 and rw % GROUP_W == 0

    kw, w2, chan, twc, tws, la, lb = _dft_tables(l, gd)
    cosf, sinf = _rope_tables(l)

    later = [four_w_out[0], ret_w_out[0], w_out[0], w_mlp1[0], w_mlp2[0]]
    (u, q, kt, v, sg, gab), later_b, mods, kct, vc = _proj(
        x, c, c_ctx, w_ada[0], b_ada, ctx, norm1_gain, w_in[0], cosf, sinf, ret_gn_gain, later, fw, rw)

    fm = _fft(u, kw, twc, tws, w2, chan, la, lb, gd, 1.0 / math.sqrt(l * gd))

    log_gamma = jax.nn.log_sigmoid(ret_decay_logit[0].astype(F32))
    z = _retention(q, kt, v, sg, kct, vc, log_gamma)

    w4, wr, wo, w1, w2 = later_b
    return _out(x, mods, fm, z, gab, w4, wr, wo, norm2_gain, w1, w2, final_gain[None, :])
```

```python
import functools
import math

import jax
import jax.numpy as jnp
import numpy as np
from jax import lax
from jax.experimental import pallas as pl
from jax.experimental.pallas import tpu as pltpu

F32 = jnp.float32
BF16 = jnp.bfloat16

GRID_W = 64
FOUR_GROUPS = 4
RET_HEAD_DIM = 64
N_MOD = 6
ROPE_BASE = 10000.0
EPS = 1e-6

LANES = 128
MXU_DIM = 256
F32_SUBLANES = 8
BF16_SUBLANES = 16
VMEM_LIMIT_BYTES = 56 * 1024 * 1024

RET_CHUNK = 128
RET_UNROLL = 32
RET_NORM_CHUNKS = 4
HEADS_PER_GROUP = MXU_DIM // RET_HEAD_DIM
GROUP_W = HEADS_PER_GROUP * RET_HEAD_DIM
STATE_BLOCK_W = LANES
FFT_LA = 128
FFT_LA_TILE = BF16_SUBLANES
FFT_SCATTER_PAD = F32_SUBLANES
TOKEN_TILE = 512
PROJ_TOKEN_TILE = 512
WEIGHT_STAGE_COLS = 512
WEIGHT_STAGE_SLOTS = 6


def _dot(a, b):
    return jnp.dot(a, b, preferred_element_type=F32)


def _norm_mod(x, gain, shift, scale):
    ms = jnp.mean(x * x, axis=-1, keepdims=True)
    y = x * lax.rsqrt(ms + EPS) * gain
    return y * (1.0 + scale) + shift


def _tile_lanes(t, reps):
    return jnp.concatenate([t] * reps, axis=1) if reps > 1 else t


def _const_spec(shape):
    nd = len(shape)
    return pl.BlockSpec(shape, lambda *_: (0,) * nd, pipeline_mode=pl.Buffered(1))


def _params(*sem):
    return pltpu.CompilerParams(dimension_semantics=sem, vmem_limit_bytes=VMEM_LIMIT_BYTES)


def _is_first_step(grid_rank):
    first = pl.program_id(0) == 0
    for axis in range(1, grid_rank):
        first = first & (pl.program_id(axis) == 0)
    return first


def _proj_prepare(before_body, c_ref, cctx_ref, wada_hbm, bada_ref, w_hbm, ctx_ref, gain_ref,
                  mods_ref, kct_ref, vc_ref, w_ref, mod_ref, cv_ref, stage_ref, sem_ref, *, fw, rw, scale):
    nb = c_ref.shape[0]
    d = w_ref.shape[0]
    if before_body:
        cv_ref[...] = jnp.zeros_like(cv_ref)
        cv_ref[0:nb, :] = c_ref[...]
        cv_ref[nb:nb + 1, :] = cctx_ref[...]
    cv = cv_ref[...]
    s = (cv * jax.nn.sigmoid(cv)).astype(BF16)

    slots, _, sc = stage_ref.shape
    ada = [(wada_hbm, c0) for c0 in range(0, wada_hbm.shape[1], sc)]
    win = [(w_hbm, c0) for c0 in range(0, w_hbm.shape[1], sc)]
    body_mods = 2 * d // sc
    chunks = ada[:body_mods] + win + ada[body_mods:]
    split = body_mods + len(win)
    ctx_after = body_mods + (fw + 3 * rw) // sc - 1

    def copy(k):
        src, c0 = chunks[k]
        slot = k % slots
        return pltpu.make_async_copy(src.at[:, pl.ds(c0, sc)], stage_ref.at[slot], sem_ref.at[slot])

    if before_body:
        for k in range(min(slots - 1, len(chunks))):
            copy(k).start()
    for k in range(0, split) if before_body else range(split, len(chunks)):
        src, c0 = chunks[k]
        if k + slots - 1 < len(chunks):
            copy(k + slots - 1).start()
        copy(k).wait()
        blk = stage_ref[k % slots].astype(BF16)
        if src is wada_hbm:
            j, off = divmod(c0, d)
            hw = sc // 2
            mod_ref[j, :, off:off + hw] = _dot(s, blk[:, :hw]) + bada_ref[:, c0:c0 + hw]
            mod_ref[j, :, off + hw:off + sc] = _dot(s, blk[:, hw:]) + bada_ref[:, c0 + hw:c0 + sc]
        else:
            w_ref[:, c0:c0 + sc] = blk
        if k == ctx_after:
            for i in range(nb):
                h = _norm_mod(ctx_ref[i], gain_ref[...],
                              mod_ref[0, nb:nb + 1, :], mod_ref[1, nb:nb + 1, :]).astype(BF16)
                kct_ref[i] = (_dot(h, w_ref[:, fw + rw:fw + 2 * rw]) * scale).T.astype(BF16)
                vc_ref[i] = _dot(h, w_ref[:, fw + 2 * rw:fw + 3 * rw]).astype(BF16)
    if not before_body:
        mods_ref[...] = mod_ref[...]


def _proj_kernel(*refs, fw, rw, d, scale, ncast):
    x_ref, cos_ref, sin_ref, w_hbm, wada_hbm = refs[:5]
    cast_in = refs[5:5 + ncast]
    gain_ref, gng_ref, c_ref, cctx_ref, bada_ref, ctx_ref = refs[5 + ncast:11 + ncast]
    outs = refs[11 + ncast:]
    u_ref, q_ref, kt_ref, v_ref, sg_ref, gab_ref = outs[:6]
    cast_out = outs[6:6 + ncast]
    mods_ref, kct_ref, vc_ref = outs[6 + ncast:9 + ncast]
    w_ref, mod_ref, cv_ref, stage_ref, sem_ref = outs[9 + ncast:]

    prepare = functools.partial(
        _proj_prepare, c_ref=c_ref, cctx_ref=cctx_ref, wada_hbm=wada_hbm, bada_ref=bada_ref, w_hbm=w_hbm,
        ctx_ref=ctx_ref, gain_ref=gain_ref, mods_ref=mods_ref, kct_ref=kct_ref, vc_ref=vc_ref, w_ref=w_ref,
        mod_ref=mod_ref, cv_ref=cv_ref, stage_ref=stage_ref, sem_ref=sem_ref, fw=fw, rw=rw, scale=scale)
    pl.when(_is_first_step(2))(functools.partial(prepare, True))

    for src, dst in zip(cast_in, cast_out):
        dst[...] = src[...].astype(BF16)

    mod = mod_ref[0:2, pl.ds(pl.program_id(0), 1), :]
    h = _norm_mod(x_ref[0], gain_ref[...], mod[0], mod[1]).astype(BF16)
    tm = h.shape[0]

    def proj(lo, width):
        return _dot(h, w_ref[:, lo:lo + width])

    u_ref[0] = proj(0, fw).astype(BF16)

    reps = rw // cos_ref.shape[1]
    cosf = _tile_lanes(cos_ref[...], reps)
    sinf = _tile_lanes(sin_ref[...], reps)
    lane = lax.broadcasted_iota(jnp.int32, (tm, rw), 1)
    first_half = (lane & (RET_HEAD_DIM // 2)) == 0

    def rope(t):
        rot = jnp.where(first_half,
                        pltpu.roll(t, rw - RET_HEAD_DIM // 2, 1),
                        pltpu.roll(t, RET_HEAD_DIM // 2, 1))
        return t * cosf + rot * sinf

    q_ref[0] = rope(proj(fw, rw)).astype(BF16)
    kt_ref[0] = (rope(proj(fw + rw, rw)) * scale).T.astype(BF16)
    v_ref[0] = proj(fw + 2 * rw, rw).astype(BF16)
    g = proj(fw + 3 * rw, rw)
    sg_ref[0] = (g * jax.nn.sigmoid(g) * gng_ref[...]).astype(BF16)
    gab_ref[0, :, :d] = proj(fw + 4 * rw, d).astype(BF16)
    gab_ref[0, :, d:] = proj(fw + 4 * rw + d, d).astype(BF16)

    pl.when(_is_first_step(2))(functools.partial(prepare, False))


def _proj(x, c, c_ctx, w_ada, b_ada, ctx, gain, w_in, cosf, sinf, gn_gain, later_weights, fw, rw):
    b, l, d = x.shape
    lc = ctx.shape[1]
    tm = min(PROJ_TOKEN_TILE, l)
    nj = l // tm
    steps = b * nj
    rows = -(-(b + 1) // F32_SUBLANES) * F32_SUBLANES
    sc = WEIGHT_STAGE_COLS
    assert w_ada.shape == (d, N_MOD * d) and d % sc == 0 and w_in.shape[1] % sc == 0
    tok = lambda width: pl.BlockSpec((1, tm, width), lambda i, j: (i, j, 0))
    out = lambda width: jax.ShapeDtypeStruct((b, l, width), BF16)
    sliced = [w.reshape(steps, w.shape[0] // steps, w.shape[1]) for w in later_weights]
    for w in sliced:
        assert w.shape[1] % BF16_SUBLANES == 0
    cast_spec = lambda w: pl.BlockSpec((1,) + w.shape[1:], lambda i, j: (i * nj + j, 0, 0))
    whole = lambda shape: pl.BlockSpec(shape, lambda i, j: (0,) * len(shape))
    res = pl.pallas_call(
        functools.partial(_proj_kernel, fw=fw, rw=rw, d=d, scale=RET_HEAD_DIM ** -0.5, ncast=len(sliced)),
        grid=(b, nj),
        in_specs=[
            tok(d),
            pl.BlockSpec((tm, cosf.shape[1]), lambda i, j: (j, 0)),
            pl.BlockSpec((tm, sinf.shape[1]), lambda i, j: (j, 0)),
            pl.BlockSpec(memory_space=pl.ANY),
            pl.BlockSpec(memory_space=pl.ANY),
        ] + [cast_spec(w) for w in sliced] + [
            _const_spec((1, d)), _const_spec((1, rw)), _const_spec((b, d)), _const_spec((1, d)),
            _const_spec((1, N_MOD * d)), _const_spec((b, lc, d)),
        ],
        out_specs=[tok(fw), tok(rw), pl.BlockSpec((1, rw, tm), lambda i, j: (i, 0, j)),
                   tok(rw), tok(rw), tok(2 * d)] + [cast_spec(w) for w in sliced]
        + [whole((N_MOD, rows, d)), whole((b, rw, lc)), whole((b, lc, rw))],
        out_shape=[out(fw), out(rw), jax.ShapeDtypeStruct((b, rw, l), BF16),
                   out(rw), out(rw), out(2 * d)]
        + [jax.ShapeDtypeStruct(w.shape, BF16) for w in sliced]
        + [jax.ShapeDtypeStruct((N_MOD, rows, d), F32),
           jax.ShapeDtypeStruct((b, rw, lc), BF16), jax.ShapeDtypeStruct((b, lc, rw), BF16)],
        scratch_shapes=[pltpu.VMEM(w_in.shape, BF16),
                        pltpu.VMEM((N_MOD, rows, d), F32),
                        pltpu.VMEM((rows, d), F32),
                        pltpu.VMEM((WEIGHT_STAGE_SLOTS, d, sc), F32),
                        pltpu.SemaphoreType.DMA((WEIGHT_STAGE_SLOTS,))],
        compiler_params=_params("arbitrary", "arbitrary"),
        name="proj",
    )(x, cosf, sinf, w_in, w_ada, *sliced, gain, gn_gain, c, c_ctx[None, :], b_ada, ctx)
    ncast = len(sliced)
    casts = [cc.reshape(w.shape) for cc, w in zip(res[6:6 + ncast], later_weights)]
    mods, kct, vc = res[6 + ncast:]
    return res[:6], casts, mods, kct, vc


def _fft_kernel(u_ref, kw_ref, twc_ref, tws_ref, w2_ref, cs_ref, o_ref, tr_ref, ti_ref, scr_ref,
                *, la, lb, gd, npb, scale):
    r = FFT_LA_TILE
    rows = lb * r
    fw = u_ref.shape[2]
    ngroups = fw // gd
    reps = fw // twc_ref.shape[2]
    pitch = scr_ref.shape[1] // la

    @pl.when(pl.program_id(0) == 0)
    def _():
        scr_ref[...] = jnp.zeros_like(scr_ref)

    for j in range(la // r):
        u = jnp.concatenate([u_ref[0, k * la + j * r:k * la + (j + 1) * r, :] for k in range(lb)], axis=0)
        half = lb // 2 + 1
        t = _dot(kw_ref[...], u)
        for k in range(lb):
            src = k if k < half else lb - k
            a = t[src * r:(src + 1) * r]
            b = t[(half + src) * r:(half + src + 1) * r]
            ct = _tile_lanes(twc_ref[j, k * r:(k + 1) * r, :], reps)
            st = _tile_lanes(tws_ref[j, k * r:(k + 1) * r, :], reps)
            if k < half:
                tr, ti = a * ct + b * st, b * ct - a * st
            else:
                tr, ti = a * ct - b * st, -(b * ct) - a * st
            tr_ref[k * la + j * r:k * la + (j + 1) * r, :] = tr.astype(BF16)
            ti_ref[k * la + j * r:k * la + (j + 1) * r, :] = ti.astype(BF16)

    for pb in range(lb // npb):
        xs = []
        for p in range(npb):
            lo = (pb * npb + p) * la
            t = jnp.concatenate([tr_ref[lo:lo + la, :], ti_ref[lo:lo + la, :]], axis=0)
            xs.append(_dot(w2_ref[...], t).astype(BF16))
        for gi in range(ngroups):
            cols = slice(gi * gd, (gi + 1) * gd)
            lhs = jnp.concatenate([jnp.concatenate([x[:la, cols], x[la:, cols]], axis=1) for x in xs], axis=0)
            fg = _dot(lhs, cs_ref[...]) * scale
            for p in range(npb):
                scr_ref[gi, pl.ds(p, la, stride=pitch), :] = fg[p * la:(p + 1) * la]
        full = jnp.concatenate([scr_ref[gi].reshape(la, pitch, gd)[:, :npb, :] for gi in range(ngroups)], axis=2)
        o_ref[0, :, pb * npb:(pb + 1) * npb, :] = full.astype(BF16)


def _fft(u, kw, twc, tws, w2, cs, la, lb, gd, scale):
    b, l, fw = u.shape
    npb = min(BF16_SUBLANES, lb)
    out = pl.pallas_call(
        functools.partial(_fft_kernel, la=la, lb=lb, gd=gd, npb=npb, scale=scale),
        grid=(b,),
        in_specs=[pl.BlockSpec((1, l, fw), lambda i: (i, 0, 0)),
                  _const_spec(kw.shape), _const_spec(twc.shape), _const_spec(tws.shape),
                  _const_spec(w2.shape), _const_spec(cs.shape)],
        out_specs=pl.BlockSpec((1, la, lb, fw), lambda i: (i, 0, 0, 0)),
        out_shape=jax.ShapeDtypeStruct((b, la, lb, fw), BF16),
        scratch_shapes=[pltpu.VMEM((l, fw), BF16), pltpu.VMEM((l, fw), BF16),
                        pltpu.VMEM((fw // gd, la * (npb + FFT_SCATTER_PAD), gd), F32)],
        compiler_params=_params("arbitrary"),
        name="fft",
    )(u, kw, twc, tws, w2, cs)
    return out.reshape(b, l, fw)


def _ret_kernel(q_ref, kt_ref, v_ref, sg_ref, kct_ref, vc_ref, lgl_ref, lgc_ref,
                z_ref, sf_ref, sb_ref, stf_ref, stb_ref, p_ref, o_ref,
                dall_ref, qdf_ref, qdb_ref, kdf_ref, kdb_ref, inc_ref, *, nchunk, unroll):
    c = RET_CHUNK
    gw = GROUP_W
    lc = vc_ref.shape[1]

    lgf, lgb = lgl_ref[0:1, :], lgl_ref[1:2, :]
    pos = lax.broadcasted_iota(jnp.int32, (c, gw), 0).astype(F32)
    qdf_ref[...] = jnp.exp(lgf * (pos + 1.0))
    qdb_ref[...] = jnp.exp(lgb * (c - pos))
    kdf_ref[...] = jnp.exp(lgf * (c - 1.0 - pos))
    kdb_ref[...] = jnp.exp(lgb * pos)
    cdf = jnp.exp(lgf * c)
    cdb = jnp.exp(lgb * c)
    cpos = lax.broadcasted_iota(jnp.int32, (lc, gw), 0).astype(F32)
    wcf = jnp.exp(lgf * (lc - 1.0 - cpos))
    wcb = jnp.exp(lgb * cpos)
    si = lax.broadcasted_iota(jnp.int32, (c, HEADS_PER_GROUP * c), 0)
    sj = lax.broadcasted_iota(jnp.int32, (c, HEADS_PER_GROUP * c), 1) & (c - 1)
    diff = (si - sj).astype(F32)
    dall_ref[...] = (jnp.where(diff >= 0, jnp.exp(lgc_ref[0:1, :] * jnp.maximum(diff, 0.0)), 0.0)
                     + jnp.where(diff <= 0, jnp.exp(lgc_ref[1:2, :] * jnp.maximum(-diff, 0.0)), 0.0))

    pw = STATE_BLOCK_W
    npair = gw // pw
    same_head_p = (lax.broadcasted_iota(jnp.int32, (pw, pw), 0) // RET_HEAD_DIM
                   == lax.broadcasted_iota(jnp.int32, (pw, pw), 1) // RET_HEAD_DIM)

    def diag_blocks(t):
        return jnp.where(same_head_p, t, 0.0)

    def weighted(t, w):
        return (t.astype(F32) * w).astype(BF16)

    def rows(ref, n):
        return ref[0, pl.ds(pl.multiple_of(n * c, c), c), :]

    def kt_chunk(n):
        return kt_ref[0, :, pl.ds(pl.multiple_of(n * c, c), c)]

    def state_update(st_ref, kt, vw, cd):
        for p in range(npair):
            cols = slice(p * pw, (p + 1) * pw)
            st_ref[p] = st_ref[p] * cd[:, cols] + diag_blocks(_dot(kt[cols, :], vw[:, cols]))

    def full_state(s_ref, n):
        zero = jnp.zeros((pw, pw), BF16)
        return jnp.concatenate(
            [jnp.concatenate([s_ref[n, p] if q == p else zero for q in range(npair)], axis=1)
             for p in range(npair)], axis=0)

    kct = kct_ref[0]
    vc = vc_ref[0]
    stf_ref[...] = jnp.zeros_like(stf_ref)
    stb_ref[...] = jnp.zeros_like(stb_ref)
    state_update(stf_ref, kct, weighted(vc, wcf), cdf)
    state_update(stb_ref, kct, weighted(vc, wcb), cdb)

    def increment_body(n, carry):
        ktn = kt_chunk(n)
        vn = rows(v_ref, n)
        vwf = weighted(vn, kdf_ref[...])
        vwb = weighted(vn, kdb_ref[...])
        for p in range(npair):
            cols = slice(p * pw, (p + 1) * pw)
            inc_ref[n, p] = _dot(ktn[cols, :], jnp.concatenate([vwf[:, cols], vwb[:, cols]], axis=1))
        return carry

    lax.fori_loop(0, nchunk, increment_body, 0, unroll=unroll)

    def scan_body(i, carry):
        nf = i
        nb = nchunk - 1 - i
        sf_ref[nf] = stf_ref[...].astype(BF16)
        sb_ref[nb] = stb_ref[...].astype(BF16)
        for p in range(npair):
            cols = slice(p * pw, (p + 1) * pw)
            stf_ref[p] = stf_ref[p] * cdf[:, cols] + diag_blocks(inc_ref[nf, p, :, :pw])
            stb_ref[p] = stb_ref[p] * cdb[:, cols] + diag_blocks(inc_ref[nb, p, :, pw:])
        return carry

    lax.fori_loop(0, nchunk, scan_body, 0, unroll=unroll)

    lane = lax.broadcasted_iota(jnp.int32, (c, gw), 1)
    sub = lax.broadcasted_iota(jnp.int32, (gw, c), 0)
    lane_masks = [(lane >= h * RET_HEAD_DIM) & (lane < (h + 1) * RET_HEAD_DIM)
                  for h in range(HEADS_PER_GROUP)]
    sub_masks = [(sub >= h * RET_HEAD_DIM) & (sub < (h + 1) * RET_HEAD_DIM)
                 for h in range(HEADS_PER_GROUP)]
    same_head = (lax.broadcasted_iota(jnp.int32, (gw, gw), 0) // RET_HEAD_DIM
                 == lax.broadcasted_iota(jnp.int32, (gw, gw), 1) // RET_HEAD_DIM)
    bd_mean = jnp.where(same_head, 1.0 / RET_HEAD_DIM, 0.0).astype(BF16)

    def score_body(n, carry):
        ktn = kt_chunk(n)
        zk = jnp.zeros_like(ktn)
        kbd = jnp.concatenate([jnp.where(m, ktn, zk) for m in sub_masks], axis=1)
        p_ref[n] = (_dot(rows(q_ref, n), kbd) * dall_ref[...]).astype(BF16)
        return carry

    lax.fori_loop(0, nchunk, score_body, 0, unroll=unroll)

    def mix_body(n, carry):
        qn = rows(q_ref, n)
        vn = rows(v_ref, n)
        zv = jnp.zeros_like(vn)
        vbd = jnp.concatenate([jnp.where(m, vn, zv) for m in lane_masks], axis=0)
        inter = [_dot(qn[:, p * pw:(p + 1) * pw], jnp.concatenate([sf_ref[n, p], sb_ref[n, p]], axis=1))
                 for p in range(npair)]
        inter_f = jnp.concatenate([t[:, :pw] for t in inter], axis=1)
        inter_b = jnp.concatenate([t[:, pw:] for t in inter], axis=1)
        o_ref[n] = _dot(p_ref[n], vbd) + qdf_ref[...] * inter_f + qdb_ref[...] * inter_b
        return carry

    lax.fori_loop(0, nchunk, mix_body, 0, unroll=unroll)

    nc = math.gcd(nchunk, RET_NORM_CHUNKS)

    def norm_body(m, carry):
        o = o_ref[pl.ds(m * nc, nc)].reshape(nc * c, gw)
        tokens = pl.ds(pl.multiple_of(m * (nc * c), nc * c), nc * c)
        ms = _dot((o * o).astype(BF16), bd_mean)
        z = sg_ref[0, tokens, :].astype(F32) * (o * lax.rsqrt(ms + EPS))
        z_ref[0, tokens, :] = z.astype(BF16)
        return carry

    lax.fori_loop(0, nchunk // nc, norm_body, 0, unroll=max(unroll // nc, 1))


def _retention(q, kt, v, sg, kct, vc, log_gamma):
    b, l, rw = q.shape
    lc = vc.shape[1]
    c = RET_CHUNK
    gw = GROUP_W
    hpg = HEADS_PER_GROUP
    ng = rw // gw
    nchunk = l // c
    pw = STATE_BLOCK_W
    npair = gw // pw
    assert c & (c - 1) == 0
    lg = log_gamma.reshape(2, ng, hpg).transpose(1, 0, 2)
    lg_lane = jnp.repeat(lg, RET_HEAD_DIM, axis=2)
    lg_col = jnp.repeat(lg, c, axis=2)
    tok = pl.BlockSpec((1, l, gw), lambda i, j: (i, 0, j))
    tokt = pl.BlockSpec((1, gw, l), lambda i, j: (i, j, 0))
    grp = lambda r, width: pl.BlockSpec((None, r, width), lambda i, j: (j, 0, 0))
    return pl.pallas_call(
        functools.partial(_ret_kernel, nchunk=nchunk, unroll=math.gcd(nchunk, RET_UNROLL)),
        grid=(b, ng),
        in_specs=[
            tok, tokt, tok, tok,
            pl.BlockSpec((1, gw, lc), lambda i, j: (i, j, 0)),
            pl.BlockSpec((1, lc, gw), lambda i, j: (i, 0, j)),
            grp(2, gw), grp(2, hpg * c),
        ],
        out_specs=tok,
        out_shape=jax.ShapeDtypeStruct((b, l, rw), BF16),
        scratch_shapes=[pltpu.VMEM((nchunk, npair, pw, pw), BF16), pltpu.VMEM((nchunk, npair, pw, pw), BF16),
                        pltpu.VMEM((npair, pw, pw), F32), pltpu.VMEM((npair, pw, pw), F32),
                        pltpu.VMEM((nchunk, c, hpg * c), BF16),
                        pltpu.VMEM((nchunk, c, gw), F32),
                        pltpu.VMEM((c, hpg * c), F32)] + [pltpu.VMEM((c, gw), F32)] * 4
        + [pltpu.VMEM((nchunk, npair, pw, 2 * pw), F32)],
        compiler_params=_params("arbitrary", "arbitrary"),
        name="ret",
    )(q, kt, v, sg, kct, vc, lg_lane, lg_col)


def _out_kernel(w4_ref, wr_ref, wo_ref, w1_ref, w2_ref, x_ref, mod_ref, fm_ref, z_ref, gab_ref,
                gain2_ref, fgain_ref, o_ref, *, ff_chunk):
    mod = mod_ref[:, pl.ds(pl.program_id(0), 1), :]
    g1, sh2, sc2, g2 = mod[2], mod[3], mod[4], mod[5]
    y_four = _dot(fm_ref[0], w4_ref[...])
    y_ret = _dot(z_ref[0], wr_ref[...])
    d = x_ref.shape[2]
    y = (jax.nn.sigmoid(gab_ref[0, :, :d].astype(F32)) * y_four
         + jax.nn.sigmoid(gab_ref[0, :, d:].astype(F32)) * y_ret)
    x1 = x_ref[0] + g1 * _dot(y.astype(BF16), wo_ref[...])
    h2 = _norm_mod(x1, gain2_ref[...], sh2, sc2).astype(BF16)
    dff = w1_ref.shape[1]
    acc = None
    for lo in range(0, dff, ff_chunk):
        hid = jnp.maximum(_dot(h2, w1_ref[:, lo:lo + ff_chunk]), 0.0)
        part = _dot((hid * hid).astype(BF16), w2_ref[lo:lo + ff_chunk, :])
        acc = part if acc is None else acc + part
    x2 = x1 + g2 * acc
    ms = jnp.mean(x2 * x2, axis=-1, keepdims=True)
    o_ref[0] = x2 * lax.rsqrt(ms + EPS) * fgain_ref[...]


def _out(x, mods, fm, z, gab, w4, wr, wo, gain2, w1, w2, fgain):
    b, l, d = x.shape
    tm = min(TOKEN_TILE, l)
    tok = lambda width: pl.BlockSpec((1, tm, width), lambda i, j: (i, j, 0))
    weights = (w4, wr, wo, w1, w2)
    return pl.pallas_call(
        functools.partial(_out_kernel, ff_chunk=min(1024, w1.shape[1])),
        grid=(b, l // tm),
        in_specs=[_const_spec(w.shape) for w in weights] + [
            tok(d),
            _const_spec(mods.shape),
            tok(fm.shape[2]), tok(z.shape[2]), tok(2 * d),
            _const_spec((1, d)), _const_spec((1, d)),
        ],
        out_specs=tok(d),
        out_shape=jax.ShapeDtypeStruct((b, l, d), F32),
        compiler_params=_params("arbitrary", "arbitrary"),
        name="out",
    )(*weights, x, mods, fm, z, gab, gain2, fgain)


def _dft_tables(l, gd):
    la = FFT_LA
    lb = l // la

    def cs(n):
        idx = np.arange(n)
        ang = 2.0 * np.pi * ((idx[:, None] * idx[None, :]) % n) / n
        return np.cos(ang), np.sin(ang)

    cb, sb = cs(lb)
    eye = np.eye(FFT_LA_TILE)
    half = lb // 2 + 1
    kw = np.concatenate([np.kron(cb[:half], eye), -np.kron(sb[:half], eye)], axis=0)
    ca, sa = cs(la)
    w2 = np.block([[ca, sa], [-sa, ca]])
    cc, sc = cs(gd)
    chan = np.concatenate([cc, sc], axis=0)
    tw = 2.0 * np.pi * (np.arange(lb)[:, None] * np.arange(la)[None, :]) / l
    tw = tw.reshape(lb, la // FFT_LA_TILE, FFT_LA_TILE).transpose(1, 0, 2).reshape(la // FFT_LA_TILE, -1)
    twc = np.repeat(np.cos(tw)[:, :, None], LANES, axis=2)
    tws = np.repeat(np.sin(tw)[:, :, None], LANES, axis=2)
    as_bf = lambda a: jnp.asarray(a, dtype=F32).astype(BF16)
    return as_bf(kw), as_bf(w2), as_bf(chan), jnp.asarray(twc, F32), jnp.asarray(tws, F32), la, lb


def _rope_tables(l):
    f32 = np.float32
    nf = RET_HEAD_DIM // 4
    inv = np.power(f32(ROPE_BASE), -np.arange(nf, dtype=f32) / f32(nf)).astype(f32)
    rows = l // GRID_W
    r, cc = np.meshgrid(np.arange(rows, dtype=f32), np.arange(GRID_W, dtype=f32), indexing="ij")
    ang = np.concatenate([r.reshape(-1)[:, None] * inv, cc.reshape(-1)[:, None] * inv], axis=-1).astype(f32)
    cos, sin = np.cos(ang).astype(f32), np.sin(ang).astype(f32)
    cos_h = np.concatenate([cos, cos], axis=1)
    sin_h = np.concatenate([-sin, sin], axis=1)
    reps = LANES // RET_HEAD_DIM
    return jnp.asarray(np.tile(cos_h, (1, reps))), jnp.asarray(np.tile(sin_h, (1, reps)))


def kernel(x, c, ctx, c_ctx, w_ada, b_ada, norm1_gain, w_in, four_w_out, ret_decay_logit,
           ret_gn_gain, ret_w_out, w_out, norm2_gain, w_mlp1, w_mlp2, final_gain):
    assert w_ada.shape[0] == 1, "single-layer block"
    b, l, d = x.shape
    lc = ctx.shape[1]
    fw = four_w_out.shape[1]
    rw = ret_w_out.shape[1]
    gd = fw // FOUR_GROUPS
    assert l % FFT_LA == 0 and l % RET_CHUNK == 0 and rw % GROUP_W == 0

    kw, w2, chan, twc, tws, la, lb = _dft_tables(l, gd)
    cosf, sinf = _rope_tables(l)

    later = [four_w_out[0], ret_w_out[0], w_out[0], w_mlp1[0], w_mlp2[0]]
    (u, q, kt, v, sg, gab), later_b, mods, kct, vc = _proj(
        x, c, c_ctx, w_ada[0], b_ada, ctx, norm1_gain, w_in[0], cosf, sinf, ret_gn_gain, later, fw, rw)

    fm = _fft(u, kw, twc, tws, w2, chan, la, lb, gd, 1.0 / math.sqrt(l * gd))

    log_gamma = jax.nn.log_sigmoid(ret_decay_logit[0].astype(F32))
    z = _retention(q, kt, v, sg, kct, vc, log_gamma)

    w4, wr, wo, w1, w2 = later_b
    return _out(x, mods, fm, z, gab, w4, wr, wo, norm2_gain, w1, w2, final_gain[None, :])
```

```python
import functools
import math

import jax
import jax.numpy as jnp
import numpy as np
from jax import lax
from jax.experimental import pallas as pl
from jax.experimental.pallas import tpu as pltpu

F32 = jnp.float32
BF16 = jnp.bfloat16

GRID_W = 64
FOUR_GROUPS = 4
RET_HEAD_DIM = 64
N_MOD = 6
ROPE_BASE = 10000.0
EPS = 1e-6

LANES = 128
MXU_DIM = 256
F32_SUBLANES = 8
BF16_SUBLANES = 16
VMEM_LIMIT_BYTES = 56 * 1024 * 1024

RET_CHUNK = 128
RET_UNROLL = 32
RET_NORM_CHUNKS = 8
HEADS_PER_GROUP = MXU_DIM // RET_HEAD_DIM
GROUP_W = HEADS_PER_GROUP * RET_HEAD_DIM
STATE_BLOCK_W = LANES
FFT_LA = 128
FFT_LA_TILE = BF16_SUBLANES
FFT_SCATTER_PAD = F32_SUBLANES
TOKEN_TILE = 512
PROJ_TOKEN_TILE = 512
WEIGHT_STAGE_COLS = 512
WEIGHT_STAGE_SLOTS = 6


def _dot(a, b):
    return jnp.dot(a, b, preferred_element_type=F32)


def _norm_mod(x, gain, shift, scale):
    ms = jnp.mean(x * x, axis=-1, keepdims=True)
    y = x * lax.rsqrt(ms + EPS) * gain
    return y * (1.0 + scale) + shift


def _tile_lanes(t, reps):
    return jnp.concatenate([t] * reps, axis=1) if reps > 1 else t


def _const_spec(shape):
    nd = len(shape)
    return pl.BlockSpec(shape, lambda *_: (0,) * nd, pipeline_mode=pl.Buffered(1))


def _params(*sem):
    return pltpu.CompilerParams(dimension_semantics=sem, vmem_limit_bytes=VMEM_LIMIT_BYTES)


def _is_first_step(grid_rank):
    first = pl.program_id(0) == 0
    for axis in range(1, grid_rank):
        first = first & (pl.program_id(axis) == 0)
    return first


def _proj_prepare(before_body, c_ref, cctx_ref, wada_hbm, bada_ref, w_hbm, ctx_ref, gain_ref,
                  mods_ref, kct_ref, vc_ref, w_ref, mod_ref, cv_ref, stage_ref, sem_ref, *, fw, rw, scale):
    nb = c_ref.shape[0]
    d = w_ref.shape[0]
    if before_body:
        cv_ref[...] = jnp.zeros_like(cv_ref)
        cv_ref[0:nb, :] = c_ref[...]
        cv_ref[nb:nb + 1, :] = cctx_ref[...]
    cv = cv_ref[...]
    s = (cv * jax.nn.sigmoid(cv)).astype(BF16)

    slots, _, sc = stage_ref.shape
    ada = [(wada_hbm, c0) for c0 in range(0, wada_hbm.shape[1], sc)]
    win = [(w_hbm, c0) for c0 in range(0, w_hbm.shape[1], sc)]
    body_mods = 2 * d // sc
    chunks = ada[:body_mods] + win + ada[body_mods:]
    split = body_mods + len(win)
    ctx_after = body_mods + (fw + 3 * rw) // sc - 1

    def copy(k):
        src, c0 = chunks[k]
        slot = k % slots
        return pltpu.make_async_copy(src.at[:, pl.ds(c0, sc)], stage_ref.at[slot], sem_ref.at[slot])

    if before_body:
        for k in range(min(slots - 1, len(chunks))):
            copy(k).start()
    for k in range(0, split) if before_body else range(split, len(chunks)):
        src, c0 = chunks[k]
        if k + slots - 1 < len(chunks):
            copy(k + slots - 1).start()
        copy(k).wait()
        blk = stage_ref[k % slots].astype(BF16)
        if src is wada_hbm:
            j, off = divmod(c0, d)
            hw = sc // 2
            mod_ref[j, :, off:off + hw] = _dot(s, blk[:, :hw]) + bada_ref[:, c0:c0 + hw]
            mod_ref[j, :, off + hw:off + sc] = _dot(s, blk[:, hw:]) + bada_ref[:, c0 + hw:c0 + sc]
        else:
            w_ref[:, c0:c0 + sc] = blk
        if k == ctx_after:
            for i in range(nb):
                h = _norm_mod(ctx_ref[i], gain_ref[...],
                              mod_ref[0, nb:nb + 1, :], mod_ref[1, nb:nb + 1, :]).astype(BF16)
                kct_ref[i] = (_dot(h, w_ref[:, fw + rw:fw + 2 * rw]) * scale).T.astype(BF16)
                vc_ref[i] = _dot(h, w_ref[:, fw + 2 * rw:fw + 3 * rw]).astype(BF16)
    if not before_body:
        mods_ref[...] = mod_ref[...]


def _proj_kernel(*refs, fw, rw, d, scale, ncast):
    x_ref, cos_ref, sin_ref, w_hbm, wada_hbm = refs[:5]
    cast_in = refs[5:5 + ncast]
    gain_ref, gng_ref, c_ref, cctx_ref, bada_ref, ctx_ref = refs[5 + ncast:11 + ncast]
    outs = refs[11 + ncast:]
    u_ref, q_ref, kt_ref, v_ref, sg_ref, gab_ref = outs[:6]
    cast_out = outs[6:6 + ncast]
    mods_ref, kct_ref, vc_ref = outs[6 + ncast:9 + ncast]
    w_ref, mod_ref, cv_ref, stage_ref, sem_ref = outs[9 + ncast:]

    prepare = functools.partial(
        _proj_prepare, c_ref=c_ref, cctx_ref=cctx_ref, wada_hbm=wada_hbm, bada_ref=bada_ref, w_hbm=w_hbm,
        ctx_ref=ctx_ref, gain_ref=gain_ref, mods_ref=mods_ref, kct_ref=kct_ref, vc_ref=vc_ref, w_ref=w_ref,
        mod_ref=mod_ref, cv_ref=cv_ref, stage_ref=stage_ref, sem_ref=sem_ref, fw=fw, rw=rw, scale=scale)
    pl.when(_is_first_step(2))(functools.partial(prepare, True))

    for src, dst in zip(cast_in, cast_out):
        dst[...] = src[...].astype(BF16)

    mod = mod_ref[0:2, pl.ds(pl.program_id(0), 1), :]
    h = _norm_mod(x_ref[0], gain_ref[...], mod[0], mod[1]).astype(BF16)
    tm = h.shape[0]

    def proj(lo, width):
        return _dot(h, w_ref[:, lo:lo + width])

    u_ref[0] = proj(0, fw).astype(BF16)

    reps = rw // cos_ref.shape[1]
    cosf = _tile_lanes(cos_ref[...], reps)
    sinf = _tile_lanes(sin_ref[...], reps)
    lane = lax.broadcasted_iota(jnp.int32, (tm, rw), 1)
    first_half = (lane & (RET_HEAD_DIM // 2)) == 0

    def rope(t):
        rot = jnp.where(first_half,
                        pltpu.roll(t, rw - RET_HEAD_DIM // 2, 1),
                        pltpu.roll(t, RET_HEAD_DIM // 2, 1))
        return t * cosf + rot * sinf

    q_ref[0] = rope(proj(fw, rw)).astype(BF16)
    kt_ref[0] = (rope(proj(fw + rw, rw)) * scale).T.astype(BF16)
    v_ref[0] = proj(fw + 2 * rw, rw).astype(BF16)
    g = proj(fw + 3 * rw, rw)
    sg_ref[0] = (g * jax.nn.sigmoid(g) * gng_ref[...]).astype(BF16)
    gab_ref[0, :, :d] = proj(fw + 4 * rw, d).astype(BF16)
    gab_ref[0, :, d:] = proj(fw + 4 * rw + d, d).astype(BF16)

    pl.when(_is_first_step(2))(functools.partial(prepare, False))


def _proj(x, c, c_ctx, w_ada, b_ada, ctx, gain, w_in, cosf, sinf, gn_gain, later_weights, fw, rw):
    b, l, d = x.shape
    lc = ctx.shape[1]
    tm = min(PROJ_TOKEN_TILE, l)
    nj = l // tm
    steps = b * nj
    rows = -(-(b + 1) // F32_SUBLANES) * F32_SUBLANES
    sc = WEIGHT_STAGE_COLS
    assert w_ada.shape == (d, N_MOD * d) and d % sc == 0 and w_in.shape[1] % sc == 0
    tok = lambda width: pl.BlockSpec((1, tm, width), lambda i, j: (i, j, 0))
    out = lambda width: jax.ShapeDtypeStruct((b, l, width), BF16)
    sliced = [w.reshape(steps, w.shape[0] // steps, w.shape[1]) for w in later_weights]
    for w in sliced:
        assert w.shape[1] % BF16_SUBLANES == 0
    cast_spec = lambda w: pl.BlockSpec((1,) + w.shape[1:], lambda i, j: (i * nj + j, 0, 0))
    whole = lambda shape: pl.BlockSpec(shape, lambda i, j: (0,) * len(shape))
    res = pl.pallas_call(
        functools.partial(_proj_kernel, fw=fw, rw=rw, d=d, scale=RET_HEAD_DIM ** -0.5, ncast=len(sliced)),
        grid=(b, nj),
        in_specs=[
            tok(d),
            pl.BlockSpec((tm, cosf.shape[1]), lambda i, j: (j, 0)),
            pl.BlockSpec((tm, sinf.shape[1]), lambda i, j: (j, 0)),
            pl.BlockSpec(memory_space=pl.ANY),
            pl.BlockSpec(memory_space=pl.ANY),
        ] + [cast_spec(w) for w in sliced] + [
            _const_spec((1, d)), _const_spec((1, rw)), _const_spec((b, d)), _const_spec((1, d)),
            _const_spec((1, N_MOD * d)), _const_spec((b, lc, d)),
        ],
        out_specs=[tok(fw), tok(rw), pl.BlockSpec((1, rw, tm), lambda i, j: (i, 0, j)),
                   tok(rw), tok(rw), tok(2 * d)] + [cast_spec(w) for w in sliced]
        + [whole((N_MOD, rows, d)), whole((b, rw, lc)), whole((b, lc, rw))],
        out_shape=[out(fw), out(rw), jax.ShapeDtypeStruct((b, rw, l), BF16),
                   out(rw), out(rw), out(2 * d)]
        + [jax.ShapeDtypeStruct(w.shape, BF16) for w in sliced]
        + [jax.ShapeDtypeStruct((N_MOD, rows, d), F32),
           jax.ShapeDtypeStruct((b, rw, lc), BF16), jax.ShapeDtypeStruct((b, lc, rw), BF16)],
        scratch_shapes=[pltpu.VMEM(w_in.shape, BF16),
                        pltpu.VMEM((N_MOD, rows, d), F32),
                        pltpu.VMEM((rows, d), F32),
                        pltpu.VMEM((WEIGHT_STAGE_SLOTS, d, sc), F32),
                        pltpu.SemaphoreType.DMA((WEIGHT_STAGE_SLOTS,))],
        compiler_params=_params("arbitrary", "arbitrary"),
        name="proj",
    )(x, cosf, sinf, w_in, w_ada, *sliced, gain, gn_gain, c, c_ctx[None, :], b_ada, ctx)
    ncast = len(sliced)
    casts = [cc.reshape(w.shape) for cc, w in zip(res[6:6 + ncast], later_weights)]
    mods, kct, vc = res[6 + ncast:]
    return res[:6], casts, mods, kct, vc


def _fft_kernel(u_ref, kw_ref, twc_ref, tws_ref, w2_ref, cs_ref, o_ref, tr_ref, ti_ref, scr_ref,
                *, la, lb, gd, npb, scale):
    r = FFT_LA_TILE
    rows = lb * r
    fw = u_ref.shape[2]
    ngroups = fw // gd
    reps = fw // twc_ref.shape[2]
    pitch = scr_ref.shape[1] // la

    @pl.when(pl.program_id(0) == 0)
    def _():
        scr_ref[...] = jnp.zeros_like(scr_ref)

    for j in range(la // r):
        u = jnp.concatenate([u_ref[0, k * la + j * r:k * la + (j + 1) * r, :] for k in range(lb)], axis=0)
        half = lb // 2 + 1
        t = _dot(kw_ref[...], u)
        for k in range(lb):
            src = k if k < half else lb - k
            a = t[src * r:(src + 1) * r]
            b = t[(half + src) * r:(half + src + 1) * r]
            ct = _tile_lanes(twc_ref[j, k * r:(k + 1) * r, :], reps)
            st = _tile_lanes(tws_ref[j, k * r:(k + 1) * r, :], reps)
            if k < half:
                tr, ti = a * ct + b * st, b * ct - a * st
            else:
                tr, ti = a * ct - b * st, -(b * ct) - a * st
            tr_ref[k * la + j * r:k * la + (j + 1) * r, :] = tr.astype(BF16)
            ti_ref[k * la + j * r:k * la + (j + 1) * r, :] = ti.astype(BF16)

    for pb in range(lb // npb):
        xs = []
        for p in range(npb):
            lo = (pb * npb + p) * la
            t = jnp.concatenate([tr_ref[lo:lo + la, :], ti_ref[lo:lo + la, :]], axis=0)
            xs.append(_dot(w2_ref[...], t).astype(BF16))
        for gi in range(ngroups):
            cols = slice(gi * gd, (gi + 1) * gd)
            lhs = jnp.concatenate([jnp.concatenate([x[:la, cols], x[la:, cols]], axis=1) for x in xs], axis=0)
            fg = _dot(lhs, cs_ref[...]) * scale
            for p in range(npb):
                scr_ref[gi, pl.ds(p, la, stride=pitch), :] = fg[p * la:(p + 1) * la]
        full = jnp.concatenate([scr_ref[gi].reshape(la, pitch, gd)[:, :npb, :] for gi in range(ngroups)], axis=2)
        o_ref[0, :, pb * npb:(pb + 1) * npb, :] = full.astype(BF16)


def _fft(u, kw, twc, tws, w2, cs, la, lb, gd, scale):
    b, l, fw = u.shape
    npb = min(BF16_SUBLANES, lb)
    out = pl.pallas_call(
        functools.partial(_fft_kernel, la=la, lb=lb, gd=gd, npb=npb, scale=scale),
        grid=(b,),
        in_specs=[pl.BlockSpec((1, l, fw), lambda i: (i, 0, 0)),
                  _const_spec(kw.shape), _const_spec(twc.shape), _const_spec(tws.shape),
                  _const_spec(w2.shape), _const_spec(cs.shape)],
        out_specs=pl.BlockSpec((1, la, lb, fw), lambda i: (i, 0, 0, 0)),
        out_shape=jax.ShapeDtypeStruct((b, la, lb, fw), BF16),
        scratch_shapes=[pltpu.VMEM((l, fw), BF16), pltpu.VMEM((l, fw), BF16),
                        pltpu.VMEM((fw // gd, la * (npb + FFT_SCATTER_PAD), gd), F32)],
        compiler_params=_params("arbitrary"),
        name="fft",
    )(u, kw, twc, tws, w2, cs)
    return out.reshape(b, l, fw)


def _ret_kernel(q_ref, kt_ref, v_ref, sg_ref, kct_ref, vc_ref, lgl_ref, lgc_ref,
                z_ref, sf_ref, sb_ref, stf_ref, stb_ref, p_ref, o_ref,
                dall_ref, qdf_ref, qdb_ref, kdf_ref, kdb_ref, inc_ref, *, nchunk, unroll):
    c = RET_CHUNK
    gw = GROUP_W
    lc = vc_ref.shape[1]

    lgf, lgb = lgl_ref[0:1, :], lgl_ref[1:2, :]
    pos = lax.broadcasted_iota(jnp.int32, (c, gw), 0).astype(F32)
    qdf_ref[...] = jnp.exp(lgf * (pos + 1.0))
    qdb_ref[...] = jnp.exp(lgb * (c - pos))
    kdf_ref[...] = jnp.exp(lgf * (c - 1.0 - pos))
    kdb_ref[...] = jnp.exp(lgb * pos)
    cdf = jnp.exp(lgf * c)
    cdb = jnp.exp(lgb * c)
    cpos = lax.broadcasted_iota(jnp.int32, (lc, gw), 0).astype(F32)
    wcf = jnp.exp(lgf * (lc - 1.0 - cpos))
    wcb = jnp.exp(lgb * cpos)
    si = lax.broadcasted_iota(jnp.int32, (c, HEADS_PER_GROUP * c), 0)
    sj = lax.broadcasted_iota(jnp.int32, (c, HEADS_PER_GROUP * c), 1) & (c - 1)
    diff = (si - sj).astype(F32)
    dall_ref[...] = (jnp.where(diff >= 0, jnp.exp(lgc_ref[0:1, :] * jnp.maximum(diff, 0.0)), 0.0)
                     + jnp.where(diff <= 0, jnp.exp(lgc_ref[1:2, :] * jnp.maximum(-diff, 0.0)), 0.0))

    pw = STATE_BLOCK_W
    npair = gw // pw
    same_head_p = (lax.broadcasted_iota(jnp.int32, (pw, pw), 0) // RET_HEAD_DIM
                   == lax.broadcasted_iota(jnp.int32, (pw, pw), 1) // RET_HEAD_DIM)

    def diag_blocks(t):
        return jnp.where(same_head_p, t, 0.0)

    def weighted(t, w):
        return (t.astype(F32) * w).astype(BF16)

    def rows(ref, n):
        return ref[0, pl.ds(pl.multiple_of(n * c, c), c), :]

    def kt_chunk(n):
        return kt_ref[0, :, pl.ds(pl.multiple_of(n * c, c), c)]

    def state_update(st_ref, kt, vw, cd):
        for p in range(npair):
            cols = slice(p * pw, (p + 1) * pw)
            st_ref[p] = st_ref[p] * cd[:, cols] + diag_blocks(_dot(kt[cols, :], vw[:, cols]))

    def full_state(s_ref, n):
        zero = jnp.zeros((pw, pw), BF16)
        return jnp.concatenate(
            [jnp.concatenate([s_ref[n, p] if q == p else zero for q in range(npair)], axis=1)
             for p in range(npair)], axis=0)

    kct = kct_ref[0]
    vc = vc_ref[0]
    stf_ref[...] = jnp.zeros_like(stf_ref)
    stb_ref[...] = jnp.zeros_like(stb_ref)
    state_update(stf_ref, kct, weighted(vc, wcf), cdf)
    state_update(stb_ref, kct, weighted(vc, wcb), cdb)

    def increment_body(n, carry):
        ktn = kt_chunk(n)
        vn = rows(v_ref, n)
        vwf = weighted(vn, kdf_ref[...])
        vwb = weighted(vn, kdb_ref[...])
        for p in range(npair):
            cols = slice(p * pw, (p + 1) * pw)
            inc_ref[n, p] = _dot(ktn[cols, :], jnp.concatenate([vwf[:, cols], vwb[:, cols]], axis=1))
        return carry

    lax.fori_loop(0, nchunk, increment_body, 0, unroll=unroll)

    def scan_body(i, carry):
        nf = i
        nb = nchunk - 1 - i
        sf_ref[nf] = stf_ref[...].astype(BF16)
        sb_ref[nb] = stb_ref[...].astype(BF16)
        for p in range(npair):
            cols = slice(p * pw, (p + 1) * pw)
            stf_ref[p] = stf_ref[p] * cdf[:, cols] + diag_blocks(inc_ref[nf, p, :, :pw])
            stb_ref[p] = stb_ref[p] * cdb[:, cols] + diag_blocks(inc_ref[nb, p, :, pw:])
        return carry

    lax.fori_loop(0, nchunk, scan_body, 0, unroll=unroll)

    lane = lax.broadcasted_iota(jnp.int32, (c, gw), 1)
    sub = lax.broadcasted_iota(jnp.int32, (gw, c), 0)
    lane_masks = [(lane >= h * RET_HEAD_DIM) & (lane < (h + 1) * RET_HEAD_DIM)
                  for h in range(HEADS_PER_GROUP)]
    sub_masks = [(sub >= h * RET_HEAD_DIM) & (sub < (h + 1) * RET_HEAD_DIM)
                 for h in range(HEADS_PER_GROUP)]
    same_head = (lax.broadcasted_iota(jnp.int32, (gw, gw), 0) // RET_HEAD_DIM
                 == lax.broadcasted_iota(jnp.int32, (gw, gw), 1) // RET_HEAD_DIM)
    bd_mean = jnp.where(same_head, 1.0 / RET_HEAD_DIM, 0.0).astype(BF16)

    def score_body(n, carry):
        ktn = kt_chunk(n)
        zk = jnp.zeros_like(ktn)
        kbd = jnp.concatenate([jnp.where(m, ktn, zk) for m in sub_masks], axis=1)
        p_ref[n] = (_dot(rows(q_ref, n), kbd) * dall_ref[...]).astype(BF16)
        return carry

    lax.fori_loop(0, nchunk, score_body, 0, unroll=unroll)

    def mix_body(n, carry):
        qn = rows(q_ref, n)
        vn = rows(v_ref, n)
        zv = jnp.zeros_like(vn)
        vbd = jnp.concatenate([jnp.where(m, vn, zv) for m in lane_masks], axis=0)
        o_ref[n] = (_dot(p_ref[n], vbd) + qdf_ref[...] * _dot(qn, full_state(sf_ref, n))
                    + qdb_ref[...] * _dot(qn, full_state(sb_ref, n)))
        return carry

    lax.fori_loop(0, nchunk, mix_body, 0, unroll=unroll)

    nc = math.gcd(nchunk, RET_NORM_CHUNKS)

    def norm_body(m, carry):
        o = o_ref[pl.ds(m * nc, nc)].reshape(nc * c, gw)
        tokens = pl.ds(pl.multiple_of(m * (nc * c), nc * c), nc * c)
        ms = _dot((o * o).astype(BF16), bd_mean)
        z = sg_ref[0, tokens, :].astype(F32) * (o * lax.rsqrt(ms + EPS))
        z_ref[0, tokens, :] = z.astype(BF16)
        return carry

    lax.fori_loop(0, nchunk // nc, norm_body, 0, unroll=max(unroll // nc, 1))


def _retention(q, kt, v, sg, kct, vc, log_gamma):
    b, l, rw = q.shape
    lc = vc.shape[1]
    c = RET_CHUNK
    gw = GROUP_W
    hpg = HEADS_PER_GROUP
    ng = rw // gw
    nchunk = l // c
    pw = STATE_BLOCK_W
    npair = gw // pw
    assert c & (c - 1) == 0
    lg = log_gamma.reshape(2, ng, hpg).transpose(1, 0, 2)
    lg_lane = jnp.repeat(lg, RET_HEAD_DIM, axis=2)
    lg_col = jnp.repeat(lg, c, axis=2)
    tok = pl.BlockSpec((1, l, gw), lambda i, j: (i, 0, j))
    tokt = pl.BlockSpec((1, gw, l), lambda i, j: (i, j, 0))
    grp = lambda r, width: pl.BlockSpec((None, r, width), lambda i, j: (j, 0, 0))
    return pl.pallas_call(
        functools.partial(_ret_kernel, nchunk=nchunk, unroll=math.gcd(nchunk, RET_UNROLL)),
        grid=(b, ng),
        in_specs=[
            tok, tokt, tok, tok,
            pl.BlockSpec((1, gw, lc), lambda i, j: (i, j, 0)),
            pl.BlockSpec((1, lc, gw), lambda i, j: (i, 0, j)),
            grp(2, gw), grp(2, hpg * c),
        ],
        out_specs=tok,
        out_shape=jax.ShapeDtypeStruct((b, l, rw), BF16),
        scratch_shapes=[pltpu.VMEM((nchunk, npair, pw, pw), BF16), pltpu.VMEM((nchunk, npair, pw, pw), BF16),
                        pltpu.VMEM((npair, pw, pw), F32), pltpu.VMEM((npair, pw, pw), F32),
                        pltpu.VMEM((nchunk, c, hpg * c), BF16),
                        pltpu.VMEM((nchunk, c, gw), F32),
                        pltpu.VMEM((c, hpg * c), F32)] + [pltpu.VMEM((c, gw), F32)] * 4
        + [pltpu.VMEM((nchunk, npair, pw, 2 * pw), F32)],
        compiler_params=_params("arbitrary", "arbitrary"),
        name="ret",
    )(q, kt, v, sg, kct, vc, lg_lane, lg_col)


def _out_kernel(w4_ref, wr_ref, wo_ref, w1_ref, w2_ref, x_ref, mod_ref, fm_ref, z_ref, gab_ref,
                gain2_ref, fgain_ref, o_ref, *, ff_chunk):
    mod = mod_ref[:, pl.ds(pl.program_id(0), 1), :]
    g1, sh2, sc2, g2 = mod[2], mod[3], mod[4], mod[5]
    y_four = _dot(fm_ref[0], w4_ref[...])
    y_ret = _dot(z_ref[0], wr_ref[...])
    d = x_ref.shape[2]
    y = (jax.nn.sigmoid(gab_ref[0, :, :d].astype(F32)) * y_four
         + jax.nn.sigmoid(gab_ref[0, :, d:].astype(F32)) * y_ret)
    x1 = x_ref[0] + g1 * _dot(y.astype(BF16), wo_ref[...])
    h2 = _norm_mod(x1, gain2_ref[...], sh2, sc2).astype(BF16)
    dff = w1_ref.shape[1]
    acc = None
    for lo in range(0, dff, ff_chunk):
        hid = jnp.maximum(_dot(h2, w1_ref[:, lo:lo + ff_chunk]), 0.0)
        part = _dot((hid * hid).astype(BF16), w2_ref[lo:lo + ff_chunk, :])
        acc = part if acc is None else acc + part
    x2 = x1 + g2 * acc
    ms = jnp.mean(x2 * x2, axis=-1, keepdims=True)
    o_ref[0] = x2 * lax.rsqrt(ms + EPS) * fgain_ref[...]


def _out(x, mods, fm, z, gab, w4, wr, wo, gain2, w1, w2, fgain):
    b, l, d = x.shape
    tm = min(TOKEN_TILE, l)
    tok = lambda width: pl.BlockSpec((1, tm, width), lambda i, j: (i, j, 0))
    weights = (w4, wr, wo, w1, w2)
    return pl.pallas_call(
        functools.partial(_out_kernel, ff_chunk=min(1024, w1.shape[1])),
        grid=(b, l // tm),
        in_specs=[_const_spec(w.shape) for w in weights] + [
            tok(d),
            _const_spec(mods.shape),
            tok(fm.shape[2]), tok(z.shape[2]), tok(2 * d),
            _const_spec((1, d)), _const_spec((1, d)),
        ],
        out_specs=tok(d),
        out_shape=jax.ShapeDtypeStruct((b, l, d), F32),
        compiler_params=_params("arbitrary", "arbitrary"),
        name="out",
    )(*weights, x, mods, fm, z, gab, gain2, fgain)


def _dft_tables(l, gd):
    la = FFT_LA
    lb = l // la

    def cs(n):
        idx = np.arange(n)
        ang = 2.0 * np.pi * ((idx[:, None] * idx[None, :]) % n) / n
        return np.cos(ang), np.sin(ang)

    cb, sb = cs(lb)
    eye = np.eye(FFT_LA_TILE)
    half = lb // 2 + 1
    kw = np.concatenate([np.kron(cb[:half], eye), -np.kron(sb[:half], eye)], axis=0)
    ca, sa = cs(la)
    w2 = np.block([[ca, sa], [-sa, ca]])
    cc, sc = cs(gd)
    chan = np.concatenate([cc, sc], axis=0)
    tw = 2.0 * np.pi * (np.arange(lb)[:, None] * np.arange(la)[None, :]) / l
    tw = tw.reshape(lb, la // FFT_LA_TILE, FFT_LA_TILE).transpose(1, 0, 2).reshape(la // FFT_LA_TILE, -1)
    twc = np.repeat(np.cos(tw)[:, :, None], LANES, axis=2)
    tws = np.repeat(np.sin(tw)[:, :, None], LANES, axis=2)
    as_bf = lambda a: jnp.asarray(a, dtype=F32).astype(BF16)
    return as_bf(kw), as_bf(w2), as_bf(chan), jnp.asarray(twc, F32), jnp.asarray(tws, F32), la, lb


def _rope_tables(l):
    f32 = np.float32
    nf = RET_HEAD_DIM // 4
    inv = np.power(f32(ROPE_BASE), -np.arange(nf, dtype=f32) / f32(nf)).astype(f32)
    rows = l // GRID_W
    r, cc = np.meshgrid(np.arange(rows, dtype=f32), np.arange(GRID_W, dtype=f32), indexing="ij")
    ang = np.concatenate([r.reshape(-1)[:, None] * inv, cc.reshape(-1)[:, None] * inv], axis=-1).astype(f32)
    cos, sin = np.cos(ang).astype(f32), np.sin(ang).astype(f32)
    cos_h = np.concatenate([cos, cos], axis=1)
    sin_h = np.concatenate([-sin, sin], axis=1)
    reps = LANES // RET_HEAD_DIM
    return jnp.asarray(np.tile(cos_h, (1, reps))), jnp.asarray(np.tile(sin_h, (1, reps)))


def kernel(x, c, ctx, c_ctx, w_ada, b_ada, norm1_gain, w_in, four_w_out, ret_decay_logit,
           ret_gn_gain, ret_w_out, w_out, norm2_gain, w_mlp1, w_mlp2, final_gain):
    assert w_ada.shape[0] == 1, "single-layer block"
    b, l, d = x.shape
    lc = ctx.shape[1]
    fw = four_w_out.shape[1]
    rw = ret_w_out.shape[1]
    gd = fw // FOUR_GROUPS
    assert l % FFT_LA == 0 and l % RET_CHUNK == 0 and rw % GROUP_W == 0

    kw, w2, chan, twc, tws, la, lb = _dft_tables(l, gd)
    cosf, sinf = _rope_tables(l)

    later = [four_w_out[0], ret_w_out[0], w_out[0], w_mlp1[0], w_mlp2[0]]
    (u, q, kt, v, sg, gab), later_b, mods, kct, vc = _proj(
        x, c, c_ctx, w_ada[0], b_ada, ctx, norm1_gain, w_in[0], cosf, sinf, ret_gn_gain, later, fw, rw)

    fm = _fft(u, kw, twc, tws, w2, chan, la, lb, gd, 1.0 / math.sqrt(l * gd))

    log_gamma = jax.nn.log_sigmoid(ret_decay_logit[0].astype(F32))
    z = _retention(q, kt, v, sg, kct, vc, log_gamma)

    w4, wr, wo, w1, w2 = later_b
    return _out(x, mods, fm, z, gab, w4, wr, wo, norm2_gain, w1, w2, final_gain[None, :])
```

```python
import functools
import math

import jax
import jax.numpy as jnp
import numpy as np
from jax import lax
from jax.experimental import pallas as pl
from jax.experimental.pallas import tpu as pltpu

F32 = jnp.float32
BF16 = jnp.bfloat16

GRID_W = 64
FOUR_GROUPS = 4
RET_HEAD_DIM = 64
N_MOD = 6
ROPE_BASE = 10000.0
EPS = 1e-6

LANES = 128
MXU_DIM = 256
F32_SUBLANES = 8
BF16_SUBLANES = 16
VMEM_LIMIT_BYTES = 56 * 1024 * 1024

RET_CHUNK = 128
RET_UNROLL = 32
RET_NORM_CHUNKS = 4
HEADS_PER_GROUP = MXU_DIM // RET_HEAD_DIM
GROUP_W = HEADS_PER_GROUP * RET_HEAD_DIM
STATE_BLOCK_W = LANES
FFT_LA = 128
FFT_LA_TILE = BF16_SUBLANES
FFT_SCATTER_PAD = F32_SUBLANES
TOKEN_TILE = 512
PROJ_TOKEN_TILE = 512
WEIGHT_STAGE_COLS = 512
WEIGHT_STAGE_SLOTS = 6


def _dot(a, b):
    return jnp.dot(a, b, preferred_element_type=F32)


def _norm_mod(x, gain, shift, scale):
    ms = jnp.mean(x * x, axis=-1, keepdims=True)
    y = x * lax.rsqrt(ms + EPS) * gain
    return y * (1.0 + scale) + shift


def _tile_lanes(t, reps):
    return jnp.concatenate([t] * reps, axis=1) if reps > 1 else t


def _const_spec(shape):
    nd = len(shape)
    return pl.BlockSpec(shape, lambda *_: (0,) * nd, pipeline_mode=pl.Buffered(1))


def _params(*sem):
    return pltpu.CompilerParams(dimension_semantics=sem, vmem_limit_bytes=VMEM_LIMIT_BYTES)


def _is_first_step(grid_rank):
    first = pl.program_id(0) == 0
    for axis in range(1, grid_rank):
        first = first & (pl.program_id(axis) == 0)
    return first


def _proj_prepare(before_body, c_ref, cctx_ref, wada_hbm, bada_ref, w_hbm, ctx_ref, gain_ref,
                  mods_ref, kct_ref, vc_ref, w_ref, mod_ref, cv_ref, stage_ref, sem_ref, *, fw, rw, scale):
    nb = c_ref.shape[0]
    d = w_ref.shape[0]
    if before_body:
        cv_ref[...] = jnp.zeros_like(cv_ref)
        cv_ref[0:nb, :] = c_ref[...]
        cv_ref[nb:nb + 1, :] = cctx_ref[...]
    cv = cv_ref[...]
    s = (cv * jax.nn.sigmoid(cv)).astype(BF16)

    slots, _, sc = stage_ref.shape
    ada = [(wada_hbm, c0) for c0 in range(0, wada_hbm.shape[1], sc)]
    win = [(w_hbm, c0) for c0 in range(0, w_hbm.shape[1], sc)]
    body_mods = 2 * d // sc
    chunks = ada[:body_mods] + win + ada[body_mods:]
    split = body_mods + len(win)
    ctx_after = body_mods + (fw + 3 * rw) // sc - 1

    def copy(k):
        src, c0 = chunks[k]
        slot = k % slots
        return pltpu.make_async_copy(src.at[:, pl.ds(c0, sc)], stage_ref.at[slot], sem_ref.at[slot])

    if before_body:
        for k in range(min(slots - 1, len(chunks))):
            copy(k).start()
    for k in range(0, split) if before_body else range(split, len(chunks)):
        src, c0 = chunks[k]
        if k + slots - 1 < len(chunks):
            copy(k + slots - 1).start()
        copy(k).wait()
        blk = stage_ref[k % slots].astype(BF16)
        if src is wada_hbm:
            j, off = divmod(c0, d)
            hw = sc // 2
            mod_ref[j, :, off:off + hw] = _dot(s, blk[:, :hw]) + bada_ref[:, c0:c0 + hw]
            mod_ref[j, :, off + hw:off + sc] = _dot(s, blk[:, hw:]) + bada_ref[:, c0 + hw:c0 + sc]
        else:
            w_ref[:, c0:c0 + sc] = blk
        if k == ctx_after:
            for i in range(nb):
                h = _norm_mod(ctx_ref[i], gain_ref[...],
                              mod_ref[0, nb:nb + 1, :], mod_ref[1, nb:nb + 1, :]).astype(BF16)
                kct_ref[i] = (_dot(h, w_ref[:, fw + rw:fw + 2 * rw]) * scale).T.astype(BF16)
                vc_ref[i] = _dot(h, w_ref[:, fw + 2 * rw:fw + 3 * rw]).astype(BF16)
    if not before_body:
        mods_ref[...] = mod_ref[...]


def _proj_kernel(*refs, fw, rw, d, scale, ncast):
    x_ref, cos_ref, sin_ref, w_hbm, wada_hbm = refs[:5]
    cast_in = refs[5:5 + ncast]
    gain_ref, gng_ref, c_ref, cctx_ref, bada_ref, ctx_ref = refs[5 + ncast:11 + ncast]
    outs = refs[11 + ncast:]
    u_ref, q_ref, kt_ref, v_ref, sg_ref, gab_ref = outs[:6]
    cast_out = outs[6:6 + ncast]
    mods_ref, kct_ref, vc_ref = outs[6 + ncast:9 + ncast]
    w_ref, mod_ref, cv_ref, stage_ref, sem_ref = outs[9 + ncast:]

    prepare = functools.partial(
        _proj_prepare, c_ref=c_ref, cctx_ref=cctx_ref, wada_hbm=wada_hbm, bada_ref=bada_ref, w_hbm=w_hbm,
        ctx_ref=ctx_ref, gain_ref=gain_ref, mods_ref=mods_ref, kct_ref=kct_ref, vc_ref=vc_ref, w_ref=w_ref,
        mod_ref=mod_ref, cv_ref=cv_ref, stage_ref=stage_ref, sem_ref=sem_ref, fw=fw, rw=rw, scale=scale)
    pl.when(_is_first_step(2))(functools.partial(prepare, True))

    for src, dst in zip(cast_in, cast_out):
        dst[...] = src[...].astype(BF16)

    mod = mod_ref[0:2, pl.ds(pl.program_id(0), 1), :]
    h = _norm_mod(x_ref[0], gain_ref[...], mod[0], mod[1]).astype(BF16)
    tm = h.shape[0]

    def proj(lo, width):
        return _dot(h, w_ref[:, lo:lo + width])

    u_ref[0] = proj(0, fw).astype(BF16)

    reps = rw // cos_ref.shape[1]
    cosf = _tile_lanes(cos_ref[...], reps)
    sinf = _tile_lanes(sin_ref[...], reps)
    lane = lax.broadcasted_iota(jnp.int32, (tm, rw), 1)
    first_half = (lane & (RET_HEAD_DIM // 2)) == 0

    def rope(t):
        rot = jnp.where(first_half,
                        pltpu.roll(t, rw - RET_HEAD_DIM // 2, 1),
                        pltpu.roll(t, RET_HEAD_DIM // 2, 1))
        return t * cosf + rot * sinf

    q_ref[0] = rope(proj(fw, rw)).astype(BF16)
    kt_ref[0] = (rope(proj(fw + rw, rw)) * scale).T.astype(BF16)
    v_ref[0] = proj(fw + 2 * rw, rw).astype(BF16)
    g = proj(fw + 3 * rw, rw)
    sg_ref[0] = (g * jax.nn.sigmoid(g) * gng_ref[...]).astype(BF16)
    gab_ref[0, :, :d] = proj(fw + 4 * rw, d).astype(BF16)
    gab_ref[0, :, d:] = proj(fw + 4 * rw + d, d).astype(BF16)

    pl.when(_is_first_step(2))(functools.partial(prepare, False))


def _proj(x, c, c_ctx, w_ada, b_ada, ctx, gain, w_in, cosf, sinf, gn_gain, later_weights, fw, rw):
    b, l, d = x.shape
    lc = ctx.shape[1]
    tm = min(PROJ_TOKEN_TILE, l)
    nj = l // tm
    steps = b * nj
    rows = -(-(b + 1) // F32_SUBLANES) * F32_SUBLANES
    sc = WEIGHT_STAGE_COLS
    assert w_ada.shape == (d, N_MOD * d) and d % sc == 0 and w_in.shape[1] % sc == 0
    tok = lambda width: pl.BlockSpec((1, tm, width), lambda i, j: (i, j, 0))
    out = lambda width: jax.ShapeDtypeStruct((b, l, width), BF16)
    sliced = [w.reshape(steps, w.shape[0] // steps, w.shape[1]) for w in later_weights]
    for w in sliced:
        assert w.shape[1] % BF16_SUBLANES == 0
    cast_spec = lambda w: pl.BlockSpec((1,) + w.shape[1:], lambda i, j: (i * nj + j, 0, 0))
    whole = lambda shape: pl.BlockSpec(shape, lambda i, j: (0,) * len(shape))
    res = pl.pallas_call(
        functools.partial(_proj_kernel, fw=fw, rw=rw, d=d, scale=RET_HEAD_DIM ** -0.5, ncast=len(sliced)),
        grid=(b, nj),
        in_specs=[
            tok(d),
            pl.BlockSpec((tm, cosf.shape[1]), lambda i, j: (j, 0)),
            pl.BlockSpec((tm, sinf.shape[1]), lambda i, j: (j, 0)),
            pl.BlockSpec(memory_space=pl.ANY),
            pl.BlockSpec(memory_space=pl.ANY),
        ] + [cast_spec(w) for w in sliced] + [
            _const_spec((1, d)), _const_spec((1, rw)), _const_spec((b, d)), _const_spec((1, d)),
            _const_spec((1, N_MOD * d)), _const_spec((b, lc, d)),
        ],
        out_specs=[tok(fw), tok(rw), pl.BlockSpec((1, rw, tm), lambda i, j: (i, 0, j)),
                   tok(rw), tok(rw), tok(2 * d)] + [cast_spec(w) for w in sliced]
        + [whole((N_MOD, rows, d)), whole((b, rw, lc)), whole((b, lc, rw))],
        out_shape=[out(fw), out(rw), jax.ShapeDtypeStruct((b, rw, l), BF16),
                   out(rw), out(rw), out(2 * d)]
        + [jax.ShapeDtypeStruct(w.shape, BF16) for w in sliced]
        + [jax.ShapeDtypeStruct((N_MOD, rows, d), F32),
           jax.ShapeDtypeStruct((b, rw, lc), BF16), jax.ShapeDtypeStruct((b, lc, rw), BF16)],
        scratch_shapes=[pltpu.VMEM(w_in.shape, BF16),
                        pltpu.VMEM((N_MOD, rows, d), F32),
                        pltpu.VMEM((rows, d), F32),
                        pltpu.VMEM((WEIGHT_STAGE_SLOTS, d, sc), F32),
                        pltpu.SemaphoreType.DMA((WEIGHT_STAGE_SLOTS,))],
        compiler_params=_params("arbitrary", "arbitrary"),
        name="proj",
    )(x, cosf, sinf, w_in, w_ada, *sliced, gain, gn_gain, c, c_ctx[None, :], b_ada, ctx)
    ncast = len(sliced)
    casts = [cc.reshape(w.shape) for cc, w in zip(res[6:6 + ncast], later_weights)]
    mods, kct, vc = res[6 + ncast:]
    return res[:6], casts, mods, kct, vc


def _fft_kernel(u_ref, kw_ref, twc_ref, tws_ref, w2_ref, cs_ref, o_ref, tr_ref, ti_ref, scr_ref,
                *, la, lb, gd, npb, scale):
    r = FFT_LA_TILE
    rows = lb * r
    fw = u_ref.shape[2]
    ngroups = fw // gd
    reps = fw // twc_ref.shape[2]
    pitch = scr_ref.shape[1] // la

    @pl.when(pl.program_id(0) == 0)
    def _():
        scr_ref[...] = jnp.zeros_like(scr_ref)

    for j in range(la // r):
        u = jnp.concatenate([u_ref[0, k * la + j * r:k * la + (j + 1) * r, :] for k in range(lb)], axis=0)
        half = lb // 2 + 1
        t = _dot(kw_ref[...], u)
        for k in range(lb):
            src = k if k < half else lb - k
            a = t[src * r:(src + 1) * r]
            b = t[(half + src) * r:(half + src + 1) * r]
            ct = _tile_lanes(twc_ref[j, k * r:(k + 1) * r, :], reps)
            st = _tile_lanes(tws_ref[j, k * r:(k + 1) * r, :], reps)
            if k < half:
                tr, ti = a * ct + b * st, b * ct - a * st
            else:
                tr, ti = a * ct - b * st, -(b * ct) - a * st
            tr_ref[k * la + j * r:k * la + (j + 1) * r, :] = tr.astype(BF16)
            ti_ref[k * la + j * r:k * la + (j + 1) * r, :] = ti.astype(BF16)

    for pb in range(lb // npb):
        xs = []
        for p in range(npb):
            lo = (pb * npb + p) * la
            t = jnp.concatenate([tr_ref[lo:lo + la, :], ti_ref[lo:lo + la, :]], axis=0)
            xs.append(_dot(w2_ref[...], t).astype(BF16))
        for gi in range(ngroups):
            cols = slice(gi * gd, (gi + 1) * gd)
            lhs = jnp.concatenate([jnp.concatenate([x[:la, cols], x[la:, cols]], axis=1) for x in xs], axis=0)
            fg = _dot(lhs, cs_ref[...]) * scale
            for p in range(npb):
                scr_ref[gi, pl.ds(p, la, stride=pitch), :] = fg[p * la:(p + 1) * la]
        full = jnp.concatenate([scr_ref[gi].reshape(la, pitch, gd)[:, :npb, :] for gi in range(ngroups)], axis=2)
        o_ref[0, :, pb * npb:(pb + 1) * npb, :] = full.astype(BF16)


def _fft(u, kw, twc, tws, w2, cs, la, lb, gd, scale):
    b, l, fw = u.shape
    npb = min(BF16_SUBLANES, lb)
    out = pl.pallas_call(
        functools.partial(_fft_kernel, la=la, lb=lb, gd=gd, npb=npb, scale=scale),
        grid=(b,),
        in_specs=[pl.BlockSpec((1, l, fw), lambda i: (i, 0, 0)),
                  _const_spec(kw.shape), _const_spec(twc.shape), _const_spec(tws.shape),
                  _const_spec(w2.shape), _const_spec(cs.shape)],
        out_specs=pl.BlockSpec((1, la, lb, fw), lambda i: (i, 0, 0, 0)),
        out_shape=jax.ShapeDtypeStruct((b, la, lb, fw), BF16),
        scratch_shapes=[pltpu.VMEM((l, fw), BF16), pltpu.VMEM((l, fw), BF16),
                        pltpu.VMEM((fw // gd, la * (npb + FFT_SCATTER_PAD), gd), F32)],
        compiler_params=_params("arbitrary"),
        name="fft",
    )(u, kw, twc, tws, w2, cs)
    return out.reshape(b, l, fw)


def _ret_kernel(q_ref, kt_ref, v_ref, sg_ref, kct_ref, vc_ref, lgl_ref, lgc_ref,
                z_ref, sf_ref, sb_ref, stf_ref, stb_ref, p_ref, o_ref,
                dall_ref, qdf_ref, qdb_ref, kdf_ref, kdb_ref, inc_ref, *, nchunk, unroll):
    c = RET_CHUNK
    gw = GROUP_W
    lc = vc_ref.shape[1]

    lgf, lgb = lgl_ref[0:1, :], lgl_ref[1:2, :]
    pos = lax.broadcasted_iota(jnp.int32, (c, gw), 0).astype(F32)
    qdf_ref[...] = jnp.exp(lgf * (pos + 1.0))
    qdb_ref[...] = jnp.exp(lgb * (c - pos))
    kdf_ref[...] = jnp.exp(lgf * (c - 1.0 - pos))
    kdb_ref[...] = jnp.exp(lgb * pos)
    cdf = jnp.exp(lgf * c)
    cdb = jnp.exp(lgb * c)
    cpos = lax.broadcasted_iota(jnp.int32, (lc, gw), 0).astype(F32)
    wcf = jnp.exp(lgf * (lc - 1.0 - cpos))
    wcb = jnp.exp(lgb * cpos)
    si = lax.broadcasted_iota(jnp.int32, (c, HEADS_PER_GROUP * c), 0)
    sj = lax.broadcasted_iota(jnp.int32, (c, HEADS_PER_GROUP * c), 1) & (c - 1)
    diff = (si - sj).astype(F32)
    dall_ref[...] = (jnp.where(diff >= 0, jnp.exp(lgc_ref[0:1, :] * jnp.maximum(diff, 0.0)), 0.0)
                     + jnp.where(diff <= 0, jnp.exp(lgc_ref[1:2, :] * jnp.maximum(-diff, 0.0)), 0.0))

    pw = STATE_BLOCK_W
    npair = gw // pw
    same_head_p = (lax.broadcasted_iota(jnp.int32, (pw, pw), 0) // RET_HEAD_DIM
                   == lax.broadcasted_iota(jnp.int32, (pw, pw), 1) // RET_HEAD_DIM)

    def diag_blocks(t):
        return jnp.where(same_head_p, t, 0.0)

    def weighted(t, w):
        return (t.astype(F32) * w).astype(BF16)

    def rows(ref, n):
        return ref[0, pl.ds(pl.multiple_of(n * c, c), c), :]

    def kt_chunk(n):
        return kt_ref[0, :, pl.ds(pl.multiple_of(n * c, c), c)]

    def state_update(st_ref, kt, vw, cd):
        for p in range(npair):
            cols = slice(p * pw, (p + 1) * pw)
            st_ref[p] = st_ref[p] * cd[:, cols] + diag_blocks(_dot(kt[cols, :], vw[:, cols]))

    def full_state(s_ref, n):
        zero = jnp.zeros((pw, pw), BF16)
        return jnp.concatenate(
            [jnp.concatenate([s_ref[n, p] if q == p else zero for q in range(npair)], axis=1)
             for p in range(npair)], axis=0)

    kct = kct_ref[0]
    vc = vc_ref[0]
    stf_ref[...] = jnp.zeros_like(stf_ref)
    stb_ref[...] = jnp.zeros_like(stb_ref)
    state_update(stf_ref, kct, weighted(vc, wcf), cdf)
    state_update(stb_ref, kct, weighted(vc, wcb), cdb)

    def increment_body(n, carry):
        ktn = kt_chunk(n)
        vn = rows(v_ref, n)
        vwf = weighted(vn, kdf_ref[...])
        vwb = weighted(vn, kdb_ref[...])
        for p in range(npair):
            cols = slice(p * pw, (p + 1) * pw)
            inc_ref[n, p] = _dot(ktn[cols, :], jnp.concatenate([vwf[:, cols], vwb[:, cols]], axis=1))
        return carry

    lax.fori_loop(0, nchunk, increment_body, 0, unroll=unroll)

    def scan_body(i, carry):
        nf = i
        nb = nchunk - 1 - i
        sf_ref[nf] = stf_ref[...].astype(BF16)
        sb_ref[nb] = stb_ref[...].astype(BF16)
        for p in range(npair):
            cols = slice(p * pw, (p + 1) * pw)
            stf_ref[p] = stf_ref[p] * cdf[:, cols] + diag_blocks(inc_ref[nf, p, :, :pw])
            stb_ref[p] = stb_ref[p] * cdb[:, cols] + diag_blocks(inc_ref[nb, p, :, pw:])
        return carry

    lax.fori_loop(0, nchunk, scan_body, 0, unroll=unroll)

    lane = lax.broadcasted_iota(jnp.int32, (c, gw), 1)
    sub = lax.broadcasted_iota(jnp.int32, (gw, c), 0)
    lane_masks = [(lane >= h * RET_HEAD_DIM) & (lane < (h + 1) * RET_HEAD_DIM)
                  for h in range(HEADS_PER_GROUP)]
    sub_masks = [(sub >= h * RET_HEAD_DIM) & (sub < (h + 1) * RET_HEAD_DIM)
                 for h in range(HEADS_PER_GROUP)]
    same_head = (lax.broadcasted_iota(jnp.int32, (gw, gw), 0) // RET_HEAD_DIM
                 == lax.broadcasted_iota(jnp.int32, (gw, gw), 1) // RET_HEAD_DIM)
    bd_mean = jnp.where(same_head, 1.0 / RET_HEAD_DIM, 0.0).astype(BF16)

    def score_body(n, carry):
        ktn = kt_chunk(n)
        zk = jnp.zeros_like(ktn)
        kbd = jnp.concatenate([jnp.where(m, ktn, zk) for m in sub_masks], axis=1)
        p_ref[n] = (_dot(rows(q_ref, n), kbd) * dall_ref[...]).astype(BF16)
        return carry

    lax.fori_loop(0, nchunk, score_body, 0, unroll=unroll)

    def mix_body(n, carry):
        qn = rows(q_ref, n)
        vn = rows(v_ref, n)
        zv = jnp.zeros_like(vn)
        vbd = jnp.concatenate([jnp.where(m, vn, zv) for m in lane_masks], axis=0)
        o_ref[n] = (_dot(p_ref[n], vbd) + qdf_ref[...] * _dot(qn, full_state(sf_ref, n))
                    + qdb_ref[...] * _dot(qn, full_state(sb_ref, n)))
        return carry

    lax.fori_loop(0, nchunk, mix_body, 0, unroll=unroll)

    nc = math.gcd(nchunk, RET_NORM_CHUNKS)

    def norm_body(m, carry):
        o = o_ref[pl.ds(m * nc, nc)].reshape(nc * c, gw)
        tokens = pl.ds(pl.multiple_of(m * (nc * c), nc * c), nc * c)
        ms = _dot((o * o).astype(BF16), bd_mean)
        z = sg_ref[0, tokens, :].astype(F32) * (o * lax.rsqrt(ms + EPS))
        z_ref[0, tokens, :] = z.astype(BF16)
        return carry

    lax.fori_loop(0, nchunk // nc, norm_body, 0, unroll=max(unroll // nc, 1))


def _retention(q, kt, v, sg, kct, vc, log_gamma):
    b, l, rw = q.shape
    lc = vc.shape[1]
    c = RET_CHUNK
    gw = GROUP_W
    hpg = HEADS_PER_GROUP
    ng = rw // gw
    nchunk = l // c
    pw = STATE_BLOCK_W
    npair = gw // pw
    assert c & (c - 1) == 0
    lg = log_gamma.reshape(2, ng, hpg).transpose(1, 0, 2)
    lg_lane = jnp.repeat(lg, RET_HEAD_DIM, axis=2)
    lg_col = jnp.repeat(lg, c, axis=2)
    tok = pl.BlockSpec((1, l, gw), lambda i, j: (i, 0, j))
    tokt = pl.BlockSpec((1, gw, l), lambda i, j: (i, j, 0))
    grp = lambda r, width: pl.BlockSpec((None, r, width), lambda i, j: (j, 0, 0))
    return pl.pallas_call(
        functools.partial(_ret_kernel, nchunk=nchunk, unroll=math.gcd(nchunk, RET_UNROLL)),
        grid=(b, ng),
        in_specs=[
            tok, tokt, tok, tok,
            pl.BlockSpec((1, gw, lc), lambda i, j: (i, j, 0)),
            pl.BlockSpec((1, lc, gw), lambda i, j: (i, 0, j)),
            grp(2, gw), grp(2, hpg * c),
        ],
        out_specs=tok,
        out_shape=jax.ShapeDtypeStruct((b, l, rw), BF16),
        scratch_shapes=[pltpu.VMEM((nchunk, npair, pw, pw), BF16), pltpu.VMEM((nchunk, npair, pw, pw), BF16),
                        pltpu.VMEM((npair, pw, pw), F32), pltpu.VMEM((npair, pw, pw), F32),
                        pltpu.VMEM((nchunk, c, hpg * c), BF16),
                        pltpu.VMEM((nchunk, c, gw), F32),
                        pltpu.VMEM((c, hpg * c), F32)] + [pltpu.VMEM((c, gw), F32)] * 4
        + [pltpu.VMEM((nchunk, npair, pw, 2 * pw), F32)],
        compiler_params=_params("arbitrary", "arbitrary"),
        name="ret",
    )(q, kt, v, sg, kct, vc, lg_lane, lg_col)


def _out_kernel(w4_ref, wr_ref, wo_ref, w1_ref, w2_ref, gain2_ref, fgain_ref,
                x_hbm, mod_hbm, fm_hbm, z_hbm, gab_hbm, o_hbm, *, ff_chunk, tm):
    b, l, d = x_hbm.shape
    tok = lambda width: pl.BlockSpec((1, tm, width), lambda i, j: (i, j, 0))
    pltpu.emit_pipeline(
        functools.partial(_out_tile, w4_ref, wr_ref, wo_ref, w1_ref, w2_ref, gain2_ref, fgain_ref, ff_chunk),
        grid=(b, l // tm),
        in_specs=[tok(d), pl.BlockSpec((1,) + mod_hbm.shape[1:], lambda i, j: (i, 0, 0)),
                  tok(fm_hbm.shape[2]), tok(z_hbm.shape[2]), tok(gab_hbm.shape[2])],
        out_specs=[tok(d)],
    )(x_hbm, mod_hbm, fm_hbm, z_hbm, gab_hbm, o_hbm)


def _out_tile(w4_ref, wr_ref, wo_ref, w1_ref, w2_ref, gain2_ref, fgain_ref, ff_chunk,
              x_ref, mod_ref, fm_ref, z_ref, gab_ref, o_ref):
    mod = mod_ref[0]
    g1, sh2, sc2, g2 = mod[2:3], mod[3:4], mod[4:5], mod[5:6]
    y_four = _dot(fm_ref[0], w4_ref[...])
    y_ret = _dot(z_ref[0], wr_ref[...])
    d = x_ref.shape[2]
    y = (jax.nn.sigmoid(gab_ref[0, :, :d].astype(F32)) * y_four
         + jax.nn.sigmoid(gab_ref[0, :, d:].astype(F32)) * y_ret)
    x1 = x_ref[0] + g1 * _dot(y.astype(BF16), wo_ref[...])
    h2 = _norm_mod(x1, gain2_ref[...], sh2, sc2).astype(BF16)
    dff = w1_ref.shape[1]
    acc = None
    for lo in range(0, dff, ff_chunk):
        hid = jnp.maximum(_dot(h2, w1_ref[:, lo:lo + ff_chunk]), 0.0)
        part = _dot((hid * hid).astype(BF16), w2_ref[lo:lo + ff_chunk, :])
        acc = part if acc is None else acc + part
    x2 = x1 + g2 * acc
    ms = jnp.mean(x2 * x2, axis=-1, keepdims=True)
    o_ref[0] = x2 * lax.rsqrt(ms + EPS) * fgain_ref[...]


def _out(x, mods, fm, z, gab, w4, wr, wo, gain2, w1, w2, fgain):
    b, l, d = x.shape
    tm = min(TOKEN_TILE, l)
    resident = (w4, wr, wo, w1, w2, gain2, fgain)
    mods_by_sample = mods[:, :b, :].transpose(1, 0, 2)
    vmem = pl.BlockSpec(memory_space=pltpu.VMEM)
    hbm = pl.BlockSpec(memory_space=pl.ANY)
    return pl.pallas_call(
        functools.partial(_out_kernel, ff_chunk=min(1024, w1.shape[1]), tm=tm),
        in_specs=[vmem] * len(resident) + [hbm] * 5,
        out_specs=hbm,
        out_shape=jax.ShapeDtypeStruct((b, l, d), F32),
        compiler_params=_params(),
        name="out",
    )(*resident, x, mods_by_sample, fm, z, gab)


def _dft_tables(l, gd):
    la = FFT_LA
    lb = l // la

    def cs(n):
        idx = np.arange(n)
        ang = 2.0 * np.pi * ((idx[:, None] * idx[None, :]) % n) / n
        return np.cos(ang), np.sin(ang)

    cb, sb = cs(lb)
    eye = np.eye(FFT_LA_TILE)
    half = lb // 2 + 1
    kw = np.concatenate([np.kron(cb[:half], eye), -np.kron(sb[:half], eye)], axis=0)
    ca, sa = cs(la)
    w2 = np.block([[ca, sa], [-sa, ca]])
    cc, sc = cs(gd)
    chan = np.concatenate([cc, sc], axis=0)
    tw = 2.0 * np.pi * (np.arange(lb)[:, None] * np.arange(la)[None, :]) / l
    tw = tw.reshape(lb, la // FFT_LA_TILE, FFT_LA_TILE).transpose(1, 0, 2).reshape(la // FFT_LA_TILE, -1)
    twc = np.repeat(np.cos(tw)[:, :, None], LANES, axis=2)
    tws = np.repeat(np.sin(tw)[:, :, None], LANES, axis=2)
    as_bf = lambda a: jnp.asarray(a, dtype=F32).astype(BF16)
    return as_bf(kw), as_bf(w2), as_bf(chan), jnp.asarray(twc, F32), jnp.asarray(tws, F32), la, lb


def _rope_tables(l):
    f32 = np.float32
    nf = RET_HEAD_DIM // 4
    inv = np.power(f32(ROPE_BASE), -np.arange(nf, dtype=f32) / f32(nf)).astype(f32)
    rows = l // GRID_W
    r, cc = np.meshgrid(np.arange(rows, dtype=f32), np.arange(GRID_W, dtype=f32), indexing="ij")
    ang = np.concatenate([r.reshape(-1)[:, None] * inv, cc.reshape(-1)[:, None] * inv], axis=-1).astype(f32)
    cos, sin = np.cos(ang).astype(f32), np.sin(ang).astype(f32)
    cos_h = np.concatenate([cos, cos], axis=1)
    sin_h = np.concatenate([-sin, sin], axis=1)
    reps = LANES // RET_HEAD_DIM
    return jnp.asarray(np.tile(cos_h, (1, reps))), jnp.asarray(np.tile(sin_h, (1, reps)))


def kernel(x, c, ctx, c_ctx, w_ada, b_ada, norm1_gain, w_in, four_w_out, ret_decay_logit,
           ret_gn_gain, ret_w_out, w_out, norm2_gain, w_mlp1, w_mlp2, final_gain):
    assert w_ada.shape[0] == 1, "single-layer block"
    b, l, d = x.shape
    lc = ctx.shape[1]
    fw = four_w_out.shape[1]
    rw = ret_w_out.shape[1]
    gd = fw // FOUR_GROUPS
    assert l % FFT_LA == 0 and l % RET_CHUNK == 0 and rw % GROUP_W == 0

    kw, w2, chan, twc, tws, la, lb = _dft_tables(l, gd)
    cosf, sinf = _rope_tables(l)

    later = [four_w_out[0], ret_w_out[0], w_out[0], w_mlp1[0], w_mlp2[0]]
    (u, q, kt, v, sg, gab), later_b, mods, kct, vc = _proj(
        x, c, c_ctx, w_ada[0], b_ada, ctx, norm1_gain, w_in[0], cosf, sinf, ret_gn_gain, later, fw, rw)

    fm = _fft(u, kw, twc, tws, w2, chan, la, lb, gd, 1.0 / math.sqrt(l * gd))

    log_gamma = jax.nn.log_sigmoid(ret_decay_logit[0].astype(F32))
    z = _retention(q, kt, v, sg, kct, vc, log_gamma)

    w4, wr, wo, w1, w2 = later_b
    return _out(x, mods, fm, z, gab, w4, wr, wo, norm2_gain, w1, w2, final_gain[None, :])
```

```python
import functools
import math

import jax
import jax.numpy as jnp
import numpy as np
from jax import lax
from jax.experimental import pallas as pl
from jax.experimental.pallas import tpu as pltpu

F32 = jnp.float32
BF16 = jnp.bfloat16

GRID_W = 64
FOUR_GROUPS = 4
RET_HEAD_DIM = 64
N_MOD = 6
ROPE_BASE = 10000.0
EPS = 1e-6

LANES = 128
MXU_DIM = 256
F32_SUBLANES = 8
BF16_SUBLANES = 16
VMEM_LIMIT_BYTES = 56 * 1024 * 1024

RET_CHUNK = 128
RET_UNROLL = 32
RET_NORM_CHUNKS = 4
HEADS_PER_GROUP = MXU_DIM // RET_HEAD_DIM
GROUP_W = HEADS_PER_GROUP * RET_HEAD_DIM
STATE_BLOCK_W = LANES
FFT_LA = 128
FFT_LA_TILE = BF16_SUBLANES
FFT_SCATTER_PAD = F32_SUBLANES
TOKEN_TILE = 512
PROJ_TOKEN_TILE = 512
WEIGHT_STAGE_COLS = 512
WEIGHT_STAGE_SLOTS = 6


def _dot(a, b):
    return jnp.dot(a, b, preferred_element_type=F32)


def _norm_mod(x, gain, shift, scale):
    ms = jnp.mean(x * x, axis=-1, keepdims=True)
    y = x * lax.rsqrt(ms + EPS) * gain
    return y * (1.0 + scale) + shift


def _tile_lanes(t, reps):
    return jnp.concatenate([t] * reps, axis=1) if reps > 1 else t


def _const_spec(shape):
    nd = len(shape)
    return pl.BlockSpec(shape, lambda *_: (0,) * nd, pipeline_mode=pl.Buffered(1))


def _params(*sem):
    return pltpu.CompilerParams(dimension_semantics=sem, vmem_limit_bytes=VMEM_LIMIT_BYTES)


def _is_first_step(grid_rank):
    first = pl.program_id(0) == 0
    for axis in range(1, grid_rank):
        first = first & (pl.program_id(axis) == 0)
    return first


def _proj_prepare(before_body, c_ref, cctx_ref, wada_hbm, bada_ref, w_hbm, ctx_ref, gain_ref,
                  mods_ref, kct_ref, vc_ref, w_ref, mod_ref, cv_ref, stage_ref, sem_ref, *, fw, rw, scale):
    nb = c_ref.shape[0]
    d = w_ref.shape[0]
    if before_body:
        cv_ref[...] = jnp.zeros_like(cv_ref)
        cv_ref[0:nb, :] = c_ref[...]
        cv_ref[nb:nb + 1, :] = cctx_ref[...]
    cv = cv_ref[...]
    s = (cv * jax.nn.sigmoid(cv)).astype(BF16)

    slots, _, sc = stage_ref.shape
    ada = [(wada_hbm, c0) for c0 in range(0, wada_hbm.shape[1], sc)]
    win = [(w_hbm, c0) for c0 in range(0, w_hbm.shape[1], sc)]
    body_mods = 2 * d // sc
    chunks = ada[:body_mods] + win + ada[body_mods:]
    split = body_mods + len(win)
    ctx_after = body_mods + (fw + 3 * rw) // sc - 1

    def copy(k):
        src, c0 = chunks[k]
        slot = k % slots
        return pltpu.make_async_copy(src.at[:, pl.ds(c0, sc)], stage_ref.at[slot], sem_ref.at[slot])

    if before_body:
        for k in range(min(slots - 1, len(chunks))):
            copy(k).start()
    for k in range(0, split) if before_body else range(split, len(chunks)):
        src, c0 = chunks[k]
        if k + slots - 1 < len(chunks):
            copy(k + slots - 1).start()
        copy(k).wait()
        blk = stage_ref[k % slots].astype(BF16)
        if src is wada_hbm:
            j, off = divmod(c0, d)
            hw = sc // 2
            mod_ref[j, :, off:off + hw] = _dot(s, blk[:, :hw]) + bada_ref[:, c0:c0 + hw]
            mod_ref[j, :, off + hw:off + sc] = _dot(s, blk[:, hw:]) + bada_ref[:, c0 + hw:c0 + sc]
        else:
            w_ref[:, c0:c0 + sc] = blk
        if k == ctx_after:
            for i in range(nb):
                h = _norm_mod(ctx_ref[i], gain_ref[...],
                              mod_ref[0, nb:nb + 1, :], mod_ref[1, nb:nb + 1, :]).astype(BF16)
                kct_ref[i] = (_dot(h, w_ref[:, fw + rw:fw + 2 * rw]) * scale).T.astype(BF16)
                vc_ref[i] = _dot(h, w_ref[:, fw + 2 * rw:fw + 3 * rw]).astype(BF16)
    if not before_body:
        mods_ref[...] = mod_ref[...]


def _proj_kernel(*refs, fw, rw, d, scale, ncast):
    x_ref, cos_ref, sin_ref, w_hbm, wada_hbm = refs[:5]
    cast_in = refs[5:5 + ncast]
    gain_ref, gng_ref, c_ref, cctx_ref, bada_ref, ctx_ref = refs[5 + ncast:11 + ncast]
    outs = refs[11 + ncast:]
    u_ref, q_ref, kt_ref, v_ref, sg_ref, gab_ref = outs[:6]
    cast_out = outs[6:6 + ncast]
    mods_ref, kct_ref, vc_ref = outs[6 + ncast:9 + ncast]
    w_ref, mod_ref, cv_ref, stage_ref, sem_ref = outs[9 + ncast:]

    prepare = functools.partial(
        _proj_prepare, c_ref=c_ref, cctx_ref=cctx_ref, wada_hbm=wada_hbm, bada_ref=bada_ref, w_hbm=w_hbm,
        ctx_ref=ctx_ref, gain_ref=gain_ref, mods_ref=mods_ref, kct_ref=kct_ref, vc_ref=vc_ref, w_ref=w_ref,
        mod_ref=mod_ref, cv_ref=cv_ref, stage_ref=stage_ref, sem_ref=sem_ref, fw=fw, rw=rw, scale=scale)
    pl.when(_is_first_step(2))(functools.partial(prepare, True))

    for src, dst in zip(cast_in, cast_out):
        dst[...] = src[...].astype(BF16)

    mod = mod_ref[0:2, pl.ds(pl.program_id(0), 1), :]
    h = _norm_mod(x_ref[0], gain_ref[...], mod[0], mod[1]).astype(BF16)
    tm = h.shape[0]

    def proj(lo, width):
        return _dot(h, w_ref[:, lo:lo + width])

    u_ref[0] = proj(0, fw).astype(BF16)

    reps = rw // cos_ref.shape[1]
    cosf = _tile_lanes(cos_ref[...], reps)
    sinf = _tile_lanes(sin_ref[...], reps)
    lane = lax.broadcasted_iota(jnp.int32, (tm, rw), 1)
    first_half = (lane & (RET_HEAD_DIM // 2)) == 0

    def rope(t):
        rot = jnp.where(first_half,
                        pltpu.roll(t, rw - RET_HEAD_DIM // 2, 1),
                        pltpu.roll(t, RET_HEAD_DIM // 2, 1))
        return t * cosf + rot * sinf

    q_ref[0] = rope(proj(fw, rw)).astype(BF16)
    kt_ref[0] = (rope(proj(fw + rw, rw)) * scale).T.astype(BF16)
    v_ref[0] = proj(fw + 2 * rw, rw).astype(BF16)
    g = proj(fw + 3 * rw, rw)
    sg_ref[0] = (g * jax.nn.sigmoid(g) * gng_ref[...]).astype(BF16)
    gab_ref[0, :, :d] = proj(fw + 4 * rw, d).astype(BF16)
    gab_ref[0, :, d:] = proj(fw + 4 * rw + d, d).astype(BF16)

    pl.when(_is_first_step(2))(functools.partial(prepare, False))


def _proj(x, c, c_ctx, w_ada, b_ada, ctx, gain, w_in, cosf, sinf, gn_gain, later_weights, fw, rw):
    b, l, d = x.shape
    lc = ctx.shape[1]
    tm = min(PROJ_TOKEN_TILE, l)
    nj = l // tm
    steps = b * nj
    rows = -(-(b + 1) // F32_SUBLANES) * F32_SUBLANES
    sc = WEIGHT_STAGE_COLS
    assert w_ada.shape == (d, N_MOD * d) and d % sc == 0 and w_in.shape[1] % sc == 0
    tok = lambda width: pl.BlockSpec((1, tm, width), lambda i, j: (i, j, 0))
    out = lambda width: jax.ShapeDtypeStruct((b, l, width), BF16)
    sliced = [w.reshape(steps, w.shape[0] // steps, w.shape[1]) for w in later_weights]
    for w in sliced:
        assert w.shape[1] % BF16_SUBLANES == 0
    cast_spec = lambda w: pl.BlockSpec((1,) + w.shape[1:], lambda i, j: (i * nj + j, 0, 0))
    whole = lambda shape: pl.BlockSpec(shape, lambda i, j: (0,) * len(shape))
    res = pl.pallas_call(
        functools.partial(_proj_kernel, fw=fw, rw=rw, d=d, scale=RET_HEAD_DIM ** -0.5, ncast=len(sliced)),
        grid=(b, nj),
        in_specs=[
            tok(d),
            pl.BlockSpec((tm, cosf.shape[1]), lambda i, j: (j, 0)),
            pl.BlockSpec((tm, sinf.shape[1]), lambda i, j: (j, 0)),
            pl.BlockSpec(memory_space=pl.ANY),
            pl.BlockSpec(memory_space=pl.ANY),
        ] + [cast_spec(w) for w in sliced] + [
            _const_spec((1, d)), _const_spec((1, rw)), _const_spec((b, d)), _const_spec((1, d)),
            _const_spec((1, N_MOD * d)), _const_spec((b, lc, d)),
        ],
        out_specs=[tok(fw), tok(rw), pl.BlockSpec((1, rw, tm), lambda i, j: (i, 0, j)),
                   tok(rw), tok(rw), tok(2 * d)] + [cast_spec(w) for w in sliced]
        + [whole((N_MOD, rows, d)), whole((b, rw, lc)), whole((b, lc, rw))],
        out_shape=[out(fw), out(rw), jax.ShapeDtypeStruct((b, rw, l), BF16),
                   out(rw), out(rw), out(2 * d)]
        + [jax.ShapeDtypeStruct(w.shape, BF16) for w in sliced]
        + [jax.ShapeDtypeStruct((N_MOD, rows, d), F32),
           jax.ShapeDtypeStruct((b, rw, lc), BF16), jax.ShapeDtypeStruct((b, lc, rw), BF16)],
        scratch_shapes=[pltpu.VMEM(w_in.shape, BF16),
                        pltpu.VMEM((N_MOD, rows, d), F32),
                        pltpu.VMEM((rows, d), F32),
                        pltpu.VMEM((WEIGHT_STAGE_SLOTS, d, sc), F32),
                        pltpu.SemaphoreType.DMA((WEIGHT_STAGE_SLOTS,))],
        compiler_params=_params("arbitrary", "arbitrary"),
        name="proj",
    )(x, cosf, sinf, w_in, w_ada, *sliced, gain, gn_gain, c, c_ctx[None, :], b_ada, ctx)
    ncast = len(sliced)
    casts = [cc.reshape(w.shape) for cc, w in zip(res[6:6 + ncast], later_weights)]
    mods, kct, vc = res[6 + ncast:]
    return res[:6], casts, mods, kct, vc


def _fft_kernel(u_ref, kw_ref, twc_ref, tws_ref, w2_ref, cs_ref, o_ref, tr_ref, ti_ref, scr_ref,
                *, la, lb, gd, npb, scale):
    r = FFT_LA_TILE
    rows = lb * r
    fw = u_ref.shape[2]
    ngroups = fw // gd
    reps = fw // twc_ref.shape[2]
    pitch = scr_ref.shape[1] // la

    @pl.when(pl.program_id(0) == 0)
    def _():
        scr_ref[...] = jnp.zeros_like(scr_ref)

    for j in range(la // r):
        u = jnp.concatenate([u_ref[0, k * la + j * r:k * la + (j + 1) * r, :] for k in range(lb)], axis=0)
        half = lb // 2 + 1
        t = _dot(kw_ref[...], u)
        for k in range(lb):
            src = k if k < half else lb - k
            a = t[src * r:(src + 1) * r]
            b = t[(half + src) * r:(half + src + 1) * r]
            ct = _tile_lanes(twc_ref[j, k * r:(k + 1) * r, :], reps)
            st = _tile_lanes(tws_ref[j, k * r:(k + 1) * r, :], reps)
            if k < half:
                tr, ti = a * ct + b * st, b * ct - a * st
            else:
                tr, ti = a * ct - b * st, -(b * ct) - a * st
            tr_ref[k * la + j * r:k * la + (j + 1) * r, :] = tr.astype(BF16)
            ti_ref[k * la + j * r:k * la + (j + 1) * r, :] = ti.astype(BF16)

    for pb in range(lb // npb):
        xs = []
        for p in range(npb):
            lo = (pb * npb + p) * la
            t = jnp.concatenate([tr_ref[lo:lo + la, :], ti_ref[lo:lo + la, :]], axis=0)
            xs.append(_dot(w2_ref[...], t).astype(BF16))
        for gi in range(ngroups):
            cols = slice(gi * gd, (gi + 1) * gd)
            lhs = jnp.concatenate([jnp.concatenate([x[:la, cols], x[la:, cols]], axis=1) for x in xs], axis=0)
            fg = _dot(lhs, cs_ref[...]) * scale
            for p in range(npb):
                scr_ref[gi, pl.ds(p, la, stride=pitch), :] = fg[p * la:(p + 1) * la]
        full = jnp.concatenate([scr_ref[gi].reshape(la, pitch, gd)[:, :npb, :] for gi in range(ngroups)], axis=2)
        o_ref[0, :, pb * npb:(pb + 1) * npb, :] = full.astype(BF16)


def _fft(u, kw, twc, tws, w2, cs, la, lb, gd, scale):
    b, l, fw = u.shape
    npb = min(BF16_SUBLANES, lb)
    out = pl.pallas_call(
        functools.partial(_fft_kernel, la=la, lb=lb, gd=gd, npb=npb, scale=scale),
        grid=(b,),
        in_specs=[pl.BlockSpec((1, l, fw), lambda i: (i, 0, 0)),
                  _const_spec(kw.shape), _const_spec(twc.shape), _const_spec(tws.shape),
                  _const_spec(w2.shape), _const_spec(cs.shape)],
        out_specs=pl.BlockSpec((1, la, lb, fw), lambda i: (i, 0, 0, 0)),
        out_shape=jax.ShapeDtypeStruct((b, la, lb, fw), BF16),
        scratch_shapes=[pltpu.VMEM((l, fw), BF16), pltpu.VMEM((l, fw), BF16),
                        pltpu.VMEM((fw // gd, la * (npb + FFT_SCATTER_PAD), gd), F32)],
        compiler_params=_params("arbitrary"),
        name="fft",
    )(u, kw, twc, tws, w2, cs)
    return out.reshape(b, l, fw)


def _ret_kernel(q_ref, kt_ref, v_ref, sg_ref, kct_ref, vc_ref, lgl_ref, lgc_ref,
                z_ref, sf_ref, sb_ref, stf_ref, stb_ref, p_ref, o_ref,
                dall_ref, qdf_ref, qdb_ref, kdf_ref, kdb_ref, inc_ref, *, nchunk, unroll):
    c = RET_CHUNK
    gw = GROUP_W
    lc = vc_ref.shape[1]

    lgf, lgb = lgl_ref[0:1, :], lgl_ref[1:2, :]
    cdf = jnp.exp(lgf * c)
    cdb = jnp.exp(lgb * c)
    cpos = lax.broadcasted_iota(jnp.int32, (lc, gw), 0).astype(F32)
    wcf = jnp.exp(lgf * (lc - 1.0 - cpos))
    wcb = jnp.exp(lgb * cpos)

    @pl.when(pl.program_id(1) == 0)
    def _():
        pos = lax.broadcasted_iota(jnp.int32, (c, gw), 0).astype(F32)
        qdf_ref[...] = jnp.exp(lgf * (pos + 1.0))
        qdb_ref[...] = jnp.exp(lgb * (c - pos))
        kdf_ref[...] = jnp.exp(lgf * (c - 1.0 - pos))
        kdb_ref[...] = jnp.exp(lgb * pos)
        si = lax.broadcasted_iota(jnp.int32, (c, HEADS_PER_GROUP * c), 0)
        sj = lax.broadcasted_iota(jnp.int32, (c, HEADS_PER_GROUP * c), 1) & (c - 1)
        diff = (si - sj).astype(F32)
        dall_ref[...] = (jnp.where(diff >= 0, jnp.exp(lgc_ref[0:1, :] * jnp.maximum(diff, 0.0)), 0.0)
                         + jnp.where(diff <= 0, jnp.exp(lgc_ref[1:2, :] * jnp.maximum(-diff, 0.0)), 0.0))

    pw = STATE_BLOCK_W
    npair = gw // pw
    same_head_p = (lax.broadcasted_iota(jnp.int32, (pw, pw), 0) // RET_HEAD_DIM
                   == lax.broadcasted_iota(jnp.int32, (pw, pw), 1) // RET_HEAD_DIM)

    def diag_blocks(t):
        return jnp.where(same_head_p, t, 0.0)

    def weighted(t, w):
        return (t.astype(F32) * w).astype(BF16)

    def rows(ref, n):
        return ref[0, pl.ds(pl.multiple_of(n * c, c), c), :]

    def kt_chunk(n):
        return kt_ref[0, :, pl.ds(pl.multiple_of(n * c, c), c)]

    def state_update(st_ref, kt, vw, cd):
        for p in range(npair):
            cols = slice(p * pw, (p + 1) * pw)
            st_ref[p] = st_ref[p] * cd[:, cols] + diag_blocks(_dot(kt[cols, :], vw[:, cols]))

    def full_state(s_ref, n):
        zero = jnp.zeros((pw, pw), BF16)
        return jnp.concatenate(
            [jnp.concatenate([s_ref[n, p] if q == p else zero for q in range(npair)], axis=1)
             for p in range(npair)], axis=0)

    kct = kct_ref[0]
    vc = vc_ref[0]
    stf_ref[...] = jnp.zeros_like(stf_ref)
    stb_ref[...] = jnp.zeros_like(stb_ref)
    state_update(stf_ref, kct, weighted(vc, wcf), cdf)
    state_update(stb_ref, kct, weighted(vc, wcb), cdb)

    def increment_body(n, carry):
        ktn = kt_chunk(n)
        vn = rows(v_ref, n)
        vwf = weighted(vn, kdf_ref[...])
        vwb = weighted(vn, kdb_ref[...])
        for p in range(npair):
            cols = slice(p * pw, (p + 1) * pw)
            inc_ref[n, p] = _dot(ktn[cols, :], jnp.concatenate([vwf[:, cols], vwb[:, cols]], axis=1))
        return carry

    lax.fori_loop(0, nchunk, increment_body, 0, unroll=unroll)

    def scan_body(i, carry):
        nf = i
        nb = nchunk - 1 - i
        sf_ref[nf] = stf_ref[...].astype(BF16)
        sb_ref[nb] = stb_ref[...].astype(BF16)
        for p in range(npair):
            cols = slice(p * pw, (p + 1) * pw)
            stf_ref[p] = stf_ref[p] * cdf[:, cols] + diag_blocks(inc_ref[nf, p, :, :pw])
            stb_ref[p] = stb_ref[p] * cdb[:, cols] + diag_blocks(inc_ref[nb, p, :, pw:])
        return carry

    lax.fori_loop(0, nchunk, scan_body, 0, unroll=unroll)

    lane = lax.broadcasted_iota(jnp.int32, (c, gw), 1)
    sub = lax.broadcasted_iota(jnp.int32, (gw, c), 0)
    lane_masks = [(lane >= h * RET_HEAD_DIM) & (lane < (h + 1) * RET_HEAD_DIM)
                  for h in range(HEADS_PER_GROUP)]
    sub_masks = [(sub >= h * RET_HEAD_DIM) & (sub < (h + 1) * RET_HEAD_DIM)
                 for h in range(HEADS_PER_GROUP)]
    same_head = (lax.broadcasted_iota(jnp.int32, (gw, gw), 0) // RET_HEAD_DIM
                 == lax.broadcasted_iota(jnp.int32, (gw, gw), 1) // RET_HEAD_DIM)
    bd_mean = jnp.where(same_head, 1.0 / RET_HEAD_DIM, 0.0).astype(BF16)

    def score_body(n, carry):
        ktn = kt_chunk(n)
        zk = jnp.zeros_like(ktn)
        kbd = jnp.concatenate([jnp.where(m, ktn, zk) for m in sub_masks], axis=1)
        p_ref[n] = (_dot(rows(q_ref, n), kbd) * dall_ref[...]).astype(BF16)
        return carry

    lax.fori_loop(0, nchunk, score_body, 0, unroll=unroll)

    def mix_body(n, carry):
        qn = rows(q_ref, n)
        vn = rows(v_ref, n)
        zv = jnp.zeros_like(vn)
        vbd = jnp.concatenate([jnp.where(m, vn, zv) for m in lane_masks], axis=0)
        o_ref[n] = (_dot(p_ref[n], vbd) + qdf_ref[...] * _dot(qn, full_state(sf_ref, n))
                    + qdb_ref[...] * _dot(qn, full_state(sb_ref, n)))
        return carry

    lax.fori_loop(0, nchunk, mix_body, 0, unroll=unroll)

    nc = math.gcd(nchunk, RET_NORM_CHUNKS)

    def norm_body(m, carry):
        o = o_ref[pl.ds(m * nc, nc)].reshape(nc * c, gw)
        tokens = pl.ds(pl.multiple_of(m * (nc * c), nc * c), nc * c)
        ms = _dot((o * o).astype(BF16), bd_mean)
        z = sg_ref[0, tokens, :].astype(F32) * (o * lax.rsqrt(ms + EPS))
        z_ref[0, tokens, :] = z.astype(BF16)
        return carry

    lax.fori_loop(0, nchunk // nc, norm_body, 0, unroll=max(unroll // nc, 1))


def _retention(q, kt, v, sg, kct, vc, log_gamma):
    b, l, rw = q.shape
    lc = vc.shape[1]
    c = RET_CHUNK
    gw = GROUP_W
    hpg = HEADS_PER_GROUP
    ng = rw // gw
    nchunk = l // c
    pw = STATE_BLOCK_W
    npair = gw // pw
    assert c & (c - 1) == 0
    lg = log_gamma.reshape(2, ng, hpg).transpose(1, 0, 2)
    lg_lane = jnp.repeat(lg, RET_HEAD_DIM, axis=2)
    lg_col = jnp.repeat(lg, c, axis=2)
    tok = pl.BlockSpec((1, l, gw), lambda j, i: (i, 0, j))
    tokt = pl.BlockSpec((1, gw, l), lambda j, i: (i, j, 0))
    grp = lambda r, width: pl.BlockSpec((None, r, width), lambda j, i: (j, 0, 0))
    return pl.pallas_call(
        functools.partial(_ret_kernel, nchunk=nchunk, unroll=math.gcd(nchunk, RET_UNROLL)),
        grid=(ng, b),
        in_specs=[
            tok, tokt, tok, tok,
            pl.BlockSpec((1, gw, lc), lambda j, i: (i, j, 0)),
            pl.BlockSpec((1, lc, gw), lambda j, i: (i, 0, j)),
            grp(2, gw), grp(2, hpg * c),
        ],
        out_specs=tok,
        out_shape=jax.ShapeDtypeStruct((b, l, rw), BF16),
        scratch_shapes=[pltpu.VMEM((nchunk, npair, pw, pw), BF16), pltpu.VMEM((nchunk, npair, pw, pw), BF16),
                        pltpu.VMEM((npair, pw, pw), F32), pltpu.VMEM((npair, pw, pw), F32),
                        pltpu.VMEM((nchunk, c, hpg * c), BF16),
                        pltpu.VMEM((nchunk, c, gw), F32),
                        pltpu.VMEM((c, hpg * c), F32)] + [pltpu.VMEM((c, gw), F32)] * 4
        + [pltpu.VMEM((nchunk, npair, pw, 2 * pw), F32)],
        compiler_params=_params("arbitrary", "arbitrary"),
        name="ret",
    )(q, kt, v, sg, kct, vc, lg_lane, lg_col)


def _out_kernel(w4_ref, wr_ref, wo_ref, w1_ref, w2_ref, x_ref, mod_ref, fm_ref, z_ref, gab_ref,
                gain2_ref, fgain_ref, o_ref, *, ff_chunk):
    mod = mod_ref[:, pl.ds(pl.program_id(0), 1), :]
    g1, sh2, sc2, g2 = mod[2], mod[3], mod[4], mod[5]
    y_four = _dot(fm_ref[0], w4_ref[...])
    y_ret = _dot(z_ref[0], wr_ref[...])
    d = x_ref.shape[2]
    y = (jax.nn.sigmoid(gab_ref[0, :, :d].astype(F32)) * y_four
         + jax.nn.sigmoid(gab_ref[0, :, d:].astype(F32)) * y_ret)
    x1 = x_ref[0] + g1 * _dot(y.astype(BF16), wo_ref[...])
    h2 = _norm_mod(x1, gain2_ref[...], sh2, sc2).astype(BF16)
    dff = w1_ref.shape[1]
    acc = None
    for lo in range(0, dff, ff_chunk):
        hid = jnp.maximum(_dot(h2, w1_ref[:, lo:lo + ff_chunk]), 0.0)
        part = _dot((hid * hid).astype(BF16), w2_ref[lo:lo + ff_chunk, :])
        acc = part if acc is None else acc + part
    x2 = x1 + g2 * acc
    ms = jnp.mean(x2 * x2, axis=-1, keepdims=True)
    o_ref[0] = x2 * lax.rsqrt(ms + EPS) * fgain_ref[...]


def _out(x, mods, fm, z, gab, w4, wr, wo, gain2, w1, w2, fgain):
    b, l, d = x.shape
    tm = min(TOKEN_TILE, l)
    tok = lambda width: pl.BlockSpec((1, tm, width), lambda i, j: (i, j, 0))
    weights = (w4, wr, wo, w1, w2)
    return pl.pallas_call(
        functools.partial(_out_kernel, ff_chunk=min(1024, w1.shape[1])),
        grid=(b, l // tm),
        in_specs=[_const_spec(w.shape) for w in weights] + [
            tok(d),
            _const_spec(mods.shape),
            tok(fm.shape[2]), tok(z.shape[2]), tok(2 * d),
            _const_spec((1, d)), _const_spec((1, d)),
        ],
        out_specs=tok(d),
        out_shape=jax.ShapeDtypeStruct((b, l, d), F32),
        compiler_params=_params("arbitrary", "arbitrary"),
        name="out",
    )(*weights, x, mods, fm, z, gab, gain2, fgain)


def _dft_tables(l, gd):
    la = FFT_LA
    lb = l // la

    def cs(n):
        idx = np.arange(n)
        ang = 2.0 * np.pi * ((idx[:, None] * idx[None, :]) % n) / n
        return np.cos(ang), np.sin(ang)

    cb, sb = cs(lb)
    eye = np.eye(FFT_LA_TILE)
    half = lb // 2 + 1
    kw = np.concatenate([np.kron(cb[:half], eye), -np.kron(sb[:half], eye)], axis=0)
    ca, sa = cs(la)
    w2 = np.block([[ca, sa], [-sa, ca]])
    cc, sc = cs(gd)
    chan = np.concatenate([cc, sc], axis=0)
    tw = 2.0 * np.pi * (np.arange(lb)[:, None] * np.arange(la)[None, :]) / l
    tw = tw.reshape(lb, la // FFT_LA_TILE, FFT_LA_TILE).transpose(1, 0, 2).reshape(la // FFT_LA_TILE, -1)
    twc = np.repeat(np.cos(tw)[:, :, None], LANES, axis=2)
    tws = np.repeat(np.sin(tw)[:, :, None], LANES, axis=2)
    as_bf = lambda a: jnp.asarray(a, dtype=F32).astype(BF16)
    return as_bf(kw), as_bf(w2), as_bf(chan), jnp.asarray(twc, F32), jnp.asarray(tws, F32), la, lb


def _rope_tables(l):
    f32 = np.float32
    nf = RET_HEAD_DIM // 4
    inv = np.power(f32(ROPE_BASE), -np.arange(nf, dtype=f32) / f32(nf)).astype(f32)
    rows = l // GRID_W
    r, cc = np.meshgrid(np.arange(rows, dtype=f32), np.arange(GRID_W, dtype=f32), indexing="ij")
    ang = np.concatenate([r.reshape(-1)[:, None] * inv, cc.reshape(-1)[:, None] * inv], axis=-1).astype(f32)
    cos, sin = np.cos(ang).astype(f32), np.sin(ang).astype(f32)
    cos_h = np.concatenate([cos, cos], axis=1)
    sin_h = np.concatenate([-sin, sin], axis=1)
    reps = LANES // RET_HEAD_DIM
    return jnp.asarray(np.tile(cos_h, (1, reps))), jnp.asarray(np.tile(sin_h, (1, reps)))


def kernel(x, c, ctx, c_ctx, w_ada, b_ada, norm1_gain, w_in, four_w_out, ret_decay_logit,
           ret_gn_gain, ret_w_out, w_out, norm2_gain, w_mlp1, w_mlp2, final_gain):
    assert w_ada.shape[0] == 1, "single-layer block"
    b, l, d = x.shape
    lc = ctx.shape[1]
    fw = four_w_out.shape[1]
    rw = ret_w_out.shape[1]
    gd = fw // FOUR_GROUPS
    assert l % FFT_LA == 0 and l % RET_CHUNK == 0 and rw % GROUP_W == 0

    kw, w2, chan, twc, tws, la, lb = _dft_tables(l, gd)
    cosf, sinf = _rope_tables(l)

    later = [four_w_out[0], ret_w_out[0], w_out[0], w_mlp1[0], w_mlp2[0]]
    (u, q, kt, v, sg, gab), later_b, mods, kct, vc = _proj(
        x, c, c_ctx, w_ada[0], b_ada, ctx, norm1_gain, w_in[0], cosf, sinf, ret_gn_gain, later, fw, rw)

    fm = _fft(u, kw, twc, tws, w2, chan, la, lb, gd, 1.0 / math.sqrt(l * gd))

    log_gamma = jax.nn.log_sigmoid(ret_decay_logit[0].astype(F32))
    z = _retention(q, kt, v, sg, kct, vc, log_gamma)

    w4, wr, wo, w1, w2 = later_b
    return _out(x, mods, fm, z, gab, w4, wr, wo, norm2_gain, w1, w2, final_gain[None, :])
```

```python
import functools
import math

import jax
import jax.numpy as jnp
import numpy as np
from jax import lax
from jax.experimental import pallas as pl
from jax.experimental.pallas import tpu as pltpu

F32 = jnp.float32
BF16 = jnp.bfloat16

GRID_W = 64
FOUR_GROUPS = 4
RET_HEAD_DIM = 64
N_MOD = 6
ROPE_BASE = 10000.0
EPS = 1e-6

LANES = 128
MXU_DIM = 256
F32_SUBLANES = 8
BF16_SUBLANES = 16
VMEM_LIMIT_BYTES = 56 * 1024 * 1024

RET_CHUNK = 128
RET_UNROLL = 32
RET_NORM_CHUNKS = 4
HEADS_PER_GROUP = MXU_DIM // RET_HEAD_DIM
GROUP_W = HEADS_PER_GROUP * RET_HEAD_DIM
STATE_BLOCK_W = LANES
FFT_LA = 128
FFT_LA_TILE = BF16_SUBLANES
FFT_SCATTER_PAD = F32_SUBLANES
TOKEN_TILE = 512
PROJ_TOKEN_TILE = 512
WEIGHT_STAGE_COLS = 512
WEIGHT_STAGE_SLOTS = 6


def _dot(a, b):
    return jnp.dot(a, b, preferred_element_type=F32)


def _norm_mod(x, gain, shift, scale):
    ms = jnp.mean(x * x, axis=-1, keepdims=True)
    y = x * lax.rsqrt(ms + EPS) * gain
    return y * (1.0 + scale) + shift


def _tile_lanes(t, reps):
    return jnp.concatenate([t] * reps, axis=1) if reps > 1 else t


def _const_spec(shape):
    nd = len(shape)
    return pl.BlockSpec(shape, lambda *_: (0,) * nd, pipeline_mode=pl.Buffered(1))


def _params(*sem, **extra):
    return pltpu.CompilerParams(dimension_semantics=sem, vmem_limit_bytes=VMEM_LIMIT_BYTES, **extra)


def _is_first_step(grid_rank):
    first = pl.program_id(0) == 0
    for axis in range(1, grid_rank):
        first = first & (pl.program_id(axis) == 0)
    return first


def _proj_prepare(before_body, c_ref, cctx_ref, wada_hbm, bada_ref, w_hbm, ctx_ref, gain_ref,
                  mods_ref, kct_ref, vc_ref, w_ref, mod_ref, cv_ref, stage_ref, sem_ref, *, fw, rw, scale):
    nb = c_ref.shape[0]
    d = w_ref.shape[0]
    if before_body:
        cv_ref[...] = jnp.zeros_like(cv_ref)
        cv_ref[0:nb, :] = c_ref[...]
        cv_ref[nb:nb + 1, :] = cctx_ref[...]
    cv = cv_ref[...]
    s = (cv * jax.nn.sigmoid(cv)).astype(BF16)

    slots, _, sc = stage_ref.shape
    ada = [(wada_hbm, c0) for c0 in range(0, wada_hbm.shape[1], sc)]
    win = [(w_hbm, c0) for c0 in range(0, w_hbm.shape[1], sc)]
    body_mods = 2 * d // sc
    chunks = ada[:body_mods] + win + ada[body_mods:]
    split = body_mods + len(win)
    ctx_after = body_mods + (fw + 3 * rw) // sc - 1

    def copy(k):
        src, c0 = chunks[k]
        slot = k % slots
        return pltpu.make_async_copy(src.at[:, pl.ds(c0, sc)], stage_ref.at[slot], sem_ref.at[slot])

    if before_body:
        for k in range(min(slots - 1, len(chunks))):
            copy(k).start()
    for k in range(0, split) if before_body else range(split, len(chunks)):
        src, c0 = chunks[k]
        if k + slots - 1 < len(chunks):
            copy(k + slots - 1).start()
        copy(k).wait()
        blk = stage_ref[k % slots].astype(BF16)
        if src is wada_hbm:
            j, off = divmod(c0, d)
            hw = sc // 2
            mod_ref[j, :, off:off + hw] = _dot(s, blk[:, :hw]) + bada_ref[:, c0:c0 + hw]
            mod_ref[j, :, off + hw:off + sc] = _dot(s, blk[:, hw:]) + bada_ref[:, c0 + hw:c0 + sc]
        else:
            w_ref[:, c0:c0 + sc] = blk
        if k == ctx_after:
            for i in range(nb):
                h = _norm_mod(ctx_ref[i], gain_ref[...],
                              mod_ref[0, nb:nb + 1, :], mod_ref[1, nb:nb + 1, :]).astype(BF16)
                kct_ref[i] = (_dot(h, w_ref[:, fw + rw:fw + 2 * rw]) * scale).T.astype(BF16)
                vc_ref[i] = _dot(h, w_ref[:, fw + 2 * rw:fw + 3 * rw]).astype(BF16)
    if not before_body:
        mods_ref[...] = mod_ref[...]


def _proj_kernel(*refs, fw, rw, d, scale, ncast):
    x_ref, cos_ref, sin_ref, w_hbm, wada_hbm = refs[:5]
    cast_in = refs[5:5 + ncast]
    gain_ref, gng_ref, c_ref, cctx_ref, bada_ref, ctx_ref = refs[5 + ncast:11 + ncast]
    outs = refs[11 + ncast:]
    u_ref, q_ref, kt_ref, v_ref, sg_ref, gab_ref = outs[:6]
    cast_out = outs[6:6 + ncast]
    mods_ref, kct_ref, vc_ref = outs[6 + ncast:9 + ncast]
    w_ref, mod_ref, cv_ref, stage_ref, sem_ref = outs[9 + ncast:]

    prepare = functools.partial(
        _proj_prepare, c_ref=c_ref, cctx_ref=cctx_ref, wada_hbm=wada_hbm, bada_ref=bada_ref, w_hbm=w_hbm,
        ctx_ref=ctx_ref, gain_ref=gain_ref, mods_ref=mods_ref, kct_ref=kct_ref, vc_ref=vc_ref, w_ref=w_ref,
        mod_ref=mod_ref, cv_ref=cv_ref, stage_ref=stage_ref, sem_ref=sem_ref, fw=fw, rw=rw, scale=scale)
    pl.when(_is_first_step(2))(functools.partial(prepare, True))

    for src, dst in zip(cast_in, cast_out):
        dst[...] = src[...].astype(BF16)

    mod = mod_ref[0:2, pl.ds(pl.program_id(0), 1), :]
    h = _norm_mod(x_ref[0], gain_ref[...], mod[0], mod[1]).astype(BF16)
    tm = h.shape[0]

    def proj(lo, width):
        return _dot(h, w_ref[:, lo:lo + width])

    u_ref[0] = proj(0, fw).astype(BF16)

    reps = rw // cos_ref.shape[1]
    cosf = _tile_lanes(cos_ref[...], reps)
    sinf = _tile_lanes(sin_ref[...], reps)
    lane = lax.broadcasted_iota(jnp.int32, (tm, rw), 1)
    first_half = (lane & (RET_HEAD_DIM // 2)) == 0

    def rope(t):
        rot = jnp.where(first_half,
                        pltpu.roll(t, rw - RET_HEAD_DIM // 2, 1),
                        pltpu.roll(t, RET_HEAD_DIM // 2, 1))
        return t * cosf + rot * sinf

    q_ref[0] = rope(proj(fw, rw)).astype(BF16)
    kt_ref[0] = (rope(proj(fw + rw, rw)) * scale).T.astype(BF16)
    v_ref[0] = proj(fw + 2 * rw, rw).astype(BF16)
    g = proj(fw + 3 * rw, rw)
    sg_ref[0] = (g * jax.nn.sigmoid(g) * gng_ref[...]).astype(BF16)
    gab_ref[0, :, :d] = proj(fw + 4 * rw, d).astype(BF16)
    gab_ref[0, :, d:] = proj(fw + 4 * rw + d, d).astype(BF16)

    pl.when(_is_first_step(2))(functools.partial(prepare, False))


def _proj(x, c, c_ctx, w_ada, b_ada, ctx, gain, w_in, cosf, sinf, gn_gain, later_weights, fw, rw):
    b, l, d = x.shape
    lc = ctx.shape[1]
    tm = min(PROJ_TOKEN_TILE, l)
    nj = l // tm
    steps = b * nj
    rows = -(-(b + 1) // F32_SUBLANES) * F32_SUBLANES
    sc = WEIGHT_STAGE_COLS
    assert w_ada.shape == (d, N_MOD * d) and d % sc == 0 and w_in.shape[1] % sc == 0
    tok = lambda width: pl.BlockSpec((1, tm, width), lambda i, j: (i, j, 0))
    out = lambda width: jax.ShapeDtypeStruct((b, l, width), BF16)
    sliced = [w.reshape(steps, w.shape[0] // steps, w.shape[1]) for w in later_weights]
    for w in sliced:
        assert w.shape[1] % BF16_SUBLANES == 0
    cast_spec = lambda w: pl.BlockSpec((1,) + w.shape[1:], lambda i, j: (i * nj + j, 0, 0))
    whole = lambda shape: pl.BlockSpec(shape, lambda i, j: (0,) * len(shape))
    res = pl.pallas_call(
        functools.partial(_proj_kernel, fw=fw, rw=rw, d=d, scale=RET_HEAD_DIM ** -0.5, ncast=len(sliced)),
        grid=(b, nj),
        in_specs=[
            tok(d),
            pl.BlockSpec((tm, cosf.shape[1]), lambda i, j: (j, 0)),
            pl.BlockSpec((tm, sinf.shape[1]), lambda i, j: (j, 0)),
            pl.BlockSpec(memory_space=pl.ANY),
            pl.BlockSpec(memory_space=pl.ANY),
        ] + [cast_spec(w) for w in sliced] + [
            _const_spec((1, d)), _const_spec((1, rw)), _const_spec((b, d)), _const_spec((1, d)),
            _const_spec((1, N_MOD * d)), _const_spec((b, lc, d)),
        ],
        out_specs=[tok(fw), tok(rw), pl.BlockSpec((1, rw, tm), lambda i, j: (i, 0, j)),
                   tok(rw), tok(rw), tok(2 * d)] + [cast_spec(w) for w in sliced]
        + [whole((N_MOD, rows, d)), whole((b, rw, lc)), whole((b, lc, rw))],
        out_shape=[out(fw), out(rw), jax.ShapeDtypeStruct((b, rw, l), BF16),
                   out(rw), out(rw), out(2 * d)]
        + [jax.ShapeDtypeStruct(w.shape, BF16) for w in sliced]
        + [jax.ShapeDtypeStruct((N_MOD, rows, d), F32),
           jax.ShapeDtypeStruct((b, rw, lc), BF16), jax.ShapeDtypeStruct((b, lc, rw), BF16)],
        scratch_shapes=[pltpu.VMEM(w_in.shape, BF16),
                        pltpu.VMEM((N_MOD, rows, d), F32),
                        pltpu.VMEM((rows, d), F32),
                        pltpu.VMEM((WEIGHT_STAGE_SLOTS, d, sc), F32),
                        pltpu.SemaphoreType.DMA((WEIGHT_STAGE_SLOTS,))],
        compiler_params=_params("arbitrary", "arbitrary"),
        name="proj",
    )(x, cosf, sinf, w_in, w_ada, *sliced, gain, gn_gain, c, c_ctx[None, :], b_ada, ctx)
    ncast = len(sliced)
    casts = [cc.reshape(w.shape) for cc, w in zip(res[6:6 + ncast], later_weights)]
    mods, kct, vc = res[6 + ncast:]
    return res[:6], casts, mods, kct, vc


def _fft_kernel(u_ref, kw_ref, twc_ref, tws_ref, w2_ref, cs_ref, o_ref, tr_ref, ti_ref, scr_ref,
                *, la, lb, gd, npb, scale):
    r = FFT_LA_TILE
    rows = lb * r
    fw = u_ref.shape[2]
    ngroups = fw // gd
    reps = fw // twc_ref.shape[2]
    pitch = scr_ref.shape[1] // la

    @pl.when(pl.program_id(0) == 0)
    def _():
        scr_ref[...] = jnp.zeros_like(scr_ref)

    for j in range(la // r):
        u = jnp.concatenate([u_ref[0, k * la + j * r:k * la + (j + 1) * r, :] for k in range(lb)], axis=0)
        half = lb // 2 + 1
        t = _dot(kw_ref[...], u)
        for k in range(lb):
            src = k if k < half else lb - k
            a = t[src * r:(src + 1) * r]
            b = t[(half + src) * r:(half + src + 1) * r]
            ct = _tile_lanes(twc_ref[j, k * r:(k + 1) * r, :], reps)
            st = _tile_lanes(tws_ref[j, k * r:(k + 1) * r, :], reps)
            if k < half:
                tr, ti = a * ct + b * st, b * ct - a * st
            else:
                tr, ti = a * ct - b * st, -(b * ct) - a * st
            tr_ref[k * la + j * r:k * la + (j + 1) * r, :] = tr.astype(BF16)
            ti_ref[k * la + j * r:k * la + (j + 1) * r, :] = ti.astype(BF16)

    for pb in range(lb // npb):
        xs = []
        for p in range(npb):
            lo = (pb * npb + p) * la
            t = jnp.concatenate([tr_ref[lo:lo + la, :], ti_ref[lo:lo + la, :]], axis=0)
            xs.append(_dot(w2_ref[...], t).astype(BF16))
        for gi in range(ngroups):
            cols = slice(gi * gd, (gi + 1) * gd)
            lhs = jnp.concatenate([jnp.concatenate([x[:la, cols], x[la:, cols]], axis=1) for x in xs], axis=0)
            fg = _dot(lhs, cs_ref[...]) * scale
            for p in range(npb):
                scr_ref[gi, pl.ds(p, la, stride=pitch), :] = fg[p * la:(p + 1) * la]
        full = jnp.concatenate([scr_ref[gi].reshape(la, pitch, gd)[:, :npb, :] for gi in range(ngroups)], axis=2)
        o_ref[0, :, pb * npb:(pb + 1) * npb, :] = full.astype(BF16)


def _fft(u, kw, twc, tws, w2, cs, la, lb, gd, scale):
    b, l, fw = u.shape
    npb = min(BF16_SUBLANES, lb)
    out = pl.pallas_call(
        functools.partial(_fft_kernel, la=la, lb=lb, gd=gd, npb=npb, scale=scale),
        grid=(b,),
        in_specs=[pl.BlockSpec((1, l, fw), lambda i: (i, 0, 0)),
                  _const_spec(kw.shape), _const_spec(twc.shape), _const_spec(tws.shape),
                  _const_spec(w2.shape), _const_spec(cs.shape)],
        out_specs=pl.BlockSpec((1, la, lb, fw), lambda i: (i, 0, 0, 0)),
        out_shape=jax.ShapeDtypeStruct((b, la, lb, fw), BF16),
        scratch_shapes=[pltpu.VMEM((l, fw), BF16), pltpu.VMEM((l, fw), BF16),
                        pltpu.VMEM((fw // gd, la * (npb + FFT_SCATTER_PAD), gd), F32)],
        compiler_params=_params("arbitrary"),
        name="fft",
    )(u, kw, twc, tws, w2, cs)
    return out.reshape(b, l, fw)


def _ret_kernel(q_ref, kt_ref, v_ref, sg_ref, kct_ref, vc_ref, lgl_ref, lgc_ref,
                z_ref, sf_ref, sb_ref, stf_ref, stb_ref, p_ref, o_ref,
                dall_ref, qdf_ref, qdb_ref, kdf_ref, kdb_ref, inc_ref, *, nchunk, unroll):
    c = RET_CHUNK
    gw = GROUP_W
    lc = vc_ref.shape[1]

    lgf, lgb = lgl_ref[0:1, :], lgl_ref[1:2, :]
    pos = lax.broadcasted_iota(jnp.int32, (c, gw), 0).astype(F32)
    qdf_ref[...] = jnp.exp(lgf * (pos + 1.0))
    qdb_ref[...] = jnp.exp(lgb * (c - pos))
    kdf_ref[...] = jnp.exp(lgf * (c - 1.0 - pos))
    kdb_ref[...] = jnp.exp(lgb * pos)
    cdf = jnp.exp(lgf * c)
    cdb = jnp.exp(lgb * c)
    cpos = lax.broadcasted_iota(jnp.int32, (lc, gw), 0).astype(F32)
    wcf = jnp.exp(lgf * (lc - 1.0 - cpos))
    wcb = jnp.exp(lgb * cpos)
    si = lax.broadcasted_iota(jnp.int32, (c, HEADS_PER_GROUP * c), 0)
    sj = lax.broadcasted_iota(jnp.int32, (c, HEADS_PER_GROUP * c), 1) & (c - 1)
    diff = (si - sj).astype(F32)
    dall_ref[...] = (jnp.where(diff >= 0, jnp.exp(lgc_ref[0:1, :] * jnp.maximum(diff, 0.0)), 0.0)
                     + jnp.where(diff <= 0, jnp.exp(lgc_ref[1:2, :] * jnp.maximum(-diff, 0.0)), 0.0))

    pw = STATE_BLOCK_W
    npair = gw // pw
    same_head_p = (lax.broadcasted_iota(jnp.int32, (pw, pw), 0) // RET_HEAD_DIM
                   == lax.broadcasted_iota(jnp.int32, (pw, pw), 1) // RET_HEAD_DIM)

    def diag_blocks(t):
        return jnp.where(same_head_p, t, 0.0)

    def weighted(t, w):
        return (t.astype(F32) * w).astype(BF16)

    def rows(ref, n):
        return ref[0, pl.ds(pl.multiple_of(n * c, c), c), :]

    def kt_chunk(n):
        return kt_ref[0, :, pl.ds(pl.multiple_of(n * c, c), c)]

    def state_update(st_ref, kt, vw, cd):
        for p in range(npair):
            cols = slice(p * pw, (p + 1) * pw)
            st_ref[p] = st_ref[p] * cd[:, cols] + diag_blocks(_dot(kt[cols, :], vw[:, cols]))

    def full_state(s_ref, n):
        zero = jnp.zeros((pw, pw), BF16)
        return jnp.concatenate(
            [jnp.concatenate([s_ref[n, p] if q == p else zero for q in range(npair)], axis=1)
             for p in range(npair)], axis=0)

    kct = kct_ref[0]
    vc = vc_ref[0]
    stf_ref[...] = jnp.zeros_like(stf_ref)
    stb_ref[...] = jnp.zeros_like(stb_ref)
    state_update(stf_ref, kct, weighted(vc, wcf), cdf)
    state_update(stb_ref, kct, weighted(vc, wcb), cdb)

    def increment_body(n, carry):
        ktn = kt_chunk(n)
        vn = rows(v_ref, n)
        vwf = weighted(vn, kdf_ref[...])
        vwb = weighted(vn, kdb_ref[...])
        for p in range(npair):
            cols = slice(p * pw, (p + 1) * pw)
            inc_ref[n, p] = _dot(ktn[cols, :], jnp.concatenate([vwf[:, cols], vwb[:, cols]], axis=1))
        return carry

    lax.fori_loop(0, nchunk, increment_body, 0, unroll=unroll)

    def scan_body(i, carry):
        nf = i
        nb = nchunk - 1 - i
        sf_ref[nf] = stf_ref[...].astype(BF16)
        sb_ref[nb] = stb_ref[...].astype(BF16)
        for p in range(npair):
            cols = slice(p * pw, (p + 1) * pw)
            stf_ref[p] = stf_ref[p] * cdf[:, cols] + diag_blocks(inc_ref[nf, p, :, :pw])
            stb_ref[p] = stb_ref[p] * cdb[:, cols] + diag_blocks(inc_ref[nb, p, :, pw:])
        return carry

    lax.fori_loop(0, nchunk, scan_body, 0, unroll=unroll)

    lane = lax.broadcasted_iota(jnp.int32, (c, gw), 1)
    sub = lax.broadcasted_iota(jnp.int32, (gw, c), 0)
    lane_masks = [(lane >= h * RET_HEAD_DIM) & (lane < (h + 1) * RET_HEAD_DIM)
                  for h in range(HEADS_PER_GROUP)]
    sub_masks = [(sub >= h * RET_HEAD_DIM) & (sub < (h + 1) * RET_HEAD_DIM)
                 for h in range(HEADS_PER_GROUP)]
    same_head = (lax.broadcasted_iota(jnp.int32, (gw, gw), 0) // RET_HEAD_DIM
                 == lax.broadcasted_iota(jnp.int32, (gw, gw), 1) // RET_HEAD_DIM)
    bd_mean = jnp.where(same_head, 1.0 / RET_HEAD_DIM, 0.0).astype(BF16)

    def score_body(n, carry):
        ktn = kt_chunk(n)
        zk = jnp.zeros_like(ktn)
        kbd = jnp.concatenate([jnp.where(m, ktn, zk) for m in sub_masks], axis=1)
        p_ref[n] = (_dot(rows(q_ref, n), kbd) * dall_ref[...]).astype(BF16)
        return carry

    lax.fori_loop(0, nchunk, score_body, 0, unroll=unroll)

    def mix_body(n, carry):
        qn = rows(q_ref, n)
        vn = rows(v_ref, n)
        zv = jnp.zeros_like(vn)
        vbd = jnp.concatenate([jnp.where(m, vn, zv) for m in lane_masks], axis=0)
        o_ref[n] = (_dot(p_ref[n], vbd) + qdf_ref[...] * _dot(qn, full_state(sf_ref, n))
                    + qdb_ref[...] * _dot(qn, full_state(sb_ref, n)))
        return carry

    lax.fori_loop(0, nchunk, mix_body, 0, unroll=unroll)

    nc = math.gcd(nchunk, RET_NORM_CHUNKS)

    def norm_body(m, carry):
        o = o_ref[pl.ds(m * nc, nc)].reshape(nc * c, gw)
        tokens = pl.ds(pl.multiple_of(m * (nc * c), nc * c), nc * c)
        ms = _dot((o * o).astype(BF16), bd_mean)
        z = sg_ref[0, tokens, :].astype(F32) * (o * lax.rsqrt(ms + EPS))
        z_ref[0, tokens, :] = z.astype(BF16)
        return carry

    lax.fori_loop(0, nchunk // nc, norm_body, 0, unroll=max(unroll // nc, 1))


def _retention(q, kt, v, sg, kct, vc, log_gamma):
    b, l, rw = q.shape
    lc = vc.shape[1]
    c = RET_CHUNK
    gw = GROUP_W
    hpg = HEADS_PER_GROUP
    ng = rw // gw
    nchunk = l // c
    pw = STATE_BLOCK_W
    npair = gw // pw
    assert c & (c - 1) == 0
    lg = log_gamma.reshape(2, ng, hpg).transpose(1, 0, 2)
    lg_lane = jnp.repeat(lg, RET_HEAD_DIM, axis=2)
    lg_col = jnp.repeat(lg, c, axis=2)
    tok = pl.BlockSpec((1, l, gw), lambda i, j: (i, 0, j))
    tokt = pl.BlockSpec((1, gw, l), lambda i, j: (i, j, 0))
    grp = lambda r, width: pl.BlockSpec((None, r, width), lambda i, j: (j, 0, 0))
    return pl.pallas_call(
        functools.partial(_ret_kernel, nchunk=nchunk, unroll=math.gcd(nchunk, RET_UNROLL)),
        grid=(b, ng),
        in_specs=[
            tok, tokt, tok, tok,
            pl.BlockSpec((1, gw, lc), lambda i, j: (i, j, 0)),
            pl.BlockSpec((1, lc, gw), lambda i, j: (i, 0, j)),
            grp(2, gw), grp(2, hpg * c),
        ],
        out_specs=tok,
        out_shape=jax.ShapeDtypeStruct((b, l, rw), BF16),
        scratch_shapes=[pltpu.VMEM((nchunk, npair, pw, pw), BF16), pltpu.VMEM((nchunk, npair, pw, pw), BF16),
                        pltpu.VMEM((npair, pw, pw), F32), pltpu.VMEM((npair, pw, pw), F32),
                        pltpu.VMEM((nchunk, c, hpg * c), BF16),
                        pltpu.VMEM((nchunk, c, gw), F32),
                        pltpu.VMEM((c, hpg * c), F32)] + [pltpu.VMEM((c, gw), F32)] * 4
        + [pltpu.VMEM((nchunk, npair, pw, 2 * pw), F32)],
        compiler_params=_params("arbitrary", "arbitrary", allow_input_fusion=[False] * 6 + [True] * 2),
        name="ret",
    )(q, kt, v, sg, kct, vc, lg_lane, lg_col)


def _out_kernel(w4_ref, wr_ref, wo_ref, w1_ref, w2_ref, x_ref, mod_ref, fm_ref, z_ref, gab_ref,
                gain2_ref, fgain_ref, o_ref, *, ff_chunk):
    mod = mod_ref[:, pl.ds(pl.program_id(0), 1), :]
    g1, sh2, sc2, g2 = mod[2], mod[3], mod[4], mod[5]
    y_four = _dot(fm_ref[0], w4_ref[...])
    y_ret = _dot(z_ref[0], wr_ref[...])
    d = x_ref.shape[2]
    y = (jax.nn.sigmoid(gab_ref[0, :, :d].astype(F32)) * y_four
         + jax.nn.sigmoid(gab_ref[0, :, d:].astype(F32)) * y_ret)
    x1 = x_ref[0] + g1 * _dot(y.astype(BF16), wo_ref[...])
    h2 = _norm_mod(x1, gain2_ref[...], sh2, sc2).astype(BF16)
    dff = w1_ref.shape[1]
    acc = None
    for lo in range(0, dff, ff_chunk):
        hid = jnp.maximum(_dot(h2, w1_ref[:, lo:lo + ff_chunk]), 0.0)
        part = _dot((hid * hid).astype(BF16), w2_ref[lo:lo + ff_chunk, :])
        acc = part if acc is None else acc + part
    x2 = x1 + g2 * acc
    ms = jnp.mean(x2 * x2, axis=-1, keepdims=True)
    o_ref[0] = x2 * lax.rsqrt(ms + EPS) * fgain_ref[...]


def _out(x, mods, fm, z, gab, w4, wr, wo, gain2, w1, w2, fgain):
    b, l, d = x.shape
    tm = min(TOKEN_TILE, l)
    tok = lambda width: pl.BlockSpec((1, tm, width), lambda i, j: (i, j, 0))
    weights = (w4, wr, wo, w1, w2)
    return pl.pallas_call(
        functools.partial(_out_kernel, ff_chunk=min(1024, w1.shape[1])),
        grid=(b, l // tm),
        in_specs=[_const_spec(w.shape) for w in weights] + [
            tok(d),
            _const_spec(mods.shape),
            tok(fm.shape[2]), tok(z.shape[2]), tok(2 * d),
            _const_spec((1, d)), _const_spec((1, d)),
        ],
        out_specs=tok(d),
        out_shape=jax.ShapeDtypeStruct((b, l, d), F32),
        compiler_params=_params("arbitrary", "arbitrary"),
        name="out",
    )(*weights, x, mods, fm, z, gab, gain2, fgain)


def _dft_tables(l, gd):
    la = FFT_LA
    lb = l // la

    def cs(n):
        idx = np.arange(n)
        ang = 2.0 * np.pi * ((idx[:, None] * idx[None, :]) % n) / n
        return np.cos(ang), np.sin(ang)

    cb, sb = cs(lb)
    eye = np.eye(FFT_LA_TILE)
    half = lb // 2 + 1
    kw = np.concatenate([np.kron(cb[:half], eye), -np.kron(sb[:half], eye)], axis=0)
    ca, sa = cs(la)
    w2 = np.block([[ca, sa], [-sa, ca]])
    cc, sc = cs(gd)
    chan = np.concatenate([cc, sc], axis=0)
    tw = 2.0 * np.pi * (np.arange(lb)[:, None] * np.arange(la)[None, :]) / l
    tw = tw.reshape(lb, la // FFT_LA_TILE, FFT_LA_TILE).transpose(1, 0, 2).reshape(la // FFT_LA_TILE, -1)
    twc = np.repeat(np.cos(tw)[:, :, None], LANES, axis=2)
    tws = np.repeat(np.sin(tw)[:, :, None], LANES, axis=2)
    as_bf = lambda a: jnp.asarray(a, dtype=F32).astype(BF16)
    return as_bf(kw), as_bf(w2), as_bf(chan), jnp.asarray(twc, F32), jnp.asarray(tws, F32), la, lb


def _rope_tables(l):
    f32 = np.float32
    nf = RET_HEAD_DIM // 4
    inv = np.power(f32(ROPE_BASE), -np.arange(nf, dtype=f32) / f32(nf)).astype(f32)
    rows = l // GRID_W
    r, cc = np.meshgrid(np.arange(rows, dtype=f32), np.arange(GRID_W, dtype=f32), indexing="ij")
    ang = np.concatenate([r.reshape(-1)[:, None] * inv, cc.reshape(-1)[:, None] * inv], axis=-1).astype(f32)
    cos, sin = np.cos(ang).astype(f32), np.sin(ang).astype(f32)
    cos_h = np.concatenate([cos, cos], axis=1)
    sin_h = np.concatenate([-sin, sin], axis=1)
    reps = LANES // RET_HEAD_DIM
    return jnp.asarray(np.tile(cos_h, (1, reps))), jnp.asarray(np.tile(sin_h, (1, reps)))


def kernel(x, c, ctx, c_ctx, w_ada, b_ada, norm1_gain, w_in, four_w_out, ret_decay_logit,
           ret_gn_gain, ret_w_out, w_out, norm2_gain, w_mlp1, w_mlp2, final_gain):
    assert w_ada.shape[0] == 1, "single-layer block"
    b, l, d = x.shape
    lc = ctx.shape[1]
    fw = four_w_out.shape[1]
    rw = ret_w_out.shape[1]
    gd = fw // FOUR_GROUPS
    assert l % FFT_LA == 0 and l % RET_CHUNK == 0 and rw % GROUP_W == 0

    kw, w2, chan, twc, tws, la, lb = _dft_tables(l, gd)
    cosf, sinf = _rope_tables(l)

    later = [four_w_out[0], ret_w_out[0], w_out[0], w_mlp1[0], w_mlp2[0]]
    (u, q, kt, v, sg, gab), later_b, mods, kct, vc = _proj(
        x, c, c_ctx, w_ada[0], b_ada, ctx, norm1_gain, w_in[0], cosf, sinf, ret_gn_gain, later, fw, rw)

    fm = _fft(u, kw, twc, tws, w2, chan, la, lb, gd, 1.0 / math.sqrt(l * gd))

    log_gamma = jax.nn.log_sigmoid(ret_decay_logit[0].astype(F32))
    z = _retention(q, kt, v, sg, kct, vc, log_gamma)

    w4, wr, wo, w1, w2 = later_b
    return _out(x, mods, fm, z, gab, w4, wr, wo, norm2_gain, w1, w2, final_gain[None, :])
```
